```python
import math
import jax, jax.numpy as jnp
from jax import lax
import numpy as np

D_MODEL = 2048
BATCH = 8
SEQ = 8192
DEPTH = 4

HEAD_DIM = 128
A_HEADS = 8
A_WIDTH = A_HEADS * HEAD_DIM
DILATED_BRANCHES = ((128, 1), (512, 4), (2048, 16))
A_QBLOCK = 128
B_WIDTH = D_MODEL - A_WIDTH
B_GROUP = 16
B_GROUPS = B_WIDTH // B_GROUP
B_STATE = 64
DT_MIN = 1e-3
DT_MAX = 1e-1
C_HEADS = D_MODEL // HEAD_DIM
C_WIDTH = C_HEADS * HEAD_DIM
GRID_W = 64
NA_ROWS_MAX = 8
NA_COLS = 16
T5_BUCKETS = 32
T5_MAX_DISTANCE = 1024
D_FF = 4 * D_MODEL
RMS_EPS = 1e-6
NEG_INF = -1e30
N_EVEN = (DEPTH + 1) // 2
N_ODD = DEPTH // 2

kernel_name = "hybrid_dilated_s5_neighbourhood_encoder"


def rmsnorm(x, g):
    x32 = x.astype(jnp.float32)
    y = x32 * lax.rsqrt(jnp.mean(x32 * x32, axis=-1, keepdims=True) + RMS_EPS)
    return (y * g.astype(jnp.float32)).astype(x.dtype)


def t5_bucket(rel):
    half = T5_BUCKETS // 2
    max_exact = half // 2
    n = jnp.abs(rel)
    nf = jnp.maximum(n, 1).astype(jnp.float32)
    large = max_exact + (jnp.log(nf / max_exact) / math.log(T5_MAX_DISTANCE / max_exact)
                         * (half - max_exact)).astype(jnp.int32)
    large = jnp.minimum(large, half - 1)
    return jnp.where(rel > 0, half, 0) + jnp.where(n < max_exact, n, large)


def dilated_branch(q, k, v, t5_table, window, dilation):
    b, h, s, hd = q.shape
    half = window // (2 * dilation)
    L = s // dilation
    nb = -(-L // A_QBLOCK)
    lp = nb * A_QBLOCK
    kb_len = A_QBLOCK + 2 * half
    scale = 1.0 / math.sqrt(hd)

    def strided(t):
        return t.reshape(b, h, L, dilation, hd).transpose(0, 1, 3, 2, 4)

    qs = jnp.pad(strided(q), ((0, 0), (0, 0), (0, 0), (0, lp - L), (0, 0)))
    qs = qs.reshape(b, h, dilation, nb, A_QBLOCK, hd)
    pad_k = ((0, 0), (0, 0), (0, 0), (half, lp - L + half), (0, 0))
    key_idx = jnp.arange(nb)[:, None] * A_QBLOCK + jnp.arange(kb_len)[None, :]
    ks = jnp.pad(strided(k), pad_k)[:, :, :, key_idx]
    vs = jnp.pad(strided(v), pad_k)[:, :, :, key_idx]

    off = jnp.arange(kb_len)[None, :] - half - jnp.arange(A_QBLOCK)[:, None]
    bias = jnp.transpose(t5_table[t5_bucket(off * dilation)], (2, 0, 1)).astype(jnp.float32)
    key_pos = key_idx - half
    valid = (jnp.abs(off) <= half)[None] & ((key_pos >= 0) & (key_pos < L))[:, None, :]

    sc = jnp.einsum('bhrnqc,bhrnkc->bhrnqk', qs, ks) * scale + bias[None, :, None, None]
    sc = jnp.where(valid, sc, NEG_INF)
    m = sc.max(axis=-1)
    p = jnp.exp(sc - m[..., None])
    l = p.sum(axis=-1)
    num = jnp.einsum('bhrnqk,bhrnkc->bhrnqc', p, vs)

    def unstride(t):
        t = t.reshape(b, h, dilation, lp, *t.shape[5:])[:, :, :, :L]
        t = jnp.moveaxis(t, 2, 3)
        return t.reshape(b, h, s, *t.shape[4:])

    return unstride(m), unstride(l), unstride(num)


def dilated_attention(q, k, v, t5_table):
    outs = [dilated_branch(q, k, v, t5_table, w, d) for (w, d) in DILATED_BRANCHES]
    m = jnp.stack([o[0] for o in outs])
    l = jnp.stack([o[1] for o in outs])
    num = jnp.stack([o[2] for o in outs])
    wgt = jnp.exp(m - m.max(axis=0, keepdims=True))
    return (wgt[..., None] * num).sum(0) / (wgt * l).sum(0)[..., None]


def s5_direction(u, lam_re, lam_im, log_step, b_re, b_im, c_re, c_im, reverse):
    f32 = jnp.float32
    step = jnp.exp(log_step.astype(f32))[:, None]
    lr = jnp.minimum(lam_re.astype(f32), -1e-4)
    li = lam_im.astype(f32)
    mag = jnp.exp(lr * step)
    ab_re = mag * jnp.cos(li * step)
    ab_im = mag * jnp.sin(li * step)
    den = lr * lr + li * li
    zr = ((ab_re - 1.0) * lr + ab_im * li) / den
    zi = (ab_im * lr - (ab_re - 1.0) * li) / den
    br = b_re.astype(f32)
    bi = b_im.astype(f32)
    bb_re = zr[..., None] * br - zi[..., None] * bi
    bb_im = zr[..., None] * bi + zi[..., None] * br
    bu_re = jnp.einsum('bsgc,gpc->bsgp', u, bb_re)
    bu_im = jnp.einsum('bsgc,gpc->bsgp', u, bb_im)
    a_re = jnp.broadcast_to(ab_re, bu_re.shape)
    a_im = jnp.broadcast_to(ab_im, bu_im.shape)

    def combine(e1, e2):
        a1r, a1i, b1r, b1i = e1
        a2r, a2i, b2r, b2i = e2
        return (a2r * a1r - a2i * a1i,
                a2r * a1i + a2i * a1r,
                a2r * b1r - a2i * b1i + b2r,
                a2r * b1i + a2i * b1r + b2i)

    _, _, xr, xi = lax.associative_scan(combine, (a_re, a_im, bu_re, bu_im), reverse=reverse, axis=1)
    return (jnp.einsum('bsgp,gcp->bsgc', xr, c_re.astype(f32))
            - jnp.einsum('bsgp,gcp->bsgc', xi, c_im.astype(f32)))


def neighbourhood_attention(q, k, v, rpb):
    b, h, s, hd = q.shape
    rows = s // GRID_W
    kr = min(NA_ROWS_MAX, rows)
    scale = 1.0 / math.sqrt(hd)
    r = jnp.arange(rows)
    c = jnp.arange(GRID_W)
    row_start = jnp.clip(r - kr // 2, 0, rows - kr)
    row_idx = row_start[:, None] + jnp.arange(kr)[None, :]
    col_start = jnp.clip(c - NA_COLS // 2, 0, GRID_W - NA_COLS)
    col_ok = (c[None, :] >= col_start[:, None]) & (c[None, :] < col_start[:, None] + NA_COLS)
    col_off = c[None, :] - c[:, None]
    row_off = row_idx - r[:, None]

    qg = q.reshape(b, h, rows, GRID_W, hd)
    kg = k.reshape(b, h, rows, GRID_W, hd)[:, :, row_idx]
    vg = v.reshape(b, h, rows, GRID_W, hd)[:, :, row_idx]
    bias = rpb[:, (row_off + NA_ROWS_MAX - 1)[:, None, :, None],
               jnp.clip(col_off + NA_COLS - 1, 0, 2 * NA_COLS - 2)[None, :, None, :]]
    sc = jnp.einsum('bhrqc,bhrjkc->bhrqjk', qg, kg) * scale + bias[None].astype(jnp.float32)
    sc = jnp.where(col_ok[:, None, :], sc, NEG_INF)
    p = jax.nn.softmax(sc.reshape(b, h, rows, GRID_W, kr * GRID_W), axis=-1).reshape(sc.shape)
    o = jnp.einsum('bhrqjk,bhrjkc->bhrqc', p, vg)
    return o.reshape(b, h, s, hd)


def mixer_ab(xn, w_in, w_out, t5_table, lam_re, lam_im, log_step, b_re, b_im, c_re, c_im, d_skip, w_glu):
    b, s, _ = xn.shape
    f32 = jnp.float32
    proj = xn @ w_in
    q, k, v, u = jnp.split(proj, [A_WIDTH, 2 * A_WIDTH, 3 * A_WIDTH], axis=-1)

    def heads(t):
        return t.astype(f32).reshape(b, s, A_HEADS, HEAD_DIM).transpose(0, 2, 1, 3)

    o_a = dilated_attention(heads(q), heads(k), heads(v), t5_table)
    o_a = o_a.transpose(0, 2, 1, 3).reshape(b, s, A_WIDTH)

    ug = u.astype(f32).reshape(b, s, B_GROUPS, B_GROUP)
    y = (s5_direction(ug, lam_re[0], lam_im[0], log_step[0], b_re, b_im, c_re[0], c_im[0], False)
         + s5_direction(ug, lam_re[1], lam_im[1], log_step[1], b_re, b_im, c_re[1], c_im[1], True)
         + d_skip.astype(f32).reshape(B_GROUPS, B_GROUP) * ug)
    y = jax.nn.gelu(y.reshape(b, s, B_WIDTH))
    o_b = y * jax.nn.sigmoid((y.astype(xn.dtype) @ w_glu).astype(f32))
    merged = jnp.concatenate([o_a, o_b], axis=-1).astype(xn.dtype)
    return merged @ w_out


def mixer_c(xn, w_qkv, w_out, rpb):
    b, s, _ = xn.shape
    q, k, v = jnp.split(xn @ w_qkv, 3, axis=-1)

    def heads(t):
        return t.astype(jnp.float32).reshape(b, s, C_HEADS, HEAD_DIM).transpose(0, 2, 1, 3)

    o = neighbourhood_attention(heads(q), heads(k), heads(v), rpb)
    o = o.transpose(0, 2, 1, 3).reshape(b, s, C_WIDTH).astype(xn.dtype)
    return o @ w_out


def squared_relu_mlp(xn, w1, w2):
    hdn = jnp.square(jax.nn.relu(xn @ w1))
    return hdn @ w2


def _fwd_setup_inputs(seed: int = 0) -> dict:
    key = jax.random.key(seed)
    ks = jax.random.split(key, 22)
    f32 = jnp.float32
    nrm = lambda k, shape, sc: (jax.random.normal(k, shape, f32) * sc)
    lam_im_init = jnp.pi * jnp.arange(B_STATE, dtype=f32)
    return {
        "x": nrm(ks[0], (BATCH, SEQ, D_MODEL), 1.0),
        "t5_bias": nrm(ks[1], (T5_BUCKETS, A_HEADS), 0.5),
        "ab_w_in": nrm(ks[2], (N_EVEN, D_MODEL, 3 * A_WIDTH + B_WIDTH), D_MODEL ** -0.5),
        "ab_w_out": nrm(ks[3], (N_EVEN, A_WIDTH + B_WIDTH, D_MODEL), (A_WIDTH + B_WIDTH) ** -0.5),
        "s5_lam_re": -0.5 + nrm(ks[4], (N_EVEN, 2, B_GROUPS, B_STATE), 0.01),
        "s5_lam_im": lam_im_init + nrm(ks[5], (N_EVEN, 2, B_GROUPS, B_STATE), 0.01),
        "s5_log_step": jax.random.uniform(ks[6], (N_EVEN, 2, B_GROUPS), f32,
                                          minval=math.log(DT_MIN), maxval=math.log(DT_MAX)),
        "s5_b_re": nrm(ks[7], (N_EVEN, B_GROUPS, B_STATE, B_GROUP), (2 * B_GROUP) ** -0.5),
        "s5_b_im": nrm(ks[8], (N_EVEN, B_GROUPS, B_STATE, B_GROUP), (2 * B_GROUP) ** -0.5),
        "s5_c_re": nrm(ks[9], (N_EVEN, 2, B_GROUPS, B_GROUP, B_STATE), (2 * B_STATE) ** -0.5),
        "s5_c_im": nrm(ks[10], (N_EVEN, 2, B_GROUPS, B_GROUP, B_STATE), (2 * B_STATE) ** -0.5),
        "s5_d": nrm(ks[11], (N_EVEN, B_WIDTH), 1.0),
        "s5_w_glu": nrm(ks[12], (N_EVEN, B_WIDTH, B_WIDTH), B_WIDTH ** -0.5),
        "c_w_qkv": nrm(ks[13], (N_ODD, D_MODEL, 3 * C_WIDTH), D_MODEL ** -0.5),
        "c_w_out": nrm(ks[14], (N_ODD, C_WIDTH, D_MODEL), C_WIDTH ** -0.5),
        "c_rpb": nrm(ks[15], (N_ODD, C_HEADS, 2 * NA_ROWS_MAX - 1, 2 * NA_COLS - 1), 0.5),
        "norm_mix": 1.0 + nrm(ks[16], (DEPTH, D_MODEL), 0.02),
        "norm_mlp": 1.0 + nrm(ks[17], (DEPTH, D_MODEL), 0.02),
        "mlp_w1": nrm(ks[18], (DEPTH, D_MODEL, D_FF), D_MODEL ** -0.5),
        "mlp_w2": nrm(ks[19], (DEPTH, D_FF, D_MODEL), D_FF ** -0.5),
        "norm_final": 1.0 + nrm(ks[20], (D_MODEL,), 0.02),
    }


def _fwd_reference(x, t5_bias, ab_w_in, ab_w_out, s5_lam_re, s5_lam_im, s5_log_step, s5_b_re, s5_b_im,
              s5_c_re, s5_c_im, s5_d, s5_w_glu, c_w_qkv, c_w_out, c_rpb, norm_mix, norm_mlp,
              mlp_w1, mlp_w2, norm_final):
    for i in range(DEPTH):
        j = i // 2
        hn = rmsnorm(x, norm_mix[i])
        if i % 2 == 0:
            mix = mixer_ab(hn, ab_w_in[j], ab_w_out[j], t5_bias, s5_lam_re[j], s5_lam_im[j],
                           s5_log_step[j], s5_b_re[j], s5_b_im[j], s5_c_re[j], s5_c_im[j],
                           s5_d[j], s5_w_glu[j])
        else:
            mix = mixer_c(hn, c_w_qkv[j], c_w_out[j], c_rpb[j])
        x = x + mix.astype(x.dtype)
        hn = rmsnorm(x, norm_mlp[i])
        x = x + squared_relu_mlp(hn, mlp_w1[i], mlp_w2[i]).astype(x.dtype)
    return rmsnorm(x, norm_final)


import jax as _jax
import jax.numpy as _jnp

TWIN_FORMAT = 'train_step'
FWD_PARAMS = ['x', 't5_bias', 'ab_w_in', 'ab_w_out', 's5_lam_re', 's5_lam_im', 's5_log_step', 's5_b_re', 's5_b_im', 's5_c_re', 's5_c_im', 's5_d', 's5_w_glu', 'c_w_qkv', 'c_w_out', 'c_rpb', 'norm_mix', 'norm_mlp', 'mlp_w1', 'mlp_w2', 'norm_final']
TWIN_WEIGHTS = ['t5_bias', 'ab_w_in', 'ab_w_out', 's5_lam_re', 's5_lam_im', 's5_log_step', 's5_b_re', 's5_b_im', 's5_c_re', 's5_c_im', 's5_d', 's5_w_glu', 'c_w_qkv', 'c_w_out', 'c_rpb', 'norm_mix', 'norm_mlp', 'mlp_w1', 'mlp_w2', 'norm_final']
TWIN_DIFF_INPUT = 'x'
TWIN_INPUTS = ['x', 't5_bias', 'ab_w_in', 'ab_w_out', 's5_lam_re', 's5_lam_im', 's5_log_step', 's5_b_re', 's5_b_im', 's5_c_re', 's5_c_im', 's5_d', 's5_w_glu', 'c_w_qkv', 'c_w_out', 'c_rpb', 'norm_mix', 'norm_mlp', 'mlp_w1', 'mlp_w2', 'norm_final', 'loss_target', 'm_t5_bias', 'm_ab_w_in', 'm_ab_w_out', 'm_s5_lam_re', 'm_s5_lam_im', 'm_s5_log_step', 'm_s5_b_re', 'm_s5_b_im', 'm_s5_c_re', 'm_s5_c_im', 'm_s5_d', 'm_s5_w_glu', 'm_c_w_qkv', 'm_c_w_out', 'm_c_rpb', 'm_norm_mix', 'm_norm_mlp', 'm_mlp_w1', 'm_mlp_w2', 'm_norm_final', 'v_t5_bias', 'v_ab_w_in', 'v_ab_w_out', 'v_s5_lam_re', 'v_s5_lam_im', 'v_s5_log_step', 'v_s5_b_re', 'v_s5_b_im', 'v_s5_c_re', 'v_s5_c_im', 'v_s5_d', 'v_s5_w_glu', 'v_c_w_qkv', 'v_c_w_out', 'v_c_rpb', 'v_norm_mix', 'v_norm_mlp', 'v_mlp_w1', 'v_mlp_w2', 'v_norm_final']
TWIN_OUTPUTS = ['loss', 'grad_x', 'grad_t5_bias', 'grad_ab_w_in', 'grad_ab_w_out', 'grad_s5_lam_re', 'grad_s5_lam_im', 'grad_s5_log_step', 'grad_s5_b_re', 'grad_s5_b_im', 'grad_s5_c_re', 'grad_s5_c_im', 'grad_s5_d', 'grad_s5_w_glu', 'grad_c_w_qkv', 'grad_c_w_out', 'grad_c_rpb', 'grad_norm_mix', 'grad_norm_mlp', 'grad_mlp_w1', 'grad_mlp_w2', 'grad_norm_final', 'delta_t5_bias', 'delta_ab_w_in', 'delta_ab_w_out', 'delta_s5_lam_re', 'delta_s5_lam_im', 'delta_s5_log_step', 'delta_s5_b_re', 'delta_s5_b_im', 'delta_s5_c_re', 'delta_s5_c_im', 'delta_s5_d', 'delta_s5_w_glu', 'delta_c_w_qkv', 'delta_c_w_out', 'delta_c_rpb', 'delta_norm_mix', 'delta_norm_mlp', 'delta_mlp_w1', 'delta_mlp_w2', 'delta_norm_final', 'new_m_t5_bias', 'new_m_ab_w_in', 'new_m_ab_w_out', 'new_m_s5_lam_re', 'new_m_s5_lam_im', 'new_m_s5_log_step', 'new_m_s5_b_re', 'new_m_s5_b_im', 'new_m_s5_c_re', 'new_m_s5_c_im', 'new_m_s5_d', 'new_m_s5_w_glu', 'new_m_c_w_qkv', 'new_m_c_w_out', 'new_m_c_rpb', 'new_m_norm_mix', 'new_m_norm_mlp', 'new_m_mlp_w1', 'new_m_mlp_w2', 'new_m_norm_final', 'new_v_t5_bias', 'new_v_ab_w_in', 'new_v_ab_w_out', 'new_v_s5_lam_re', 'new_v_s5_lam_im', 'new_v_s5_log_step', 'new_v_s5_b_re', 'new_v_s5_b_im', 'new_v_s5_c_re', 'new_v_s5_c_im', 'new_v_s5_d', 'new_v_s5_w_glu', 'new_v_c_w_qkv', 'new_v_c_w_out', 'new_v_c_rpb', 'new_v_norm_mix', 'new_v_norm_mlp', 'new_v_mlp_w1', 'new_v_mlp_w2', 'new_v_norm_final']
TWIN_LEAF_KINDS = {'loss': 'loss', 'grad_x': 'grad_x', 'grad_t5_bias': 'grad_w', 'grad_ab_w_in': 'grad_w', 'grad_ab_w_out': 'grad_w', 'grad_s5_lam_re': 'grad_w', 'grad_s5_lam_im': 'grad_w', 'grad_s5_log_step': 'grad_w', 'grad_s5_b_re': 'grad_w', 'grad_s5_b_im': 'grad_w', 'grad_s5_c_re': 'grad_w', 'grad_s5_c_im': 'grad_w', 'grad_s5_d': 'grad_w', 'grad_s5_w_glu': 'grad_w', 'grad_c_w_qkv': 'grad_w', 'grad_c_w_out': 'grad_w', 'grad_c_rpb': 'grad_w', 'grad_norm_mix': 'grad_w', 'grad_norm_mlp': 'grad_w', 'grad_mlp_w1': 'grad_w', 'grad_mlp_w2': 'grad_w', 'grad_norm_final': 'grad_w', 'delta_t5_bias': 'delta_w', 'delta_ab_w_in': 'delta_w', 'delta_ab_w_out': 'delta_w', 'delta_s5_lam_re': 'delta_w', 'delta_s5_lam_im': 'delta_w', 'delta_s5_log_step': 'delta_w', 'delta_s5_b_re': 'delta_w', 'delta_s5_b_im': 'delta_w', 'delta_s5_c_re': 'delta_w', 'delta_s5_c_im': 'delta_w', 'delta_s5_d': 'delta_w', 'delta_s5_w_glu': 'delta_w', 'delta_c_w_qkv': 'delta_w', 'delta_c_w_out': 'delta_w', 'delta_c_rpb': 'delta_w', 'delta_norm_mix': 'delta_w', 'delta_norm_mlp': 'delta_w', 'delta_mlp_w1': 'delta_w', 'delta_mlp_w2': 'delta_w', 'delta_norm_final': 'delta_w', 'new_m_t5_bias': 'new_m', 'new_m_ab_w_in': 'new_m', 'new_m_ab_w_out': 'new_m', 'new_m_s5_lam_re': 'new_m', 'new_m_s5_lam_im': 'new_m', 'new_m_s5_log_step': 'new_m', 'new_m_s5_b_re': 'new_m', 'new_m_s5_b_im': 'new_m', 'new_m_s5_c_re': 'new_m', 'new_m_s5_c_im': 'new_m', 'new_m_s5_d': 'new_m', 'new_m_s5_w_glu': 'new_m', 'new_m_c_w_qkv': 'new_m', 'new_m_c_w_out': 'new_m', 'new_m_c_rpb': 'new_m', 'new_m_norm_mix': 'new_m', 'new_m_norm_mlp': 'new_m', 'new_m_mlp_w1': 'new_m', 'new_m_mlp_w2': 'new_m', 'new_m_norm_final': 'new_m', 'new_v_t5_bias': 'new_v', 'new_v_ab_w_in': 'new_v', 'new_v_ab_w_out': 'new_v', 'new_v_s5_lam_re': 'new_v', 'new_v_s5_lam_im': 'new_v', 'new_v_s5_log_step': 'new_v', 'new_v_s5_b_re': 'new_v', 'new_v_s5_b_im': 'new_v', 'new_v_s5_c_re': 'new_v', 'new_v_s5_c_im': 'new_v', 'new_v_s5_d': 'new_v', 'new_v_s5_w_glu': 'new_v', 'new_v_c_w_qkv': 'new_v', 'new_v_c_w_out': 'new_v', 'new_v_c_rpb': 'new_v', 'new_v_norm_mix': 'new_v', 'new_v_norm_mlp': 'new_v', 'new_v_mlp_w1': 'new_v', 'new_v_mlp_w2': 'new_v', 'new_v_norm_final': 'new_v'}


def _forward(args):
    return _fwd_reference(*[args[k] for k in FWD_PARAMS])


def _output_shape():
    def fwd():
        inp = _fwd_setup_inputs(0)
        return _fwd_reference(*[inp[k] for k in FWD_PARAMS])
    out = _jax.eval_shape(fwd)
    return out.shape, out.dtype

N_MICROBATCH = 1
ADAM_LR = 0.001
ADAM_B1 = 0.9
ADAM_B2 = 0.999
ADAM_EPS = 1e-08
ADAM_WD = 0.01
ADAM_STEP = 10
PER_EXAMPLE_BATCH_AXIS = {'x': 0, 'loss_target': 0}
SHARED_INPUTS = []
_WEIGHT_DTYPES = {'t5_bias': _jnp.float32, 'ab_w_in': _jnp.float32, 'ab_w_out': _jnp.float32, 's5_lam_re': _jnp.float32, 's5_lam_im': _jnp.float32, 's5_log_step': _jnp.float32, 's5_b_re': _jnp.float32, 's5_b_im': _jnp.float32, 's5_c_re': _jnp.float32, 's5_c_im': _jnp.float32, 's5_d': _jnp.float32, 's5_w_glu': _jnp.float32, 'c_w_qkv': _jnp.float32, 'c_w_out': _jnp.float32, 'c_rpb': _jnp.float32, 'norm_mix': _jnp.float32, 'norm_mlp': _jnp.float32, 'mlp_w1': _jnp.float32, 'mlp_w2': _jnp.float32, 'norm_final': _jnp.float32}
MOMENT_SCALE = {'t5_bias': 3.905196e-02, 'ab_w_in': 3.376934e-02, 'ab_w_out': 5.732782e-02, 's5_lam_re': 4.198375e-03, 's5_lam_im': 4.367837e-03, 's5_log_step': 1.326892e+00, 's5_b_re': 2.809222e-03, 's5_b_im': 2.825315e-03, 's5_c_re': 3.697496e-03, 's5_c_im': 3.966291e-03, 's5_d': 6.261367e-02, 's5_w_glu': 1.369335e-02, 'c_w_qkv': 4.573063e-02, 'c_w_out': 8.298802e-02, 'c_rpb': 6.662484e-03, 'norm_mix': 6.147582e-02, 'norm_mlp': 1.031430e-01, 'mlp_w1': 5.185886e-02, 'mlp_w2': 1.371136e-01, 'norm_final': 3.417663e+01}


def _to_microbatches(a, axis):
    t = _jnp.moveaxis(a, axis, 0)
    t = t.reshape((N_MICROBATCH, t.shape[0] // N_MICROBATCH) + t.shape[1:])
    return _jnp.moveaxis(t, 1, axis + 1)


def setup_inputs(seed: int = 0) -> dict:
    inp = _fwd_setup_inputs(seed)
    key = _jax.random.fold_in(_jax.random.key(seed), 7919)
    shape, _ = _output_shape()
    out = dict(inp)
    out["loss_target"] = _jax.random.normal(_jax.random.fold_in(key, 0), shape, _jnp.float32)
    for i, name in enumerate(TWIN_WEIGHTS):
        w = inp[name].astype(_jnp.float32)
        if MOMENT_SCALE is None:
            s = _jnp.sqrt(_jnp.mean(_jnp.square(w)) + 1e-30)
        else:
            s = MOMENT_SCALE[name]
        km, kv = _jax.random.split(_jax.random.fold_in(key, i + 1))
        out[name] = w
        out["m_" + name] = s * _jax.random.normal(km, w.shape, _jnp.float32)
        out["v_" + name] = (s * s) * _jax.random.uniform(kv, w.shape, _jnp.float32, 0.5, 1.5)
    if N_MICROBATCH > 1:
        for name, axis in PER_EXAMPLE_BATCH_AXIS.items():
            out[name] = _to_microbatches(out[name], axis)
    return {'x': out['x'], 't5_bias': out['t5_bias'], 'ab_w_in': out['ab_w_in'], 'ab_w_out': out['ab_w_out'], 's5_lam_re': out['s5_lam_re'], 's5_lam_im': out['s5_lam_im'], 's5_log_step': out['s5_log_step'], 's5_b_re': out['s5_b_re'], 's5_b_im': out['s5_b_im'], 's5_c_re': out['s5_c_re'], 's5_c_im': out['s5_c_im'], 's5_d': out['s5_d'], 's5_w_glu': out['s5_w_glu'], 'c_w_qkv': out['c_w_qkv'], 'c_w_out': out['c_w_out'], 'c_rpb': out['c_rpb'], 'norm_mix': out['norm_mix'], 'norm_mlp': out['norm_mlp'], 'mlp_w1': out['mlp_w1'], 'mlp_w2': out['mlp_w2'], 'norm_final': out['norm_final'], 'loss_target': out['loss_target'], 'm_t5_bias': out['m_t5_bias'], 'm_ab_w_in': out['m_ab_w_in'], 'm_ab_w_out': out['m_ab_w_out'], 'm_s5_lam_re': out['m_s5_lam_re'], 'm_s5_lam_im': out['m_s5_lam_im'], 'm_s5_log_step': out['m_s5_log_step'], 'm_s5_b_re': out['m_s5_b_re'], 'm_s5_b_im': out['m_s5_b_im'], 'm_s5_c_re': out['m_s5_c_re'], 'm_s5_c_im': out['m_s5_c_im'], 'm_s5_d': out['m_s5_d'], 'm_s5_w_glu': out['m_s5_w_glu'], 'm_c_w_qkv': out['m_c_w_qkv'], 'm_c_w_out': out['m_c_w_out'], 'm_c_rpb': out['m_c_rpb'], 'm_norm_mix': out['m_norm_mix'], 'm_norm_mlp': out['m_norm_mlp'], 'm_mlp_w1': out['m_mlp_w1'], 'm_mlp_w2': out['m_mlp_w2'], 'm_norm_final': out['m_norm_final'], 'v_t5_bias': out['v_t5_bias'], 'v_ab_w_in': out['v_ab_w_in'], 'v_ab_w_out': out['v_ab_w_out'], 'v_s5_lam_re': out['v_s5_lam_re'], 'v_s5_lam_im': out['v_s5_lam_im'], 'v_s5_log_step': out['v_s5_log_step'], 'v_s5_b_re': out['v_s5_b_re'], 'v_s5_b_im': out['v_s5_b_im'], 'v_s5_c_re': out['v_s5_c_re'], 'v_s5_c_im': out['v_s5_c_im'], 'v_s5_d': out['v_s5_d'], 'v_s5_w_glu': out['v_s5_w_glu'], 'v_c_w_qkv': out['v_c_w_qkv'], 'v_c_w_out': out['v_c_w_out'], 'v_c_rpb': out['v_c_rpb'], 'v_norm_mix': out['v_norm_mix'], 'v_norm_mlp': out['v_norm_mlp'], 'v_mlp_w1': out['v_mlp_w1'], 'v_mlp_w2': out['v_mlp_w2'], 'v_norm_final': out['v_norm_final']}


def _loss(weights, diff, rest, loss_target):
    with _jax.named_scope("forward"):
        args = {**rest, TWIN_DIFF_INPUT: diff, **{k: w.astype(_WEIGHT_DTYPES[k]) for k, w in weights.items()}}
        y = _forward(args)
    with _jax.named_scope("loss_head"):
        err = _jnp.square(y.astype(_jnp.float32) - loss_target)
        return 0.5 * _jnp.sum(_jnp.mean(err, axis=-1)) if err.ndim else 0.5 * err


def _adamw(w, g, m, v):
    m = ADAM_B1 * m + (1.0 - ADAM_B1) * g
    v = ADAM_B2 * v + (1.0 - ADAM_B2) * _jnp.square(g)
    m_hat = m / (1.0 - ADAM_B1 ** ADAM_STEP)
    v_hat = v / (1.0 - ADAM_B2 ** ADAM_STEP)
    delta = -ADAM_LR * (m_hat / (_jnp.sqrt(v_hat) + ADAM_EPS) + ADAM_WD * w)
    return delta, m, v


def reference(x, t5_bias, ab_w_in, ab_w_out, s5_lam_re, s5_lam_im, s5_log_step, s5_b_re, s5_b_im, s5_c_re, s5_c_im, s5_d, s5_w_glu, c_w_qkv, c_w_out, c_rpb, norm_mix, norm_mlp, mlp_w1, mlp_w2, norm_final, loss_target, m_t5_bias, m_ab_w_in, m_ab_w_out, m_s5_lam_re, m_s5_lam_im, m_s5_log_step, m_s5_b_re, m_s5_b_im, m_s5_c_re, m_s5_c_im, m_s5_d, m_s5_w_glu, m_c_w_qkv, m_c_w_out, m_c_rpb, m_norm_mix, m_norm_mlp, m_mlp_w1, m_mlp_w2, m_norm_final, v_t5_bias, v_ab_w_in, v_ab_w_out, v_s5_lam_re, v_s5_lam_im, v_s5_log_step, v_s5_b_re, v_s5_b_im, v_s5_c_re, v_s5_c_im, v_s5_d, v_s5_w_glu, v_c_w_qkv, v_c_w_out, v_c_rpb, v_norm_mix, v_norm_mlp, v_mlp_w1, v_mlp_w2, v_norm_final):
    given = dict(x=x, t5_bias=t5_bias, ab_w_in=ab_w_in, ab_w_out=ab_w_out, s5_lam_re=s5_lam_re, s5_lam_im=s5_lam_im, s5_log_step=s5_log_step, s5_b_re=s5_b_re, s5_b_im=s5_b_im, s5_c_re=s5_c_re, s5_c_im=s5_c_im, s5_d=s5_d, s5_w_glu=s5_w_glu, c_w_qkv=c_w_qkv, c_w_out=c_w_out, c_rpb=c_rpb, norm_mix=norm_mix, norm_mlp=norm_mlp, mlp_w1=mlp_w1, mlp_w2=mlp_w2, norm_final=norm_final, loss_target=loss_target, m_t5_bias=m_t5_bias, m_ab_w_in=m_ab_w_in, m_ab_w_out=m_ab_w_out, m_s5_lam_re=m_s5_lam_re, m_s5_lam_im=m_s5_lam_im, m_s5_log_step=m_s5_log_step, m_s5_b_re=m_s5_b_re, m_s5_b_im=m_s5_b_im, m_s5_c_re=m_s5_c_re, m_s5_c_im=m_s5_c_im, m_s5_d=m_s5_d, m_s5_w_glu=m_s5_w_glu, m_c_w_qkv=m_c_w_qkv, m_c_w_out=m_c_w_out, m_c_rpb=m_c_rpb, m_norm_mix=m_norm_mix, m_norm_mlp=m_norm_mlp, m_mlp_w1=m_mlp_w1, m_mlp_w2=m_mlp_w2, m_norm_final=m_norm_final, v_t5_bias=v_t5_bias, v_ab_w_in=v_ab_w_in, v_ab_w_out=v_ab_w_out, v_s5_lam_re=v_s5_lam_re, v_s5_lam_im=v_s5_lam_im, v_s5_log_step=v_s5_log_step, v_s5_b_re=v_s5_b_re, v_s5_b_im=v_s5_b_im, v_s5_c_re=v_s5_c_re, v_s5_c_im=v_s5_c_im, v_s5_d=v_s5_d, v_s5_w_glu=v_s5_w_glu, v_c_w_qkv=v_c_w_qkv, v_c_w_out=v_c_w_out, v_c_rpb=v_c_rpb, v_norm_mix=v_norm_mix, v_norm_mlp=v_norm_mlp, v_mlp_w1=v_mlp_w1, v_mlp_w2=v_mlp_w2, v_norm_final=v_norm_final)
    weights = {n: given[n] for n in TWIN_WEIGHTS}
    shared = {n: given[n] for n in SHARED_INPUTS}
    per_example = {n: given[n] for n in ['x']}
    grad_fn = _jax.value_and_grad(_loss, argnums=(0, 1))

    def one_microbatch(ex, loss_target):
        ex = dict(ex)
        diff = ex.pop(TWIN_DIFF_INPUT)
        return grad_fn(weights, diff, {**shared, **ex}, loss_target)

    if N_MICROBATCH == 1:
        loss, (grad_w, grad_x) = one_microbatch(per_example, given["loss_target"])
    else:
        def body(carry, xs):
            loss_sum, grad_sum = carry
            l_k, (gw_k, gx_k) = one_microbatch(xs[0], xs[1])
            with _jax.named_scope("update"):
                return (loss_sum + l_k, _jax.tree.map(_jnp.add, grad_sum, gw_k)), gx_k

        init = (_jnp.zeros((), _jnp.float32), _jax.tree.map(_jnp.zeros_like, weights))
        (loss, grad_w), grad_x = _jax.lax.scan(body, init, (per_example, given["loss_target"]))
    with _jax.named_scope("update"):
        delta_w, new_m, new_v = {}, {}, {}
        for n in TWIN_WEIGHTS:
            delta_w[n], new_m[n], new_v[n] = _adamw(weights[n], grad_w[n], given["m_" + n], given["v_" + n])
    return (loss, grad_x, *[grad_w[n] for n in TWIN_WEIGHTS], *[delta_w[n] for n in TWIN_WEIGHTS],
            *[new_m[n] for n in TWIN_WEIGHTS], *[new_v[n] for n in TWIN_WEIGHTS])
```

```python
import functools
import math

import jax
import jax.numpy as jnp
from jax import lax
from jax.experimental import pallas as pl
from jax.experimental.pallas import tpu as pltpu

F32 = jnp.float32
BF16 = jnp.bfloat16

HEAD_DIM = 128
LANES = 128
DILATED_BRANCHES = ((128, 1), (512, 4), (2048, 16))
A_QBLOCK = 128
DIL_HALF = 64
B_GROUP = 16
B_STATE = 64
GROUPS_PER_BLOCK = LANES // B_GROUP
STATE_PER_BLOCK = GROUPS_PER_BLOCK * B_STATE
GRID_W = 64
NA_ROWS_MAX = 8
NA_COLS = 16
T5_BUCKETS = 32
T5_MAX_DISTANCE = 1024
RMS_EPS = 1e-6
NEG_INF = -1e30
ADAM_LR = 0.001
ADAM_B1 = 0.9
ADAM_B2 = 0.999
ADAM_EPS = 1e-08
ADAM_WD = 0.01
ADAM_STEP = 10
V7X_VMEM_LIMIT_BYTES = 56 * 1024 * 1024

NN = (((1,), (0,)), ((), ()))
NT = (((1,), (1,)), ((), ()))
TN = (((0,), (0,)), ((), ()))
MESH = pl.DeviceIdType.MESH


def _tile(n, pref):
    t = pref
    while t >= 8:
        if n % t == 0:
            return t
        t //= 2
    return n


def _pcall(body, *, name, grid, in_specs, out_specs, out_shape, scratch=(), sem=None, **static):
    params = dict(vmem_limit_bytes=V7X_VMEM_LIMIT_BYTES)
    if sem is not None:
        params["dimension_semantics"] = sem
    return pl.pallas_call(
        functools.partial(body, **static), name=name, grid=grid, in_specs=in_specs, out_specs=out_specs,
        out_shape=out_shape, scratch_shapes=list(scratch), compiler_params=pltpu.CompilerParams(**params))


def _mm_body(a_ref, b_ref, *rest, nk, dims, mode):
    acc_ref = rest[-1]
    k = pl.program_id(2)

    @pl.when(k == 0)
    def _():
        acc_ref[...] = jnp.zeros_like(acc_ref)

    acc_ref[...] += lax.dot_general(a_ref[...].astype(BF16), b_ref[...].astype(BF16), dims,
                                    preferred_element_type=F32)

    @pl.when(k == nk - 1)
    def _():
        acc = acc_ref[...]
        if mode == "plain":
            rest[0][...] = acc.astype(rest[0].dtype)
        elif mode == "res":
            rest[1][...] = (rest[0][...] + acc).astype(rest[1].dtype)
        elif mode == "relu2":
            rest[0][...] = acc
            r = jnp.maximum(acc, 0.0)
            rest[1][...] = (r * r).astype(rest[1].dtype)
        elif mode == "dact":
            rest[1][...] = (acc * (2.0 * jnp.maximum(rest[0][...], 0.0))).astype(rest[1].dtype)


def _w_dims(w, split):
    J, _, Kd, Nd = w.shape
    return (J, Kd, J * Nd, Kd, Nd) if split == "col" else (J, J * Kd, Nd, Kd, Nd)


def _w_spec(split, layer, tk, tn, Kd, Nd, kn_of):
    kps, nps = Kd // tk, Nd // tn

    def index(*g):
        kb, nb = kn_of(*g)
        if split == "col":
            return nb // nps, layer, kb, nb % nps
        return kb // kps, layer, kb % kps, nb

    return pl.BlockSpec((None, None, tk, tn), index)


def _mm_nn(a, w, layer, split, *, name, out_dtype=F32, mode="plain", res=None):
    M = a.shape[0]
    J, K, N, Kd, Nd = _w_dims(w, split)
    tm, tn, tk = _tile(M, 1024), _tile(Nd, 1024), _tile(Kd, 512)
    in_specs = [pl.BlockSpec((tm, tk), lambda i, j, k: (i, k)),
                _w_spec(split, layer, tk, tn, Kd, Nd, lambda i, j, k: (k, j))]
    args = [a, w]
    o_spec = pl.BlockSpec((tm, tn), lambda i, j, k: (i, j))
    if mode == "res":
        in_specs.append(o_spec)
        args.append(res)
    if mode == "relu2":
        out_shape = (jax.ShapeDtypeStruct((M, N), F32), jax.ShapeDtypeStruct((M, N), BF16))
        out_specs = (o_spec, o_spec)
    else:
        out_shape = jax.ShapeDtypeStruct((M, N), out_dtype)
        out_specs = o_spec
    return _pcall(_mm_body, name=name, grid=(M // tm, N // tn, K // tk), in_specs=in_specs, out_specs=out_specs,
                  out_shape=out_shape, scratch=[pltpu.VMEM((tm, tn), F32)],
                  sem=("parallel", "parallel", "arbitrary"), nk=K // tk, dims=NN, mode=mode)(*args)


def _mm_nt(a, w, layer, split, *, name, out_dtype=F32, mode="plain", act=None):
    M = a.shape[0]
    J, K, N, Kd, Nd = _w_dims(w, split)
    tm, tko, tc = _tile(M, 1024), _tile(Kd, 1024), _tile(Nd, 512)
    in_specs = [pl.BlockSpec((tm, tc), lambda i, j, c: (i, c)),
                _w_spec(split, layer, tko, tc, Kd, Nd, lambda i, j, c: (j, c))]
    args = [a, w]
    o_spec = pl.BlockSpec((tm, tko), lambda i, j, c: (i, j))
    if mode == "dact":
        in_specs.append(o_spec)
        args.append(act)
    return _pcall(_mm_body, name=name, grid=(M // tm, K // tko, N // tc), in_specs=in_specs, out_specs=o_spec,
                  out_shape=jax.ShapeDtypeStruct((M, K), out_dtype), scratch=[pltpu.VMEM((tm, tko), F32)],
                  sem=("parallel", "parallel", "arbitrary"), nk=N // tc, dims=NT, mode=mode)(*args)


def _mm_tn(a, b, J, split, *, name):
    M, K = a.shape
    N = b.shape[1]
    Kd, Nd = (K, N // J) if split == "col" else (K // J, N)
    tk, tn, tc = _tile(Kd, 1024), _tile(Nd, 1024), _tile(M, 512)
    kps, nps = Kd // tk, Nd // tn
    in_specs = [pl.BlockSpec((tc, tk), lambda i, j, c: (c, i)),
                pl.BlockSpec((tc, tn), lambda i, j, c: (c, j))]
    if split == "col":
        o_spec = pl.BlockSpec((None, tk, tn), lambda i, j, c: (j // nps, i, j % nps))
    else:
        o_spec = pl.BlockSpec((None, tk, tn), lambda i, j, c: (i // kps, i % kps, j))
    return _pcall(_mm_body, name=name, grid=(K // tk, N // tn, M // tc), in_specs=in_specs, out_specs=o_spec,
                  out_shape=jax.ShapeDtypeStruct((J, Kd, Nd), F32), scratch=[pltpu.VMEM((tk, tn), F32)],
                  sem=("parallel", "parallel", "arbitrary"), nk=M // tc, dims=TN, mode="plain")(a, b)


def _bmm_body(*refs, n, dims):
    out_ref = refs[2 * n]
    acc = None
    for i in range(n):
        p = lax.dot_general(refs[i][...].astype(BF16), refs[n + i][...].astype(BF16), dims,
                            preferred_element_type=F32)
        acc = p if acc is None else acc + p
    out_ref[...] = acc.astype(out_ref.dtype)


def _bmm_nn(a_list, b_list, *, name):
    M = a_list[0].shape[0]
    NB, Ka, No = b_list[0].shape
    tm = _tile(M, 512)
    n = len(a_list)
    in_specs = ([pl.BlockSpec((tm, Ka), lambda i, j: (i, j))] * n
                + [pl.BlockSpec((None, Ka, No), lambda i, j: (j, 0, 0))] * n)
    return _pcall(_bmm_body, name=name, grid=(M // tm, NB), in_specs=in_specs,
                  out_specs=pl.BlockSpec((tm, No), lambda i, j: (i, j)),
                  out_shape=jax.ShapeDtypeStruct((M, NB * No), F32), sem=("parallel", "parallel"),
                  n=n, dims=NN)(*a_list, *b_list)


def _bmm_nt(a_list, b_list, *, name):
    M = a_list[0].shape[0]
    NB, Ka, No = b_list[0].shape
    tm = _tile(M, 512)
    n = len(a_list)
    in_specs = ([pl.BlockSpec((tm, No), lambda i, j: (i, j))] * n
                + [pl.BlockSpec((None, Ka, No), lambda i, j: (j, 0, 0))] * n)
    return _pcall(_bmm_body, name=name, grid=(M // tm, NB), in_specs=in_specs,
                  out_specs=pl.BlockSpec((tm, Ka), lambda i, j: (i, j)),
                  out_shape=jax.ShapeDtypeStruct((M, NB * Ka), F32), sem=("parallel", "parallel"),
                  n=n, dims=NT)(*a_list, *b_list)


def _bmm_tn_body(a_ref, c_ref, out_ref):
    @pl.when(pl.program_id(1) == 0)
    def _():
        out_ref[...] = jnp.zeros_like(out_ref)

    out_ref[...] += lax.dot_general(a_ref[...].astype(BF16), c_ref[...].astype(BF16), TN,
                                    preferred_element_type=F32)


def _bmm_tn(a, c, NB, *, name):
    M = a.shape[0]
    Ka, No = a.shape[1] // NB, c.shape[1] // NB
    tm = _tile(M, 512)
    return _pcall(_bmm_tn_body, name=name, grid=(NB, M // tm),
                  in_specs=[pl.BlockSpec((tm, Ka), lambda j, m: (m, j)), pl.BlockSpec((tm, No), lambda j, m: (m, j))],
                  out_specs=pl.BlockSpec((None, Ka, No), lambda j, m: (j, 0, 0)),
                  out_shape=jax.ShapeDtypeStruct((NB, Ka, No), F32), sem=("parallel", "arbitrary"))(a, c)


def _ew_body(*refs, fn, n_in, n_out, n_acc):
    res = fn(*[r[...] for r in refs[:n_in]])
    if not isinstance(res, (tuple, list)):
        res = (res,)
    outs = refs[n_in:n_in + n_out]
    accs = refs[n_in + n_out:]
    for o, r in zip(outs, res[:n_out]):
        o[...] = r.astype(o.dtype)
    if n_acc:
        first = pl.program_id(1) == 0
        for a, r in zip(accs, res[n_out:]):
            @pl.when(first)
            def _(a=a):
                a[...] = jnp.zeros_like(a)

            a[...] += r


def _ew(fn, rows, vecs=(), *, out_dtypes=(), n_acc=0, width=None, ncol=1, tr=256, name):
    rows = [r if isinstance(r, tuple) else (r, 0) for r in rows]
    R = rows[0][0].shape[0]
    C = width if width is not None else rows[0][0].shape[1]
    tr = _tile(R, tr)
    in_specs = [pl.BlockSpec((tr, C), functools.partial(lambda j, i, off: (i, off + j), off=off)) for _, off in rows]
    in_specs += [pl.BlockSpec((1, C), lambda j, i: (0, j)) for _ in vecs]
    out_shape = [jax.ShapeDtypeStruct((R, ncol * C), dt) for dt in out_dtypes]
    out_specs = [pl.BlockSpec((tr, C), lambda j, i: (i, j)) for _ in out_dtypes]
    out_shape += [jax.ShapeDtypeStruct((1, ncol * C), F32)] * n_acc
    out_specs += [pl.BlockSpec((1, C), lambda j, i: (0, j))] * n_acc
    res = _pcall(_ew_body, name=name, grid=(ncol, R // tr), in_specs=in_specs, out_specs=tuple(out_specs),
                 out_shape=tuple(out_shape), sem=("parallel", "arbitrary" if n_acc else "parallel"),
                 fn=fn, n_in=len(rows) + len(vecs), n_out=len(out_dtypes), n_acc=n_acc)(
        *[a for a, _ in rows], *vecs)
    return res


def _rms_fwd(x, g, *, name):
    def fn(x, g):
        r = lax.rsqrt(jnp.mean(x * x, axis=1, keepdims=True) + RMS_EPS)
        return (x * r) * g

    return _ew(fn, [x], [g], out_dtypes=[BF16], name=name)[0]


def _rms_bwd(dh, x, g, dres, *, name):
    def fn(dh, x, dres, g):
        r = lax.rsqrt(jnp.mean(x * x, axis=1, keepdims=True) + RMS_EPS)
        y = x * r
        dy = dh * g
        dx = r * (dy - y * jnp.mean(dy * y, axis=1, keepdims=True))
        return dres + dx, jnp.sum(dh * y, axis=0, keepdims=True)

    return _ew(fn, [dh, x, dres], [g], out_dtypes=[F32], n_acc=1, name=name)


def _loss_and_grad(x, g, target, *, name):
    D = x.shape[1]

    def fn(x, t, g):
        r = lax.rsqrt(jnp.mean(x * x, axis=1, keepdims=True) + RMS_EPS)
        y = x * r
        diff = y * g - t
        dh = diff * (1.0 / D)
        dy = dh * g
        dx = r * (dy - y * jnp.mean(dy * y, axis=1, keepdims=True))
        loss = jnp.sum(jnp.sum(diff * diff, axis=1, keepdims=True), axis=0, keepdims=True) * (0.5 / D)
        return dx, jnp.sum(dh * y, axis=0, keepdims=True), jnp.broadcast_to(loss, (1, D))

    return _ew(fn, [x, target], [g], out_dtypes=[F32], n_acc=2, name=name)


def _gelu(y):
    c = math.sqrt(2.0 / math.pi)
    return 0.5 * y * (1.0 + jnp.tanh(c * (y + 0.044715 * (y * y * y))))


def _gelu_grad(y):
    c = math.sqrt(2.0 / math.pi)
    t = jnp.tanh(c * (y + 0.044715 * (y * y * y)))
    return 0.5 * (1.0 + t) + 0.5 * y * (1.0 - t * t) * (c * (1.0 + 3 * 0.044715 * (y * y)))


def _adamw(w, g, m, v, *, name):
    def fn(w, g, m, v):
        m2 = ADAM_B1 * m + (1.0 - ADAM_B1) * g
        v2 = ADAM_B2 * v + (1.0 - ADAM_B2) * (g * g)
        m_hat = m2 / (1.0 - ADAM_B1 ** ADAM_STEP)
        v_hat = v2 / (1.0 - ADAM_B2 ** ADAM_STEP)
        delta = -ADAM_LR * (m_hat / (jnp.sqrt(v_hat) + ADAM_EPS) + ADAM_WD * w)
        return delta, m2, v2

    return _ew(fn, [w, g, m, v], out_dtypes=[F32, F32, F32], name=name)


def _window(kind, blk, QB, rows):
    if kind == "dil":
        return pl.multiple_of(blk * QB, QB), 0
    kr = min(NA_ROWS_MAX, rows)
    rs = jnp.clip(blk - kr // 2, 0, rows - kr)
    return pl.multiple_of(rs * GRID_W, GRID_W), blk - rs


def _scores(q, kw, bias, kind, start, QB, W, L_valid, scale):
    s = lax.dot_general(q, kw, NT, preferred_element_type=F32) * scale + bias
    if kind == "dil":
        kp = start + lax.broadcasted_iota(jnp.int32, (QB, W), 1)
        s = jnp.where((kp >= DIL_HALF) & (kp < DIL_HALF + L_valid), s, NEG_INF)
    return s


def _wattn_fwd_body(q_ref, k_ref, v_ref, b_ref, *outs, QB, SUB, W, kind, L_valid, rows, scale):
    n = pl.program_id(1)
    for i in range(SUB):
        start, pat = _window(kind, n * SUB + i, QB, rows)
        sl = slice(i * QB, (i + 1) * QB)
        q = q_ref[sl, :]
        kw = k_ref[pl.ds(start, W), :]
        vw = v_ref[pl.ds(start, W), :]
        s = _scores(q, kw, b_ref[pat], kind, start, QB, W, L_valid, scale)
        m = jnp.max(s, axis=1, keepdims=True)
        e = jnp.exp(s - m)
        l = jnp.sum(e, axis=1, keepdims=True)
        if kind == "na":
            p = (e * (1.0 / l)).astype(BF16)
            o = lax.dot_general(p, vw, NN, preferred_element_type=F32)
            outs[0][sl, :] = o.astype(outs[0].dtype)
            outs[1][sl, :] = jnp.broadcast_to(m + jnp.log(l), (QB, LANES))
        else:
            outs[0][sl, :] = lax.dot_general(e.astype(BF16), vw, NN, preferred_element_type=F32)
            outs[1][sl, :] = jnp.broadcast_to(m, (QB, LANES))
            outs[2][sl, :] = jnp.broadcast_to(l, (QB, LANES))


def _wattn_bwd_body(q_ref, do_ref, lse_ref, dl_ref, k_ref, v_ref, b_ref, dq_ref, dk_ref, dv_ref, db_ref, *,
                    QB, SUB, W, kind, L_valid, rows, scale, dgroup):
    cb = pl.program_id(0)
    n = pl.program_id(1)

    @pl.when(n == 0)
    def _():
        dk_ref[...] = jnp.zeros_like(dk_ref)
        dv_ref[...] = jnp.zeros_like(dv_ref)

    @pl.when((n == 0) & (cb % dgroup == 0))
    def _():
        db_ref[...] = jnp.zeros_like(db_ref)

    for i in range(SUB):
        start, pat = _window(kind, n * SUB + i, QB, rows)
        sl = slice(i * QB, (i + 1) * QB)
        q = q_ref[sl, :]
        do = do_ref[sl, :]
        kw = k_ref[pl.ds(start, W), :]
        vw = v_ref[pl.ds(start, W), :]
        s = _scores(q, kw, b_ref[pat], kind, start, QB, W, L_valid, scale)
        p = jnp.exp(s - lse_ref[sl, :][:, :1])
        dp = lax.dot_general(do, vw, NT, preferred_element_type=F32)
        ds = p * (dp - dl_ref[sl, :][:, :1])
        db_ref[pat] += ds
        dsb = ds.astype(BF16)
        dq_ref[sl, :] = (lax.dot_general(dsb, kw, NN, preferred_element_type=F32) * scale).astype(dq_ref.dtype)
        dk_ref[pl.ds(start, W), :] += lax.dot_general(dsb, q, TN, preferred_element_type=F32) * scale
        dv_ref[pl.ds(start, W), :] += lax.dot_general(p.astype(BF16), do, TN, preferred_element_type=F32)


def _wattn_geometry(kind, LQ, dil):
    if kind == "dil":
        QB, W, rows = A_QBLOCK, A_QBLOCK + 2 * DIL_HALF, 0
    else:
        rows = LQ // GRID_W
        QB, W = GRID_W, min(NA_ROWS_MAX, rows) * GRID_W
    SUB = 4 if (LQ // QB) % 4 == 0 else 1
    return QB, W, rows, SUB


def _wattn_fwd(q, k, v, bias, *, kind, H, dil, qoff, koff, voff, name):
    LQ, LK = q.shape[0], k.shape[0]
    QB, W, rows, SUB = _wattn_geometry(kind, LQ, dil)
    ncb = H * dil
    NP = bias.shape[1]

    def col(cb, off):
        return off + (cb % dil) * H + cb // dil

    in_specs = [pl.BlockSpec((QB * SUB, LANES), lambda cb, n: (n, col(cb, qoff))),
                pl.BlockSpec((LK, LANES), lambda cb, n: (0, col(cb, koff))),
                pl.BlockSpec((LK, LANES), lambda cb, n: (0, col(cb, voff))),
                pl.BlockSpec((None, NP, QB, W), lambda cb, n: (cb // dil, 0, 0, 0))]
    o_spec = pl.BlockSpec((QB * SUB, LANES), lambda cb, n: (n, col(cb, 0)))
    shape = (LQ, ncb * LANES)
    if kind == "na":
        out_shape = (jax.ShapeDtypeStruct(shape, BF16), jax.ShapeDtypeStruct(shape, F32))
    else:
        out_shape = (jax.ShapeDtypeStruct(shape, F32),) * 3
    return _pcall(_wattn_fwd_body, name=name, grid=(ncb, LQ // (QB * SUB)), in_specs=in_specs,
                  out_specs=(o_spec,) * len(out_shape), out_shape=out_shape, sem=("parallel", "parallel"),
                  QB=QB, SUB=SUB, W=W, kind=kind, L_valid=LQ, rows=rows, scale=1.0 / math.sqrt(HEAD_DIM))(
        q, k, v, bias)


def _wattn_bwd(q, do, lse, delta, k, v, bias, *, kind, H, dil, qoff, koff, voff, dq_dtype, name):
    LQ, LK = q.shape[0], k.shape[0]
    QB, W, rows, SUB = _wattn_geometry(kind, LQ, dil)
    ncb = H * dil
    NP = bias.shape[1]

    def col(cb, off):
        return off + (cb % dil) * H + cb // dil

    q_spec = lambda off: pl.BlockSpec((QB * SUB, LANES), lambda cb, n: (n, col(cb, off)))
    kv_spec = lambda off: pl.BlockSpec((LK, LANES), lambda cb, n: (0, col(cb, off)))
    b_spec = pl.BlockSpec((None, NP, QB, W), lambda cb, n: (cb // dil, 0, 0, 0))
    in_specs = [q_spec(qoff), q_spec(0), q_spec(0), q_spec(0), kv_spec(koff), kv_spec(voff), b_spec]
    out_shape = (jax.ShapeDtypeStruct((LQ, ncb * LANES), dq_dtype), jax.ShapeDtypeStruct((LK, ncb * LANES), F32),
                 jax.ShapeDtypeStruct((LK, ncb * LANES), F32), jax.ShapeDtypeStruct(bias.shape, F32))
    out_specs = (q_spec(0), kv_spec(0), kv_spec(0), b_spec)
    return _pcall(_wattn_bwd_body, name=name, grid=(ncb, LQ // (QB * SUB)), in_specs=in_specs, out_specs=out_specs,
                  out_shape=out_shape, sem=("arbitrary", "arbitrary"),
                  QB=QB, SUB=SUB, W=W, kind=kind, L_valid=LQ, rows=rows, scale=1.0 / math.sqrt(HEAD_DIM),
                  dgroup=dil)(q, do, lse, delta, k, v, bias)


def _attn_delta(do, do_off, o, *, name):
    def fn(do, o):
        return jnp.broadcast_to(jnp.sum(do * o.astype(F32), axis=1, keepdims=True), do.shape), do

    return _ew(fn, [(do, do_off), o], out_dtypes=[F32, BF16], width=LANES, ncol=o.shape[1] // LANES, tr=512,
               name=name)


def _scan_fwd_body(bre_ref, bim_ref, are_ref, aim_ref, xre_ref, xim_ref, cr_ref, ci_ref, *, TC, reverse):
    @pl.when(pl.program_id(1) == 0)
    def _():
        cr_ref[...] = jnp.zeros_like(cr_ref)
        ci_ref[...] = jnp.zeros_like(ci_ref)

    ar = are_ref[...]
    ai = aim_ref[...]

    def step(s, carry):
        xr, xi = carry
        tau = TC - 1 - s if reverse else s
        nxr = ar * xr - ai * xi + bre_ref[pl.ds(tau, 1), :]
        nxi = ar * xi + ai * xr + bim_ref[pl.ds(tau, 1), :]
        xre_ref[pl.ds(tau, 1), :] = nxr
        xim_ref[pl.ds(tau, 1), :] = nxi
        return nxr, nxi

    xr, xi = lax.fori_loop(0, TC, step, (cr_ref[0:1, :], ci_ref[0:1, :]), unroll=8)
    cr_ref[0:1, :] = xr
    ci_ref[0:1, :] = xi


def _scan_geometry(S, NCH):
    return _tile(S, 512), _tile(NCH, 512)


def _scan_fwd(bu_re, bu_im, a_re, a_im, *, reverse, name):
    S, NCH = bu_re.shape
    TC, LB = _scan_geometry(S, NCH)
    nT = S // TC
    tmap = (lambda l, t: (nT - 1 - t, l)) if reverse else (lambda l, t: (t, l))
    row = pl.BlockSpec((TC, LB), tmap)
    vec = pl.BlockSpec((1, LB), lambda l, t: (0, l))
    return _pcall(_scan_fwd_body, name=name, grid=(NCH // LB, nT), in_specs=[row, row, vec, vec],
                  out_specs=(row, row), out_shape=(jax.ShapeDtypeStruct((S, NCH), F32),) * 2,
                  scratch=[pltpu.VMEM((8, LB), F32), pltpu.VMEM((8, LB), F32)], sem=("parallel", "arbitrary"),
                  TC=TC, reverse=reverse)(bu_re, bu_im, a_re, a_im)


def _scan_bwd_body(gre_ref, gim_ref, xre_ref, xim_ref, are_ref, aim_ref, hre_ref, him_ref, dar_ref, dai_ref,
                   cr_ref, ci_ref, *, TC, reverse):
    @pl.when(pl.program_id(1) == 0)
    def _():
        cr_ref[...] = jnp.zeros_like(cr_ref)
        ci_ref[...] = jnp.zeros_like(ci_ref)
        dar_ref[...] = jnp.zeros_like(dar_ref)
        dai_ref[...] = jnp.zeros_like(dai_ref)

    ar = are_ref[...]
    ai = aim_ref[...]

    def step(s, carry):
        hr, hi, sr, si = carry
        tau = TC - 1 - s if reverse else s
        xr = xre_ref[pl.ds(tau, 1), :]
        xi = xim_ref[pl.ds(tau, 1), :]
        sr = sr + (hr * xr + hi * xi)
        si = si + (hi * xr - hr * xi)
        nhr = gre_ref[pl.ds(tau, 1), :] + (ar * hr + ai * hi)
        nhi = gim_ref[pl.ds(tau, 1), :] + (ar * hi - ai * hr)
        hre_ref[pl.ds(tau, 1), :] = nhr
        him_ref[pl.ds(tau, 1), :] = nhi
        return nhr, nhi, sr, si

    z = jnp.zeros_like(ar)
    hr, hi, sr, si = lax.fori_loop(0, TC, step, (cr_ref[0:1, :], ci_ref[0:1, :], z, z), unroll=8)
    cr_ref[0:1, :] = hr
    ci_ref[0:1, :] = hi
    dar_ref[...] += sr
    dai_ref[...] += si


def _scan_bwd(g_re, g_im, x_re, x_im, a_re, a_im, *, reverse, name):
    S, NCH = g_re.shape
    TC, LB = _scan_geometry(S, NCH)
    nT = S // TC
    back = not reverse
    tmap = (lambda l, t: (nT - 1 - t, l)) if back else (lambda l, t: (t, l))
    row = pl.BlockSpec((TC, LB), tmap)
    vec = pl.BlockSpec((1, LB), lambda l, t: (0, l))
    return _pcall(_scan_bwd_body, name=name, grid=(NCH // LB, nT), in_specs=[row, row, row, row, vec, vec],
                  out_specs=(row, row, vec, vec),
                  out_shape=(jax.ShapeDtypeStruct((S, NCH), F32),) * 2 + (jax.ShapeDtypeStruct((1, NCH), F32),) * 2,
                  scratch=[pltpu.VMEM((8, LB), F32), pltpu.VMEM((8, LB), F32)], sem=("parallel", "arbitrary"),
                  TC=TC, reverse=back)(g_re, g_im, x_re, x_im, a_re, a_im)


def _t5_bucket(rel):
    half = T5_BUCKETS // 2
    max_exact = half // 2
    n = jnp.abs(rel)
    nf = jnp.maximum(n, 1).astype(F32)
    large = max_exact + (jnp.log(nf / max_exact) / math.log(T5_MAX_DISTANCE / max_exact)
                         * (half - max_exact)).astype(jnp.int32)
    large = jnp.minimum(large, half - 1)
    return jnp.where(rel > 0, half, 0) + jnp.where(n < max_exact, n, large)


def _dil_bias(t5_bias, dil):
    W = A_QBLOCK + 2 * DIL_HALF
    off = jnp.arange(W)[None, :] - DIL_HALF - jnp.arange(A_QBLOCK)[:, None]
    b = jnp.transpose(t5_bias[_t5_bucket(off * dil)], (2, 0, 1)).astype(F32)
    return jnp.where(jnp.abs(off) <= DIL_HALF, b, NEG_INF)[:, None]


def _na_bias(rpb, rows):
    kr = min(NA_ROWS_MAX, rows)
    ro = (jnp.arange(kr)[None, :] - jnp.arange(kr)[:, None]) + NA_ROWS_MAX - 1
    c = jnp.arange(GRID_W)
    col_start = jnp.clip(c - NA_COLS // 2, 0, GRID_W - NA_COLS)
    col_ok = (c[None, :] >= col_start[:, None]) & (c[None, :] < col_start[:, None] + NA_COLS)
    co = jnp.clip(c[None, :] - c[:, None] + NA_COLS - 1, 0, 2 * NA_COLS - 2)
    b = rpb[:, ro[:, None, :, None], co[None, :, None, :]].astype(F32)
    b = jnp.where(col_ok[None, None, :, None, :], b, NEG_INF)
    return b.reshape(rpb.shape[0], kr, GRID_W, kr * GRID_W)


def _s5_mats(lam_re, lam_im, log_step, b_re, b_im, c_re, c_im):
    G, P, C = b_re.shape
    NB = G // GROUPS_PER_BLOCK
    eye = jnp.eye(GROUPS_PER_BLOCK, dtype=F32)

    def bd_in(bb):
        t = bb.reshape(NB, GROUPS_PER_BLOCK, P, C).transpose(0, 1, 3, 2)
        return jnp.einsum('jgcp,gh->jgchp', t, eye).reshape(NB, GROUPS_PER_BLOCK * C, GROUPS_PER_BLOCK * P)

    def bd_out(cc):
        t = cc.reshape(NB, GROUPS_PER_BLOCK, C, P).transpose(0, 1, 3, 2)
        return jnp.einsum('jgpc,gh->jgphc', t, eye).reshape(NB, GROUPS_PER_BLOCK * P, GROUPS_PER_BLOCK * C)

    out = []
    for d in range(2):
        step = jnp.exp(log_step[d].astype(F32))[:, None]
        lr = jnp.minimum(lam_re[d].astype(F32), -1e-4)
        li = lam_im[d].astype(F32)
        mag = jnp.exp(lr * step)
        ab_re = mag * jnp.cos(li * step)
        ab_im = mag * jnp.sin(li * step)
        den = lr * lr + li * li
        zr = ((ab_re - 1.0) * lr + ab_im * li) / den
        zi = (ab_im * lr - (ab_re - 1.0) * li) / den
        bb_re = zr[..., None] * b_re - zi[..., None] * b_im
        bb_im = zr[..., None] * b_im + zi[..., None] * b_re
        out.append((ab_re.reshape(1, G * P), ab_im.reshape(1, G * P), bd_in(bb_re), bd_in(bb_im),
                    bd_out(c_re[d].astype(F32)), bd_out(-c_im[d].astype(F32))))
    return tuple(out)


def _sigmoid(z):
    return 1.0 / (1.0 + jnp.exp(-z))


def _strided(t, dil, pad):
    S, C = t.shape
    t = t.reshape(S // dil, dil * C)
    return jnp.pad(t, ((DIL_HALF, DIL_HALF), (0, 0))) if pad else t


def _dilated_fwd(q, k, v, t5_bias):
    S, AW = q.shape
    H = AW // HEAD_DIM
    parts = []
    for _, dil in DILATED_BRANCHES:
        num, m, l = _wattn_fwd(_strided(q, dil, False), _strided(k, dil, True), _strided(v, dil, True),
                               _dil_bias(t5_bias, dil), kind="dil", H=H, dil=dil, qoff=0, koff=0, voff=0,
                               name=f"dilated{dil}_fwd")
        parts += [num.reshape(S, AW), m.reshape(S, AW), l.reshape(S, AW)]

    def merge(n1, m1, l1, n2, m2, l2, n3, m3, l3):
        mx = jnp.maximum(jnp.maximum(m1, m2), m3)
        w1, w2, w3 = jnp.exp(m1 - mx), jnp.exp(m2 - mx), jnp.exp(m3 - mx)
        den = w1 * l1 + w2 * l2 + w3 * l3
        o = (w1 * n1 + w2 * n2 + w3 * n3) / den
        return o, o, mx + jnp.log(den)

    return _ew(merge, parts, out_dtypes=[F32, BF16, F32], name="dilated_merge")


def _dilated_bwd(q, k, v, t5_bias, do, lse, delta):
    S, AW = q.shape
    H = AW // HEAD_DIM
    dqs, dks, dvs = [], [], []
    dt5 = jnp.zeros(t5_bias.shape, F32)
    for _, dil in DILATED_BRANCHES:
        bias, bias_vjp = jax.vjp(functools.partial(_dil_bias, dil=dil), t5_bias)
        dq, dk, dv, db = _wattn_bwd(_strided(q, dil, False), _strided(do, dil, False), _strided(lse, dil, False),
                                    _strided(delta, dil, False), _strided(k, dil, True), _strided(v, dil, True),
                                    bias, kind="dil", H=H, dil=dil, qoff=0, koff=0, voff=0, dq_dtype=F32,
                                    name=f"dilated{dil}_bwd")
        dqs.append(dq.reshape(S, AW))
        dks.append(dk[DIL_HALF:-DIL_HALF].reshape(S, AW))
        dvs.append(dv[DIL_HALF:-DIL_HALF].reshape(S, AW))
        dt5 = dt5 + bias_vjp(db)[0]
    add3 = lambda a, b, c: a + b + c
    return (_ew(add3, dqs, out_dtypes=[BF16], name="dilated_dq_sum")[0],
            _ew(add3, dks, out_dtypes=[BF16], name="dilated_dk_sum")[0],
            _ew(add3, dvs, out_dtypes=[BF16], name="dilated_dv_sum")[0], dt5)


def _s5_fwd(u, mats, d_skip, w_glu, j):
    xs = []
    for d in range(2):
        a_re, a_im, b_r, b_i, _, _ = mats[d]
        bu_re = _bmm_nn([u], [b_r], name="s5_bu")
        bu_im = _bmm_nn([u], [b_i], name="s5_bu")
        xs += list(_scan_fwd(bu_re, bu_im, a_re, a_im, reverse=(d == 1), name=f"s5_scan_fwd{d}"))
    ylin = _bmm_nn(xs, [mats[0][4], mats[0][5], mats[1][4], mats[1][5]], name="s5_cx")

    def act(yl, u, dsk):
        y = yl + dsk * u
        return y, _gelu(y)

    y, yg = _ew(act, [ylin, u], [d_skip], out_dtypes=[F32, F32], name="s5_gelu")
    z = _mm_nn(yg, w_glu, j, "row", name="s5_glu_fwd")
    ob = _ew(lambda yg, z: yg * _sigmoid(z), [yg, z], out_dtypes=[BF16], name="s5_gate")[0]
    return ob, (xs, y, yg, z)


def _s5_bwd(dmerged, ob_off, u, mats, d_skip, w_glu, j, saved):
    xs, y, yg, z = saved
    BW = u.shape[1]
    NB = BW // LANES

    def gate_bwd(dob, yg, z):
        sg = _sigmoid(z)
        return dob * yg * (sg * (1.0 - sg)), dob * sg

    dz, dyg1 = _ew(gate_bwd, [(dmerged, ob_off), yg, z], out_dtypes=[BF16, F32], width=LANES, ncol=NB,
                   name="s5_gate_bwd")
    dw_glu = _mm_tn(yg, dz, w_glu.shape[0], "row", name="s5_glu_dw")
    dyg2 = _mm_nt(dz, w_glu, j, "row", name="s5_glu_dx")

    def act_bwd(d1, d2, y, u):
        dy = (d1 + d2) * _gelu_grad(y)
        return dy, jnp.sum(dy * u, axis=0, keepdims=True)

    dy, dd = _ew(act_bwd, [dyg1, dyg2, y, u], out_dtypes=[F32], n_acc=1, name="s5_gelu_bwd")
    dmats, hs = [], []
    for d in range(2):
        a_re, a_im, b_r, b_i, c_r, c_in = mats[d]
        x_re, x_im = xs[2 * d], xs[2 * d + 1]
        g_re = _bmm_nt([dy], [c_r], name="s5_dcx")
        g_im = _bmm_nt([dy], [c_in], name="s5_dcx")
        dc_r = _bmm_tn(x_re, dy, NB, name="s5_dc")
        dc_in = _bmm_tn(x_im, dy, NB, name="s5_dc")
        h_re, h_im, da_re, da_im = _scan_bwd(g_re, g_im, x_re, x_im, a_re, a_im, reverse=(d == 1),
                                             name=f"s5_scan_bwd{d}")
        db_r = _bmm_tn(u, h_re, NB, name="s5_db")
        db_i = _bmm_tn(u, h_im, NB, name="s5_db")
        hs += [h_re, h_im]
        dmats.append((da_re, da_im, db_r, db_i, dc_r, dc_in))
    du_b = _bmm_nt(hs, [mats[0][2], mats[0][3], mats[1][2], mats[1][3]], name="s5_du")
    du = _ew(lambda dy, dub, dsk: dy * dsk + dub, [dy, du_b], [d_skip], out_dtypes=[BF16], name="s5_du_sum")[0]
    return du, tuple(dmats), dd, dw_glu


def _ab_fwd(x, j, P, W):
    t5 = P["t5_bias"]
    AW = t5.shape[1] * HEAD_DIM
    hn = _rms_fwd(x, P["norm_mix"][2 * j][None], name="rms_fwd")
    proj = _mm_nn(hn, W["ab_w_in"], j, "col", name="ab_in_fwd")
    q, k, v = (proj[:, i * AW:(i + 1) * AW].astype(BF16) for i in range(3))
    u = proj[:, 3 * AW:]
    oa32, oa16, lse = _dilated_fwd(q, k, v, t5)
    mats = _s5_mats(*(P[n][j] for n in _S5_PARAMS))
    ob, s5_saved = _s5_fwd(u, mats, P["s5_d"][j][None], W["s5_w_glu"], j)
    merged = jnp.concatenate([oa16, ob], axis=1)
    x1 = _mm_nn(merged, W["ab_w_out"], j, "row", mode="res", res=x, name="ab_out_fwd")
    return x1, (x, hn, q, k, v, u, oa32, lse, merged, s5_saved)


def _ab_bwd(dx1, j, P, W, saved):
    x, hn, q, k, v, u, oa32, lse, merged, s5_saved = saved
    t5 = P["t5_bias"]
    AW = t5.shape[1] * HEAD_DIM
    J = W["ab_w_in"].shape[0]
    dmerged = _mm_nt(dx1, W["ab_w_out"], j, "row", name="ab_out_dx")
    dw_out = _mm_tn(merged, dx1, J, "row", name="ab_out_dw")
    delta, do16 = _attn_delta(dmerged, 0, oa32, name="dilated_delta")
    dq, dk, dv, dt5 = _dilated_bwd(q, k, v, t5, do16, lse, delta)
    s5_params = tuple(P[n][j] for n in _S5_PARAMS)
    mats, mats_vjp = jax.vjp(_s5_mats, *s5_params)
    du, dmats, dd, dw_glu = _s5_bwd(dmerged, AW // LANES, u, mats, P["s5_d"][j][None], W["s5_w_glu"], j, s5_saved)
    ds5 = mats_vjp(dmats)
    dproj = jnp.concatenate([dq, dk, dv, du], axis=1)
    dw_in = _mm_tn(hn, dproj, J, "col", name="ab_in_dw")
    dhn = _mm_nt(dproj, W["ab_w_in"], j, "col", name="ab_in_dx")
    dx, dg = _rms_bwd(dhn, x, P["norm_mix"][2 * j][None], dx1, name="rms_bwd")
    small = dict(zip(_S5_PARAMS, ds5), s5_d=dd[0], t5_bias=dt5)
    return dx, dg[0], dict(ab_w_in=dw_in, ab_w_out=dw_out, s5_w_glu=dw_glu), small


def _c_fwd(x, j, P, W):
    H = P["c_rpb"].shape[1]
    hn = _rms_fwd(x, P["norm_mix"][2 * j + 1][None], name="rms_fwd")
    qkv = _mm_nn(hn, W["c_w_qkv"], j, "col", out_dtype=BF16, name="c_qkv_fwd")
    bias = _na_bias(P["c_rpb"][j], x.shape[0] // GRID_W)
    o, lse = _wattn_fwd(qkv, qkv, qkv, bias, kind="na", H=H, dil=1, qoff=0, koff=H, voff=2 * H, name="na_fwd")
    x1 = _mm_nn(o, W["c_w_out"], j, "row", mode="res", res=x, name="c_out_fwd")
    return x1, (x, hn, qkv, o, lse)


def _c_bwd(dx1, j, P, W, saved):
    x, hn, qkv, o, lse = saved
    H = P["c_rpb"].shape[1]
    J = W["c_w_qkv"].shape[0]
    do = _mm_nt(dx1, W["c_w_out"], j, "row", name="c_out_dx")
    dw_out = _mm_tn(o, dx1, J, "row", name="c_out_dw")
    delta, do16 = _attn_delta(do, 0, o, name="na_delta")
    bias, bias_vjp = jax.vjp(functools.partial(_na_bias, rows=x.shape[0] // GRID_W), P["c_rpb"][j])
    dq, dk, dv, db = _wattn_bwd(qkv, do16, lse, delta, qkv, qkv, bias, kind="na", H=H, dil=1, qoff=0, koff=H,
                                voff=2 * H, dq_dtype=BF16, name="na_bwd")
    dqkv = jnp.concatenate([dq, dk.astype(BF16), dv.astype(BF16)], axis=1)
    dw_qkv = _mm_tn(hn, dqkv, J, "col", name="c_qkv_dw")
    dhn = _mm_nt(dqkv, W["c_w_qkv"], j, "col", name="c_qkv_dx")
    dx, dg = _rms_bwd(dhn, x, P["norm_mix"][2 * j + 1][None], dx1, name="rms_bwd")
    return dx, dg[0], dict(c_w_qkv=dw_qkv, c_w_out=dw_out), dict(c_rpb=bias_vjp(db)[0])


def _mlp_fwd(x, i, P, W):
    hn = _rms_fwd(x, P["norm_mlp"][i][None], name="rms_fwd")
    a, hdn = _mm_nn(hn, W["mlp_w1"], i, "col", mode="relu2", name="mlp_w1_fwd")
    x2 = _mm_nn(hdn, W["mlp_w2"], i, "row", mode="res", res=x, name="mlp_w2_fwd")
    return x2, (x, hn, a, hdn)


def _mlp_bwd(dx2, i, P, W, saved):
    x, hn, a, hdn = saved
    J = W["mlp_w1"].shape[0]
    da = _mm_nt(dx2, W["mlp_w2"], i, "row", out_dtype=BF16, mode="dact", act=a, name="mlp_w2_dx")
    dw2 = _mm_tn(hdn, dx2, J, "row", name="mlp_w2_dw")
    dw1 = _mm_tn(hn, da, J, "col", name="mlp_w1_dw")
    dhn = _mm_nt(da, W["mlp_w1"], i, "col", name="mlp_w1_dx")
    dx, dg = _rms_bwd(dhn, x, P["norm_mlp"][i][None], dx2, name="rms_bwd")
    return dx, dg[0], dict(mlp_w1=dw1, mlp_w2=dw2)


_S5_PARAMS = ("s5_lam_re", "s5_lam_im", "s5_log_step", "s5_b_re", "s5_b_im", "s5_c_re", "s5_c_im")
_BIG = ("ab_w_in", "ab_w_out", "s5_w_glu", "c_w_qkv", "c_w_out", "mlp_w1", "mlp_w2")
_SMALL = ("t5_bias", "s5_lam_re", "s5_lam_im", "s5_log_step", "s5_b_re", "s5_b_im", "s5_c_re", "s5_c_im", "s5_d",
          "c_rpb", "norm_mix", "norm_mlp", "norm_final")
_WEIGHTS = ("t5_bias", "ab_w_in", "ab_w_out", "s5_lam_re", "s5_lam_im", "s5_log_step", "s5_b_re", "s5_b_im",
            "s5_c_re", "s5_c_im", "s5_d", "s5_w_glu", "c_w_qkv", "c_w_out", "c_rpb", "norm_mix", "norm_mlp",
            "mlp_w1", "mlp_w2", "norm_final")


def _local_grads(x, target, P, W):
    depth = P["norm_mix"].shape[0]
    saved = []
    h = x
    for i in range(depth):
        h, s_mix = (_ab_fwd if i % 2 == 0 else _c_fwd)(h, i // 2, P, W)
        h, s_mlp = _mlp_fwd(h, i, P, W)
        saved.append((s_mix, s_mlp))
    dh, dg_final, loss_cols = _loss_and_grad(h, P["norm_final"][None], target, name="loss_head")
    big = {n: [None] * P[n].shape[0] for n in _BIG}
    small = {n: jnp.zeros(P[n].shape, F32) for n in _SMALL}
    small["norm_final"] = dg_final[0]
    for i in reversed(range(depth)):
        s_mix, s_mlp = saved[i]
        j = i // 2
        dh, dg, dbig = _mlp_bwd(dh, i, P, W, s_mlp)
        small["norm_mlp"] = small["norm_mlp"].at[i].set(dg)
        for n, g in dbig.items():
            big[n][i] = g
        dh, dg, dbig, dsmall = (_ab_bwd if i % 2 == 0 else _c_bwd)(dh, j, P, W, s_mix)
        small["norm_mix"] = small["norm_mix"].at[i].set(dg)
        for n, g in dbig.items():
            big[n][j] = g
        for n, g in dsmall.items():
            if n == "t5_bias":
                small[n] = small[n] + g
            else:
                small[n] = small[n].at[j].set(g.reshape(P[n].shape[1:]))
    return loss_cols[0, 0], dh, big, small


def _place():
    x, y, c = lax.axis_index("x"), lax.axis_index("y"), lax.axis_index("c")
    return x, y, c, ((1 - x, y), (x, 1 - y), (1 - x, 1 - y))


def _comm_call(body, arrays, out_shape, sems, *, name, **static):
    hbm = pl.BlockSpec(memory_space=pltpu.HBM)
    return pl.pallas_call(
        functools.partial(body, n=len(arrays), **static), name=name, in_specs=[hbm] * len(arrays),
        out_specs=tuple([hbm] * len(out_shape)), out_shape=tuple(out_shape),
        scratch_shapes=[pltpu.SemaphoreType.DMA((k,)) for k in sems])(*arrays)


def _allgather_body(*refs, n):
    ins, outs = refs[:n], refs[n:2 * n]
    send_sems, recv_sems, local_sems = refs[2 * n:]
    x, y, c, chips = _place()
    me = 2 * x + y

    def remote(t, p, slot):
        return pltpu.make_async_remote_copy(
            src_ref=ins[t], dst_ref=outs[t].at[slot], send_sem=send_sems.at[3 * t + p],
            recv_sem=recv_sems.at[3 * t + p], device_id=(*chips[p], c), device_id_type=MESH)

    local = [pltpu.make_async_copy(ins[t], outs[t].at[me], local_sems.at[t]) for t in range(n)]
    sends = [remote(t, p, me) for t in range(n) for p in range(3)]
    for cp in local + sends:
        cp.start()
    for t in range(n):
        for p, (px, py) in enumerate(chips):
            remote(t, p, 2 * px + py).wait_recv()
    for cp in sends:
        cp.wait_send()
    for cp in local:
        cp.wait()


def _allgather(shards):
    n = len(shards)
    out_shape = [jax.ShapeDtypeStruct((4,) + s.shape, s.dtype) for s in shards]
    return _comm_call(_allgather_body, shards, out_shape, (3 * n, 3 * n, n), name="weights_allgather")


def _pair_exchange_body(*refs, n):
    ins, outs = refs[:n], refs[n:2 * n]
    send_sems, recv_sems = refs[2 * n:]
    x, y, c, _ = _place()
    cps = []
    for t in range(n):
        h = outs[t].shape[0]
        cps.append(pltpu.make_async_remote_copy(
            src_ref=ins[t].at[pl.ds((1 - c) * h, h)], dst_ref=outs[t], send_sem=send_sems.at[t],
            recv_sem=recv_sems.at[t], device_id=(x, y, 1 - c), device_id_type=MESH))
    for cp in cps:
        cp.start()
    for cp in cps:
        cp.wait()


def _chip_exchange_body(*refs, n):
    ins, outs = refs[:n], refs[n:2 * n]
    send_sems, recv_sems = refs[2 * n:]
    x, y, c, chips = _place()
    cps = []
    for t in range(n):
        h = ins[t].shape[0]
        for p, (px, py) in enumerate(chips):
            cps.append(pltpu.make_async_remote_copy(
                src_ref=ins[t].at[pl.ds(0, h), 2 * px + py], dst_ref=outs[t].at[p], send_sem=send_sems.at[3 * t + p],
                recv_sem=recv_sems.at[3 * t + p], device_id=(px, py, c), device_id_type=MESH))
    for cp in cps:
        cp.start()
    for cp in cps:
        cp.wait()


def _pair_share_body(*refs, n):
    ins, outs = refs[:n], refs[n:2 * n]
    send_sems, recv_sems, local_sems = refs[2 * n:]
    x, y, c, _ = _place()
    local, sends, recvs = [], [], []
    for t in range(n):
        h = ins[t].shape[0]
        local.append(pltpu.make_async_copy(ins[t], outs[t].at[pl.ds(c * h, h)], local_sems.at[t]))
        for half, group in ((c, sends), (1 - c, recvs)):
            group.append(pltpu.make_async_remote_copy(
                src_ref=ins[t], dst_ref=outs[t].at[pl.ds(half * h, h)], send_sem=send_sems.at[t],
                recv_sem=recv_sems.at[t], device_id=(x, y, 1 - c), device_id_type=MESH))
    for cp in local + sends:
        cp.start()
    for cp in recvs:
        cp.wait_recv()
    for cp in sends:
        cp.wait_send()
    for cp in local:
        cp.wait()


def _allreduce_body(in_ref, out_ref, send_sems, recv_sems, local_sem, n):
    x, y, c, _ = _place()
    flip = lambda v, bit: 1 - v if bit else v
    peers = [(flip(x, k & 4), flip(y, k & 2), flip(c, k & 1)) for k in range(1, 8)]

    def remote(k, slot):
        return pltpu.make_async_remote_copy(
            src_ref=in_ref, dst_ref=out_ref.at[slot], send_sem=send_sems.at[k], recv_sem=recv_sems.at[k],
            device_id=peers[k], device_id_type=MESH)

    local = pltpu.make_async_copy(in_ref, out_ref.at[4 * x + 2 * y + c], local_sem.at[0])
    sends = [remote(k, 4 * x + 2 * y + c) for k in range(7)]
    local.start()
    for cp in sends:
        cp.start()
    for k, (px, py, pc) in enumerate(peers):
        remote(k, 4 * px + 2 * py + pc).wait_recv()
    for cp in sends:
        cp.wait_send()
    local.wait()


def _sliced_call(body, scalars, arrays, in_maps, blocks, out_block, out_map, out_shape, grid, *, name):
    grid_spec = pltpu.PrefetchScalarGridSpec(
        num_scalar_prefetch=1, grid=grid,
        in_specs=[pl.BlockSpec(b, m) for b, m in zip(blocks, in_maps)],
        out_specs=pl.BlockSpec(out_block, out_map))
    return pl.pallas_call(
        functools.partial(body), name=name, grid_spec=grid_spec, out_shape=out_shape,
        compiler_params=pltpu.CompilerParams(vmem_limit_bytes=V7X_VMEM_LIMIT_BYTES))(scalars, *arrays)


def _chip_sum_body(s_ref, g_ref, r_ref, o_ref):
    o_ref[...] = (g_ref[...] + r_ref[...]).astype(o_ref.dtype)


def _final_sum_body(s_ref, g_ref, r1_ref, a_ref, b_ref, c_ref, o_ref):
    o_ref[...] = (((g_ref[...] + r1_ref[...]) + a_ref[...].astype(F32)) + b_ref[...].astype(F32)) + c_ref[...].astype(F32)


def _reduce_big(stacks, place):
    n = len(stacks)
    half = [jax.ShapeDtypeStruct((s.shape[0] // 2,) + s.shape[1:], F32) for s in stacks]
    from_pair = _comm_call(_pair_exchange_body, stacks, half, (n, n), name="grads_pair_exchange")
    chip16 = []
    for s, r in zip(stacks, from_pair):
        h, J, Kd, Nd = r.shape
        tr = _tile(Kd, 256)
        blk = (None, None, tr, Nd)
        chip16.append(_sliced_call(
            _chip_sum_body, place, [s, r],
            [lambda l, j, i, p: (p[0] * h + l, j, i, 0), lambda l, j, i, p: (l, j, i, 0)], [blk, blk], blk,
            lambda l, j, i, p: (l, j, i, 0), jax.ShapeDtypeStruct(r.shape, BF16), (h, J, Kd // tr),
            name="grads_chip_sum"))
    recv = [jax.ShapeDtypeStruct((3, a.shape[0]) + a.shape[2:], BF16) for a in chip16]
    from_chips = _comm_call(_chip_exchange_body, chip16, recv, (3 * n, 3 * n), name="grads_chip_exchange")
    sums = []
    for s, r, f in zip(stacks, from_pair, from_chips):
        h, J, Kd, Nd = r.shape
        tr = _tile(Kd, 256)
        blk4, blk3 = (None, None, tr, Nd), (None, tr, Nd)
        mine = lambda l, i, p: (l, p[1], i, 0)
        sums.append(_sliced_call(
            _final_sum_body, place, [s, r, f, f, f],
            [lambda l, i, p: (p[0] * h + l, p[1], i, 0), mine] + [functools.partial(lambda l, i, p, q: (q, l, i, 0), q=q)
                                                                  for q in range(3)],
            [blk4, blk4, blk4, blk4, blk4], blk3, lambda l, i, p: (l, i, 0),
            jax.ShapeDtypeStruct((h, Kd, Nd), F32), (h, Kd // tr), name="grads_final_sum"))
    full = [jax.ShapeDtypeStruct((2 * s.shape[0],) + s.shape[1:], F32) for s in sums]
    return _comm_call(_pair_share_body, sums, full, (n, n, n), name="grads_pair_share")


def _allreduce_small(buf):
    gathered = pl.pallas_call(
        functools.partial(_allreduce_body, n=1), name="small_allgather",
        in_specs=[pl.BlockSpec(memory_space=pltpu.HBM)], out_specs=pl.BlockSpec(memory_space=pltpu.HBM),
        out_shape=jax.ShapeDtypeStruct((8,) + buf.shape, F32),
        scratch_shapes=[pltpu.SemaphoreType.DMA((7,)), pltpu.SemaphoreType.DMA((7,)),
                        pltpu.SemaphoreType.DMA((1,))])(buf)

    def total(*b):
        acc = b[0]
        for t in b[1:]:
            acc = acc + t
        return acc

    return _ew(total, [gathered[i] for i in range(8)], out_dtypes=[F32], name="small_sum")[0]


def _pack(parts):
    flat = jnp.concatenate([p.reshape(-1).astype(F32) for p in parts])
    rows = -(-flat.shape[0] // (8 * LANES)) * 8
    return jnp.pad(flat, (0, rows * LANES - flat.shape[0])).reshape(rows, LANES)


def _unpack(buf, shapes):
    flat = buf.reshape(-1)
    out, at = [], 0
    for s in shapes:
        size = math.prod(s)
        out.append(flat[at:at + size].reshape(s))
        at += size
    return out


_INPUTS = ("x",) + _WEIGHTS + ("loss_target",) + tuple("m_" + n for n in _WEIGHTS) + tuple("v_" + n for n in _WEIGHTS)


def kernel(x, t5_bias, ab_w_in, ab_w_out, s5_lam_re, s5_lam_im, s5_log_step, s5_b_re, s5_b_im, s5_c_re, s5_c_im,
           s5_d, s5_w_glu, c_w_qkv, c_w_out, c_rpb, norm_mix, norm_mlp, mlp_w1, mlp_w2, norm_final, loss_target,
           m_t5_bias, m_ab_w_in, m_ab_w_out, m_s5_lam_re, m_s5_lam_im, m_s5_log_step, m_s5_b_re, m_s5_b_im,
           m_s5_c_re, m_s5_c_im, m_s5_d, m_s5_w_glu, m_c_w_qkv, m_c_w_out, m_c_rpb, m_norm_mix, m_norm_mlp,
           m_mlp_w1, m_mlp_w2, m_norm_final, v_t5_bias, v_ab_w_in, v_ab_w_out, v_s5_lam_re, v_s5_lam_im,
           v_s5_log_step, v_s5_b_re, v_s5_b_im, v_s5_c_re, v_s5_c_im, v_s5_d, v_s5_w_glu, v_c_w_qkv, v_c_w_out,
           v_c_rpb, v_norm_mix, v_norm_mlp, v_mlp_w1, v_mlp_w2, v_norm_final):
    args = (x, t5_bias, ab_w_in, ab_w_out, s5_lam_re, s5_lam_im, s5_log_step, s5_b_re, s5_b_im, s5_c_re, s5_c_im,
            s5_d, s5_w_glu, c_w_qkv, c_w_out, c_rpb, norm_mix, norm_mlp, mlp_w1, mlp_w2, norm_final, loss_target,
            m_t5_bias, m_ab_w_in, m_ab_w_out, m_s5_lam_re, m_s5_lam_im, m_s5_log_step, m_s5_b_re, m_s5_b_im,
            m_s5_c_re, m_s5_c_im, m_s5_d, m_s5_w_glu, m_c_w_qkv, m_c_w_out, m_c_rpb, m_norm_mix, m_norm_mlp,
            m_mlp_w1, m_mlp_w2, m_norm_final, v_t5_bias, v_ab_w_in, v_ab_w_out, v_s5_lam_re, v_s5_lam_im,
            v_s5_log_step, v_s5_b_re, v_s5_b_im, v_s5_c_re, v_s5_c_im, v_s5_d, v_s5_w_glu, v_c_w_qkv, v_c_w_out,
            v_c_rpb, v_norm_mix, v_norm_mlp, v_mlp_w1, v_mlp_w2, v_norm_final)
    A = dict(zip(_INPUTS, args, strict=True))
    P = {n: A[n] for n in _WEIGHTS}
    place = jnp.stack([lax.axis_index("c"), 2 * lax.axis_index("x") + lax.axis_index("y")]).astype(jnp.int32)

    gathered = _allgather([P[n].astype(BF16) for n in _BIG])
    W = dict(zip(_BIG, gathered))
    loss, dx, big, small = _local_grads(A["x"][0], A["loss_target"][0], P, W)

    stacks = [jnp.stack(big[n]) for n in _BIG]
    big_grads = dict(zip(_BIG, _reduce_big(stacks, place)))
    small_shapes = [P[n].shape for n in _SMALL] + [(1,)]
    reduced = _unpack(_allreduce_small(_pack([small[n] for n in _SMALL] + [loss.reshape(1)])), small_shapes)
    small_grads = dict(zip(_SMALL, reduced[:-1]))
    loss = reduced[-1][0]

    grads, delta, new_m, new_v = {}, {}, {}, {}
    for n in _BIG:
        g = big_grads[n]
        two_d = lambda t: t.reshape(-1, t.shape[-1])
        d, m, v = _adamw(two_d(P[n]), two_d(g), two_d(A["m_" + n]), two_d(A["v_" + n]), name="adamw")
        grads[n] = g
        delta[n], new_m[n], new_v[n] = (t.reshape(g.shape) for t in (d, m, v))
    d, m, v = _adamw(_pack([P[n] for n in _SMALL]), _pack([small_grads[n] for n in _SMALL]),
                     _pack([A["m_" + n] for n in _SMALL]), _pack([A["v_" + n] for n in _SMALL]), name="adamw_small")
    shapes = [P[n].shape for n in _SMALL]
    for n, dn, mn, vn in zip(_SMALL, _unpack(d, shapes), _unpack(m, shapes), _unpack(v, shapes)):
        grads[n] = small_grads[n]
        delta[n], new_m[n], new_v[n] = dn, mn, vn
    return (loss, dx[None], *[grads[n] for n in _WEIGHTS], *[delta[n] for n in _WEIGHTS],
            *[new_m[n] for n in _WEIGHTS], *[new_v[n] for n in _WEIGHTS])
```

```python
import functools
import math

import jax
import jax.numpy as jnp
from jax import lax
from jax.experimental import pallas as pl
from jax.experimental.pallas import tpu as pltpu

F32 = jnp.float32
BF16 = jnp.bfloat16

HEAD_DIM = 128
LANES = 128
DILATED_BRANCHES = ((128, 1), (512, 4), (2048, 16))
A_QBLOCK = 128
DIL_HALF = 64
B_GROUP = 16
B_STATE = 64
GROUPS_PER_BLOCK = LANES // B_GROUP
STATE_PER_BLOCK = GROUPS_PER_BLOCK * B_STATE
GRID_W = 64
NA_ROWS_MAX = 8
NA_COLS = 16
T5_BUCKETS = 32
T5_MAX_DISTANCE = 1024
RMS_EPS = 1e-6
NEG_INF = -1e30
ADAM_LR = 0.001
ADAM_B1 = 0.9
ADAM_B2 = 0.999
ADAM_EPS = 1e-08
ADAM_WD = 0.01
ADAM_STEP = 10
V7X_VMEM_LIMIT_BYTES = 56 * 1024 * 1024

NN = (((1,), (0,)), ((), ()))
NT = (((1,), (1,)), ((), ()))
TN = (((0,), (0,)), ((), ()))
MESH = pl.DeviceIdType.MESH


def _tile(n, pref):
    t = pref
    while t >= 8:
        if n % t == 0:
            return t
        t //= 2
    return n


def _pcall(body, *, name, grid, in_specs, out_specs, out_shape, scratch=(), sem=None, **static):
    params = dict(vmem_limit_bytes=V7X_VMEM_LIMIT_BYTES)
    if sem is not None:
        params["dimension_semantics"] = sem
    return pl.pallas_call(
        functools.partial(body, **static), name=name, grid=grid, in_specs=in_specs, out_specs=out_specs,
        out_shape=out_shape, scratch_shapes=list(scratch), compiler_params=pltpu.CompilerParams(**params))


def _mm_finish(acc, rest, mode):
    if mode == "plain":
        rest[0][...] = acc.astype(rest[0].dtype)
    elif mode == "res":
        rest[1][...] = (rest[0][...] + acc).astype(rest[1].dtype)
    elif mode == "relu2":
        rest[0][...] = acc
        r = jnp.maximum(acc, 0.0)
        rest[1][...] = (r * r).astype(rest[1].dtype)
    elif mode == "dact":
        rest[1][...] = (acc * (2.0 * jnp.maximum(rest[0][...], 0.0))).astype(rest[1].dtype)


def _mm_body(a_ref, b_ref, *rest, nk, dims, mode):
    prod = lax.dot_general(a_ref[...].astype(BF16), b_ref[...].astype(BF16), dims, preferred_element_type=F32)
    if nk == 1:
        _mm_finish(prod, rest, mode)
        return
    acc_ref = rest[-1]
    k = pl.program_id(2)

    @pl.when(k == 0)
    def _():
        acc_ref[...] = prod

    @pl.when(k > 0)
    def _():
        acc_ref[...] += prod

    @pl.when(k == nk - 1)
    def _():
        _mm_finish(acc_ref[...], rest, mode)


def _w_dims(w, split):
    J, _, Kd, Nd = w.shape
    return (J, Kd, J * Nd, Kd, Nd) if split == "col" else (J, J * Kd, Nd, Kd, Nd)


def _w_spec(split, layer, tk, tn, Kd, Nd, kn_of):
    kps, nps = Kd // tk, Nd // tn

    def index(*g):
        kb, nb = kn_of(*g)
        if split == "col":
            return nb // nps, layer, kb, nb % nps
        return kb // kps, layer, kb % kps, nb

    return pl.BlockSpec((None, None, tk, tn), index)


def _mm_nn(a, w, layer, split, *, name, out_dtype=F32, mode="plain", res=None):
    M = a.shape[0]
    J, K, N, Kd, Nd = _w_dims(w, split)
    tm, tn, tk = _tile(M, 1024), _tile(Nd, 1024), _tile(Kd, 2048)
    in_specs = [pl.BlockSpec((tm, tk), lambda i, j, k: (i, k)),
                _w_spec(split, layer, tk, tn, Kd, Nd, lambda i, j, k: (k, j))]
    args = [a, w]
    o_spec = pl.BlockSpec((tm, tn), lambda i, j, k: (i, j))
    if mode == "res":
        in_specs.append(o_spec)
        args.append(res)
    if mode == "relu2":
        out_shape = (jax.ShapeDtypeStruct((M, N), F32), jax.ShapeDtypeStruct((M, N), BF16))
        out_specs = (o_spec, o_spec)
    else:
        out_shape = jax.ShapeDtypeStruct((M, N), out_dtype)
        out_specs = o_spec
    return _pcall(_mm_body, name=name, grid=(M // tm, N // tn, K // tk), in_specs=in_specs, out_specs=out_specs,
                  out_shape=out_shape, scratch=[pltpu.VMEM((tm, tn), F32)],
                  sem=("parallel", "parallel", "arbitrary"), nk=K // tk, dims=NN, mode=mode)(*args)


def _mm_nt(a, w, layer, split, *, name, out_dtype=F32, mode="plain", act=None):
    M = a.shape[0]
    J, K, N, Kd, Nd = _w_dims(w, split)
    tm, tko, tc = _tile(M, 1024), _tile(Kd, 1024), _tile(Nd, 2048)
    in_specs = [pl.BlockSpec((tm, tc), lambda i, j, c: (i, c)),
                _w_spec(split, layer, tko, tc, Kd, Nd, lambda i, j, c: (j, c))]
    args = [a, w]
    o_spec = pl.BlockSpec((tm, tko), lambda i, j, c: (i, j))
    if mode == "dact":
        in_specs.append(o_spec)
        args.append(act)
    return _pcall(_mm_body, name=name, grid=(M // tm, K // tko, N // tc), in_specs=in_specs, out_specs=o_spec,
                  out_shape=jax.ShapeDtypeStruct((M, K), out_dtype), scratch=[pltpu.VMEM((tm, tko), F32)],
                  sem=("parallel", "parallel", "arbitrary"), nk=N // tc, dims=NT, mode=mode)(*args)


def _mm_tn(a, b, J, split, *, name):
    M, K = a.shape
    N = b.shape[1]
    Kd, Nd = (K, N // J) if split == "col" else (K // J, N)
    tk, tn, tc = _tile(Kd, 1024), _tile(Nd, 1024), _tile(M, 2048)
    kps, nps = Kd // tk, Nd // tn
    in_specs = [pl.BlockSpec((tc, tk), lambda i, j, c: (c, i)),
                pl.BlockSpec((tc, tn), lambda i, j, c: (c, j))]
    if split == "col":
        o_spec = pl.BlockSpec((None, tk, tn), lambda i, j, c: (j // nps, i, j % nps))
    else:
        o_spec = pl.BlockSpec((None, tk, tn), lambda i, j, c: (i // kps, i % kps, j))
    return _pcall(_mm_body, name=name, grid=(K // tk, N // tn, M // tc), in_specs=in_specs, out_specs=o_spec,
                  out_shape=jax.ShapeDtypeStruct((J, Kd, Nd), F32), scratch=[pltpu.VMEM((tk, tn), F32)],
                  sem=("parallel", "parallel", "arbitrary"), nk=M // tc, dims=TN, mode="plain")(a, b)


def _bmm_body(*refs, n, dims):
    out_ref = refs[2 * n]
    acc = None
    for i in range(n):
        p = lax.dot_general(refs[i][...].astype(BF16), refs[n + i][...].astype(BF16), dims,
                            preferred_element_type=F32)
        acc = p if acc is None else acc + p
    out_ref[...] = acc.astype(out_ref.dtype)


def _bmm_nn(a_list, b_list, *, name):
    M = a_list[0].shape[0]
    NB, Ka, No = b_list[0].shape
    tm = _tile(M, 512)
    n = len(a_list)
    in_specs = ([pl.BlockSpec((tm, Ka), lambda i, j: (i, j))] * n
                + [pl.BlockSpec((None, Ka, No), lambda i, j: (j, 0, 0))] * n)
    return _pcall(_bmm_body, name=name, grid=(M // tm, NB), in_specs=in_specs,
                  out_specs=pl.BlockSpec((tm, No), lambda i, j: (i, j)),
                  out_shape=jax.ShapeDtypeStruct((M, NB * No), F32), sem=("parallel", "parallel"),
                  n=n, dims=NN)(*a_list, *b_list)


def _bmm_nt(a_list, b_list, *, name):
    M = a_list[0].shape[0]
    NB, Ka, No = b_list[0].shape
    tm = _tile(M, 512)
    n = len(a_list)
    in_specs = ([pl.BlockSpec((tm, No), lambda i, j: (i, j))] * n
                + [pl.BlockSpec((None, Ka, No), lambda i, j: (j, 0, 0))] * n)
    return _pcall(_bmm_body, name=name, grid=(M // tm, NB), in_specs=in_specs,
                  out_specs=pl.BlockSpec((tm, Ka), lambda i, j: (i, j)),
                  out_shape=jax.ShapeDtypeStruct((M, NB * Ka), F32), sem=("parallel", "parallel"),
                  n=n, dims=NT)(*a_list, *b_list)


def _bmm_tn_body(a_ref, c_ref, out_ref):
    @pl.when(pl.program_id(1) == 0)
    def _():
        out_ref[...] = jnp.zeros_like(out_ref)

    out_ref[...] += lax.dot_general(a_ref[...].astype(BF16), c_ref[...].astype(BF16), TN,
                                    preferred_element_type=F32)


def _bmm_tn(a, c, NB, *, name):
    M = a.shape[0]
    Ka, No = a.shape[1] // NB, c.shape[1] // NB
    tm = _tile(M, 512)
    return _pcall(_bmm_tn_body, name=name, grid=(NB, M // tm),
                  in_specs=[pl.BlockSpec((tm, Ka), lambda j, m: (m, j)), pl.BlockSpec((tm, No), lambda j, m: (m, j))],
                  out_specs=pl.BlockSpec((None, Ka, No), lambda j, m: (j, 0, 0)),
                  out_shape=jax.ShapeDtypeStruct((NB, Ka, No), F32), sem=("parallel", "arbitrary"))(a, c)


def _ew_body(*refs, fn, n_in, n_out, n_acc):
    res = fn(*[r[...] for r in refs[:n_in]])
    if not isinstance(res, (tuple, list)):
        res = (res,)
    outs = refs[n_in:n_in + n_out]
    accs = refs[n_in + n_out:]
    for o, r in zip(outs, res[:n_out]):
        o[...] = r.astype(o.dtype)
    if n_acc:
        first = pl.program_id(1) == 0
        for a, r in zip(accs, res[n_out:]):
            @pl.when(first)
            def _(a=a):
                a[...] = jnp.zeros_like(a)

            a[...] += r


def _ew(fn, rows, vecs=(), *, out_dtypes=(), n_acc=0, width=None, ncol=1, tr=256, name):
    rows = [r if isinstance(r, tuple) else (r, 0) for r in rows]
    R = rows[0][0].shape[0]
    C = width if width is not None else rows[0][0].shape[1]
    tr = _tile(R, tr)
    in_specs = [pl.BlockSpec((tr, C), functools.partial(lambda j, i, off: (i, off + j), off=off)) for _, off in rows]
    in_specs += [pl.BlockSpec((1, C), lambda j, i: (0, j)) for _ in vecs]
    out_shape = [jax.ShapeDtypeStruct((R, ncol * C), dt) for dt in out_dtypes]
    out_specs = [pl.BlockSpec((tr, C), lambda j, i: (i, j)) for _ in out_dtypes]
    out_shape += [jax.ShapeDtypeStruct((1, ncol * C), F32)] * n_acc
    out_specs += [pl.BlockSpec((1, C), lambda j, i: (0, j))] * n_acc
    res = _pcall(_ew_body, name=name, grid=(ncol, R // tr), in_specs=in_specs, out_specs=tuple(out_specs),
                 out_shape=tuple(out_shape), sem=("parallel", "arbitrary" if n_acc else "parallel"),
                 fn=fn, n_in=len(rows) + len(vecs), n_out=len(out_dtypes), n_acc=n_acc)(
        *[a for a, _ in rows], *vecs)
    return res


def _rms_fwd(x, g, *, name):
    def fn(x, g):
        r = lax.rsqrt(jnp.mean(x * x, axis=1, keepdims=True) + RMS_EPS)
        return (x * r) * g

    return _ew(fn, [x], [g], out_dtypes=[BF16], name=name)[0]


def _rms_bwd(dh, x, g, dres, *, name):
    def fn(dh, x, dres, g):
        r = lax.rsqrt(jnp.mean(x * x, axis=1, keepdims=True) + RMS_EPS)
        y = x * r
        dy = dh * g
        dx = r * (dy - y * jnp.mean(dy * y, axis=1, keepdims=True))
        return dres + dx, jnp.sum(dh * y, axis=0, keepdims=True)

    return _ew(fn, [dh, x, dres], [g], out_dtypes=[F32], n_acc=1, name=name)


def _loss_and_grad(x, g, target, *, name):
    D = x.shape[1]

    def fn(x, t, g):
        r = lax.rsqrt(jnp.mean(x * x, axis=1, keepdims=True) + RMS_EPS)
        y = x * r
        diff = y * g - t
        dh = diff * (1.0 / D)
        dy = dh * g
        dx = r * (dy - y * jnp.mean(dy * y, axis=1, keepdims=True))
        loss = jnp.sum(jnp.sum(diff * diff, axis=1, keepdims=True), axis=0, keepdims=True) * (0.5 / D)
        return dx, jnp.sum(dh * y, axis=0, keepdims=True), jnp.broadcast_to(loss, (1, D))

    return _ew(fn, [x, target], [g], out_dtypes=[F32], n_acc=2, name=name)


def _gelu(y):
    c = math.sqrt(2.0 / math.pi)
    return 0.5 * y * (1.0 + jnp.tanh(c * (y + 0.044715 * (y * y * y))))


def _gelu_grad(y):
    c = math.sqrt(2.0 / math.pi)
    t = jnp.tanh(c * (y + 0.044715 * (y * y * y)))
    return 0.5 * (1.0 + t) + 0.5 * y * (1.0 - t * t) * (c * (1.0 + 3 * 0.044715 * (y * y)))


def _adamw(w, g, m, v, *, name):
    def fn(w, g, m, v):
        m2 = ADAM_B1 * m + (1.0 - ADAM_B1) * g
        v2 = ADAM_B2 * v + (1.0 - ADAM_B2) * (g * g)
        m_hat = m2 / (1.0 - ADAM_B1 ** ADAM_STEP)
        v_hat = v2 / (1.0 - ADAM_B2 ** ADAM_STEP)
        delta = -ADAM_LR * (m_hat / (jnp.sqrt(v_hat) + ADAM_EPS) + ADAM_WD * w)
        return delta, m2, v2

    return _ew(fn, [w, g, m, v], out_dtypes=[F32, F32, F32], name=name)


def _window(kind, blk, QB, rows):
    if kind == "dil":
        return pl.multiple_of(blk * QB, QB), 0
    kr = min(NA_ROWS_MAX, rows)
    rs = jnp.clip(blk - kr // 2, 0, rows - kr)
    return pl.multiple_of(rs * GRID_W, GRID_W), blk - rs


def _scores(q, kw, bias, kind, start, QB, W, L_valid, scale):
    s = lax.dot_general(q, kw, NT, preferred_element_type=F32) * scale + bias
    if kind == "dil":
        kp = start + lax.broadcasted_iota(jnp.int32, (QB, W), 1)
        s = jnp.where((kp >= DIL_HALF) & (kp < DIL_HALF + L_valid), s, NEG_INF)
    return s


def _wattn_fwd_body(q_ref, k_ref, v_ref, b_ref, *outs, QB, SUB, W, kind, L_valid, rows, scale):
    n = pl.program_id(1)
    wins = [_window(kind, n * SUB + i, QB, rows) for i in range(SUB)]
    sls = [slice(i * QB, (i + 1) * QB) for i in range(SUB)]
    ss = [_scores(q_ref[sl, :], k_ref[pl.ds(start, W), :], b_ref[pat], kind, start, QB, W, L_valid, scale)
          for sl, (start, pat) in zip(sls, wins)]
    ms = [jnp.max(s, axis=1, keepdims=True) for s in ss]
    es = [jnp.exp(s - m) for s, m in zip(ss, ms)]
    ls = [jnp.sum(e, axis=1, keepdims=True) for e in es]
    for sl, (start, _), m, e, l in zip(sls, wins, ms, es, ls):
        vw = v_ref[pl.ds(start, W), :]
        if kind == "na":
            p = (e * (1.0 / l)).astype(BF16)
            o = lax.dot_general(p, vw, NN, preferred_element_type=F32)
            outs[0][sl, :] = o.astype(outs[0].dtype)
            outs[1][sl, :] = jnp.broadcast_to(m + jnp.log(l), (QB, LANES))
        else:
            outs[0][sl, :] = lax.dot_general(e.astype(BF16), vw, NN, preferred_element_type=F32)
            outs[1][sl, :] = jnp.broadcast_to(m, (QB, LANES))
            outs[2][sl, :] = jnp.broadcast_to(l, (QB, LANES))


def _wattn_bwd_body(q_ref, do_ref, lse_ref, dl_ref, k_ref, v_ref, b_ref, dq_ref, dk_ref, dv_ref, db_ref, *,
                    QB, SUB, W, kind, L_valid, rows, scale, dgroup):
    cb = pl.program_id(0)
    n = pl.program_id(1)

    @pl.when(n == 0)
    def _():
        dk_ref[...] = jnp.zeros_like(dk_ref)
        dv_ref[...] = jnp.zeros_like(dv_ref)

    @pl.when((n == 0) & (cb % dgroup == 0))
    def _():
        db_ref[...] = jnp.zeros_like(db_ref)

    wins = [_window(kind, n * SUB + i, QB, rows) for i in range(SUB)]
    sls = [slice(i * QB, (i + 1) * QB) for i in range(SUB)]
    ss = [_scores(q_ref[sl, :], k_ref[pl.ds(start, W), :], b_ref[pat], kind, start, QB, W, L_valid, scale)
          for sl, (start, pat) in zip(sls, wins)]
    dps = [lax.dot_general(do_ref[sl, :], v_ref[pl.ds(start, W), :], NT, preferred_element_type=F32)
           for sl, (start, _) in zip(sls, wins)]
    ps = [jnp.exp(s - lse_ref[sl, :][:, :1]) for s, sl in zip(ss, sls)]
    dss = [p * (dp - dl_ref[sl, :][:, :1]) for p, dp, sl in zip(ps, dps, sls)]
    for sl, (start, pat), p, ds in zip(sls, wins, ps, dss):
        q = q_ref[sl, :]
        do = do_ref[sl, :]
        db_ref[pat] += ds
        dsb = ds.astype(BF16)
        dq_ref[sl, :] = (lax.dot_general(dsb, k_ref[pl.ds(start, W), :], NN, preferred_element_type=F32)
                         * scale).astype(dq_ref.dtype)
        dk_ref[pl.ds(start, W), :] += lax.dot_general(dsb, q, TN, preferred_element_type=F32) * scale
        dv_ref[pl.ds(start, W), :] += lax.dot_general(p.astype(BF16), do, TN, preferred_element_type=F32)


def _wattn_geometry(kind, LQ, dil):
    if kind == "dil":
        QB, W, rows = A_QBLOCK, A_QBLOCK + 2 * DIL_HALF, 0
    else:
        rows = LQ // GRID_W
        QB, W = GRID_W, min(NA_ROWS_MAX, rows) * GRID_W
    SUB = 4 if (LQ // QB) % 4 == 0 else 1
    return QB, W, rows, SUB


def _wattn_fwd(q, k, v, bias, *, kind, H, dil, qoff, koff, voff, name):
    LQ, LK = q.shape[0], k.shape[0]
    QB, W, rows, SUB = _wattn_geometry(kind, LQ, dil)
    ncb = H * dil
    NP = bias.shape[1]

    def col(cb, off):
        return off + (cb % dil) * H + cb // dil

    in_specs = [pl.BlockSpec((QB * SUB, LANES), lambda cb, n: (n, col(cb, qoff))),
                pl.BlockSpec((LK, LANES), lambda cb, n: (0, col(cb, koff))),
                pl.BlockSpec((LK, LANES), lambda cb, n: (0, col(cb, voff))),
                pl.BlockSpec((None, NP, QB, W), lambda cb, n: (cb // dil, 0, 0, 0))]
    o_spec = pl.BlockSpec((QB * SUB, LANES), lambda cb, n: (n, col(cb, 0)))
    shape = (LQ, ncb * LANES)
    if kind == "na":
        out_shape = (jax.ShapeDtypeStruct(shape, BF16), jax.ShapeDtypeStruct(shape, F32))
    else:
        out_shape = (jax.ShapeDtypeStruct(shape, F32),) * 3
    return _pcall(_wattn_fwd_body, name=name, grid=(ncb, LQ // (QB * SUB)), in_specs=in_specs,
                  out_specs=(o_spec,) * len(out_shape), out_shape=out_shape, sem=("parallel", "parallel"),
                  QB=QB, SUB=SUB, W=W, kind=kind, L_valid=LQ, rows=rows, scale=1.0 / math.sqrt(HEAD_DIM))(
        q, k, v, bias)


def _wattn_bwd(q, do, lse, delta, k, v, bias, *, kind, H, dil, qoff, koff, voff, dq_dtype, name):
    LQ, LK = q.shape[0], k.shape[0]
    QB, W, rows, SUB = _wattn_geometry(kind, LQ, dil)
    ncb = H * dil
    NP = bias.shape[1]

    def col(cb, off):
        return off + (cb % dil) * H + cb // dil

    q_spec = lambda off: pl.BlockSpec((QB * SUB, LANES), lambda cb, n: (n, col(cb, off)))
    kv_spec = lambda off: pl.BlockSpec((LK, LANES), lambda cb, n: (0, col(cb, off)))
    b_spec = pl.BlockSpec((None, NP, QB, W), lambda cb, n: (cb // dil, 0, 0, 0))
    in_specs = [q_spec(qoff), q_spec(0), q_spec(0), q_spec(0), kv_spec(koff), kv_spec(voff), b_spec]
    out_shape = (jax.ShapeDtypeStruct((LQ, ncb * LANES), dq_dtype), jax.ShapeDtypeStruct((LK, ncb * LANES), F32),
                 jax.ShapeDtypeStruct((LK, ncb * LANES), F32), jax.ShapeDtypeStruct(bias.shape, F32))
    out_specs = (q_spec(0), kv_spec(0), kv_spec(0), b_spec)
    return _pcall(_wattn_bwd_body, name=name, grid=(ncb, LQ // (QB * SUB)), in_specs=in_specs, out_specs=out_specs,
                  out_shape=out_shape, sem=("arbitrary", "arbitrary"),
                  QB=QB, SUB=SUB, W=W, kind=kind, L_valid=LQ, rows=rows, scale=1.0 / math.sqrt(HEAD_DIM),
                  dgroup=dil)(q, do, lse, delta, k, v, bias)


def _attn_delta(do, do_off, o, *, name):
    def fn(do, o):
        return jnp.broadcast_to(jnp.sum(do * o.astype(F32), axis=1, keepdims=True), do.shape), do

    return _ew(fn, [(do, do_off), o], out_dtypes=[F32, BF16], width=LANES, ncol=o.shape[1] // LANES, tr=512,
               name=name)


def _scan_fwd_body(bre_ref, bim_ref, are_ref, aim_ref, xre_ref, xim_ref, cr_ref, ci_ref, *, TC, reverse):
    @pl.when(pl.program_id(1) == 0)
    def _():
        cr_ref[...] = jnp.zeros_like(cr_ref)
        ci_ref[...] = jnp.zeros_like(ci_ref)

    ar = are_ref[...]
    ai = aim_ref[...]

    def step(s, carry):
        xr, xi = carry
        tau = TC - 1 - s if reverse else s
        nxr = ar * xr - ai * xi + bre_ref[pl.ds(tau, 1), :]
        nxi = ar * xi + ai * xr + bim_ref[pl.ds(tau, 1), :]
        xre_ref[pl.ds(tau, 1), :] = nxr
        xim_ref[pl.ds(tau, 1), :] = nxi
        return nxr, nxi

    xr, xi = lax.fori_loop(0, TC, step, (cr_ref[0:1, :], ci_ref[0:1, :]), unroll=8)
    cr_ref[0:1, :] = xr
    ci_ref[0:1, :] = xi


def _scan_geometry(S, NCH):
    return _tile(S, 512), _tile(NCH, 512)


def _scan_fwd(bu_re, bu_im, a_re, a_im, *, reverse, name):
    S, NCH = bu_re.shape
    TC, LB = _scan_geometry(S, NCH)
    nT = S // TC
    tmap = (lambda l, t: (nT - 1 - t, l)) if reverse else (lambda l, t: (t, l))
    row = pl.BlockSpec((TC, LB), tmap)
    vec = pl.BlockSpec((1, LB), lambda l, t: (0, l))
    return _pcall(_scan_fwd_body, name=name, grid=(NCH // LB, nT), in_specs=[row, row, vec, vec],
                  out_specs=(row, row), out_shape=(jax.ShapeDtypeStruct((S, NCH), F32),) * 2,
                  scratch=[pltpu.VMEM((8, LB), F32), pltpu.VMEM((8, LB), F32)], sem=("parallel", "arbitrary"),
                  TC=TC, reverse=reverse)(bu_re, bu_im, a_re, a_im)


def _scan_bwd_body(gre_ref, gim_ref, xre_ref, xim_ref, are_ref, aim_ref, hre_ref, him_ref, dar_ref, dai_ref,
                   cr_ref, ci_ref, *, TC, reverse):
    @pl.when(pl.program_id(1) == 0)
    def _():
        cr_ref[...] = jnp.zeros_like(cr_ref)
        ci_ref[...] = jnp.zeros_like(ci_ref)
        dar_ref[...] = jnp.zeros_like(dar_ref)
        dai_ref[...] = jnp.zeros_like(dai_ref)

    ar = are_ref[...]
    ai = aim_ref[...]

    def step(s, carry):
        hr, hi, sr, si = carry
        tau = TC - 1 - s if reverse else s
        xr = xre_ref[pl.ds(tau, 1), :]
        xi = xim_ref[pl.ds(tau, 1), :]
        sr = sr + (hr * xr + hi * xi)
        si = si + (hi * xr - hr * xi)
        nhr = gre_ref[pl.ds(tau, 1), :] + (ar * hr + ai * hi)
        nhi = gim_ref[pl.ds(tau, 1), :] + (ar * hi - ai * hr)
        hre_ref[pl.ds(tau, 1), :] = nhr
        him_ref[pl.ds(tau, 1), :] = nhi
        return nhr, nhi, sr, si

    z = jnp.zeros_like(ar)
    hr, hi, sr, si = lax.fori_loop(0, TC, step, (cr_ref[0:1, :], ci_ref[0:1, :], z, z), unroll=8)
    cr_ref[0:1, :] = hr
    ci_ref[0:1, :] = hi
    dar_ref[...] += sr
    dai_ref[...] += si


def _scan_bwd(g_re, g_im, x_re, x_im, a_re, a_im, *, reverse, name):
    S, NCH = g_re.shape
    TC, LB = _scan_geometry(S, NCH)
    nT = S // TC
    back = not reverse
    tmap = (lambda l, t: (nT - 1 - t, l)) if back else (lambda l, t: (t, l))
    row = pl.BlockSpec((TC, LB), tmap)
    vec = pl.BlockSpec((1, LB), lambda l, t: (0, l))
    return _pcall(_scan_bwd_body, name=name, grid=(NCH // LB, nT), in_specs=[row, row, row, row, vec, vec],
                  out_specs=(row, row, vec, vec),
                  out_shape=(jax.ShapeDtypeStruct((S, NCH), F32),) * 2 + (jax.ShapeDtypeStruct((1, NCH), F32),) * 2,
                  scratch=[pltpu.VMEM((8, LB), F32), pltpu.VMEM((8, LB), F32)], sem=("parallel", "arbitrary"),
                  TC=TC, reverse=back)(g_re, g_im, x_re, x_im, a_re, a_im)


def _bf(ref):
    return ref[...].astype(BF16)


def _s5_fwd_body(u_ref, br_ref, bi_ref, cr_ref, cin_ref, are_ref, aim_ref, xre_ref, xim_ref, y_ref,
                 bre_s, bim_s, car_r, car_i, *, TC, reverse):
    @pl.when(pl.program_id(1) == 0)
    def _():
        car_r[...] = jnp.zeros_like(car_r)
        car_i[...] = jnp.zeros_like(car_i)

    ub = _bf(u_ref)
    bre_s[...] = lax.dot_general(ub, _bf(br_ref), NN, preferred_element_type=F32)
    bim_s[...] = lax.dot_general(ub, _bf(bi_ref), NN, preferred_element_type=F32)
    ar = are_ref[...]
    ai = aim_ref[...]

    def step(s, carry):
        xr, xi = carry
        tau = TC - 1 - s if reverse else s
        nxr = ar * xr - ai * xi + bre_s[pl.ds(tau, 1), :]
        nxi = ar * xi + ai * xr + bim_s[pl.ds(tau, 1), :]
        xre_ref[pl.ds(tau, 1), :] = nxr
        xim_ref[pl.ds(tau, 1), :] = nxi
        return nxr, nxi

    xr, xi = lax.fori_loop(0, TC, step, (car_r[0:1, :], car_i[0:1, :]), unroll=8)
    car_r[0:1, :] = xr
    car_i[0:1, :] = xi
    y_ref[...] = (lax.dot_general(_bf(xre_ref), _bf(cr_ref), NN, preferred_element_type=F32)
                  + lax.dot_general(_bf(xim_ref), _bf(cin_ref), NN, preferred_element_type=F32))


def _s5_block_specs(S, BW, reverse):
    NB = BW // LANES
    TC = _tile(S, 512)
    nT = S // TC
    tmap = (lambda l, t: (nT - 1 - t, l)) if reverse else (lambda l, t: (t, l))
    narrow = pl.BlockSpec((TC, LANES), tmap)
    wide = pl.BlockSpec((TC, STATE_PER_BLOCK), tmap)
    vec = pl.BlockSpec((1, STATE_PER_BLOCK), lambda l, t: (0, l))
    w_in = pl.BlockSpec((None, LANES, STATE_PER_BLOCK), lambda l, t: (l, 0, 0))
    w_out = pl.BlockSpec((None, STATE_PER_BLOCK, LANES), lambda l, t: (l, 0, 0))
    return NB, TC, nT, narrow, wide, vec, w_in, w_out


def _s5_scan_fwd(u, mat, *, reverse, name):
    a_re, a_im, b_r, b_i, c_r, c_in = mat
    S, BW = u.shape
    NB, TC, nT, narrow, wide, vec, w_in, w_out = _s5_block_specs(S, BW, reverse)
    state = jax.ShapeDtypeStruct((S, NB * STATE_PER_BLOCK), F32)
    return _pcall(_s5_fwd_body, name=name, grid=(NB, nT), in_specs=[narrow, w_in, w_in, w_out, w_out, vec, vec],
                  out_specs=(wide, wide, narrow), out_shape=(state, state, jax.ShapeDtypeStruct((S, BW), F32)),
                  scratch=[pltpu.VMEM((TC, STATE_PER_BLOCK), F32)] * 2 + [pltpu.VMEM((8, STATE_PER_BLOCK), F32)] * 2,
                  sem=("parallel", "arbitrary"), TC=TC, reverse=reverse)(u, b_r, b_i, c_r, c_in, a_re, a_im)


def _s5_bwd_body(dy_ref, u_ref, xre_ref, xim_ref, br_ref, bi_ref, cr_ref, cin_ref, are_ref, aim_ref,
                 du_ref, dar_ref, dai_ref, dbr_ref, dbi_ref, dcr_ref, dcin_ref, hre_s, him_s, car_r, car_i, *,
                 TC, reverse):
    @pl.when(pl.program_id(1) == 0)
    def _():
        for r in (car_r, car_i, dar_ref, dai_ref, dbr_ref, dbi_ref, dcr_ref, dcin_ref):
            r[...] = jnp.zeros_like(r)

    dyb = _bf(dy_ref)
    hre_s[...] = lax.dot_general(dyb, _bf(cr_ref), NT, preferred_element_type=F32)
    him_s[...] = lax.dot_general(dyb, _bf(cin_ref), NT, preferred_element_type=F32)
    dcr_ref[...] += lax.dot_general(_bf(xre_ref), dyb, TN, preferred_element_type=F32)
    dcin_ref[...] += lax.dot_general(_bf(xim_ref), dyb, TN, preferred_element_type=F32)
    ar = are_ref[...]
    ai = aim_ref[...]

    def step(s, carry):
        hr, hi, sr, si = carry
        tau = TC - 1 - s if reverse else s
        xr = xre_ref[pl.ds(tau, 1), :]
        xi = xim_ref[pl.ds(tau, 1), :]
        sr = sr + (hr * xr + hi * xi)
        si = si + (hi * xr - hr * xi)
        nhr = hre_s[pl.ds(tau, 1), :] + (ar * hr + ai * hi)
        nhi = him_s[pl.ds(tau, 1), :] + (ar * hi - ai * hr)
        hre_s[pl.ds(tau, 1), :] = nhr
        him_s[pl.ds(tau, 1), :] = nhi
        return nhr, nhi, sr, si

    z = jnp.zeros_like(ar)
    hr, hi, sr, si = lax.fori_loop(0, TC, step, (car_r[0:1, :], car_i[0:1, :], z, z), unroll=8)
    car_r[0:1, :] = hr
    car_i[0:1, :] = hi
    dar_ref[...] += sr
    dai_ref[...] += si
    hrb, hib, ub = _bf(hre_s), _bf(him_s), _bf(u_ref)
    du_ref[...] = (lax.dot_general(hrb, _bf(br_ref), NT, preferred_element_type=F32)
                   + lax.dot_general(hib, _bf(bi_ref), NT, preferred_element_type=F32))
    dbr_ref[...] += lax.dot_general(ub, hrb, TN, preferred_element_type=F32)
    dbi_ref[...] += lax.dot_general(ub, hib, TN, preferred_element_type=F32)


def _s5_scan_bwd(dy, u, x_re, x_im, mat, *, reverse, name):
    a_re, a_im, b_r, b_i, c_r, c_in = mat
    S, BW = u.shape
    NB, TC, nT, narrow, wide, vec, w_in, w_out = _s5_block_specs(S, BW, not reverse)
    shapes = (jax.ShapeDtypeStruct((S, BW), F32),) + tuple(jax.ShapeDtypeStruct(m.shape, F32) for m in mat)
    res = _pcall(_s5_bwd_body, name=name, grid=(NB, nT),
                 in_specs=[narrow, narrow, wide, wide, w_in, w_in, w_out, w_out, vec, vec],
                 out_specs=(narrow, vec, vec, w_in, w_in, w_out, w_out), out_shape=shapes,
                 scratch=[pltpu.VMEM((TC, STATE_PER_BLOCK), F32)] * 2 + [pltpu.VMEM((8, STATE_PER_BLOCK), F32)] * 2,
                 sem=("parallel", "arbitrary"), TC=TC, reverse=not reverse)(
        dy, u, x_re, x_im, b_r, b_i, c_r, c_in, a_re, a_im)
    return res[0], tuple(res[1:])


def _t5_bucket(rel):
    half = T5_BUCKETS // 2
    max_exact = half // 2
    n = jnp.abs(rel)
    nf = jnp.maximum(n, 1).astype(F32)
    large = max_exact + (jnp.log(nf / max_exact) / math.log(T5_MAX_DISTANCE / max_exact)
                         * (half - max_exact)).astype(jnp.int32)
    large = jnp.minimum(large, half - 1)
    return jnp.where(rel > 0, half, 0) + jnp.where(n < max_exact, n, large)


def _dil_bias(t5_bias, dil):
    W = A_QBLOCK + 2 * DIL_HALF
    off = jnp.arange(W)[None, :] - DIL_HALF - jnp.arange(A_QBLOCK)[:, None]
    pick = (_t5_bucket(off * dil)[..., None] == jnp.arange(T5_BUCKETS)).astype(F32)
    b = jnp.einsum('qkb,bh->hqk', pick, t5_bias.astype(F32), precision=lax.Precision.HIGHEST)
    return jnp.where(jnp.abs(off) <= DIL_HALF, b, NEG_INF)[:, None]


def _na_bias(rpb, rows):
    kr = min(NA_ROWS_MAX, rows)
    ro = (jnp.arange(kr)[None, :] - jnp.arange(kr)[:, None]) + NA_ROWS_MAX - 1
    c = jnp.arange(GRID_W)
    col_start = jnp.clip(c - NA_COLS // 2, 0, GRID_W - NA_COLS)
    col_ok = (c[None, :] >= col_start[:, None]) & (c[None, :] < col_start[:, None] + NA_COLS)
    co = jnp.clip(c[None, :] - c[:, None] + NA_COLS - 1, 0, 2 * NA_COLS - 2)
    pick_r = (ro[..., None] == jnp.arange(2 * NA_ROWS_MAX - 1)).astype(F32)
    pick_c = (co[..., None] == jnp.arange(2 * NA_COLS - 1)).astype(F32)
    b = jnp.einsum('hrqk,pjr->hpqjk',
                   jnp.einsum('hrc,qkc->hrqk', rpb.astype(F32), pick_c, precision=lax.Precision.HIGHEST),
                   pick_r, precision=lax.Precision.HIGHEST)
    b = jnp.where(col_ok[None, None, :, None, :], b, NEG_INF)
    return b.reshape(rpb.shape[0], kr, GRID_W, kr * GRID_W)


def _s5_mats(lam_re, lam_im, log_step, b_re, b_im, c_re, c_im):
    G, P, C = b_re.shape
    NB = G // GROUPS_PER_BLOCK
    eye = jnp.eye(GROUPS_PER_BLOCK, dtype=F32)

    def bd_in(bb):
        t = bb.reshape(NB, GROUPS_PER_BLOCK, P, C).transpose(0, 1, 3, 2)
        return jnp.einsum('jgcp,gh->jgchp', t, eye).reshape(NB, GROUPS_PER_BLOCK * C, GROUPS_PER_BLOCK * P)

    def bd_out(cc):
        t = cc.reshape(NB, GROUPS_PER_BLOCK, C, P).transpose(0, 1, 3, 2)
        return jnp.einsum('jgpc,gh->jgphc', t, eye).reshape(NB, GROUPS_PER_BLOCK * P, GROUPS_PER_BLOCK * C)

    out = []
    for d in range(2):
        step = jnp.exp(log_step[d].astype(F32))[:, None]
        lr = jnp.minimum(lam_re[d].astype(F32), -1e-4)
        li = lam_im[d].astype(F32)
        mag = jnp.exp(lr * step)
        ab_re = mag * jnp.cos(li * step)
        ab_im = mag * jnp.sin(li * step)
        den = lr * lr + li * li
        zr = ((ab_re - 1.0) * lr + ab_im * li) / den
        zi = (ab_im * lr - (ab_re - 1.0) * li) / den
        bb_re = zr[..., None] * b_re - zi[..., None] * b_im
        bb_im = zr[..., None] * b_im + zi[..., None] * b_re
        out.append((ab_re.reshape(1, G * P), ab_im.reshape(1, G * P), bd_in(bb_re), bd_in(bb_im),
                    bd_out(c_re[d].astype(F32)), bd_out(-c_im[d].astype(F32))))
    return tuple(out)


def _sigmoid(z):
    return 1.0 / (1.0 + jnp.exp(-z))


def _strided(t, dil, pad):
    S, C = t.shape
    t = t.reshape(S // dil, dil * C)
    return jnp.pad(t, ((DIL_HALF, DIL_HALF), (0, 0))) if pad else t


def _dilated_fwd(q, k, v, t5_bias):
    S, AW = q.shape
    H = AW // HEAD_DIM
    parts = []
    for _, dil in DILATED_BRANCHES:
        num, m, l = _wattn_fwd(_strided(q, dil, False), _strided(k, dil, True), _strided(v, dil, True),
                               _dil_bias(t5_bias, dil), kind="dil", H=H, dil=dil, qoff=0, koff=0, voff=0,
                               name=f"dilated{dil}_fwd")
        parts += [num.reshape(S, AW), m.reshape(S, AW), l.reshape(S, AW)]

    def merge(n1, m1, l1, n2, m2, l2, n3, m3, l3):
        mx = jnp.maximum(jnp.maximum(m1, m2), m3)
        w1, w2, w3 = jnp.exp(m1 - mx), jnp.exp(m2 - mx), jnp.exp(m3 - mx)
        den = w1 * l1 + w2 * l2 + w3 * l3
        o = (w1 * n1 + w2 * n2 + w3 * n3) / den
        return o, o, mx + jnp.log(den)

    return _ew(merge, parts, out_dtypes=[F32, BF16, F32], name="dilated_merge")


def _dilated_bwd(q, k, v, t5_bias, do, lse, delta):
    S, AW = q.shape
    H = AW // HEAD_DIM
    dqs, dks, dvs = [], [], []
    dt5 = jnp.zeros(t5_bias.shape, F32)
    for _, dil in DILATED_BRANCHES:
        bias, bias_vjp = jax.vjp(functools.partial(_dil_bias, dil=dil), t5_bias)
        dq, dk, dv, db = _wattn_bwd(_strided(q, dil, False), _strided(do, dil, False), _strided(lse, dil, False),
                                    _strided(delta, dil, False), _strided(k, dil, True), _strided(v, dil, True),
                                    bias, kind="dil", H=H, dil=dil, qoff=0, koff=0, voff=0, dq_dtype=F32,
                                    name=f"dilated{dil}_bwd")
        dqs.append(dq.reshape(S, AW))
        dks.append(dk[DIL_HALF:-DIL_HALF].reshape(S, AW))
        dvs.append(dv[DIL_HALF:-DIL_HALF].reshape(S, AW))
        dt5 = dt5 + bias_vjp(db)[0]
    add3 = lambda a, b, c: a + b + c
    return (_ew(add3, dqs, out_dtypes=[BF16], name="dilated_dq_sum")[0],
            _ew(add3, dks, out_dtypes=[BF16], name="dilated_dk_sum")[0],
            _ew(add3, dvs, out_dtypes=[BF16], name="dilated_dv_sum")[0], dt5)


def _s5_fwd(u, mats, d_skip, w_glu, j):
    xs, ys = [], []
    for d in range(2):
        x_re, x_im, y_d = _s5_scan_fwd(u, mats[d], reverse=(d == 1), name=f"s5_scan_fwd{d}")
        xs += [x_re, x_im]
        ys.append(y_d)

    def act(y0, y1, u, dsk):
        y = (y0 + y1) + dsk * u
        return y, _gelu(y)

    y, yg = _ew(act, ys + [u], [d_skip], out_dtypes=[F32, F32], name="s5_gelu")
    z = _mm_nn(yg, w_glu, j, "row", name="s5_glu_fwd")
    ob = _ew(lambda yg, z: yg * _sigmoid(z), [yg, z], out_dtypes=[BF16], name="s5_gate")[0]
    return ob, (xs, y, yg, z)


def _s5_bwd(dmerged, ob_off, u, mats, d_skip, w_glu, j, saved):
    xs, y, yg, z = saved
    BW = u.shape[1]
    NB = BW // LANES

    def gate_bwd(dob, yg, z):
        sg = _sigmoid(z)
        return dob * yg * (sg * (1.0 - sg)), dob * sg

    dz, dyg1 = _ew(gate_bwd, [(dmerged, ob_off), yg, z], out_dtypes=[BF16, F32], width=LANES, ncol=NB,
                   name="s5_gate_bwd")
    dw_glu = _mm_tn(yg, dz, w_glu.shape[0], "row", name="s5_glu_dw")
    dyg2 = _mm_nt(dz, w_glu, j, "row", name="s5_glu_dx")

    def act_bwd(d1, d2, y, u):
        dy = (d1 + d2) * _gelu_grad(y)
        return dy, jnp.sum(dy * u, axis=0, keepdims=True)

    dy, dd = _ew(act_bwd, [dyg1, dyg2, y, u], out_dtypes=[F32], n_acc=1, name="s5_gelu_bwd")
    dmats, dus = [], []
    for d in range(2):
        du_d, dmat = _s5_scan_bwd(dy, u, xs[2 * d], xs[2 * d + 1], mats[d], reverse=(d == 1),
                                  name=f"s5_scan_bwd{d}")
        dus.append(du_d)
        dmats.append(dmat)
    du = _ew(lambda dy, d0, d1, dsk: dy * dsk + (d0 + d1), [dy] + dus, [d_skip], out_dtypes=[BF16],
             name="s5_du_sum")[0]
    return du, tuple(dmats), dd, dw_glu


def _ab_fwd(x, j, P, W):
    t5 = P["t5_bias"]
    AW = t5.shape[1] * HEAD_DIM
    hn = _rms_fwd(x, P["norm_mix"][2 * j][None], name="rms_fwd")
    proj = _mm_nn(hn, W["ab_w_in"], j, "col", name="ab_in_fwd")
    q, k, v = (proj[:, i * AW:(i + 1) * AW].astype(BF16) for i in range(3))
    u = proj[:, 3 * AW:]
    oa32, oa16, lse = _dilated_fwd(q, k, v, t5)
    mats = _s5_mats(*(P[n][j] for n in _S5_PARAMS))
    ob, s5_saved = _s5_fwd(u, mats, P["s5_d"][j][None], W["s5_w_glu"], j)
    merged = jnp.concatenate([oa16, ob], axis=1)
    x1 = _mm_nn(merged, W["ab_w_out"], j, "row", mode="res", res=x, name="ab_out_fwd")
    return x1, (x, hn, q, k, v, u, oa32, lse, merged, s5_saved)


def _ab_bwd(dx1, j, P, W, saved):
    x, hn, q, k, v, u, oa32, lse, merged, s5_saved = saved
    t5 = P["t5_bias"]
    AW = t5.shape[1] * HEAD_DIM
    J = W["ab_w_in"].shape[0]
    dmerged = _mm_nt(dx1, W["ab_w_out"], j, "row", name="ab_out_dx")
    dw_out = _mm_tn(merged, dx1, J, "row", name="ab_out_dw")
    delta, do16 = _attn_delta(dmerged, 0, oa32, name="dilated_delta")
    dq, dk, dv, dt5 = _dilated_bwd(q, k, v, t5, do16, lse, delta)
    s5_params = tuple(P[n][j] for n in _S5_PARAMS)
    mats, mats_vjp = jax.vjp(_s5_mats, *s5_params)
    du, dmats, dd, dw_glu = _s5_bwd(dmerged, AW // LANES, u, mats, P["s5_d"][j][None], W["s5_w_glu"], j, s5_saved)
    ds5 = mats_vjp(dmats)
    dproj = jnp.concatenate([dq, dk, dv, du], axis=1)
    dw_in = _mm_tn(hn, dproj, J, "col", name="ab_in_dw")
    dhn = _mm_nt(dproj, W["ab_w_in"], j, "col", name="ab_in_dx")
    dx, dg = _rms_bwd(dhn, x, P["norm_mix"][2 * j][None], dx1, name="rms_bwd")
    small = dict(zip(_S5_PARAMS, ds5), s5_d=dd[0], t5_bias=dt5)
    return dx, dg[0], dict(ab_w_in=dw_in, ab_w_out=dw_out, s5_w_glu=dw_glu), small


def _c_fwd(x, j, P, W):
    H = P["c_rpb"].shape[1]
    hn = _rms_fwd(x, P["norm_mix"][2 * j + 1][None], name="rms_fwd")
    qkv = _mm_nn(hn, W["c_w_qkv"], j, "col", out_dtype=BF16, name="c_qkv_fwd")
    bias = _na_bias(P["c_rpb"][j], x.shape[0] // GRID_W)
    o, lse = _wattn_fwd(qkv, qkv, qkv, bias, kind="na", H=H, dil=1, qoff=0, koff=H, voff=2 * H, name="na_fwd")
    x1 = _mm_nn(o, W["c_w_out"], j, "row", mode="res", res=x, name="c_out_fwd")
    return x1, (x, hn, qkv, o, lse)


def _c_bwd(dx1, j, P, W, saved):
    x, hn, qkv, o, lse = saved
    H = P["c_rpb"].shape[1]
    J = W["c_w_qkv"].shape[0]
    do = _mm_nt(dx1, W["c_w_out"], j, "row", name="c_out_dx")
    dw_out = _mm_tn(o, dx1, J, "row", name="c_out_dw")
    delta, do16 = _attn_delta(do, 0, o, name="na_delta")
    bias, bias_vjp = jax.vjp(functools.partial(_na_bias, rows=x.shape[0] // GRID_W), P["c_rpb"][j])
    dq, dk, dv, db = _wattn_bwd(qkv, do16, lse, delta, qkv, qkv, bias, kind="na", H=H, dil=1, qoff=0, koff=H,
                                voff=2 * H, dq_dtype=BF16, name="na_bwd")
    dqkv = jnp.concatenate([dq, dk.astype(BF16), dv.astype(BF16)], axis=1)
    dw_qkv = _mm_tn(hn, dqkv, J, "col", name="c_qkv_dw")
    dhn = _mm_nt(dqkv, W["c_w_qkv"], j, "col", name="c_qkv_dx")
    dx, dg = _rms_bwd(dhn, x, P["norm_mix"][2 * j + 1][None], dx1, name="rms_bwd")
    return dx, dg[0], dict(c_w_qkv=dw_qkv, c_w_out=dw_out), dict(c_rpb=bias_vjp(db)[0])


def _mlp_fwd(x, i, P, W):
    hn = _rms_fwd(x, P["norm_mlp"][i][None], name="rms_fwd")
    a, hdn = _mm_nn(hn, W["mlp_w1"], i, "col", mode="relu2", name="mlp_w1_fwd")
    x2 = _mm_nn(hdn, W["mlp_w2"], i, "row", mode="res", res=x, name="mlp_w2_fwd")
    return x2, (x, hn, a, hdn)


def _mlp_bwd(dx2, i, P, W, saved):
    x, hn, a, hdn = saved
    J = W["mlp_w1"].shape[0]
    da = _mm_nt(dx2, W["mlp_w2"], i, "row", out_dtype=BF16, mode="dact", act=a, name="mlp_w2_dx")
    dw2 = _mm_tn(hdn, dx2, J, "row", name="mlp_w2_dw")
    dw1 = _mm_tn(hn, da, J, "col", name="mlp_w1_dw")
    dhn = _mm_nt(da, W["mlp_w1"], i, "col", name="mlp_w1_dx")
    dx, dg = _rms_bwd(dhn, x, P["norm_mlp"][i][None], dx2, name="rms_bwd")
    return dx, dg[0], dict(mlp_w1=dw1, mlp_w2=dw2)


_S5_PARAMS = ("s5_lam_re", "s5_lam_im", "s5_log_step", "s5_b_re", "s5_b_im", "s5_c_re", "s5_c_im")
_BIG = ("ab_w_in", "ab_w_out", "s5_w_glu", "c_w_qkv", "c_w_out", "mlp_w1", "mlp_w2")
_SMALL = ("t5_bias", "s5_lam_re", "s5_lam_im", "s5_log_step", "s5_b_re", "s5_b_im", "s5_c_re", "s5_c_im", "s5_d",
          "c_rpb", "norm_mix", "norm_mlp", "norm_final")
_WEIGHTS = ("t5_bias", "ab_w_in", "ab_w_out", "s5_lam_re", "s5_lam_im", "s5_log_step", "s5_b_re", "s5_b_im",
            "s5_c_re", "s5_c_im", "s5_d", "s5_w_glu", "c_w_qkv", "c_w_out", "c_rpb", "norm_mix", "norm_mlp",
            "mlp_w1", "mlp_w2", "norm_final")


def _local_grads(x, target, P, W):
    depth = P["norm_mix"].shape[0]
    saved = []
    h = x
    for i in range(depth):
        h, s_mix = (_ab_fwd if i % 2 == 0 else _c_fwd)(h, i // 2, P, W)
        h, s_mlp = _mlp_fwd(h, i, P, W)
        saved.append((s_mix, s_mlp))
    dh, dg_final, loss_cols = _loss_and_grad(h, P["norm_final"][None], target, name="loss_head")
    big = {n: [None] * P[n].shape[0] for n in _BIG}
    small = {n: jnp.zeros(P[n].shape, F32) for n in _SMALL}
    small["norm_final"] = dg_final[0]
    for i in reversed(range(depth)):
        s_mix, s_mlp = saved[i]
        j = i // 2
        dh, dg, dbig = _mlp_bwd(dh, i, P, W, s_mlp)
        small["norm_mlp"] = small["norm_mlp"].at[i].set(dg)
        for n, g in dbig.items():
            big[n][i] = g
        dh, dg, dbig, dsmall = (_ab_bwd if i % 2 == 0 else _c_bwd)(dh, j, P, W, s_mix)
        small["norm_mix"] = small["norm_mix"].at[i].set(dg)
        for n, g in dbig.items():
            big[n][j] = g
        for n, g in dsmall.items():
            if n == "t5_bias":
                small[n] = small[n] + g
            else:
                small[n] = small[n].at[j].set(g.reshape(P[n].shape[1:]))
    return loss_cols[0, 0], dh, big, small


def _place():
    x, y, c = lax.axis_index("x"), lax.axis_index("y"), lax.axis_index("c")
    return x, y, c, ((1 - x, y), (x, 1 - y), (1 - x, 1 - y))


def _comm_call(body, arrays, out_shape, sems, *, name, **static):
    hbm = pl.BlockSpec(memory_space=pltpu.HBM)
    return pl.pallas_call(
        functools.partial(body, n=len(arrays), **static), name=name, in_specs=[hbm] * len(arrays),
        out_specs=tuple([hbm] * len(out_shape)), out_shape=tuple(out_shape),
        scratch_shapes=[pltpu.SemaphoreType.DMA((k,)) for k in sems])(*arrays)


def _allgather_body(*refs, n):
    ins, outs = refs[:n], refs[n:2 * n]
    send_sems, recv_sems, local_sems = refs[2 * n:]
    x, y, c, chips = _place()
    me = 2 * x + y

    def remote(t, p, slot):
        return pltpu.make_async_remote_copy(
            src_ref=ins[t], dst_ref=outs[t].at[slot], send_sem=send_sems.at[3 * t + p],
            recv_sem=recv_sems.at[3 * t + p], device_id=(*chips[p], c), device_id_type=MESH)

    local = [pltpu.make_async_copy(ins[t], outs[t].at[me], local_sems.at[t]) for t in range(n)]
    sends = [remote(t, p, me) for t in range(n) for p in range(3)]
    for cp in local + sends:
        cp.start()
    for t in range(n):
        for p, (px, py) in enumerate(chips):
            remote(t, p, 2 * px + py).wait_recv()
    for cp in sends:
        cp.wait_send()
    for cp in local:
        cp.wait()


def _allgather(shards):
    n = len(shards)
    out_shape = [jax.ShapeDtypeStruct((4,) + s.shape, s.dtype) for s in shards]
    return _comm_call(_allgather_body, shards, out_shape, (3 * n, 3 * n, n), name="weights_allgather")


def _pair_exchange_body(*refs, n):
    ins, outs = refs[:n], refs[n:2 * n]
    send_sems, recv_sems = refs[2 * n:]
    x, y, c, _ = _place()
    cps = []
    for t in range(n):
        h = outs[t].shape[0]
        cps.append(pltpu.make_async_remote_copy(
            src_ref=ins[t].at[pl.ds((1 - c) * h, h)], dst_ref=outs[t], send_sem=send_sems.at[t],
            recv_sem=recv_sems.at[t], device_id=(x, y, 1 - c), device_id_type=MESH))
    for cp in cps:
        cp.start()
    for cp in cps:
        cp.wait()


def _chip_exchange_body(*refs, n):
    ins, outs = refs[:n], refs[n:2 * n]
    send_sems, recv_sems = refs[2 * n:]
    x, y, c, chips = _place()
    cps = []
    for t in range(n):
        h = ins[t].shape[0]
        for p, (px, py) in enumerate(chips):
            cps.append(pltpu.make_async_remote_copy(
                src_ref=ins[t].at[pl.ds(0, h), 2 * px + py], dst_ref=outs[t].at[p], send_sem=send_sems.at[3 * t + p],
                recv_sem=recv_sems.at[3 * t + p], device_id=(px, py, c), device_id_type=MESH))
    for cp in cps:
        cp.start()
    for cp in cps:
        cp.wait()


def _pair_share_body(*refs, n):
    ins, outs = refs[:n], refs[n:2 * n]
    send_sems, recv_sems, local_sems = refs[2 * n:]
    x, y, c, _ = _place()
    local, sends, recvs = [], [], []
    for t in range(n):
        h = ins[t].shape[0]
        local.append(pltpu.make_async_copy(ins[t], outs[t].at[pl.ds(c * h, h)], local_sems.at[t]))
        for half, group in ((c, sends), (1 - c, recvs)):
            group.append(pltpu.make_async_remote_copy(
                src_ref=ins[t], dst_ref=outs[t].at[pl.ds(half * h, h)], send_sem=send_sems.at[t],
                recv_sem=recv_sems.at[t], device_id=(x, y, 1 - c), device_id_type=MESH))
    for cp in local + sends:
        cp.start()
    for cp in recvs:
        cp.wait_recv()
    for cp in sends:
        cp.wait_send()
    for cp in local:
        cp.wait()


def _allreduce_body(in_ref, out_ref, send_sems, recv_sems, local_sem, n):
    x, y, c, _ = _place()
    flip = lambda v, bit: 1 - v if bit else v
    peers = [(flip(x, k & 4), flip(y, k & 2), flip(c, k & 1)) for k in range(1, 8)]

    def remote(k, slot):
        return pltpu.make_async_remote_copy(
            src_ref=in_ref, dst_ref=out_ref.at[slot], send_sem=send_sems.at[k], recv_sem=recv_sems.at[k],
            device_id=peers[k], device_id_type=MESH)

    local = pltpu.make_async_copy(in_ref, out_ref.at[4 * x + 2 * y + c], local_sem.at[0])
    sends = [remote(k, 4 * x + 2 * y + c) for k in range(7)]
    local.start()
    for cp in sends:
        cp.start()
    for k, (px, py, pc) in enumerate(peers):
        remote(k, 4 * px + 2 * py + pc).wait_recv()
    for cp in sends:
        cp.wait_send()
    local.wait()


def _sliced_call(body, scalars, arrays, in_maps, blocks, out_block, out_map, out_shape, grid, *, name):
    grid_spec = pltpu.PrefetchScalarGridSpec(
        num_scalar_prefetch=1, grid=grid,
        in_specs=[pl.BlockSpec(b, m) for b, m in zip(blocks, in_maps)],
        out_specs=pl.BlockSpec(out_block, out_map))
    return pl.pallas_call(
        functools.partial(body), name=name, grid_spec=grid_spec, out_shape=out_shape,
        compiler_params=pltpu.CompilerParams(vmem_limit_bytes=V7X_VMEM_LIMIT_BYTES))(scalars, *arrays)


def _chip_sum_body(s_ref, g_ref, r_ref, o_ref):
    o_ref[...] = (g_ref[...] + r_ref[...]).astype(o_ref.dtype)


def _final_sum_body(s_ref, g_ref, r1_ref, a_ref, b_ref, c_ref, o_ref):
    o_ref[...] = (((g_ref[...] + r1_ref[...]) + a_ref[...].astype(F32)) + b_ref[...].astype(F32)) + c_ref[...].astype(F32)


def _reduce_big(stacks, place):
    n = len(stacks)
    half = [jax.ShapeDtypeStruct((s.shape[0] // 2,) + s.shape[1:], F32) for s in stacks]
    from_pair = _comm_call(_pair_exchange_body, stacks, half, (n, n), name="grads_pair_exchange")
    chip16 = []
    for s, r in zip(stacks, from_pair):
        h, J, Kd, Nd = r.shape
        tr = _tile(Kd, 256)
        blk = (None, None, tr, Nd)
        chip16.append(_sliced_call(
            _chip_sum_body, place, [s, r],
            [lambda l, j, i, p: (p[0] * h + l, j, i, 0), lambda l, j, i, p: (l, j, i, 0)], [blk, blk], blk,
            lambda l, j, i, p: (l, j, i, 0), jax.ShapeDtypeStruct(r.shape, BF16), (h, J, Kd // tr),
            name="grads_chip_sum"))
    recv = [jax.ShapeDtypeStruct((3, a.shape[0]) + a.shape[2:], BF16) for a in chip16]
    from_chips = _comm_call(_chip_exchange_body, chip16, recv, (3 * n, 3 * n), name="grads_chip_exchange")
    sums = []
    for s, r, f in zip(stacks, from_pair, from_chips):
        h, J, Kd, Nd = r.shape
        tr = _tile(Kd, 256)
        blk4, blk3 = (None, None, tr, Nd), (None, tr, Nd)
        mine = lambda l, i, p: (l, p[1], i, 0)
        sums.append(_sliced_call(
            _final_sum_body, place, [s, r, f, f, f],
            [lambda l, i, p: (p[0] * h + l, p[1], i, 0), mine] + [functools.partial(lambda l, i, p, q: (q, l, i, 0), q=q)
                                                                  for q in range(3)],
            [blk4, blk4, blk4, blk4, blk4], blk3, lambda l, i, p: (l, i, 0),
            jax.ShapeDtypeStruct((h, Kd, Nd), F32), (h, Kd // tr), name="grads_final_sum"))
    full = [jax.ShapeDtypeStruct((2 * s.shape[0],) + s.shape[1:], F32) for s in sums]
    return _comm_call(_pair_share_body, sums, full, (n, n, n), name="grads_pair_share")


def _allreduce_small(buf):
    gathered = pl.pallas_call(
        functools.partial(_allreduce_body, n=1), name="small_allgather",
        in_specs=[pl.BlockSpec(memory_space=pltpu.HBM)], out_specs=pl.BlockSpec(memory_space=pltpu.HBM),
        out_shape=jax.ShapeDtypeStruct((8,) + buf.shape, F32),
        scratch_shapes=[pltpu.SemaphoreType.DMA((7,)), pltpu.SemaphoreType.DMA((7,)),
                        pltpu.SemaphoreType.DMA((1,))])(buf)

    def total(*b):
        acc = b[0]
        for t in b[1:]:
            acc = acc + t
        return acc

    return _ew(total, [gathered[i] for i in range(8)], out_dtypes=[F32], name="small_sum")[0]


def _pack(parts):
    flat = jnp.concatenate([p.reshape(-1).astype(F32) for p in parts])
    rows = -(-flat.shape[0] // (8 * LANES)) * 8
    return jnp.pad(flat, (0, rows * LANES - flat.shape[0])).reshape(rows, LANES)


def _unpack(buf, shapes):
    flat = buf.reshape(-1)
    out, at = [], 0
    for s in shapes:
        size = math.prod(s)
        out.append(flat[at:at + size].reshape(s))
        at += size
    return out


_INPUTS = ("x",) + _WEIGHTS + ("loss_target",) + tuple("m_" + n for n in _WEIGHTS) + tuple("v_" + n for n in _WEIGHTS)


def kernel(x, t5_bias, ab_w_in, ab_w_out, s5_lam_re, s5_lam_im, s5_log_step, s5_b_re, s5_b_im, s5_c_re, s5_c_im,
           s5_d, s5_w_glu, c_w_qkv, c_w_out, c_rpb, norm_mix, norm_mlp, mlp_w1, mlp_w2, norm_final, loss_target,
           m_t5_bias, m_ab_w_in, m_ab_w_out, m_s5_lam_re, m_s5_lam_im, m_s5_log_step, m_s5_b_re, m_s5_b_im,
           m_s5_c_re, m_s5_c_im, m_s5_d, m_s5_w_glu, m_c_w_qkv, m_c_w_out, m_c_rpb, m_norm_mix, m_norm_mlp,
           m_mlp_w1, m_mlp_w2, m_norm_final, v_t5_bias, v_ab_w_in, v_ab_w_out, v_s5_lam_re, v_s5_lam_im,
           v_s5_log_step, v_s5_b_re, v_s5_b_im, v_s5_c_re, v_s5_c_im, v_s5_d, v_s5_w_glu, v_c_w_qkv, v_c_w_out,
           v_c_rpb, v_norm_mix, v_norm_mlp, v_mlp_w1, v_mlp_w2, v_norm_final):
    args = (x, t5_bias, ab_w_in, ab_w_out, s5_lam_re, s5_lam_im, s5_log_step, s5_b_re, s5_b_im, s5_c_re, s5_c_im,
            s5_d, s5_w_glu, c_w_qkv, c_w_out, c_rpb, norm_mix, norm_mlp, mlp_w1, mlp_w2, norm_final, loss_target,
            m_t5_bias, m_ab_w_in, m_ab_w_out, m_s5_lam_re, m_s5_lam_im, m_s5_log_step, m_s5_b_re, m_s5_b_im,
            m_s5_c_re, m_s5_c_im, m_s5_d, m_s5_w_glu, m_c_w_qkv, m_c_w_out, m_c_rpb, m_norm_mix, m_norm_mlp,
            m_mlp_w1, m_mlp_w2, m_norm_final, v_t5_bias, v_ab_w_in, v_ab_w_out, v_s5_lam_re, v_s5_lam_im,
            v_s5_log_step, v_s5_b_re, v_s5_b_im, v_s5_c_re, v_s5_c_im, v_s5_d, v_s5_w_glu, v_c_w_qkv, v_c_w_out,
            v_c_rpb, v_norm_mix, v_norm_mlp, v_mlp_w1, v_mlp_w2, v_norm_final)
    A = dict(zip(_INPUTS, args, strict=True))
    P = {n: A[n] for n in _WEIGHTS}
    place = jnp.stack([lax.axis_index("c"), 2 * lax.axis_index("x") + lax.axis_index("y")]).astype(jnp.int32)

    gathered = _allgather([P[n].astype(BF16) for n in _BIG])
    W = dict(zip(_BIG, gathered))
    loss, dx, big, small = _local_grads(A["x"][0], A["loss_target"][0], P, W)

    stacks = [jnp.stack(big[n]) for n in _BIG]
    big_grads = dict(zip(_BIG, _reduce_big(stacks, place)))
    small_shapes = [P[n].shape for n in _SMALL] + [(1,)]
    reduced = _unpack(_allreduce_small(_pack([small[n] for n in _SMALL] + [loss.reshape(1)])), small_shapes)
    small_grads = dict(zip(_SMALL, reduced[:-1]))
    loss = reduced[-1][0]

    grads, delta, new_m, new_v = {}, {}, {}, {}
    for n in _BIG:
        g = big_grads[n]
        two_d = lambda t: t.reshape(-1, t.shape[-1])
        d, m, v = _adamw(two_d(P[n]), two_d(g), two_d(A["m_" + n]), two_d(A["v_" + n]), name="adamw")
        grads[n] = g
        delta[n], new_m[n], new_v[n] = (t.reshape(g.shape) for t in (d, m, v))
    d, m, v = _adamw(_pack([P[n] for n in _SMALL]), _pack([small_grads[n] for n in _SMALL]),
                     _pack([A["m_" + n] for n in _SMALL]), _pack([A["v_" + n] for n in _SMALL]), name="adamw_small")
    shapes = [P[n].shape for n in _SMALL]
    for n, dn, mn, vn in zip(_SMALL, _unpack(d, shapes), _unpack(m, shapes), _unpack(v, shapes)):
        grads[n] = small_grads[n]
        delta[n], new_m[n], new_v[n] = dn, mn, vn
    return (loss, dx[None], *[grads[n] for n in _WEIGHTS], *[delta[n] for n in _WEIGHTS],
            *[new_m[n] for n in _WEIGHTS], *[new_v[n] for n in _WEIGHTS])
```

```python
import functools
import math

import jax
import jax.numpy as jnp
from jax import lax
from jax.experimental import pallas as pl
from jax.experimental.pallas import tpu as pltpu

F32 = jnp.float32
BF16 = jnp.bfloat16

HEAD_DIM = 128
LANES = 128
DILATED_BRANCHES = ((128, 1), (512, 4), (2048, 16))
A_QBLOCK = 128
DIL_HALF = 64
B_GROUP = 16
B_STATE = 64
GROUPS_PER_BLOCK = LANES // B_GROUP
STATE_PER_BLOCK = GROUPS_PER_BLOCK * B_STATE
GRID_W = 64
NA_ROWS_MAX = 8
NA_COLS = 16
T5_BUCKETS = 32
T5_MAX_DISTANCE = 1024
RMS_EPS = 1e-6
NEG_INF = -1e30
ADAM_LR = 0.001
ADAM_B1 = 0.9
ADAM_B2 = 0.999
ADAM_EPS = 1e-08
ADAM_WD = 0.01
ADAM_STEP = 10
V7X_VMEM_LIMIT_BYTES = 56 * 1024 * 1024

NN = (((1,), (0,)), ((), ()))
NT = (((1,), (1,)), ((), ()))
TN = (((0,), (0,)), ((), ()))
MESH = pl.DeviceIdType.MESH


def _tile(n, pref):
    t = pref
    while t >= 8:
        if n % t == 0:
            return t
        t //= 2
    return n


def _pcall(body, *, name, grid, in_specs, out_specs, out_shape, scratch=(), sem=None, **static):
    params = dict(vmem_limit_bytes=V7X_VMEM_LIMIT_BYTES)
    if sem is not None:
        params["dimension_semantics"] = sem
    return pl.pallas_call(
        functools.partial(body, **static), name=name, grid=grid, in_specs=in_specs, out_specs=out_specs,
        out_shape=out_shape, scratch_shapes=list(scratch), compiler_params=pltpu.CompilerParams(**params))


def _mm_finish(acc, rest, mode):
    if mode == "plain":
        rest[0][...] = acc.astype(rest[0].dtype)
    elif mode == "res":
        rest[1][...] = (rest[0][...] + acc).astype(rest[1].dtype)
    elif mode == "relu2":
        rest[0][...] = acc
        r = jnp.maximum(acc, 0.0)
        rest[1][...] = (r * r).astype(rest[1].dtype)
    elif mode == "dact":
        rest[1][...] = (acc * (2.0 * jnp.maximum(rest[0][...], 0.0))).astype(rest[1].dtype)


def _mm_body(a_ref, b_ref, *rest, nk, dims, mode):
    prod = lax.dot_general(a_ref[...].astype(BF16), b_ref[...].astype(BF16), dims, preferred_element_type=F32)
    if nk == 1:
        _mm_finish(prod, rest, mode)
        return
    acc_ref = rest[-1]
    k = pl.program_id(2)

    @pl.when(k == 0)
    def _():
        acc_ref[...] = prod

    @pl.when(k > 0)
    def _():
        acc_ref[...] += prod

    @pl.when(k == nk - 1)
    def _():
        _mm_finish(acc_ref[...], rest, mode)


def _w_dims(w, split):
    J, _, Kd, Nd = w.shape
    return (J, Kd, J * Nd, Kd, Nd) if split == "col" else (J, J * Kd, Nd, Kd, Nd)


def _w_spec(split, layer, tk, tn, Kd, Nd, kn_of):
    kps, nps = Kd // tk, Nd // tn

    def index(*g):
        kb, nb = kn_of(*g)
        if split == "col":
            return nb // nps, layer, kb, nb % nps
        return kb // kps, layer, kb % kps, nb

    return pl.BlockSpec((None, None, tk, tn), index)


def _mm_nn(a, w, layer, split, *, name, out_dtype=F32, mode="plain", res=None):
    M = a.shape[0]
    J, K, N, Kd, Nd = _w_dims(w, split)
    tm, tn, tk = _tile(M, 1024), _tile(Nd, 1024), _tile(Kd, 2048)
    in_specs = [pl.BlockSpec((tm, tk), lambda i, j, k: (i, k)),
                _w_spec(split, layer, tk, tn, Kd, Nd, lambda i, j, k: (k, j))]
    args = [a, w]
    o_spec = pl.BlockSpec((tm, tn), lambda i, j, k: (i, j))
    if mode == "res":
        in_specs.append(o_spec)
        args.append(res)
    if mode == "relu2":
        out_shape = (jax.ShapeDtypeStruct((M, N), F32), jax.ShapeDtypeStruct((M, N), BF16))
        out_specs = (o_spec, o_spec)
    else:
        out_shape = jax.ShapeDtypeStruct((M, N), out_dtype)
        out_specs = o_spec
    return _pcall(_mm_body, name=name, grid=(M // tm, N // tn, K // tk), in_specs=in_specs, out_specs=out_specs,
                  out_shape=out_shape, scratch=[pltpu.VMEM((tm, tn), F32)],
                  sem=("parallel", "parallel", "arbitrary"), nk=K // tk, dims=NN, mode=mode)(*args)


def _mm_nt(a, w, layer, split, *, name, out_dtype=F32, mode="plain", act=None):
    M = a.shape[0]
    J, K, N, Kd, Nd = _w_dims(w, split)
    tm, tko, tc = _tile(M, 1024), _tile(Kd, 1024), _tile(Nd, 2048)
    in_specs = [pl.BlockSpec((tm, tc), lambda i, j, c: (i, c)),
                _w_spec(split, layer, tko, tc, Kd, Nd, lambda i, j, c: (j, c))]
    args = [a, w]
    o_spec = pl.BlockSpec((tm, tko), lambda i, j, c: (i, j))
    if mode == "dact":
        in_specs.append(o_spec)
        args.append(act)
    return _pcall(_mm_body, name=name, grid=(M // tm, K // tko, N // tc), in_specs=in_specs, out_specs=o_spec,
                  out_shape=jax.ShapeDtypeStruct((M, K), out_dtype), scratch=[pltpu.VMEM((tm, tko), F32)],
                  sem=("parallel", "parallel", "arbitrary"), nk=N // tc, dims=NT, mode=mode)(*args)


def _mm_tn(a, b, J, split, *, name):
    M, K = a.shape
    N = b.shape[1]
    Kd, Nd = (K, N // J) if split == "col" else (K // J, N)
    tk, tn, tc = _tile(Kd, 1024), _tile(Nd, 1024), _tile(M, 2048)
    kps, nps = Kd // tk, Nd // tn
    in_specs = [pl.BlockSpec((tc, tk), lambda i, j, c: (c, i)),
                pl.BlockSpec((tc, tn), lambda i, j, c: (c, j))]
    if split == "col":
        o_spec = pl.BlockSpec((None, tk, tn), lambda i, j, c: (j // nps, i, j % nps))
    else:
        o_spec = pl.BlockSpec((None, tk, tn), lambda i, j, c: (i // kps, i % kps, j))
    return _pcall(_mm_body, name=name, grid=(K // tk, N // tn, M // tc), in_specs=in_specs, out_specs=o_spec,
                  out_shape=jax.ShapeDtypeStruct((J, Kd, Nd), F32), scratch=[pltpu.VMEM((tk, tn), F32)],
                  sem=("parallel", "parallel", "arbitrary"), nk=M // tc, dims=TN, mode="plain")(a, b)


def _bmm_body(*refs, n, dims):
    out_ref = refs[2 * n]
    acc = None
    for i in range(n):
        p = lax.dot_general(refs[i][...].astype(BF16), refs[n + i][...].astype(BF16), dims,
                            preferred_element_type=F32)
        acc = p if acc is None else acc + p
    out_ref[...] = acc.astype(out_ref.dtype)


def _bmm_nn(a_list, b_list, *, name):
    M = a_list[0].shape[0]
    NB, Ka, No = b_list[0].shape
    tm = _tile(M, 512)
    n = len(a_list)
    in_specs = ([pl.BlockSpec((tm, Ka), lambda i, j: (i, j))] * n
                + [pl.BlockSpec((None, Ka, No), lambda i, j: (j, 0, 0))] * n)
    return _pcall(_bmm_body, name=name, grid=(M // tm, NB), in_specs=in_specs,
                  out_specs=pl.BlockSpec((tm, No), lambda i, j: (i, j)),
                  out_shape=jax.ShapeDtypeStruct((M, NB * No), F32), sem=("parallel", "parallel"),
                  n=n, dims=NN)(*a_list, *b_list)


def _bmm_nt(a_list, b_list, *, name):
    M = a_list[0].shape[0]
    NB, Ka, No = b_list[0].shape
    tm = _tile(M, 512)
    n = len(a_list)
    in_specs = ([pl.BlockSpec((tm, No), lambda i, j: (i, j))] * n
                + [pl.BlockSpec((None, Ka, No), lambda i, j: (j, 0, 0))] * n)
    return _pcall(_bmm_body, name=name, grid=(M // tm, NB), in_specs=in_specs,
                  out_specs=pl.BlockSpec((tm, Ka), lambda i, j: (i, j)),
                  out_shape=jax.ShapeDtypeStruct((M, NB * Ka), F32), sem=("parallel", "parallel"),
                  n=n, dims=NT)(*a_list, *b_list)


def _bmm_tn_body(a_ref, c_ref, out_ref):
    @pl.when(pl.program_id(1) == 0)
    def _():
        out_ref[...] = jnp.zeros_like(out_ref)

    out_ref[...] += lax.dot_general(a_ref[...].astype(BF16), c_ref[...].astype(BF16), TN,
                                    preferred_element_type=F32)


def _bmm_tn(a, c, NB, *, name):
    M = a.shape[0]
    Ka, No = a.shape[1] // NB, c.shape[1] // NB
    tm = _tile(M, 512)
    return _pcall(_bmm_tn_body, name=name, grid=(NB, M // tm),
                  in_specs=[pl.BlockSpec((tm, Ka), lambda j, m: (m, j)), pl.BlockSpec((tm, No), lambda j, m: (m, j))],
                  out_specs=pl.BlockSpec((None, Ka, No), lambda j, m: (j, 0, 0)),
                  out_shape=jax.ShapeDtypeStruct((NB, Ka, No), F32), sem=("parallel", "arbitrary"))(a, c)


def _ew_body(*refs, fn, n_in, n_out, n_acc):
    res = fn(*[r[...] for r in refs[:n_in]])
    if not isinstance(res, (tuple, list)):
        res = (res,)
    outs = refs[n_in:n_in + n_out]
    accs = refs[n_in + n_out:]
    for o, r in zip(outs, res[:n_out]):
        o[...] = r.astype(o.dtype)
    if n_acc:
        first = pl.program_id(1) == 0
        for a, r in zip(accs, res[n_out:]):
            @pl.when(first)
            def _(a=a):
                a[...] = jnp.zeros_like(a)

            a[...] += r


def _ew(fn, rows, vecs=(), *, out_dtypes=(), n_acc=0, width=None, ncol=1, tr=256, name):
    rows = [r if isinstance(r, tuple) else (r, 0) for r in rows]
    R = rows[0][0].shape[0]
    C = width if width is not None else rows[0][0].shape[1]
    tr = _tile(R, tr)
    in_specs = [pl.BlockSpec((tr, C), functools.partial(lambda j, i, off: (i, off + j), off=off)) for _, off in rows]
    in_specs += [pl.BlockSpec((1, C), lambda j, i: (0, j)) for _ in vecs]
    out_shape = [jax.ShapeDtypeStruct((R, ncol * C), dt) for dt in out_dtypes]
    out_specs = [pl.BlockSpec((tr, C), lambda j, i: (i, j)) for _ in out_dtypes]
    out_shape += [jax.ShapeDtypeStruct((1, ncol * C), F32)] * n_acc
    out_specs += [pl.BlockSpec((1, C), lambda j, i: (0, j))] * n_acc
    res = _pcall(_ew_body, name=name, grid=(ncol, R // tr), in_specs=in_specs, out_specs=tuple(out_specs),
                 out_shape=tuple(out_shape), sem=("parallel", "arbitrary" if n_acc else "parallel"),
                 fn=fn, n_in=len(rows) + len(vecs), n_out=len(out_dtypes), n_acc=n_acc)(
        *[a for a, _ in rows], *vecs)
    return res


def _rms_fwd(x, g, *, name):
    def fn(x, g):
        r = lax.rsqrt(jnp.mean(x * x, axis=1, keepdims=True) + RMS_EPS)
        return (x * r) * g

    return _ew(fn, [x], [g], out_dtypes=[BF16], name=name)[0]


def _rms_bwd(dh, x, g, dres, *, name):
    def fn(dh, x, dres, g):
        r = lax.rsqrt(jnp.mean(x * x, axis=1, keepdims=True) + RMS_EPS)
        y = x * r
        dy = dh * g
        dx = r * (dy - y * jnp.mean(dy * y, axis=1, keepdims=True))
        return dres + dx, jnp.sum(dh * y, axis=0, keepdims=True)

    return _ew(fn, [dh, x, dres], [g], out_dtypes=[F32], n_acc=1, name=name)


def _loss_and_grad(x, g, target, *, name):
    D = x.shape[1]

    def fn(x, t, g):
        r = lax.rsqrt(jnp.mean(x * x, axis=1, keepdims=True) + RMS_EPS)
        y = x * r
        diff = y * g - t
        dh = diff * (1.0 / D)
        dy = dh * g
        dx = r * (dy - y * jnp.mean(dy * y, axis=1, keepdims=True))
        loss = jnp.sum(jnp.sum(diff * diff, axis=1, keepdims=True), axis=0, keepdims=True) * (0.5 / D)
        return dx, jnp.sum(dh * y, axis=0, keepdims=True), jnp.broadcast_to(loss, (1, D))

    return _ew(fn, [x, target], [g], out_dtypes=[F32], n_acc=2, name=name)


def _gelu(y):
    c = math.sqrt(2.0 / math.pi)
    return 0.5 * y * (1.0 + jnp.tanh(c * (y + 0.044715 * (y * y * y))))


def _gelu_grad(y):
    c = math.sqrt(2.0 / math.pi)
    t = jnp.tanh(c * (y + 0.044715 * (y * y * y)))
    return 0.5 * (1.0 + t) + 0.5 * y * (1.0 - t * t) * (c * (1.0 + 3 * 0.044715 * (y * y)))


def _adamw(w, g, m, v, *, name):
    def fn(w, g, m, v):
        m2 = ADAM_B1 * m + (1.0 - ADAM_B1) * g
        v2 = ADAM_B2 * v + (1.0 - ADAM_B2) * (g * g)
        m_hat = m2 / (1.0 - ADAM_B1 ** ADAM_STEP)
        v_hat = v2 / (1.0 - ADAM_B2 ** ADAM_STEP)
        delta = -ADAM_LR * (m_hat / (jnp.sqrt(v_hat) + ADAM_EPS) + ADAM_WD * w)
        return delta, m2, v2

    return _ew(fn, [w, g, m, v], out_dtypes=[F32, F32, F32], name=name)


def _window(kind, blk, QB, rows):
    if kind == "dil":
        return pl.multiple_of(blk * QB, QB), 0
    kr = min(NA_ROWS_MAX, rows)
    rs = jnp.clip(blk - kr // 2, 0, rows - kr)
    return pl.multiple_of(rs * GRID_W, GRID_W), blk - rs


def _scores(q, kw, bias, kind, start, QB, W, L_valid, scale):
    s = lax.dot_general(q, kw, NT, preferred_element_type=F32) * scale + bias
    if kind == "dil":
        kp = start + lax.broadcasted_iota(jnp.int32, (QB, W), 1)
        s = jnp.where((kp >= DIL_HALF) & (kp < DIL_HALF + L_valid), s, NEG_INF)
    return s


def _wattn_fwd_body(q_ref, k_ref, v_ref, b_ref, *outs, QB, SUB, W, kind, L_valid, rows, scale):
    n = pl.program_id(1)
    wins = [_window(kind, n * SUB + i, QB, rows) for i in range(SUB)]
    sls = [slice(i * QB, (i + 1) * QB) for i in range(SUB)]
    ss = [_scores(q_ref[sl, :], k_ref[pl.ds(start, W), :], b_ref[pat], kind, start, QB, W, L_valid, scale)
          for sl, (start, pat) in zip(sls, wins)]
    ms = [jnp.max(s, axis=1, keepdims=True) for s in ss]
    es = [jnp.exp(s - m) for s, m in zip(ss, ms)]
    ls = [jnp.sum(e, axis=1, keepdims=True) for e in es]
    for sl, (start, _), m, e, l in zip(sls, wins, ms, es, ls):
        vw = v_ref[pl.ds(start, W), :]
        if kind == "na":
            p = (e * (1.0 / l)).astype(BF16)
            o = lax.dot_general(p, vw, NN, preferred_element_type=F32)
            outs[0][sl, :] = o.astype(outs[0].dtype)
            outs[1][sl, :] = jnp.broadcast_to(m + jnp.log(l), (QB, LANES))
        else:
            outs[0][sl, :] = lax.dot_general(e.astype(BF16), vw, NN, preferred_element_type=F32)
            outs[1][sl, :] = jnp.broadcast_to(m, (QB, LANES))
            outs[2][sl, :] = jnp.broadcast_to(l, (QB, LANES))


def _wattn_bwd_body(q_ref, do_ref, lse_ref, dl_ref, k_ref, v_ref, b_ref, dq_ref, dk_ref, dv_ref, db_ref, *,
                    QB, SUB, W, kind, L_valid, rows, scale, dgroup):
    cb = pl.program_id(0)
    n = pl.program_id(1)

    @pl.when(n == 0)
    def _():
        dk_ref[...] = jnp.zeros_like(dk_ref)
        dv_ref[...] = jnp.zeros_like(dv_ref)

    @pl.when((n == 0) & (cb % dgroup == 0))
    def _():
        db_ref[...] = jnp.zeros_like(db_ref)

    wins = [_window(kind, n * SUB + i, QB, rows) for i in range(SUB)]
    sls = [slice(i * QB, (i + 1) * QB) for i in range(SUB)]
    ss = [_scores(q_ref[sl, :], k_ref[pl.ds(start, W), :], b_ref[pat], kind, start, QB, W, L_valid, scale)
          for sl, (start, pat) in zip(sls, wins)]
    dps = [lax.dot_general(do_ref[sl, :], v_ref[pl.ds(start, W), :], NT, preferred_element_type=F32)
           for sl, (start, _) in zip(sls, wins)]
    ps = [jnp.exp(s - lse_ref[sl, :][:, :1]) for s, sl in zip(ss, sls)]
    dss = [p * (dp - dl_ref[sl, :][:, :1]) for p, dp, sl in zip(ps, dps, sls)]
    for sl, (start, pat), p, ds in zip(sls, wins, ps, dss):
        q = q_ref[sl, :]
        do = do_ref[sl, :]
        db_ref[pat] += ds
        dsb = ds.astype(BF16)
        dq_ref[sl, :] = (lax.dot_general(dsb, k_ref[pl.ds(start, W), :], NN, preferred_element_type=F32)
                         * scale).astype(dq_ref.dtype)
        dk_ref[pl.ds(start, W), :] += lax.dot_general(dsb, q, TN, preferred_element_type=F32) * scale
        dv_ref[pl.ds(start, W), :] += lax.dot_general(p.astype(BF16), do, TN, preferred_element_type=F32)


def _wattn_geometry(kind, LQ, dil):
    if kind == "dil":
        QB, W, rows = A_QBLOCK, A_QBLOCK + 2 * DIL_HALF, 0
    else:
        rows = LQ // GRID_W
        QB, W = GRID_W, min(NA_ROWS_MAX, rows) * GRID_W
    SUB = 4 if (LQ // QB) % 4 == 0 else 1
    return QB, W, rows, SUB


def _wattn_fwd(q, k, v, bias, *, kind, H, dil, qoff, koff, voff, name):
    LQ, LK = q.shape[0], k.shape[0]
    QB, W, rows, SUB = _wattn_geometry(kind, LQ, dil)
    ncb = H * dil
    NP = bias.shape[1]

    def col(cb, off):
        return off + (cb % dil) * H + cb // dil

    in_specs = [pl.BlockSpec((QB * SUB, LANES), lambda cb, n: (n, col(cb, qoff))),
                pl.BlockSpec((LK, LANES), lambda cb, n: (0, col(cb, koff))),
                pl.BlockSpec((LK, LANES), lambda cb, n: (0, col(cb, voff))),
                pl.BlockSpec((None, NP, QB, W), lambda cb, n: (cb // dil, 0, 0, 0))]
    o_spec = pl.BlockSpec((QB * SUB, LANES), lambda cb, n: (n, col(cb, 0)))
    shape = (LQ, ncb * LANES)
    if kind == "na":
        out_shape = (jax.ShapeDtypeStruct(shape, BF16), jax.ShapeDtypeStruct(shape, F32))
    else:
        out_shape = (jax.ShapeDtypeStruct(shape, F32),) * 3
    return _pcall(_wattn_fwd_body, name=name, grid=(ncb, LQ // (QB * SUB)), in_specs=in_specs,
                  out_specs=(o_spec,) * len(out_shape), out_shape=out_shape, sem=("parallel", "parallel"),
                  QB=QB, SUB=SUB, W=W, kind=kind, L_valid=LQ, rows=rows, scale=1.0 / math.sqrt(HEAD_DIM))(
        q, k, v, bias)


def _wattn_bwd(q, do, lse, delta, k, v, bias, *, kind, H, dil, qoff, koff, voff, dq_dtype, name):
    LQ, LK = q.shape[0], k.shape[0]
    QB, W, rows, SUB = _wattn_geometry(kind, LQ, dil)
    ncb = H * dil
    NP = bias.shape[1]

    def col(cb, off):
        return off + (cb % dil) * H + cb // dil

    q_spec = lambda off: pl.BlockSpec((QB * SUB, LANES), lambda cb, n: (n, col(cb, off)))
    kv_spec = lambda off: pl.BlockSpec((LK, LANES), lambda cb, n: (0, col(cb, off)))
    b_spec = pl.BlockSpec((None, NP, QB, W), lambda cb, n: (cb // dil, 0, 0, 0))
    in_specs = [q_spec(qoff), q_spec(0), q_spec(0), q_spec(0), kv_spec(koff), kv_spec(voff), b_spec]
    out_shape = (jax.ShapeDtypeStruct((LQ, ncb * LANES), dq_dtype), jax.ShapeDtypeStruct((LK, ncb * LANES), F32),
                 jax.ShapeDtypeStruct((LK, ncb * LANES), F32), jax.ShapeDtypeStruct(bias.shape, F32))
    out_specs = (q_spec(0), kv_spec(0), kv_spec(0), b_spec)
    return _pcall(_wattn_bwd_body, name=name, grid=(ncb, LQ // (QB * SUB)), in_specs=in_specs, out_specs=out_specs,
                  out_shape=out_shape, sem=("arbitrary", "arbitrary"),
                  QB=QB, SUB=SUB, W=W, kind=kind, L_valid=LQ, rows=rows, scale=1.0 / math.sqrt(HEAD_DIM),
                  dgroup=dil)(q, do, lse, delta, k, v, bias)


def _attn_delta(do, do_off, o, *, name):
    def fn(do, o):
        return jnp.broadcast_to(jnp.sum(do * o.astype(F32), axis=1, keepdims=True), do.shape), do

    return _ew(fn, [(do, do_off), o], out_dtypes=[F32, BF16], width=LANES, ncol=o.shape[1] // LANES, tr=512,
               name=name)


def _scan_fwd_body(bre_ref, bim_ref, are_ref, aim_ref, xre_ref, xim_ref, cr_ref, ci_ref, *, TC, reverse):
    @pl.when(pl.program_id(1) == 0)
    def _():
        cr_ref[...] = jnp.zeros_like(cr_ref)
        ci_ref[...] = jnp.zeros_like(ci_ref)

    ar = are_ref[...]
    ai = aim_ref[...]

    def step(s, carry):
        xr, xi = carry
        tau = TC - 1 - s if reverse else s
        nxr = ar * xr - ai * xi + bre_ref[pl.ds(tau, 1), :]
        nxi = ar * xi + ai * xr + bim_ref[pl.ds(tau, 1), :]
        xre_ref[pl.ds(tau, 1), :] = nxr
        xim_ref[pl.ds(tau, 1), :] = nxi
        return nxr, nxi

    xr, xi = lax.fori_loop(0, TC, step, (cr_ref[0:1, :], ci_ref[0:1, :]), unroll=8)
    cr_ref[0:1, :] = xr
    ci_ref[0:1, :] = xi


def _scan_geometry(S, NCH):
    return _tile(S, 512), _tile(NCH, 512)


def _scan_fwd(bu_re, bu_im, a_re, a_im, *, reverse, name):
    S, NCH = bu_re.shape
    TC, LB = _scan_geometry(S, NCH)
    nT = S // TC
    tmap = (lambda l, t: (nT - 1 - t, l)) if reverse else (lambda l, t: (t, l))
    row = pl.BlockSpec((TC, LB), tmap)
    vec = pl.BlockSpec((1, LB), lambda l, t: (0, l))
    return _pcall(_scan_fwd_body, name=name, grid=(NCH // LB, nT), in_specs=[row, row, vec, vec],
                  out_specs=(row, row), out_shape=(jax.ShapeDtypeStruct((S, NCH), F32),) * 2,
                  scratch=[pltpu.VMEM((8, LB), F32), pltpu.VMEM((8, LB), F32)], sem=("parallel", "arbitrary"),
                  TC=TC, reverse=reverse)(bu_re, bu_im, a_re, a_im)


def _scan_bwd_body(gre_ref, gim_ref, xre_ref, xim_ref, are_ref, aim_ref, hre_ref, him_ref, dar_ref, dai_ref,
                   cr_ref, ci_ref, *, TC, reverse):
    @pl.when(pl.program_id(1) == 0)
    def _():
        cr_ref[...] = jnp.zeros_like(cr_ref)
        ci_ref[...] = jnp.zeros_like(ci_ref)
        dar_ref[...] = jnp.zeros_like(dar_ref)
        dai_ref[...] = jnp.zeros_like(dai_ref)

    ar = are_ref[...]
    ai = aim_ref[...]

    def step(s, carry):
        hr, hi, sr, si = carry
        tau = TC - 1 - s if reverse else s
        xr = xre_ref[pl.ds(tau, 1), :]
        xi = xim_ref[pl.ds(tau, 1), :]
        sr = sr + (hr * xr + hi * xi)
        si = si + (hi * xr - hr * xi)
        nhr = gre_ref[pl.ds(tau, 1), :] + (ar * hr + ai * hi)
        nhi = gim_ref[pl.ds(tau, 1), :] + (ar * hi - ai * hr)
        hre_ref[pl.ds(tau, 1), :] = nhr
        him_ref[pl.ds(tau, 1), :] = nhi
        return nhr, nhi, sr, si

    z = jnp.zeros_like(ar)
    hr, hi, sr, si = lax.fori_loop(0, TC, step, (cr_ref[0:1, :], ci_ref[0:1, :], z, z), unroll=8)
    cr_ref[0:1, :] = hr
    ci_ref[0:1, :] = hi
    dar_ref[...] += sr
    dai_ref[...] += si


def _scan_bwd(g_re, g_im, x_re, x_im, a_re, a_im, *, reverse, name):
    S, NCH = g_re.shape
    TC, LB = _scan_geometry(S, NCH)
    nT = S // TC
    back = not reverse
    tmap = (lambda l, t: (nT - 1 - t, l)) if back else (lambda l, t: (t, l))
    row = pl.BlockSpec((TC, LB), tmap)
    vec = pl.BlockSpec((1, LB), lambda l, t: (0, l))
    return _pcall(_scan_bwd_body, name=name, grid=(NCH // LB, nT), in_specs=[row, row, row, row, vec, vec],
                  out_specs=(row, row, vec, vec),
                  out_shape=(jax.ShapeDtypeStruct((S, NCH), F32),) * 2 + (jax.ShapeDtypeStruct((1, NCH), F32),) * 2,
                  scratch=[pltpu.VMEM((8, LB), F32), pltpu.VMEM((8, LB), F32)], sem=("parallel", "arbitrary"),
                  TC=TC, reverse=back)(g_re, g_im, x_re, x_im, a_re, a_im)


def _bf(ref):
    return ref[...].astype(BF16)


def _s5_fwd_body(u_ref, br_ref, bi_ref, cr_ref, cin_ref, are_ref, aim_ref, xre_ref, xim_ref, y_ref,
                 bre_s, bim_s, car_r, car_i, *, TC, reverse):
    @pl.when(pl.program_id(1) == 0)
    def _():
        car_r[...] = jnp.zeros_like(car_r)
        car_i[...] = jnp.zeros_like(car_i)

    ub = _bf(u_ref)
    bre_s[...] = lax.dot_general(ub, _bf(br_ref), NN, preferred_element_type=F32)
    bim_s[...] = lax.dot_general(ub, _bf(bi_ref), NN, preferred_element_type=F32)
    ar = are_ref[...]
    ai = aim_ref[...]

    def step(s, carry):
        xr, xi = carry
        tau = TC - 1 - s if reverse else s
        nxr = ar * xr - ai * xi + bre_s[pl.ds(tau, 1), :]
        nxi = ar * xi + ai * xr + bim_s[pl.ds(tau, 1), :]
        xre_ref[pl.ds(tau, 1), :] = nxr
        xim_ref[pl.ds(tau, 1), :] = nxi
        return nxr, nxi

    xr, xi = lax.fori_loop(0, TC, step, (car_r[0:1, :], car_i[0:1, :]), unroll=8)
    car_r[0:1, :] = xr
    car_i[0:1, :] = xi
    y_ref[...] = (lax.dot_general(_bf(xre_ref), _bf(cr_ref), NN, preferred_element_type=F32)
                  + lax.dot_general(_bf(xim_ref), _bf(cin_ref), NN, preferred_element_type=F32))


def _s5_block_specs(S, BW, reverse):
    NB = BW // LANES
    TC = _tile(S, 512)
    nT = S // TC
    tmap = (lambda l, t: (nT - 1 - t, l)) if reverse else (lambda l, t: (t, l))
    narrow = pl.BlockSpec((TC, LANES), tmap)
    wide = pl.BlockSpec((TC, STATE_PER_BLOCK), tmap)
    vec = pl.BlockSpec((1, STATE_PER_BLOCK), lambda l, t: (0, l))
    w_in = pl.BlockSpec((None, LANES, STATE_PER_BLOCK), lambda l, t: (l, 0, 0))
    w_out = pl.BlockSpec((None, STATE_PER_BLOCK, LANES), lambda l, t: (l, 0, 0))
    return NB, TC, nT, narrow, wide, vec, w_in, w_out


def _s5_scan_fwd(u, mat, *, reverse, name):
    a_re, a_im, b_r, b_i, c_r, c_in = mat
    S, BW = u.shape
    NB, TC, nT, narrow, wide, vec, w_in, w_out = _s5_block_specs(S, BW, reverse)
    state = jax.ShapeDtypeStruct((S, NB * STATE_PER_BLOCK), F32)
    return _pcall(_s5_fwd_body, name=name, grid=(NB, nT), in_specs=[narrow, w_in, w_in, w_out, w_out, vec, vec],
                  out_specs=(wide, wide, narrow), out_shape=(state, state, jax.ShapeDtypeStruct((S, BW), F32)),
                  scratch=[pltpu.VMEM((TC, STATE_PER_BLOCK), F32)] * 2 + [pltpu.VMEM((8, STATE_PER_BLOCK), F32)] * 2,
                  sem=("parallel", "arbitrary"), TC=TC, reverse=reverse)(u, b_r, b_i, c_r, c_in, a_re, a_im)


def _s5_bwd_body(dy_ref, u_ref, xre_ref, xim_ref, br_ref, bi_ref, cr_ref, cin_ref, are_ref, aim_ref,
                 du_ref, dar_ref, dai_ref, dbr_ref, dbi_ref, dcr_ref, dcin_ref, hre_s, him_s, car_r, car_i, *,
                 TC, reverse):
    @pl.when(pl.program_id(1) == 0)
    def _():
        for r in (car_r, car_i, dar_ref, dai_ref, dbr_ref, dbi_ref, dcr_ref, dcin_ref):
            r[...] = jnp.zeros_like(r)

    dyb = _bf(dy_ref)
    hre_s[...] = lax.dot_general(dyb, _bf(cr_ref), NT, preferred_element_type=F32)
    him_s[...] = lax.dot_general(dyb, _bf(cin_ref), NT, preferred_element_type=F32)
    dcr_ref[...] += lax.dot_general(_bf(xre_ref), dyb, TN, preferred_element_type=F32)
    dcin_ref[...] += lax.dot_general(_bf(xim_ref), dyb, TN, preferred_element_type=F32)
    ar = are_ref[...]
    ai = aim_ref[...]

    def step(s, carry):
        hr, hi, sr, si = carry
        tau = TC - 1 - s if reverse else s
        xr = xre_ref[pl.ds(tau, 1), :]
        xi = xim_ref[pl.ds(tau, 1), :]
        sr = sr + (hr * xr + hi * xi)
        si = si + (hi * xr - hr * xi)
        nhr = hre_s[pl.ds(tau, 1), :] + (ar * hr + ai * hi)
        nhi = him_s[pl.ds(tau, 1), :] + (ar * hi - ai * hr)
        hre_s[pl.ds(tau, 1), :] = nhr
        him_s[pl.ds(tau, 1), :] = nhi
        return nhr, nhi, sr, si

    z = jnp.zeros_like(ar)
    hr, hi, sr, si = lax.fori_loop(0, TC, step, (car_r[0:1, :], car_i[0:1, :], z, z), unroll=8)
    car_r[0:1, :] = hr
    car_i[0:1, :] = hi
    dar_ref[...] += sr
    dai_ref[...] += si
    hrb, hib, ub = _bf(hre_s), _bf(him_s), _bf(u_ref)
    du_ref[...] = (lax.dot_general(hrb, _bf(br_ref), NT, preferred_element_type=F32)
                   + lax.dot_general(hib, _bf(bi_ref), NT, preferred_element_type=F32))
    dbr_ref[...] += lax.dot_general(ub, hrb, TN, preferred_element_type=F32)
    dbi_ref[...] += lax.dot_general(ub, hib, TN, preferred_element_type=F32)


def _s5_scan_bwd(dy, u, x_re, x_im, mat, *, reverse, name):
    a_re, a_im, b_r, b_i, c_r, c_in = mat
    S, BW = u.shape
    NB, TC, nT, narrow, wide, vec, w_in, w_out = _s5_block_specs(S, BW, not reverse)
    shapes = (jax.ShapeDtypeStruct((S, BW), F32),) + tuple(jax.ShapeDtypeStruct(m.shape, F32) for m in mat)
    res = _pcall(_s5_bwd_body, name=name, grid=(NB, nT),
                 in_specs=[narrow, narrow, wide, wide, w_in, w_in, w_out, w_out, vec, vec],
                 out_specs=(narrow, vec, vec, w_in, w_in, w_out, w_out), out_shape=shapes,
                 scratch=[pltpu.VMEM((TC, STATE_PER_BLOCK), F32)] * 2 + [pltpu.VMEM((8, STATE_PER_BLOCK), F32)] * 2,
                 sem=("parallel", "arbitrary"), TC=TC, reverse=not reverse)(
        dy, u, x_re, x_im, b_r, b_i, c_r, c_in, a_re, a_im)
    return res[0], tuple(res[1:])


def _t5_bucket(rel):
    half = T5_BUCKETS // 2
    max_exact = half // 2
    n = jnp.abs(rel)
    nf = jnp.maximum(n, 1).astype(F32)
    large = max_exact + (jnp.log(nf / max_exact) / math.log(T5_MAX_DISTANCE / max_exact)
                         * (half - max_exact)).astype(jnp.int32)
    large = jnp.minimum(large, half - 1)
    return jnp.where(rel > 0, half, 0) + jnp.where(n < max_exact, n, large)


def _dil_bias(t5_bias, dil):
    W = A_QBLOCK + 2 * DIL_HALF
    off = jnp.arange(W)[None, :] - DIL_HALF - jnp.arange(A_QBLOCK)[:, None]
    pick = (_t5_bucket(off * dil)[..., None] == jnp.arange(T5_BUCKETS)).astype(F32)
    b = jnp.einsum('qkb,bh->hqk', pick, t5_bias.astype(F32), precision=lax.Precision.HIGHEST)
    return jnp.where(jnp.abs(off) <= DIL_HALF, b, NEG_INF)[:, None]


def _na_bias(rpb, rows):
    kr = min(NA_ROWS_MAX, rows)
    ro = (jnp.arange(kr)[None, :] - jnp.arange(kr)[:, None]) + NA_ROWS_MAX - 1
    c = jnp.arange(GRID_W)
    col_start = jnp.clip(c - NA_COLS // 2, 0, GRID_W - NA_COLS)
    col_ok = (c[None, :] >= col_start[:, None]) & (c[None, :] < col_start[:, None] + NA_COLS)
    co = jnp.clip(c[None, :] - c[:, None] + NA_COLS - 1, 0, 2 * NA_COLS - 2)
    pick_r = (ro[..., None] == jnp.arange(2 * NA_ROWS_MAX - 1)).astype(F32)
    pick_c = (co[..., None] == jnp.arange(2 * NA_COLS - 1)).astype(F32)
    b = jnp.einsum('hrqk,pjr->hpqjk',
                   jnp.einsum('hrc,qkc->hrqk', rpb.astype(F32), pick_c, precision=lax.Precision.HIGHEST),
                   pick_r, precision=lax.Precision.HIGHEST)
    b = jnp.where(col_ok[None, None, :, None, :], b, NEG_INF)
    return b.reshape(rpb.shape[0], kr, GRID_W, kr * GRID_W)


def _s5_mats(lam_re, lam_im, log_step, b_re, b_im, c_re, c_im):
    G, P, C = b_re.shape
    NB = G // GROUPS_PER_BLOCK
    eye = jnp.eye(GROUPS_PER_BLOCK, dtype=F32)

    def bd_in(bb):
        t = bb.reshape(NB, GROUPS_PER_BLOCK, P, C).transpose(0, 1, 3, 2)
        return jnp.einsum('jgcp,gh->jgchp', t, eye).reshape(NB, GROUPS_PER_BLOCK * C, GROUPS_PER_BLOCK * P)

    def bd_out(cc):
        t = cc.reshape(NB, GROUPS_PER_BLOCK, C, P).transpose(0, 1, 3, 2)
        return jnp.einsum('jgpc,gh->jgphc', t, eye).reshape(NB, GROUPS_PER_BLOCK * P, GROUPS_PER_BLOCK * C)

    out = []
    for d in range(2):
        step = jnp.exp(log_step[d].astype(F32))[:, None]
        lr = jnp.minimum(lam_re[d].astype(F32), -1e-4)
        li = lam_im[d].astype(F32)
        mag = jnp.exp(lr * step)
        ab_re = mag * jnp.cos(li * step)
        ab_im = mag * jnp.sin(li * step)
        den = lr * lr + li * li
        zr = ((ab_re - 1.0) * lr + ab_im * li) / den
        zi = (ab_im * lr - (ab_re - 1.0) * li) / den
        bb_re = zr[..., None] * b_re - zi[..., None] * b_im
        bb_im = zr[..., None] * b_im + zi[..., None] * b_re
        out.append((ab_re.reshape(1, G * P), ab_im.reshape(1, G * P), bd_in(bb_re), bd_in(bb_im),
                    bd_out(c_re[d].astype(F32)), bd_out(-c_im[d].astype(F32))))
    return tuple(out)


def _sigmoid(z):
    return 1.0 / (1.0 + jnp.exp(-z))


def _strided(t, dil, pad):
    S, C = t.shape
    t = t.reshape(S // dil, dil * C)
    return jnp.pad(t, ((DIL_HALF, DIL_HALF), (0, 0))) if pad else t


def _dilated_fwd(q, k, v, t5_bias):
    S, AW = q.shape
    H = AW // HEAD_DIM
    parts = []
    for _, dil in DILATED_BRANCHES:
        num, m, l = _wattn_fwd(_strided(q, dil, False), _strided(k, dil, True), _strided(v, dil, True),
                               _dil_bias(t5_bias, dil), kind="dil", H=H, dil=dil, qoff=0, koff=0, voff=0,
                               name=f"dilated{dil}_fwd")
        parts += [num.reshape(S, AW), m.reshape(S, AW), l.reshape(S, AW)]

    def merge(n1, m1, l1, n2, m2, l2, n3, m3, l3):
        mx = jnp.maximum(jnp.maximum(m1, m2), m3)
        w1, w2, w3 = jnp.exp(m1 - mx), jnp.exp(m2 - mx), jnp.exp(m3 - mx)
        den = w1 * l1 + w2 * l2 + w3 * l3
        o = (w1 * n1 + w2 * n2 + w3 * n3) / den
        return o, o, mx + jnp.log(den)

    return _ew(merge, parts, out_dtypes=[F32, BF16, F32], name="dilated_merge")


def _dilated_bwd(q, k, v, t5_bias, do, lse, delta):
    S, AW = q.shape
    H = AW // HEAD_DIM
    dqs, dks, dvs = [], [], []
    dt5 = jnp.zeros(t5_bias.shape, F32)
    for _, dil in DILATED_BRANCHES:
        bias, bias_vjp = jax.vjp(functools.partial(_dil_bias, dil=dil), t5_bias)
        dq, dk, dv, db = _wattn_bwd(_strided(q, dil, False), _strided(do, dil, False), _strided(lse, dil, False),
                                    _strided(delta, dil, False), _strided(k, dil, True), _strided(v, dil, True),
                                    bias, kind="dil", H=H, dil=dil, qoff=0, koff=0, voff=0, dq_dtype=F32,
                                    name=f"dilated{dil}_bwd")
        dqs.append(dq.reshape(S, AW))
        dks.append(dk[DIL_HALF:-DIL_HALF].reshape(S, AW))
        dvs.append(dv[DIL_HALF:-DIL_HALF].reshape(S, AW))
        dt5 = dt5 + bias_vjp(db)[0]
    add3 = lambda a, b, c: a + b + c
    return (_ew(add3, dqs, out_dtypes=[BF16], name="dilated_dq_sum")[0],
            _ew(add3, dks, out_dtypes=[BF16], name="dilated_dk_sum")[0],
            _ew(add3, dvs, out_dtypes=[BF16], name="dilated_dv_sum")[0], dt5)


def _s5_fwd(u, mats, d_skip, w_glu, j):
    xs, ys = [], []
    for d in range(2):
        x_re, x_im, y_d = _s5_scan_fwd(u, mats[d], reverse=(d == 1), name=f"s5_scan_fwd{d}")
        xs += [x_re, x_im]
        ys.append(y_d)

    def act(y0, y1, u, dsk):
        y = (y0 + y1) + dsk * u
        return y, _gelu(y)

    y, yg = _ew(act, ys + [u], [d_skip], out_dtypes=[F32, F32], name="s5_gelu")
    z = _mm_nn(yg, w_glu, j, "row", name="s5_glu_fwd")
    ob = _ew(lambda yg, z: yg * _sigmoid(z), [yg, z], out_dtypes=[BF16], name="s5_gate")[0]
    return ob, (xs, y, yg, z)


def _s5_bwd(dmerged, ob_off, u, mats, d_skip, w_glu, j, saved):
    xs, y, yg, z = saved
    BW = u.shape[1]
    NB = BW // LANES

    def gate_bwd(dob, yg, z):
        sg = _sigmoid(z)
        return dob * yg * (sg * (1.0 - sg)), dob * sg

    dz, dyg1 = _ew(gate_bwd, [(dmerged, ob_off), yg, z], out_dtypes=[BF16, F32], width=LANES, ncol=NB,
                   name="s5_gate_bwd")
    dw_glu = _mm_tn(yg, dz, w_glu.shape[0], "row", name="s5_glu_dw")
    dyg2 = _mm_nt(dz, w_glu, j, "row", name="s5_glu_dx")

    def act_bwd(d1, d2, y, u):
        dy = (d1 + d2) * _gelu_grad(y)
        return dy, jnp.sum(dy * u, axis=0, keepdims=True)

    dy, dd = _ew(act_bwd, [dyg1, dyg2, y, u], out_dtypes=[F32], n_acc=1, name="s5_gelu_bwd")
    dmats, dus = [], []
    for d in range(2):
        du_d, dmat = _s5_scan_bwd(dy, u, xs[2 * d], xs[2 * d + 1], mats[d], reverse=(d == 1),
                                  name=f"s5_scan_bwd{d}")
        dus.append(du_d)
        dmats.append(dmat)
    du = _ew(lambda dy, d0, d1, dsk: dy * dsk + (d0 + d1), [dy] + dus, [d_skip], out_dtypes=[BF16],
             name="s5_du_sum")[0]
    return du, tuple(dmats), dd, dw_glu


def _ab_fwd(x, j, P, W):
    t5 = P["t5_bias"]
    AW = t5.shape[1] * HEAD_DIM
    hn = _rms_fwd(x, P["norm_mix"][2 * j][None], name="rms_fwd")
    proj = _mm_nn(hn, W["ab_w_in"], j, "col", name="ab_in_fwd")
    q, k, v = (proj[:, i * AW:(i + 1) * AW].astype(BF16) for i in range(3))
    u = proj[:, 3 * AW:]
    oa32, oa16, lse = _dilated_fwd(q, k, v, t5)
    mats = _s5_mats(*(P[n][j] for n in _S5_PARAMS))
    ob, s5_saved = _s5_fwd(u, mats, P["s5_d"][j][None], W["s5_w_glu"], j)
    merged = jnp.concatenate([oa16, ob], axis=1)
    x1 = _mm_nn(merged, W["ab_w_out"], j, "row", mode="res", res=x, name="ab_out_fwd")
    return x1, (x, hn, q, k, v, u, oa32, lse, merged, s5_saved)


def _ab_bwd(dx1, j, P, W, saved):
    x, hn, q, k, v, u, oa32, lse, merged, s5_saved = saved
    t5 = P["t5_bias"]
    AW = t5.shape[1] * HEAD_DIM
    J = W["ab_w_in"].shape[0]
    dmerged = _mm_nt(dx1, W["ab_w_out"], j, "row", name="ab_out_dx")
    dw_out = _mm_tn(merged, dx1, J, "row", name="ab_out_dw")
    delta, do16 = _attn_delta(dmerged, 0, oa32, name="dilated_delta")
    dq, dk, dv, dt5 = _dilated_bwd(q, k, v, t5, do16, lse, delta)
    s5_params = tuple(P[n][j] for n in _S5_PARAMS)
    mats, mats_vjp = jax.vjp(_s5_mats, *s5_params)
    du, dmats, dd, dw_glu = _s5_bwd(dmerged, AW // LANES, u, mats, P["s5_d"][j][None], W["s5_w_glu"], j, s5_saved)
    ds5 = mats_vjp(dmats)
    dproj = jnp.concatenate([dq, dk, dv, du], axis=1)
    dw_in = _mm_tn(hn, dproj, J, "col", name="ab_in_dw")
    dhn = _mm_nt(dproj, W["ab_w_in"], j, "col", name="ab_in_dx")
    dx, dg = _rms_bwd(dhn, x, P["norm_mix"][2 * j][None], dx1, name="rms_bwd")
    small = dict(zip(_S5_PARAMS, ds5), s5_d=dd[0], t5_bias=dt5)
    return dx, dg[0], dict(ab_w_in=dw_in, ab_w_out=dw_out, s5_w_glu=dw_glu), small


def _c_fwd(x, j, P, W):
    H = P["c_rpb"].shape[1]
    hn = _rms_fwd(x, P["norm_mix"][2 * j + 1][None], name="rms_fwd")
    qkv = _mm_nn(hn, W["c_w_qkv"], j, "col", out_dtype=BF16, name="c_qkv_fwd")
    bias = _na_bias(P["c_rpb"][j], x.shape[0] // GRID_W)
    o, lse = _wattn_fwd(qkv, qkv, qkv, bias, kind="na", H=H, dil=1, qoff=0, koff=H, voff=2 * H, name="na_fwd")
    x1 = _mm_nn(o, W["c_w_out"], j, "row", mode="res", res=x, name="c_out_fwd")
    return x1, (x, hn, qkv, o, lse)


def _c_bwd(dx1, j, P, W, saved):
    x, hn, qkv, o, lse = saved
    H = P["c_rpb"].shape[1]
    J = W["c_w_qkv"].shape[0]
    do = _mm_nt(dx1, W["c_w_out"], j, "row", name="c_out_dx")
    dw_out = _mm_tn(o, dx1, J, "row", name="c_out_dw")
    delta, do16 = _attn_delta(do, 0, o, name="na_delta")
    bias, bias_vjp = jax.vjp(functools.partial(_na_bias, rows=x.shape[0] // GRID_W), P["c_rpb"][j])
    dq, dk, dv, db = _wattn_bwd(qkv, do16, lse, delta, qkv, qkv, bias, kind="na", H=H, dil=1, qoff=0, koff=H,
                                voff=2 * H, dq_dtype=BF16, name="na_bwd")
    dqkv = jnp.concatenate([dq, dk.astype(BF16), dv.astype(BF16)], axis=1)
    dw_qkv = _mm_tn(hn, dqkv, J, "col", name="c_qkv_dw")
    dhn = _mm_nt(dqkv, W["c_w_qkv"], j, "col", name="c_qkv_dx")
    dx, dg = _rms_bwd(dhn, x, P["norm_mix"][2 * j + 1][None], dx1, name="rms_bwd")
    return dx, dg[0], dict(c_w_qkv=dw_qkv, c_w_out=dw_out), dict(c_rpb=bias_vjp(db)[0])


def _mlp_fwd(x, i, P, W):
    hn = _rms_fwd(x, P["norm_mlp"][i][None], name="rms_fwd")
    a, hdn = _mm_nn(hn, W["mlp_w1"], i, "col", mode="relu2", name="mlp_w1_fwd")
    x2 = _mm_nn(hdn, W["mlp_w2"], i, "row", mode="res", res=x, name="mlp_w2_fwd")
    return x2, (x, hn, a, hdn)


def _mlp_bwd(dx2, i, P, W, saved):
    x, hn, a, hdn = saved
    J = W["mlp_w1"].shape[0]
    da = _mm_nt(dx2, W["mlp_w2"], i, "row", out_dtype=BF16, mode="dact", act=a, name="mlp_w2_dx")
    dw2 = _mm_tn(hdn, dx2, J, "row", name="mlp_w2_dw")
    dw1 = _mm_tn(hn, da, J, "col", name="mlp_w1_dw")
    dhn = _mm_nt(da, W["mlp_w1"], i, "col", name="mlp_w1_dx")
    dx, dg = _rms_bwd(dhn, x, P["norm_mlp"][i][None], dx2, name="rms_bwd")
    return dx, dg[0], dict(mlp_w1=dw1, mlp_w2=dw2)


_S5_PARAMS = ("s5_lam_re", "s5_lam_im", "s5_log_step", "s5_b_re", "s5_b_im", "s5_c_re", "s5_c_im")
_BIG = ("ab_w_in", "ab_w_out", "s5_w_glu", "c_w_qkv", "c_w_out", "mlp_w1", "mlp_w2")
_SMALL = ("t5_bias", "s5_lam_re", "s5_lam_im", "s5_log_step", "s5_b_re", "s5_b_im", "s5_c_re", "s5_c_im", "s5_d",
          "c_rpb", "norm_mix", "norm_mlp", "norm_final")
_WEIGHTS = ("t5_bias", "ab_w_in", "ab_w_out", "s5_lam_re", "s5_lam_im", "s5_log_step", "s5_b_re", "s5_b_im",
            "s5_c_re", "s5_c_im", "s5_d", "s5_w_glu", "c_w_qkv", "c_w_out", "c_rpb", "norm_mix", "norm_mlp",
            "mlp_w1", "mlp_w2", "norm_final")


def _local_grads(x, target, P, W):
    depth = P["norm_mix"].shape[0]
    saved = []
    h = x
    for i in range(depth):
        h, s_mix = (_ab_fwd if i % 2 == 0 else _c_fwd)(h, i // 2, P, W)
        h, s_mlp = _mlp_fwd(h, i, P, W)
        saved.append((s_mix, s_mlp))
    dh, dg_final, loss_cols = _loss_and_grad(h, P["norm_final"][None], target, name="loss_head")
    big = {n: [None] * P[n].shape[0] for n in _BIG}
    small = {n: jnp.zeros(P[n].shape, F32) for n in _SMALL}
    small["norm_final"] = dg_final[0]
    for i in reversed(range(depth)):
        s_mix, s_mlp = saved[i]
        j = i // 2
        dh, dg, dbig = _mlp_bwd(dh, i, P, W, s_mlp)
        small["norm_mlp"] = small["norm_mlp"].at[i].set(dg)
        for n, g in dbig.items():
            big[n][i] = g
        dh, dg, dbig, dsmall = (_ab_bwd if i % 2 == 0 else _c_bwd)(dh, j, P, W, s_mix)
        small["norm_mix"] = small["norm_mix"].at[i].set(dg)
        for n, g in dbig.items():
            big[n][j] = g
        for n, g in dsmall.items():
            if n == "t5_bias":
                small[n] = small[n] + g
            else:
                small[n] = small[n].at[j].set(g.reshape(P[n].shape[1:]))
    return loss_cols[0, 0], dh, big, small


def _place():
    x, y, c = lax.axis_index("x"), lax.axis_index("y"), lax.axis_index("c")
    return x, y, c, ((1 - x, y), (x, 1 - y), (1 - x, 1 - y))


def _comm_call(body, arrays, out_shape, sems, *, name, in_place=False, **static):
    hbm = pl.BlockSpec(memory_space=pltpu.HBM)
    return pl.pallas_call(
        functools.partial(body, n=len(arrays), **static), name=name, in_specs=[hbm] * len(arrays),
        out_specs=tuple([hbm] * len(out_shape)), out_shape=tuple(out_shape),
        input_output_aliases={t: t for t in range(len(arrays))} if in_place else {},
        scratch_shapes=[pltpu.SemaphoreType.DMA((k,)) for k in sems])(*arrays)


def _cast_body(s_ref, w_ref, o_ref):
    o_ref[...] = w_ref[...].astype(o_ref.dtype)


def _allgather_body(*refs, n):
    bufs = refs[n:2 * n]
    ici_send, ici_recv, pair_send, pair_recv = refs[2 * n:]
    x, y, c, chips = _place()
    slots = [2 * x + y] + [2 * px + py for px, py in chips]

    def part(t, slot, half):
        h = bufs[t].shape[1] // 2
        return bufs[t].at[slot, pl.ds(half * h, h)]

    def over_ici(t, p, slot):
        return pltpu.make_async_remote_copy(
            src_ref=part(t, slot, c), dst_ref=part(t, slot, c), send_sem=ici_send.at[3 * t + p],
            recv_sem=ici_recv.at[3 * t + p], device_id=(*chips[p], c), device_id_type=MESH)

    def to_pair(t, p, half):
        return pltpu.make_async_remote_copy(
            src_ref=part(t, slots[1 + p], half), dst_ref=part(t, slots[1 + p], half), send_sem=pair_send.at[3 * t + p],
            recv_sem=pair_recv.at[3 * t + p], device_id=(x, y, 1 - c), device_id_type=MESH)

    sends = [over_ici(t, p, slots[0]) for t in range(n) for p in range(3)]
    for cp in sends:
        cp.start()
    for t in range(n):
        for p in range(3):
            over_ici(t, p, slots[1 + p]).wait_recv()
            cp = to_pair(t, p, c)
            cp.start()
            sends.append(cp)
    for t in range(n):
        for p in range(3):
            to_pair(t, p, 1 - c).wait_recv()
    for cp in sends:
        cp.wait_send()


def _allgather(weights, place):
    bufs = []
    for w in weights:
        L, Kd, Nd = w.shape
        tr = _tile(Kd, 256)
        bufs.append(_sliced_call(
            _cast_body, place, [w], [lambda l, i, p: (l, i, 0)], [(None, tr, Nd)], (None, None, tr, Nd),
            lambda l, i, p: (p[1], l, i, 0), jax.ShapeDtypeStruct((4, L, Kd, Nd), BF16), (L, Kd // tr),
            name="weights_cast"))
    n = len(bufs)
    return _comm_call(_allgather_body, bufs, [jax.ShapeDtypeStruct(b.shape, b.dtype) for b in bufs],
                      (3 * n,) * 4, in_place=True, name="weights_allgather")


def _pair_exchange_body(*refs, n):
    ins, outs = refs[:n], refs[n:2 * n]
    send_sems, recv_sems = refs[2 * n:]
    x, y, c, _ = _place()
    cps = []
    for t in range(n):
        h = outs[t].shape[0]
        cps.append(pltpu.make_async_remote_copy(
            src_ref=ins[t].at[pl.ds((1 - c) * h, h)], dst_ref=outs[t], send_sem=send_sems.at[t],
            recv_sem=recv_sems.at[t], device_id=(x, y, 1 - c), device_id_type=MESH))
    for cp in cps:
        cp.start()
    for cp in cps:
        cp.wait()


def _chip_exchange_body(*refs, n):
    ins, outs = refs[:n], refs[n:2 * n]
    send_sems, recv_sems = refs[2 * n:]
    x, y, c, chips = _place()
    cps = []
    for t in range(n):
        h = ins[t].shape[0]
        for p, (px, py) in enumerate(chips):
            cps.append(pltpu.make_async_remote_copy(
                src_ref=ins[t].at[pl.ds(0, h), 2 * px + py], dst_ref=outs[t].at[p], send_sem=send_sems.at[3 * t + p],
                recv_sem=recv_sems.at[3 * t + p], device_id=(px, py, c), device_id_type=MESH))
    for cp in cps:
        cp.start()
    for cp in cps:
        cp.wait()


def _pair_share_body(*refs, n):
    bufs = refs[n:2 * n]
    send_sems, recv_sems = refs[2 * n:]
    x, y, c, _ = _place()
    sends, recvs = [], []
    for t in range(n):
        h = bufs[t].shape[0] // 2
        for half, group in ((c, sends), (1 - c, recvs)):
            rows = bufs[t].at[pl.ds(half * h, h)]
            group.append(pltpu.make_async_remote_copy(
                src_ref=rows, dst_ref=rows, send_sem=send_sems.at[t], recv_sem=recv_sems.at[t],
                device_id=(x, y, 1 - c), device_id_type=MESH))
    for cp in sends:
        cp.start()
    for cp in recvs:
        cp.wait_recv()
    for cp in sends:
        cp.wait_send()


def _allreduce_body(in_ref, out_ref, send_sems, recv_sems, local_sem, n):
    x, y, c, _ = _place()
    flip = lambda v, bit: 1 - v if bit else v
    peers = [(flip(x, k & 4), flip(y, k & 2), flip(c, k & 1)) for k in range(1, 8)]

    def remote(k, slot):
        return pltpu.make_async_remote_copy(
            src_ref=in_ref, dst_ref=out_ref.at[slot], send_sem=send_sems.at[k], recv_sem=recv_sems.at[k],
            device_id=peers[k], device_id_type=MESH)

    local = pltpu.make_async_copy(in_ref, out_ref.at[4 * x + 2 * y + c], local_sem.at[0])
    sends = [remote(k, 4 * x + 2 * y + c) for k in range(7)]
    local.start()
    for cp in sends:
        cp.start()
    for k, (px, py, pc) in enumerate(peers):
        remote(k, 4 * px + 2 * py + pc).wait_recv()
    for cp in sends:
        cp.wait_send()
    local.wait()


def _sliced_call(body, scalars, arrays, in_maps, blocks, out_block, out_map, out_shape, grid, *, name):
    grid_spec = pltpu.PrefetchScalarGridSpec(
        num_scalar_prefetch=1, grid=grid,
        in_specs=[pl.BlockSpec(b, m) for b, m in zip(blocks, in_maps)],
        out_specs=pl.BlockSpec(out_block, out_map))
    return pl.pallas_call(
        functools.partial(body), name=name, grid_spec=grid_spec, out_shape=out_shape,
        compiler_params=pltpu.CompilerParams(vmem_limit_bytes=V7X_VMEM_LIMIT_BYTES))(scalars, *arrays)


def _chip_sum_body(s_ref, g_ref, r_ref, o_ref):
    o_ref[...] = (g_ref[...] + r_ref[...]).astype(o_ref.dtype)


def _final_sum_body(s_ref, g_ref, r1_ref, a_ref, b_ref, c_ref, o_ref):
    o_ref[...] = (((g_ref[...] + r1_ref[...]) + a_ref[...].astype(F32)) + b_ref[...].astype(F32)) + c_ref[...].astype(F32)


def _reduce_big(stacks, place):
    n = len(stacks)
    half = [jax.ShapeDtypeStruct((s.shape[0] // 2,) + s.shape[1:], F32) for s in stacks]
    from_pair = _comm_call(_pair_exchange_body, stacks, half, (n, n), name="grads_pair_exchange")
    chip16 = []
    for s, r in zip(stacks, from_pair):
        h, J, Kd, Nd = r.shape
        tr = _tile(Kd, 256)
        blk = (None, None, tr, Nd)
        chip16.append(_sliced_call(
            _chip_sum_body, place, [s, r],
            [lambda l, j, i, p: (p[0] * h + l, j, i, 0), lambda l, j, i, p: (l, j, i, 0)], [blk, blk], blk,
            lambda l, j, i, p: (l, j, i, 0), jax.ShapeDtypeStruct(r.shape, BF16), (h, J, Kd // tr),
            name="grads_chip_sum"))
    recv = [jax.ShapeDtypeStruct((3, a.shape[0]) + a.shape[2:], BF16) for a in chip16]
    from_chips = _comm_call(_chip_exchange_body, chip16, recv, (3 * n, 3 * n), name="grads_chip_exchange")
    sums = []
    for s, r, f in zip(stacks, from_pair, from_chips):
        h, J, Kd, Nd = r.shape
        tr = _tile(Kd, 256)
        blk4, blk3 = (None, None, tr, Nd), (None, tr, Nd)
        mine = lambda l, i, p: (l, p[1], i, 0)
        sums.append(_sliced_call(
            _final_sum_body, place, [s, r, f, f, f],
            [lambda l, i, p: (p[0] * h + l, p[1], i, 0), mine] + [functools.partial(lambda l, i, p, q: (q, l, i, 0), q=q)
                                                                  for q in range(3)],
            [blk4, blk4, blk4, blk4, blk4], blk3, lambda l, i, p: (p[0] * h + l, i, 0),
            jax.ShapeDtypeStruct((2 * h, Kd, Nd), F32), (h, Kd // tr), name="grads_final_sum"))
    return _comm_call(_pair_share_body, sums, [jax.ShapeDtypeStruct(s.shape, F32) for s in sums], (n, n),
                      in_place=True, name="grads_pair_share")


def _allreduce_small(buf):
    gathered = pl.pallas_call(
        functools.partial(_allreduce_body, n=1), name="small_allgather",
        in_specs=[pl.BlockSpec(memory_space=pltpu.HBM)], out_specs=pl.BlockSpec(memory_space=pltpu.HBM),
        out_shape=jax.ShapeDtypeStruct((8,) + buf.shape, F32),
        scratch_shapes=[pltpu.SemaphoreType.DMA((7,)), pltpu.SemaphoreType.DMA((7,)),
                        pltpu.SemaphoreType.DMA((1,))])(buf)

    def total(*b):
        acc = b[0]
        for t in b[1:]:
            acc = acc + t
        return acc

    return _ew(total, [gathered[i] for i in range(8)], out_dtypes=[F32], name="small_sum")[0]


def _pack(parts):
    flat = jnp.concatenate([p.reshape(-1).astype(F32) for p in parts])
    rows = -(-flat.shape[0] // (8 * LANES)) * 8
    return jnp.pad(flat, (0, rows * LANES - flat.shape[0])).reshape(rows, LANES)


def _unpack(buf, shapes):
    flat = buf.reshape(-1)
    out, at = [], 0
    for s in shapes:
        size = math.prod(s)
        out.append(flat[at:at + size].reshape(s))
        at += size
    return out


_INPUTS = ("x",) + _WEIGHTS + ("loss_target",) + tuple("m_" + n for n in _WEIGHTS) + tuple("v_" + n for n in _WEIGHTS)


def kernel(x, t5_bias, ab_w_in, ab_w_out, s5_lam_re, s5_lam_im, s5_log_step, s5_b_re, s5_b_im, s5_c_re, s5_c_im,
           s5_d, s5_w_glu, c_w_qkv, c_w_out, c_rpb, norm_mix, norm_mlp, mlp_w1, mlp_w2, norm_final, loss_target,
           m_t5_bias, m_ab_w_in, m_ab_w_out, m_s5_lam_re, m_s5_lam_im, m_s5_log_step, m_s5_b_re, m_s5_b_im,
           m_s5_c_re, m_s5_c_im, m_s5_d, m_s5_w_glu, m_c_w_qkv, m_c_w_out, m_c_rpb, m_norm_mix, m_norm_mlp,
           m_mlp_w1, m_mlp_w2, m_norm_final, v_t5_bias, v_ab_w_in, v_ab_w_out, v_s5_lam_re, v_s5_lam_im,
           v_s5_log_step, v_s5_b_re, v_s5_b_im, v_s5_c_re, v_s5_c_im, v_s5_d, v_s5_w_glu, v_c_w_qkv, v_c_w_out,
           v_c_rpb, v_norm_mix, v_norm_mlp, v_mlp_w1, v_mlp_w2, v_norm_final):
    args = (x, t5_bias, ab_w_in, ab_w_out, s5_lam_re, s5_lam_im, s5_log_step, s5_b_re, s5_b_im, s5_c_re, s5_c_im,
            s5_d, s5_w_glu, c_w_qkv, c_w_out, c_rpb, norm_mix, norm_mlp, mlp_w1, mlp_w2, norm_final, loss_target,
            m_t5_bias, m_ab_w_in, m_ab_w_out, m_s5_lam_re, m_s5_lam_im, m_s5_log_step, m_s5_b_re, m_s5_b_im,
            m_s5_c_re, m_s5_c_im, m_s5_d, m_s5_w_glu, m_c_w_qkv, m_c_w_out, m_c_rpb, m_norm_mix, m_norm_mlp,
            m_mlp_w1, m_mlp_w2, m_norm_final, v_t5_bias, v_ab_w_in, v_ab_w_out, v_s5_lam_re, v_s5_lam_im,
            v_s5_log_step, v_s5_b_re, v_s5_b_im, v_s5_c_re, v_s5_c_im, v_s5_d, v_s5_w_glu, v_c_w_qkv, v_c_w_out,
            v_c_rpb, v_norm_mix, v_norm_mlp, v_mlp_w1, v_mlp_w2, v_norm_final)
    A = dict(zip(_INPUTS, args, strict=True))
    P = {n: A[n] for n in _WEIGHTS}
    place = jnp.stack([lax.axis_index("c"), 2 * lax.axis_index("x") + lax.axis_index("y")]).astype(jnp.int32)

    gathered = _allgather([P[n] for n in _BIG], place)
    W = dict(zip(_BIG, gathered))
    loss, dx, big, small = _local_grads(A["x"][0], A["loss_target"][0], P, W)

    stacks = [jnp.stack(big[n]) for n in _BIG]
    big_grads = dict(zip(_BIG, _reduce_big(stacks, place)))
    small_shapes = [P[n].shape for n in _SMALL] + [(1,)]
    reduced = _unpack(_allreduce_small(_pack([small[n] for n in _SMALL] + [loss.reshape(1)])), small_shapes)
    small_grads = dict(zip(_SMALL, reduced[:-1]))
    loss = reduced[-1][0]

    grads, delta, new_m, new_v = {}, {}, {}, {}
    for n in _BIG:
        g = big_grads[n]
        two_d = lambda t: t.reshape(-1, t.shape[-1])
        d, m, v = _adamw(two_d(P[n]), two_d(g), two_d(A["m_" + n]), two_d(A["v_" + n]), name="adamw")
        grads[n] = g
        delta[n], new_m[n], new_v[n] = (t.reshape(g.shape) for t in (d, m, v))
    d, m, v = _adamw(_pack([P[n] for n in _SMALL]), _pack([small_grads[n] for n in _SMALL]),
                     _pack([A["m_" + n] for n in _SMALL]), _pack([A["v_" + n] for n in _SMALL]), name="adamw_small")
    shapes = [P[n].shape for n in _SMALL]
    for n, dn, mn, vn in zip(_SMALL, _unpack(d, shapes), _unpack(m, shapes), _unpack(v, shapes)):
        grads[n] = small_grads[n]
        delta[n], new_m[n], new_v[n] = dn, mn, vn
    return (loss, dx[None], *[grads[n] for n in _WEIGHTS], *[delta[n] for n in _WEIGHTS],
            *[new_m[n] for n in _WEIGHTS], *[new_v[n] for n in _WEIGHTS])
```

```python
import functools
import math

import jax
import jax.numpy as jnp
from jax import lax
from jax.experimental import pallas as pl
from jax.experimental.pallas import tpu as pltpu

F32 = jnp.float32
BF16 = jnp.bfloat16

HEAD_DIM = 128
LANES = 128
DILATED_BRANCHES = ((128, 1), (512, 4), (2048, 16))
A_QBLOCK = 128
DIL_HALF = 64
B_GROUP = 16
B_STATE = 64
GROUPS_PER_BLOCK = LANES // B_GROUP
STATE_PER_BLOCK = GROUPS_PER_BLOCK * B_STATE
GRID_W = 64
NA_ROWS_MAX = 8
NA_COLS = 16
T5_BUCKETS = 32
T5_MAX_DISTANCE = 1024
RMS_EPS = 1e-6
NEG_INF = -1e30
ADAM_LR = 0.001
ADAM_B1 = 0.9
ADAM_B2 = 0.999
ADAM_EPS = 1e-08
ADAM_WD = 0.01
ADAM_STEP = 10
V7X_VMEM_LIMIT_BYTES = 56 * 1024 * 1024

NN = (((1,), (0,)), ((), ()))
NT = (((1,), (1,)), ((), ()))
TN = (((0,), (0,)), ((), ()))
MESH = pl.DeviceIdType.MESH


def _tile(n, pref):
    if n <= pref:
        return n
    for t in range(pref - pref % LANES, 0, -LANES):
        if n % t == 0:
            return t
    t = pref
    while t >= 8:
        if n % t == 0:
            return t
        t //= 2
    return n


def _pcall(body, *, name, grid, in_specs, out_specs, out_shape, scratch=(), sem=None, **static):
    params = dict(vmem_limit_bytes=V7X_VMEM_LIMIT_BYTES)
    if sem is not None:
        params["dimension_semantics"] = sem
    return pl.pallas_call(
        functools.partial(body, **static), name=name, grid=grid, in_specs=in_specs, out_specs=out_specs,
        out_shape=out_shape, scratch_shapes=list(scratch), compiler_params=pltpu.CompilerParams(**params))


def _mm_finish(acc, rest, mode):
    if mode == "plain":
        rest[0][...] = acc.astype(rest[0].dtype)
    elif mode == "res":
        rest[1][...] = (rest[0][...] + acc).astype(rest[1].dtype)
    elif mode == "relu2":
        rest[0][...] = acc
        r = jnp.maximum(acc, 0.0)
        rest[1][...] = (r * r).astype(rest[1].dtype)
    elif mode == "dact":
        rest[1][...] = (acc * (2.0 * jnp.maximum(rest[0][...], 0.0))).astype(rest[1].dtype)


def _mm_body(a_ref, b_ref, *rest, nk, dims, mode):
    prod = lax.dot_general(a_ref[...].astype(BF16), b_ref[...].astype(BF16), dims, preferred_element_type=F32)
    if nk == 1:
        _mm_finish(prod, rest, mode)
        return
    acc_ref = rest[-1]
    k = pl.program_id(2)

    @pl.when(k == 0)
    def _():
        acc_ref[...] = prod

    @pl.when(k > 0)
    def _():
        acc_ref[...] += prod

    @pl.when(k == nk - 1)
    def _():
        _mm_finish(acc_ref[...], rest, mode)


def _w_dims(w, split):
    J, _, Kd, Nd = w.shape
    return (J, Kd, J * Nd, Kd, Nd) if split == "col" else (J, J * Kd, Nd, Kd, Nd)


def _w_spec(split, layer, tk, tn, Kd, Nd, kn_of):
    kps, nps = Kd // tk, Nd // tn

    def index(*g):
        kb, nb = kn_of(*g)
        if split == "col":
            return nb // nps, layer, kb, nb % nps
        return kb // kps, layer, kb % kps, nb

    return pl.BlockSpec((None, None, tk, tn), index)


def _mm_nn(a, w, layer, split, *, name, out_dtype=F32, mode="plain", res=None):
    M = a.shape[0]
    J, K, N, Kd, Nd = _w_dims(w, split)
    tm, tn, tk = _tile(M, 1024), _tile(Nd, 1024), _tile(Kd, 2048)
    in_specs = [pl.BlockSpec((tm, tk), lambda i, j, k: (i, k)),
                _w_spec(split, layer, tk, tn, Kd, Nd, lambda i, j, k: (k, j))]
    args = [a, w]
    o_spec = pl.BlockSpec((tm, tn), lambda i, j, k: (i, j))
    if mode == "res":
        in_specs.append(o_spec)
        args.append(res)
    if mode == "relu2":
        out_shape = (jax.ShapeDtypeStruct((M, N), F32), jax.ShapeDtypeStruct((M, N), BF16))
        out_specs = (o_spec, o_spec)
    else:
        out_shape = jax.ShapeDtypeStruct((M, N), out_dtype)
        out_specs = o_spec
    return _pcall(_mm_body, name=name, grid=(M // tm, N // tn, K // tk), in_specs=in_specs, out_specs=out_specs,
                  out_shape=out_shape, scratch=[pltpu.VMEM((tm, tn), F32)],
                  sem=("parallel", "parallel", "arbitrary"), nk=K // tk, dims=NN, mode=mode)(*args)


def _mm_nt(a, w, layer, split, *, name, out_dtype=F32, mode="plain", act=None):
    M = a.shape[0]
    J, K, N, Kd, Nd = _w_dims(w, split)
    tm, tko, tc = _tile(M, 1024), _tile(Kd, 1024), _tile(Nd, 2048)
    in_specs = [pl.BlockSpec((tm, tc), lambda i, j, c: (i, c)),
                _w_spec(split, layer, tko, tc, Kd, Nd, lambda i, j, c: (j, c))]
    args = [a, w]
    o_spec = pl.BlockSpec((tm, tko), lambda i, j, c: (i, j))
    if mode == "dact":
        in_specs.append(o_spec)
        args.append(act)
    return _pcall(_mm_body, name=name, grid=(M // tm, K // tko, N // tc), in_specs=in_specs, out_specs=o_spec,
                  out_shape=jax.ShapeDtypeStruct((M, K), out_dtype), scratch=[pltpu.VMEM((tm, tko), F32)],
                  sem=("parallel", "parallel", "arbitrary"), nk=N // tc, dims=NT, mode=mode)(*args)


def _mm_tn(a, b, J, split, *, name):
    M, K = a.shape
    N = b.shape[1]
    Kd, Nd = (K, N // J) if split == "col" else (K // J, N)
    tk, tn, tc = _tile(Kd, 1024), _tile(Nd, 1024), _tile(M, 2048)
    kps, nps = Kd // tk, Nd // tn
    in_specs = [pl.BlockSpec((tc, tk), lambda i, j, c: (c, i)),
                pl.BlockSpec((tc, tn), lambda i, j, c: (c, j))]
    if split == "col":
        o_spec = pl.BlockSpec((None, tk, tn), lambda i, j, c: (j // nps, i, j % nps))
    else:
        o_spec = pl.BlockSpec((None, tk, tn), lambda i, j, c: (i // kps, i % kps, j))
    return _pcall(_mm_body, name=name, grid=(K // tk, N // tn, M // tc), in_specs=in_specs, out_specs=o_spec,
                  out_shape=jax.ShapeDtypeStruct((J, Kd, Nd), F32), scratch=[pltpu.VMEM((tk, tn), F32)],
                  sem=("parallel", "parallel", "arbitrary"), nk=M // tc, dims=TN, mode="plain")(a, b)


def _bmm_body(*refs, n, dims):
    out_ref = refs[2 * n]
    acc = None
    for i in range(n):
        p = lax.dot_general(refs[i][...].astype(BF16), refs[n + i][...].astype(BF16), dims,
                            preferred_element_type=F32)
        acc = p if acc is None else acc + p
    out_ref[...] = acc.astype(out_ref.dtype)


def _bmm_nn(a_list, b_list, *, name):
    M = a_list[0].shape[0]
    NB, Ka, No = b_list[0].shape
    tm = _tile(M, 512)
    n = len(a_list)
    in_specs = ([pl.BlockSpec((tm, Ka), lambda i, j: (i, j))] * n
                + [pl.BlockSpec((None, Ka, No), lambda i, j: (j, 0, 0))] * n)
    return _pcall(_bmm_body, name=name, grid=(M // tm, NB), in_specs=in_specs,
                  out_specs=pl.BlockSpec((tm, No), lambda i, j: (i, j)),
                  out_shape=jax.ShapeDtypeStruct((M, NB * No), F32), sem=("parallel", "parallel"),
                  n=n, dims=NN)(*a_list, *b_list)


def _bmm_nt(a_list, b_list, *, name):
    M = a_list[0].shape[0]
    NB, Ka, No = b_list[0].shape
    tm = _tile(M, 512)
    n = len(a_list)
    in_specs = ([pl.BlockSpec((tm, No), lambda i, j: (i, j))] * n
                + [pl.BlockSpec((None, Ka, No), lambda i, j: (j, 0, 0))] * n)
    return _pcall(_bmm_body, name=name, grid=(M // tm, NB), in_specs=in_specs,
                  out_specs=pl.BlockSpec((tm, Ka), lambda i, j: (i, j)),
                  out_shape=jax.ShapeDtypeStruct((M, NB * Ka), F32), sem=("parallel", "parallel"),
                  n=n, dims=NT)(*a_list, *b_list)


def _bmm_tn_body(a_ref, c_ref, out_ref):
    @pl.when(pl.program_id(1) == 0)
    def _():
        out_ref[...] = jnp.zeros_like(out_ref)

    out_ref[...] += lax.dot_general(a_ref[...].astype(BF16), c_ref[...].astype(BF16), TN,
                                    preferred_element_type=F32)


def _bmm_tn(a, c, NB, *, name):
    M = a.shape[0]
    Ka, No = a.shape[1] // NB, c.shape[1] // NB
    tm = _tile(M, 512)
    return _pcall(_bmm_tn_body, name=name, grid=(NB, M // tm),
                  in_specs=[pl.BlockSpec((tm, Ka), lambda j, m: (m, j)), pl.BlockSpec((tm, No), lambda j, m: (m, j))],
                  out_specs=pl.BlockSpec((None, Ka, No), lambda j, m: (j, 0, 0)),
                  out_shape=jax.ShapeDtypeStruct((NB, Ka, No), F32), sem=("parallel", "arbitrary"))(a, c)


def _ew_body(*refs, fn, n_in, n_out, n_acc):
    res = fn(*[r[...] for r in refs[:n_in]])
    if not isinstance(res, (tuple, list)):
        res = (res,)
    outs = refs[n_in:n_in + n_out]
    accs = refs[n_in + n_out:]
    for o, r in zip(outs, res[:n_out]):
        o[...] = r.astype(o.dtype)
    if n_acc:
        first = pl.program_id(1) == 0
        for a, r in zip(accs, res[n_out:]):
            @pl.when(first)
            def _(a=a):
                a[...] = jnp.zeros_like(a)

            a[...] += r


def _ew(fn, rows, vecs=(), *, out_dtypes=(), n_acc=0, width=None, ncol=1, tr=256, name):
    rows = [r if isinstance(r, tuple) else (r, 0) for r in rows]
    R = rows[0][0].shape[0]
    C = width if width is not None else rows[0][0].shape[1]
    tr = _tile(R, tr)
    in_specs = [pl.BlockSpec((tr, C), functools.partial(lambda j, i, off: (i, off + j), off=off)) for _, off in rows]
    in_specs += [pl.BlockSpec((1, C), lambda j, i: (0, j)) for _ in vecs]
    out_shape = [jax.ShapeDtypeStruct((R, ncol * C), dt) for dt in out_dtypes]
    out_specs = [pl.BlockSpec((tr, C), lambda j, i: (i, j)) for _ in out_dtypes]
    out_shape += [jax.ShapeDtypeStruct((1, ncol * C), F32)] * n_acc
    out_specs += [pl.BlockSpec((1, C), lambda j, i: (0, j))] * n_acc
    res = _pcall(_ew_body, name=name, grid=(ncol, R // tr), in_specs=in_specs, out_specs=tuple(out_specs),
                 out_shape=tuple(out_shape), sem=("parallel", "arbitrary" if n_acc else "parallel"),
                 fn=fn, n_in=len(rows) + len(vecs), n_out=len(out_dtypes), n_acc=n_acc)(
        *[a for a, _ in rows], *vecs)
    return res


def _rms_fwd(x, g, *, name):
    def fn(x, g):
        r = lax.rsqrt(jnp.mean(x * x, axis=1, keepdims=True) + RMS_EPS)
        return (x * r) * g

    return _ew(fn, [x], [g], out_dtypes=[BF16], name=name)[0]


def _rms_bwd(dh, x, g, dres, *, name):
    def fn(dh, x, dres, g):
        r = lax.rsqrt(jnp.mean(x * x, axis=1, keepdims=True) + RMS_EPS)
        y = x * r
        dy = dh * g
        dx = r * (dy - y * jnp.mean(dy * y, axis=1, keepdims=True))
        return dres + dx, jnp.sum(dh * y, axis=0, keepdims=True)

    return _ew(fn, [dh, x, dres], [g], out_dtypes=[F32], n_acc=1, name=name)


def _loss_and_grad(x, g, target, *, name):
    D = x.shape[1]

    def fn(x, t, g):
        r = lax.rsqrt(jnp.mean(x * x, axis=1, keepdims=True) + RMS_EPS)
        y = x * r
        diff = y * g - t
        dh = diff * (1.0 / D)
        dy = dh * g
        dx = r * (dy - y * jnp.mean(dy * y, axis=1, keepdims=True))
        loss = jnp.sum(jnp.sum(diff * diff, axis=1, keepdims=True), axis=0, keepdims=True) * (0.5 / D)
        return dx, jnp.sum(dh * y, axis=0, keepdims=True), jnp.broadcast_to(loss, (1, D))

    return _ew(fn, [x, target], [g], out_dtypes=[F32], n_acc=2, name=name)


def _gelu(y):
    c = math.sqrt(2.0 / math.pi)
    return 0.5 * y * (1.0 + jnp.tanh(c * (y + 0.044715 * (y * y * y))))


def _gelu_grad(y):
    c = math.sqrt(2.0 / math.pi)
    t = jnp.tanh(c * (y + 0.044715 * (y * y * y)))
    return 0.5 * (1.0 + t) + 0.5 * y * (1.0 - t * t) * (c * (1.0 + 3 * 0.044715 * (y * y)))


def _adamw(w, g, m, v, *, name):
    def fn(w, g, m, v):
        m2 = ADAM_B1 * m + (1.0 - ADAM_B1) * g
        v2 = ADAM_B2 * v + (1.0 - ADAM_B2) * (g * g)
        m_hat = m2 / (1.0 - ADAM_B1 ** ADAM_STEP)
        v_hat = v2 / (1.0 - ADAM_B2 ** ADAM_STEP)
        delta = -ADAM_LR * (m_hat / (jnp.sqrt(v_hat) + ADAM_EPS) + ADAM_WD * w)
        return delta, m2, v2

    return _ew(fn, [w, g, m, v], out_dtypes=[F32, F32, F32], name=name)


def _window(kind, blk, QB, rows):
    if kind == "dil":
        return pl.multiple_of(blk * QB, QB), 0
    kr = min(NA_ROWS_MAX, rows)
    rs = jnp.clip(blk - kr // 2, 0, rows - kr)
    return pl.multiple_of(rs * GRID_W, GRID_W), blk - rs


def _scores(q, kw, bias, kind, start, QB, W, L_valid, scale):
    s = lax.dot_general(q, kw, NT, preferred_element_type=F32) * scale + bias
    if kind == "dil":
        kp = start + lax.broadcasted_iota(jnp.int32, (QB, W), 1)
        s = jnp.where((kp >= DIL_HALF) & (kp < DIL_HALF + L_valid), s, NEG_INF)
    return s


def _wattn_fwd_body(q_ref, k_ref, v_ref, b_ref, *outs, QB, SUB, W, kind, L_valid, rows, scale):
    n = pl.program_id(1)
    wins = [_window(kind, n * SUB + i, QB, rows) for i in range(SUB)]
    sls = [slice(i * QB, (i + 1) * QB) for i in range(SUB)]
    ss = [_scores(q_ref[sl, :], k_ref[pl.ds(start, W), :], b_ref[pat], kind, start, QB, W, L_valid, scale)
          for sl, (start, pat) in zip(sls, wins)]
    ms = [jnp.max(s, axis=1, keepdims=True) for s in ss]
    es = [jnp.exp(s - m) for s, m in zip(ss, ms)]
    ls = [jnp.sum(e, axis=1, keepdims=True) for e in es]
    for sl, (start, _), m, e, l in zip(sls, wins, ms, es, ls):
        vw = v_ref[pl.ds(start, W), :]
        if kind == "na":
            p = (e * (1.0 / l)).astype(BF16)
            o = lax.dot_general(p, vw, NN, preferred_element_type=F32)
            outs[0][sl, :] = o.astype(outs[0].dtype)
            outs[1][sl, :] = jnp.broadcast_to(m + jnp.log(l), (QB, LANES))
        else:
            outs[0][sl, :] = lax.dot_general(e.astype(BF16), vw, NN, preferred_element_type=F32)
            outs[1][sl, :] = jnp.broadcast_to(m, (QB, LANES))
            outs[2][sl, :] = jnp.broadcast_to(l, (QB, LANES))


def _wattn_bwd_body(q_ref, do_ref, lse_ref, dl_ref, k_ref, v_ref, b_ref, dq_ref, dk_ref, dv_ref, db_ref, *,
                    QB, SUB, W, kind, L_valid, rows, scale, dgroup):
    cb = pl.program_id(0)
    n = pl.program_id(1)

    @pl.when(n == 0)
    def _():
        dk_ref[...] = jnp.zeros_like(dk_ref)
        dv_ref[...] = jnp.zeros_like(dv_ref)

    @pl.when((n == 0) & (cb % dgroup == 0))
    def _():
        db_ref[...] = jnp.zeros_like(db_ref)

    wins = [_window(kind, n * SUB + i, QB, rows) for i in range(SUB)]
    sls = [slice(i * QB, (i + 1) * QB) for i in range(SUB)]
    ss = [_scores(q_ref[sl, :], k_ref[pl.ds(start, W), :], b_ref[pat], kind, start, QB, W, L_valid, scale)
          for sl, (start, pat) in zip(sls, wins)]
    dps = [lax.dot_general(do_ref[sl, :], v_ref[pl.ds(start, W), :], NT, preferred_element_type=F32)
           for sl, (start, _) in zip(sls, wins)]
    ps = [jnp.exp(s - lse_ref[sl, :][:, :1]) for s, sl in zip(ss, sls)]
    dss = [p * (dp - dl_ref[sl, :][:, :1]) for p, dp, sl in zip(ps, dps, sls)]
    for sl, (start, pat), p, ds in zip(sls, wins, ps, dss):
        q = q_ref[sl, :]
        do = do_ref[sl, :]
        db_ref[pat] += ds
        dsb = ds.astype(BF16)
        dq_ref[sl, :] = (lax.dot_general(dsb, k_ref[pl.ds(start, W), :], NN, preferred_element_type=F32)
                         * scale).astype(dq_ref.dtype)
        dk_ref[pl.ds(start, W), :] += lax.dot_general(dsb, q, TN, preferred_element_type=F32) * scale
        dv_ref[pl.ds(start, W), :] += lax.dot_general(p.astype(BF16), do, TN, preferred_element_type=F32)


def _wattn_geometry(kind, LQ, dil):
    if kind == "dil":
        QB, W, rows = A_QBLOCK, A_QBLOCK + 2 * DIL_HALF, 0
    else:
        rows = LQ // GRID_W
        QB, W = GRID_W, min(NA_ROWS_MAX, rows) * GRID_W
    SUB = 4 if (LQ // QB) % 4 == 0 else 1
    return QB, W, rows, SUB


def _wattn_fwd(q, k, v, bias, *, kind, H, dil, qoff, koff, voff, name):
    LQ, LK = q.shape[0], k.shape[0]
    QB, W, rows, SUB = _wattn_geometry(kind, LQ, dil)
    ncb = H * dil
    NP = bias.shape[1]

    def col(cb, off):
        return off + (cb % dil) * H + cb // dil

    in_specs = [pl.BlockSpec((QB * SUB, LANES), lambda cb, n: (n, col(cb, qoff))),
                pl.BlockSpec((LK, LANES), lambda cb, n: (0, col(cb, koff))),
                pl.BlockSpec((LK, LANES), lambda cb, n: (0, col(cb, voff))),
                pl.BlockSpec((None, NP, QB, W), lambda cb, n: (cb // dil, 0, 0, 0))]
    o_spec = pl.BlockSpec((QB * SUB, LANES), lambda cb, n: (n, col(cb, 0)))
    shape = (LQ, ncb * LANES)
    if kind == "na":
        out_shape = (jax.ShapeDtypeStruct(shape, BF16), jax.ShapeDtypeStruct(shape, F32))
    else:
        out_shape = (jax.ShapeDtypeStruct(shape, F32),) * 3
    return _pcall(_wattn_fwd_body, name=name, grid=(ncb, LQ // (QB * SUB)), in_specs=in_specs,
                  out_specs=(o_spec,) * len(out_shape), out_shape=out_shape, sem=("parallel", "parallel"),
                  QB=QB, SUB=SUB, W=W, kind=kind, L_valid=LQ, rows=rows, scale=1.0 / math.sqrt(HEAD_DIM))(
        q, k, v, bias)


def _wattn_bwd(q, do, lse, delta, k, v, bias, *, kind, H, dil, qoff, koff, voff, dq_dtype, name):
    LQ, LK = q.shape[0], k.shape[0]
    QB, W, rows, SUB = _wattn_geometry(kind, LQ, dil)
    ncb = H * dil
    NP = bias.shape[1]

    def col(cb, off):
        return off + (cb % dil) * H + cb // dil

    q_spec = lambda off: pl.BlockSpec((QB * SUB, LANES), lambda cb, n: (n, col(cb, off)))
    kv_spec = lambda off: pl.BlockSpec((LK, LANES), lambda cb, n: (0, col(cb, off)))
    b_spec = pl.BlockSpec((None, NP, QB, W), lambda cb, n: (cb // dil, 0, 0, 0))
    in_specs = [q_spec(qoff), q_spec(0), q_spec(0), q_spec(0), kv_spec(koff), kv_spec(voff), b_spec]
    out_shape = (jax.ShapeDtypeStruct((LQ, ncb * LANES), dq_dtype), jax.ShapeDtypeStruct((LK, ncb * LANES), F32),
                 jax.ShapeDtypeStruct((LK, ncb * LANES), F32), jax.ShapeDtypeStruct(bias.shape, F32))
    out_specs = (q_spec(0), kv_spec(0), kv_spec(0), b_spec)
    return _pcall(_wattn_bwd_body, name=name, grid=(ncb, LQ // (QB * SUB)), in_specs=in_specs, out_specs=out_specs,
                  out_shape=out_shape, sem=("arbitrary", "arbitrary"),
                  QB=QB, SUB=SUB, W=W, kind=kind, L_valid=LQ, rows=rows, scale=1.0 / math.sqrt(HEAD_DIM),
                  dgroup=dil)(q, do, lse, delta, k, v, bias)


def _attn_delta(do, do_off, o, *, name):
    def fn(do, o):
        return jnp.broadcast_to(jnp.sum(do * o.astype(F32), axis=1, keepdims=True), do.shape), do

    return _ew(fn, [(do, do_off), o], out_dtypes=[F32, BF16], width=LANES, ncol=o.shape[1] // LANES, tr=512,
               name=name)


def _scan_fwd_body(bre_ref, bim_ref, are_ref, aim_ref, xre_ref, xim_ref, cr_ref, ci_ref, *, TC, reverse):
    @pl.when(pl.program_id(1) == 0)
    def _():
        cr_ref[...] = jnp.zeros_like(cr_ref)
        ci_ref[...] = jnp.zeros_like(ci_ref)

    ar = are_ref[...]
    ai = aim_ref[...]

    def step(s, carry):
        xr, xi = carry
        tau = TC - 1 - s if reverse else s
        nxr = ar * xr - ai * xi + bre_ref[pl.ds(tau, 1), :]
        nxi = ar * xi + ai * xr + bim_ref[pl.ds(tau, 1), :]
        xre_ref[pl.ds(tau, 1), :] = nxr
        xim_ref[pl.ds(tau, 1), :] = nxi
        return nxr, nxi

    xr, xi = lax.fori_loop(0, TC, step, (cr_ref[0:1, :], ci_ref[0:1, :]), unroll=8)
    cr_ref[0:1, :] = xr
    ci_ref[0:1, :] = xi


def _scan_geometry(S, NCH):
    return _tile(S, 512), _tile(NCH, 512)


def _scan_fwd(bu_re, bu_im, a_re, a_im, *, reverse, name):
    S, NCH = bu_re.shape
    TC, LB = _scan_geometry(S, NCH)
    nT = S // TC
    tmap = (lambda l, t: (nT - 1 - t, l)) if reverse else (lambda l, t: (t, l))
    row = pl.BlockSpec((TC, LB), tmap)
    vec = pl.BlockSpec((1, LB), lambda l, t: (0, l))
    return _pcall(_scan_fwd_body, name=name, grid=(NCH // LB, nT), in_specs=[row, row, vec, vec],
                  out_specs=(row, row), out_shape=(jax.ShapeDtypeStruct((S, NCH), F32),) * 2,
                  scratch=[pltpu.VMEM((8, LB), F32), pltpu.VMEM((8, LB), F32)], sem=("parallel", "arbitrary"),
                  TC=TC, reverse=reverse)(bu_re, bu_im, a_re, a_im)


def _scan_bwd_body(gre_ref, gim_ref, xre_ref, xim_ref, are_ref, aim_ref, hre_ref, him_ref, dar_ref, dai_ref,
                   cr_ref, ci_ref, *, TC, reverse):
    @pl.when(pl.program_id(1) == 0)
    def _():
        cr_ref[...] = jnp.zeros_like(cr_ref)
        ci_ref[...] = jnp.zeros_like(ci_ref)
        dar_ref[...] = jnp.zeros_like(dar_ref)
        dai_ref[...] = jnp.zeros_like(dai_ref)

    ar = are_ref[...]
    ai = aim_ref[...]

    def step(s, carry):
        hr, hi, sr, si = carry
        tau = TC - 1 - s if reverse else s
        xr = xre_ref[pl.ds(tau, 1), :]
        xi = xim_ref[pl.ds(tau, 1), :]
        sr = sr + (hr * xr + hi * xi)
        si = si + (hi * xr - hr * xi)
        nhr = gre_ref[pl.ds(tau, 1), :] + (ar * hr + ai * hi)
        nhi = gim_ref[pl.ds(tau, 1), :] + (ar * hi - ai * hr)
        hre_ref[pl.ds(tau, 1), :] = nhr
        him_ref[pl.ds(tau, 1), :] = nhi
        return nhr, nhi, sr, si

    z = jnp.zeros_like(ar)
    hr, hi, sr, si = lax.fori_loop(0, TC, step, (cr_ref[0:1, :], ci_ref[0:1, :], z, z), unroll=8)
    cr_ref[0:1, :] = hr
    ci_ref[0:1, :] = hi
    dar_ref[...] += sr
    dai_ref[...] += si


def _scan_bwd(g_re, g_im, x_re, x_im, a_re, a_im, *, reverse, name):
    S, NCH = g_re.shape
    TC, LB = _scan_geometry(S, NCH)
    nT = S // TC
    back = not reverse
    tmap = (lambda l, t: (nT - 1 - t, l)) if back else (lambda l, t: (t, l))
    row = pl.BlockSpec((TC, LB), tmap)
    vec = pl.BlockSpec((1, LB), lambda l, t: (0, l))
    return _pcall(_scan_bwd_body, name=name, grid=(NCH // LB, nT), in_specs=[row, row, row, row, vec, vec],
                  out_specs=(row, row, vec, vec),
                  out_shape=(jax.ShapeDtypeStruct((S, NCH), F32),) * 2 + (jax.ShapeDtypeStruct((1, NCH), F32),) * 2,
                  scratch=[pltpu.VMEM((8, LB), F32), pltpu.VMEM((8, LB), F32)], sem=("parallel", "arbitrary"),
                  TC=TC, reverse=back)(g_re, g_im, x_re, x_im, a_re, a_im)


def _bf(ref):
    return ref[...].astype(BF16)


def _s5_fwd_body(u_ref, br_ref, bi_ref, cr_ref, cin_ref, are_ref, aim_ref, xre_ref, xim_ref, y_ref,
                 bre_s, bim_s, car_r, car_i, *, TC, reverse):
    @pl.when(pl.program_id(1) == 0)
    def _():
        car_r[...] = jnp.zeros_like(car_r)
        car_i[...] = jnp.zeros_like(car_i)

    ub = _bf(u_ref)
    bre_s[...] = lax.dot_general(ub, _bf(br_ref), NN, preferred_element_type=F32)
    bim_s[...] = lax.dot_general(ub, _bf(bi_ref), NN, preferred_element_type=F32)
    ar = are_ref[...]
    ai = aim_ref[...]

    def step(s, carry):
        xr, xi = carry
        tau = TC - 1 - s if reverse else s
        nxr = ar * xr - ai * xi + bre_s[pl.ds(tau, 1), :]
        nxi = ar * xi + ai * xr + bim_s[pl.ds(tau, 1), :]
        xre_ref[pl.ds(tau, 1), :] = nxr
        xim_ref[pl.ds(tau, 1), :] = nxi
        return nxr, nxi

    xr, xi = lax.fori_loop(0, TC, step, (car_r[0:1, :], car_i[0:1, :]), unroll=8)
    car_r[0:1, :] = xr
    car_i[0:1, :] = xi
    y_ref[...] = (lax.dot_general(_bf(xre_ref), _bf(cr_ref), NN, preferred_element_type=F32)
                  + lax.dot_general(_bf(xim_ref), _bf(cin_ref), NN, preferred_element_type=F32))


def _s5_block_specs(S, BW, reverse):
    NB = BW // LANES
    TC = _tile(S, 512)
    nT = S // TC
    tmap = (lambda l, t: (nT - 1 - t, l)) if reverse else (lambda l, t: (t, l))
    narrow = pl.BlockSpec((TC, LANES), tmap)
    wide = pl.BlockSpec((TC, STATE_PER_BLOCK), tmap)
    vec = pl.BlockSpec((1, STATE_PER_BLOCK), lambda l, t: (0, l))
    w_in = pl.BlockSpec((None, LANES, STATE_PER_BLOCK), lambda l, t: (l, 0, 0))
    w_out = pl.BlockSpec((None, STATE_PER_BLOCK, LANES), lambda l, t: (l, 0, 0))
    return NB, TC, nT, narrow, wide, vec, w_in, w_out


def _s5_scan_fwd(u, mat, *, reverse, name):
    a_re, a_im, b_r, b_i, c_r, c_in = mat
    S, BW = u.shape
    NB, TC, nT, narrow, wide, vec, w_in, w_out = _s5_block_specs(S, BW, reverse)
    state = jax.ShapeDtypeStruct((S, NB * STATE_PER_BLOCK), F32)
    return _pcall(_s5_fwd_body, name=name, grid=(NB, nT), in_specs=[narrow, w_in, w_in, w_out, w_out, vec, vec],
                  out_specs=(wide, wide, narrow), out_shape=(state, state, jax.ShapeDtypeStruct((S, BW), F32)),
                  scratch=[pltpu.VMEM((TC, STATE_PER_BLOCK), F32)] * 2 + [pltpu.VMEM((8, STATE_PER_BLOCK), F32)] * 2,
                  sem=("parallel", "arbitrary"), TC=TC, reverse=reverse)(u, b_r, b_i, c_r, c_in, a_re, a_im)


def _s5_bwd_body(dy_ref, u_ref, xre_ref, xim_ref, br_ref, bi_ref, cr_ref, cin_ref, are_ref, aim_ref,
                 du_ref, dar_ref, dai_ref, dbr_ref, dbi_ref, dcr_ref, dcin_ref, hre_s, him_s, car_r, car_i, *,
                 TC, reverse):
    @pl.when(pl.program_id(1) == 0)
    def _():
        for r in (car_r, car_i, dar_ref, dai_ref, dbr_ref, dbi_ref, dcr_ref, dcin_ref):
            r[...] = jnp.zeros_like(r)

    dyb = _bf(dy_ref)
    hre_s[...] = lax.dot_general(dyb, _bf(cr_ref), NT, preferred_element_type=F32)
    him_s[...] = lax.dot_general(dyb, _bf(cin_ref), NT, preferred_element_type=F32)
    dcr_ref[...] += lax.dot_general(_bf(xre_ref), dyb, TN, preferred_element_type=F32)
    dcin_ref[...] += lax.dot_general(_bf(xim_ref), dyb, TN, preferred_element_type=F32)
    ar = are_ref[...]
    ai = aim_ref[...]

    def step(s, carry):
        hr, hi, sr, si = carry
        tau = TC - 1 - s if reverse else s
        xr = xre_ref[pl.ds(tau, 1), :]
        xi = xim_ref[pl.ds(tau, 1), :]
        sr = sr + (hr * xr + hi * xi)
        si = si + (hi * xr - hr * xi)
        nhr = hre_s[pl.ds(tau, 1), :] + (ar * hr + ai * hi)
        nhi = him_s[pl.ds(tau, 1), :] + (ar * hi - ai * hr)
        hre_s[pl.ds(tau, 1), :] = nhr
        him_s[pl.ds(tau, 1), :] = nhi
        return nhr, nhi, sr, si

    z = jnp.zeros_like(ar)
    hr, hi, sr, si = lax.fori_loop(0, TC, step, (car_r[0:1, :], car_i[0:1, :], z, z), unroll=8)
    car_r[0:1, :] = hr
    car_i[0:1, :] = hi
    dar_ref[...] += sr
    dai_ref[...] += si
    hrb, hib, ub = _bf(hre_s), _bf(him_s), _bf(u_ref)
    du_ref[...] = (lax.dot_general(hrb, _bf(br_ref), NT, preferred_element_type=F32)
                   + lax.dot_general(hib, _bf(bi_ref), NT, preferred_element_type=F32))
    dbr_ref[...] += lax.dot_general(ub, hrb, TN, preferred_element_type=F32)
    dbi_ref[...] += lax.dot_general(ub, hib, TN, preferred_element_type=F32)


def _s5_scan_bwd(dy, u, x_re, x_im, mat, *, reverse, name):
    a_re, a_im, b_r, b_i, c_r, c_in = mat
    S, BW = u.shape
    NB, TC, nT, narrow, wide, vec, w_in, w_out = _s5_block_specs(S, BW, not reverse)
    shapes = (jax.ShapeDtypeStruct((S, BW), F32),) + tuple(jax.ShapeDtypeStruct(m.shape, F32) for m in mat)
    res = _pcall(_s5_bwd_body, name=name, grid=(NB, nT),
                 in_specs=[narrow, narrow, wide, wide, w_in, w_in, w_out, w_out, vec, vec],
                 out_specs=(narrow, vec, vec, w_in, w_in, w_out, w_out), out_shape=shapes,
                 scratch=[pltpu.VMEM((TC, STATE_PER_BLOCK), F32)] * 2 + [pltpu.VMEM((8, STATE_PER_BLOCK), F32)] * 2,
                 sem=("parallel", "arbitrary"), TC=TC, reverse=not reverse)(
        dy, u, x_re, x_im, b_r, b_i, c_r, c_in, a_re, a_im)
    return res[0], tuple(res[1:])


SUBLANES = 8
QUADS = STATE_PER_BLOCK // LANES


def _cmul(ar, ai, br, bi):
    return ar * br - ai * bi, ar * bi + ai * br


def _seg_rows(tau, SEG, desc):
    return pl.ds(SEG - 1 - tau if desc else tau, SUBLANES, stride=SEG)


def _seg_carries(j, desc, end_r, end_i, a64r, a64i, car_r, car_i, cst_r, cst_i):
    cols = slice(j * LANES, (j + 1) * LANES)
    cr, ci = car_r[0:1, cols], car_i[0:1, cols]
    for i in (reversed(range(SUBLANES)) if desc else range(SUBLANES)):
        cst_r[j, i:i + 1, :] = cr
        cst_i[j, i:i + 1, :] = ci
        pr, pi = _cmul(a64r, a64i, cr, ci)
        cr, ci = pr + end_r[j, i:i + 1, :], pi + end_i[j, i:i + 1, :]
    car_r[0:1, cols] = cr
    car_i[0:1, cols] = ci


def _seg_correct(j, SEG, xr_ref, xi_ref, pr_ref, pi_ref, cst_r, cst_i):
    cols = slice(j * LANES, (j + 1) * LANES)
    pr, pi = pr_ref[:, cols], pi_ref[:, cols]
    for i in range(SUBLANES):
        rows = slice(i * SEG, (i + 1) * SEG)
        dr, di = _cmul(pr, pi, cst_r[j, i:i + 1, :], cst_i[j, i:i + 1, :])
        xr_ref[j, rows, :] += dr
        xi_ref[j, rows, :] += di


def _s5seg_fwd_body(u_ref, br_ref, bi_ref, cr_ref, cin_ref, are_ref, aim_ref, pr_ref, pi_ref, a64r_ref, a64i_ref,
                    xre_ref, xim_ref, y_ref, bre_s, bim_s, car_r, car_i, cst_r, cst_i, end_r, end_i, *, TC, desc):
    SEG = TC // SUBLANES

    @pl.when(pl.program_id(1) == 0)
    def _():
        car_r[...] = jnp.zeros_like(car_r)
        car_i[...] = jnp.zeros_like(car_i)

    ub = _bf(u_ref)
    cols = [slice(j * LANES, (j + 1) * LANES) for j in range(QUADS)]
    for j in range(QUADS):
        bre_s[j] = lax.dot_general(ub, br_ref[:, cols[j]].astype(BF16), NN, preferred_element_type=F32)
        bim_s[j] = lax.dot_general(ub, bi_ref[:, cols[j]].astype(BF16), NN, preferred_element_type=F32)
    ar = [are_ref[:, c] for c in cols]
    ai = [aim_ref[:, c] for c in cols]
    xr = [jnp.zeros((SUBLANES, LANES), F32)] * QUADS
    xi = [jnp.zeros((SUBLANES, LANES), F32)] * QUADS
    for tau in range(SEG):
        rows = _seg_rows(tau, SEG, desc)
        for j in range(QUADS):
            pr, pi = _cmul(ar[j], ai[j], xr[j], xi[j])
            xr[j] = pr + bre_s[j, rows, :]
            xi[j] = pi + bim_s[j, rows, :]
            xre_ref[j, rows, :] = xr[j]
            xim_ref[j, rows, :] = xi[j]
    y = None
    for j in range(QUADS):
        end_r[j] = xr[j]
        end_i[j] = xi[j]
        _seg_carries(j, desc, end_r, end_i, a64r_ref[:, cols[j]], a64i_ref[:, cols[j]], car_r, car_i, cst_r, cst_i)
        _seg_correct(j, SEG, xre_ref, xim_ref, pr_ref, pi_ref, cst_r, cst_i)
        part = (lax.dot_general(xre_ref[j].astype(BF16), cr_ref[cols[j], :].astype(BF16), NN, preferred_element_type=F32)
                + lax.dot_general(xim_ref[j].astype(BF16), cin_ref[cols[j], :].astype(BF16), NN, preferred_element_type=F32))
        y = part if y is None else y + part
    y_ref[...] = y


def _s5seg_specs(S, BW, desc):
    NB = BW // LANES
    TC = _tile(S, 512)
    nT = S // TC
    tmap = (lambda l, t: (nT - 1 - t, l)) if desc else (lambda l, t: (t, l))
    xmap = (lambda l, t: (l, nT - 1 - t, 0)) if desc else (lambda l, t: (l, t, 0))
    narrow = pl.BlockSpec((TC, LANES), tmap)
    state = pl.BlockSpec((QUADS, TC, LANES), xmap)
    vec = pl.BlockSpec((1, STATE_PER_BLOCK), lambda l, t: (0, l))
    tab = pl.BlockSpec((TC // SUBLANES, STATE_PER_BLOCK), lambda l, t: (0, l))
    w_in = pl.BlockSpec((None, LANES, STATE_PER_BLOCK), lambda l, t: (l, 0, 0))
    w_out = pl.BlockSpec((None, STATE_PER_BLOCK, LANES), lambda l, t: (l, 0, 0))
    scratch = ([pltpu.VMEM((QUADS, TC, LANES), F32)] * 2 + [pltpu.VMEM((SUBLANES, STATE_PER_BLOCK), F32)] * 2
               + [pltpu.VMEM((QUADS, SUBLANES, LANES), F32)] * 4)
    return NB, TC, nT, narrow, state, vec, tab, w_in, w_out, scratch


def _powers(ar, ai, n):
    pr, pi = jnp.ones_like(ar), jnp.zeros_like(ai)
    mr, mi = ar, ai
    while pr.shape[0] < n + 1:
        qr, qi = _cmul(mr, mi, pr, pi)
        pr, pi = jnp.concatenate([pr, qr]), jnp.concatenate([pi, qi])
        mr, mi = _cmul(mr, mi, mr, mi)
    return pr[:n + 1], pi[:n + 1]


def _seg_tables(a_re, a_im, SEG, desc):
    pr, pi = _powers(lax.stop_gradient(a_re), lax.stop_gradient(a_im), SEG)
    nat = (lambda t: t[::-1]) if desc else (lambda t: t)
    return nat(pr[1:]), nat(pi[1:]), pr[:-1], pi[:-1], pr[SEG:], pi[SEG:]


def _s5_scan_fwd(u, mat, *, reverse, name):
    a_re, a_im, b_r, b_i, c_r, c_in = mat
    S, BW = u.shape
    NB, TC, nT, narrow, state, vec, tab, w_in, w_out, scratch = _s5seg_specs(S, BW, reverse)
    p_re, p_im, _, _, a64r, a64i = _seg_tables(a_re, a_im, TC // SUBLANES, reverse)
    xs = jax.ShapeDtypeStruct((NB * QUADS, S, LANES), F32)
    return _pcall(_s5seg_fwd_body, name=name, grid=(NB, nT),
                  in_specs=[narrow, w_in, w_in, w_out, w_out, vec, vec, tab, tab, vec, vec],
                  out_specs=(state, state, narrow), out_shape=(xs, xs, jax.ShapeDtypeStruct((S, BW), F32)),
                  scratch=scratch, sem=("parallel", "arbitrary"), TC=TC, desc=reverse)(
        u, b_r, b_i, c_r, c_in, a_re, a_im, p_re, p_im, a64r, a64i)


def _s5seg_bwd_body(dy_ref, u_ref, xre_ref, xim_ref, br_ref, bi_ref, cr_ref, cin_ref, are_ref, aim_ref,
                    pr_ref, pi_ref, qr_ref, qi_ref, a64r_ref, a64i_ref,
                    du_ref, dar_ref, dai_ref, dbr_ref, dbi_ref, dcr_ref, dcin_ref,
                    hre_s, him_s, car_r, car_i, cst_r, cst_i, end_r, end_i, *, TC, desc):
    SEG = TC // SUBLANES

    @pl.when(pl.program_id(1) == 0)
    def _():
        for r in (car_r, car_i, dar_ref, dai_ref, dbr_ref, dbi_ref, dcr_ref, dcin_ref):
            r[...] = jnp.zeros_like(r)

    dyb = _bf(dy_ref)
    cols = [slice(j * LANES, (j + 1) * LANES) for j in range(QUADS)]
    for j in range(QUADS):
        hre_s[j] = lax.dot_general(dyb, cr_ref[cols[j], :].astype(BF16), NT, preferred_element_type=F32)
        him_s[j] = lax.dot_general(dyb, cin_ref[cols[j], :].astype(BF16), NT, preferred_element_type=F32)
        dcr_ref[cols[j], :] += lax.dot_general(xre_ref[j].astype(BF16), dyb, TN, preferred_element_type=F32)
        dcin_ref[cols[j], :] += lax.dot_general(xim_ref[j].astype(BF16), dyb, TN, preferred_element_type=F32)
    ar = [are_ref[:, c] for c in cols]
    ai = [aim_ref[:, c] for c in cols]
    zero = [jnp.zeros((SUBLANES, LANES), F32)] * QUADS
    hr, hi, dr, di, er, ei = (list(zero) for _ in range(6))
    for tau in range(SEG):
        rows = _seg_rows(tau, SEG, desc)
        for j in range(QUADS):
            xr, xi = xre_ref[j, rows, :], xim_ref[j, rows, :]
            qr, qi = qr_ref[tau:tau + 1, cols[j]], qi_ref[tau:tau + 1, cols[j]]
            dr[j] = dr[j] + (hr[j] * xr + hi[j] * xi)
            di[j] = di[j] + (hi[j] * xr - hr[j] * xi)
            er[j] = er[j] + (qr * xr + qi * xi)
            ei[j] = ei[j] + (qi * xr - qr * xi)
            pr, pi = _cmul(ar[j], ai[j], hr[j], hi[j])
            hr[j] = pr + hre_s[j, rows, :]
            hi[j] = pi + him_s[j, rows, :]
            hre_s[j, rows, :] = hr[j]
            him_s[j, rows, :] = hi[j]
    ub = _bf(u_ref)
    du = None
    for j in range(QUADS):
        end_r[j] = hr[j]
        end_i[j] = hi[j]
        _seg_carries(j, desc, end_r, end_i, a64r_ref[:, cols[j]], a64i_ref[:, cols[j]], car_r, car_i, cst_r, cst_i)
        fr, fi = _cmul(cst_r[j], cst_i[j], er[j], ei[j])
        dar_ref[:, cols[j]] += jnp.sum(dr[j] + fr, axis=0, keepdims=True)
        dai_ref[:, cols[j]] += jnp.sum(di[j] + fi, axis=0, keepdims=True)
        _seg_correct(j, SEG, hre_s, him_s, pr_ref, pi_ref, cst_r, cst_i)
        hrb, hib = hre_s[j].astype(BF16), him_s[j].astype(BF16)
        part = (lax.dot_general(hrb, br_ref[:, cols[j]].astype(BF16), NT, preferred_element_type=F32)
                + lax.dot_general(hib, bi_ref[:, cols[j]].astype(BF16), NT, preferred_element_type=F32))
        du = part if du is None else du + part
        dbr_ref[:, cols[j]] += lax.dot_general(ub, hrb, TN, preferred_element_type=F32)
        dbi_ref[:, cols[j]] += lax.dot_general(ub, hib, TN, preferred_element_type=F32)
    du_ref[...] = du


def _s5_scan_bwd(dy, u, x_re, x_im, mat, *, reverse, name):
    a_re, a_im, b_r, b_i, c_r, c_in = mat
    S, BW = u.shape
    desc = not reverse
    NB, TC, nT, narrow, state, vec, tab, w_in, w_out, scratch = _s5seg_specs(S, BW, desc)
    p_re, p_im, q_re, q_im, a64r, a64i = _seg_tables(a_re, -a_im, TC // SUBLANES, desc)
    shapes = (jax.ShapeDtypeStruct((S, BW), F32),) + tuple(jax.ShapeDtypeStruct(m.shape, F32) for m in mat)
    res = _pcall(_s5seg_bwd_body, name=name, grid=(NB, nT),
                 in_specs=[narrow, narrow, state, state, w_in, w_in, w_out, w_out, vec, vec, tab, tab, tab, tab, vec, vec],
                 out_specs=(narrow, vec, vec, w_in, w_in, w_out, w_out), out_shape=shapes,
                 scratch=scratch, sem=("parallel", "arbitrary"), TC=TC, desc=desc)(
        dy, u, x_re, x_im, b_r, b_i, c_r, c_in, a_re, -a_im, p_re, p_im, q_re, q_im, a64r, a64i)
    return res[0], tuple(res[1:])


def _interleave(t, inverse=False):
    S, C = t.shape
    TC = _tile(S, 512)
    a, b = (TC // SUBLANES, SUBLANES) if inverse else (SUBLANES, TC // SUBLANES)
    return t.reshape(S // TC, a, b, C).transpose(0, 2, 1, 3).reshape(S, C)


def _tile_carries(desc, end_r, end_i, a64r, a64i, car_r, car_i, cst_r, cst_i):
    cr, ci = car_r[0:1, :], car_i[0:1, :]
    for i in (reversed(range(SUBLANES)) if desc else range(SUBLANES)):
        cst_r[i:i + 1, :] = cr
        cst_i[i:i + 1, :] = ci
        pr, pi = _cmul(a64r, a64i, cr, ci)
        cr, ci = pr + end_r[i:i + 1, :], pi + end_i[i:i + 1, :]
    car_r[0:1, :] = cr
    car_i[0:1, :] = ci


def _tile_correct(SEG, xr_ref, xi_ref, pr_ref, pi_ref, cst_r, cst_i):
    cr, ci = cst_r[...], cst_i[...]
    for t in range(SEG):
        rows = pl.ds(SUBLANES * t, SUBLANES)
        dr, di = _cmul(pr_ref[t:t + 1, :], pi_ref[t:t + 1, :], cr, ci)
        xr_ref[rows, :] += dr
        xi_ref[rows, :] += di


def _s5il_fwd_body(u_ref, br_ref, bi_ref, cr_ref, cin_ref, are_ref, aim_ref, pr_ref, pi_ref, a64r_ref, a64i_ref,
                   xre_ref, xim_ref, y_ref, bre_s, bim_s, car_r, car_i, cst_r, cst_i, end_r, end_i, *, TC, desc):
    SEG = TC // SUBLANES

    @pl.when(pl.program_id(1) == 0)
    def _():
        car_r[...] = jnp.zeros_like(car_r)
        car_i[...] = jnp.zeros_like(car_i)

    ub = _bf(u_ref)
    bre_s[...] = lax.dot_general(ub, _bf(br_ref), NN, preferred_element_type=F32)
    bim_s[...] = lax.dot_general(ub, _bf(bi_ref), NN, preferred_element_type=F32)
    ar, ai = are_ref[...], aim_ref[...]
    xr = xi = jnp.zeros((SUBLANES, STATE_PER_BLOCK), F32)
    for tau in range(SEG):
        rows = pl.ds(SUBLANES * (SEG - 1 - tau if desc else tau), SUBLANES)
        pr, pi = _cmul(ar, ai, xr, xi)
        xr = pr + bre_s[rows, :]
        xi = pi + bim_s[rows, :]
        xre_ref[rows, :] = xr
        xim_ref[rows, :] = xi
    end_r[...] = xr
    end_i[...] = xi
    _tile_carries(desc, end_r, end_i, a64r_ref[...], a64i_ref[...], car_r, car_i, cst_r, cst_i)
    _tile_correct(SEG, xre_ref, xim_ref, pr_ref, pi_ref, cst_r, cst_i)
    y_ref[...] = (lax.dot_general(_bf(xre_ref), _bf(cr_ref), NN, preferred_element_type=F32)
                  + lax.dot_general(_bf(xim_ref), _bf(cin_ref), NN, preferred_element_type=F32))


def _s5il_specs(S, BW, desc):
    NB = BW // LANES
    TC = _tile(S, 512)
    nT = S // TC
    tmap = (lambda l, t: (nT - 1 - t, l)) if desc else (lambda l, t: (t, l))
    narrow = pl.BlockSpec((TC, LANES), tmap)
    wide = pl.BlockSpec((TC, STATE_PER_BLOCK), tmap)
    vec = pl.BlockSpec((1, STATE_PER_BLOCK), lambda l, t: (0, l))
    tab = pl.BlockSpec((TC // SUBLANES, STATE_PER_BLOCK), lambda l, t: (0, l))
    w_in = pl.BlockSpec((None, LANES, STATE_PER_BLOCK), lambda l, t: (l, 0, 0))
    w_out = pl.BlockSpec((None, STATE_PER_BLOCK, LANES), lambda l, t: (l, 0, 0))
    scratch = [pltpu.VMEM((TC, STATE_PER_BLOCK), F32)] * 2 + [pltpu.VMEM((SUBLANES, STATE_PER_BLOCK), F32)] * 6
    return NB, TC, nT, narrow, wide, vec, tab, w_in, w_out, scratch


def _s5_scan_fwd(u, mat, *, reverse, name):
    a_re, a_im, b_r, b_i, c_r, c_in = mat
    S, BW = u.shape
    NB, TC, nT, narrow, wide, vec, tab, w_in, w_out, scratch = _s5il_specs(S, BW, reverse)
    p_re, p_im, _, _, a64r, a64i = _seg_tables(a_re, a_im, TC // SUBLANES, reverse)
    xs = jax.ShapeDtypeStruct((S, NB * STATE_PER_BLOCK), F32)
    return _pcall(_s5il_fwd_body, name=name, grid=(NB, nT),
                  in_specs=[narrow, w_in, w_in, w_out, w_out, vec, vec, tab, tab, vec, vec],
                  out_specs=(wide, wide, narrow), out_shape=(xs, xs, jax.ShapeDtypeStruct((S, BW), F32)),
                  scratch=scratch, sem=("parallel", "arbitrary"), TC=TC, desc=reverse)(
        u, b_r, b_i, c_r, c_in, a_re, a_im, p_re, p_im, a64r, a64i)


def _s5il_bwd_body(dy_ref, u_ref, xre_ref, xim_ref, br_ref, bi_ref, cr_ref, cin_ref, are_ref, aim_ref,
                   pr_ref, pi_ref, qr_ref, qi_ref, a64r_ref, a64i_ref,
                   du_ref, dar_ref, dai_ref, dbr_ref, dbi_ref, dcr_ref, dcin_ref,
                   hre_s, him_s, car_r, car_i, cst_r, cst_i, end_r, end_i, *, TC, desc):
    SEG = TC // SUBLANES

    @pl.when(pl.program_id(1) == 0)
    def _():
        for r in (car_r, car_i, dar_ref, dai_ref, dbr_ref, dbi_ref, dcr_ref, dcin_ref):
            r[...] = jnp.zeros_like(r)

    dyb = _bf(dy_ref)
    hre_s[...] = lax.dot_general(dyb, _bf(cr_ref), NT, preferred_element_type=F32)
    him_s[...] = lax.dot_general(dyb, _bf(cin_ref), NT, preferred_element_type=F32)
    dcr_ref[...] += lax.dot_general(_bf(xre_ref), dyb, TN, preferred_element_type=F32)
    dcin_ref[...] += lax.dot_general(_bf(xim_ref), dyb, TN, preferred_element_type=F32)
    ar, ai = are_ref[...], aim_ref[...]
    hr = hi = dr = di = er = ei = jnp.zeros((SUBLANES, STATE_PER_BLOCK), F32)
    for tau in range(SEG):
        rows = pl.ds(SUBLANES * (SEG - 1 - tau if desc else tau), SUBLANES)
        xr, xi = xre_ref[rows, :], xim_ref[rows, :]
        qr, qi = qr_ref[tau:tau + 1, :], qi_ref[tau:tau + 1, :]
        dr = dr + (hr * xr + hi * xi)
        di = di + (hi * xr - hr * xi)
        er = er + (qr * xr + qi * xi)
        ei = ei + (qi * xr - qr * xi)
        pr, pi = _cmul(ar, ai, hr, hi)
        hr = pr + hre_s[rows, :]
        hi = pi + him_s[rows, :]
        hre_s[rows, :] = hr
        him_s[rows, :] = hi
    end_r[...] = hr
    end_i[...] = hi
    _tile_carries(desc, end_r, end_i, a64r_ref[...], a64i_ref[...], car_r, car_i, cst_r, cst_i)
    fr, fi = _cmul(cst_r[...], cst_i[...], er, ei)
    dar_ref[...] += jnp.sum(dr + fr, axis=0, keepdims=True)
    dai_ref[...] += jnp.sum(di + fi, axis=0, keepdims=True)
    _tile_correct(SEG, hre_s, him_s, pr_ref, pi_ref, cst_r, cst_i)
    hrb, hib, ub = _bf(hre_s), _bf(him_s), _bf(u_ref)
    du_ref[...] = (lax.dot_general(hrb, _bf(br_ref), NT, preferred_element_type=F32)
                   + lax.dot_general(hib, _bf(bi_ref), NT, preferred_element_type=F32))
    dbr_ref[...] += lax.dot_general(ub, hrb, TN, preferred_element_type=F32)
    dbi_ref[...] += lax.dot_general(ub, hib, TN, preferred_element_type=F32)


def _s5_scan_bwd(dy, u, x_re, x_im, mat, *, reverse, name):
    a_re, a_im, b_r, b_i, c_r, c_in = mat
    S, BW = u.shape
    desc = not reverse
    NB, TC, nT, narrow, wide, vec, tab, w_in, w_out, scratch = _s5il_specs(S, BW, desc)
    p_re, p_im, q_re, q_im, a64r, a64i = _seg_tables(a_re, -a_im, TC // SUBLANES, desc)
    shapes = (jax.ShapeDtypeStruct((S, BW), F32),) + tuple(jax.ShapeDtypeStruct(m.shape, F32) for m in mat)
    res = _pcall(_s5il_bwd_body, name=name, grid=(NB, nT),
                 in_specs=[narrow, narrow, wide, wide, w_in, w_in, w_out, w_out, vec, vec, tab, tab, tab, tab, vec, vec],
                 out_specs=(narrow, vec, vec, w_in, w_in, w_out, w_out), out_shape=shapes,
                 scratch=scratch, sem=("parallel", "arbitrary"), TC=TC, desc=desc)(
        dy, u, x_re, x_im, b_r, b_i, c_r, c_in, a_re, -a_im, p_re, p_im, q_re, q_im, a64r, a64i)
    return res[0], tuple(res[1:])


def _t5_bucket(rel):
    half = T5_BUCKETS // 2
    max_exact = half // 2
    n = jnp.abs(rel)
    nf = jnp.maximum(n, 1).astype(F32)
    large = max_exact + (jnp.log(nf / max_exact) / math.log(T5_MAX_DISTANCE / max_exact)
                         * (half - max_exact)).astype(jnp.int32)
    large = jnp.minimum(large, half - 1)
    return jnp.where(rel > 0, half, 0) + jnp.where(n < max_exact, n, large)


def _dil_bias(t5_bias, dil):
    W = A_QBLOCK + 2 * DIL_HALF
    off = jnp.arange(W)[None, :] - DIL_HALF - jnp.arange(A_QBLOCK)[:, None]
    pick = (_t5_bucket(off * dil)[..., None] == jnp.arange(T5_BUCKETS)).astype(F32)
    b = jnp.einsum('qkb,bh->hqk', pick, t5_bias.astype(F32), precision=lax.Precision.HIGHEST)
    return jnp.where(jnp.abs(off) <= DIL_HALF, b, NEG_INF)[:, None]


def _na_bias(rpb, rows):
    kr = min(NA_ROWS_MAX, rows)
    ro = (jnp.arange(kr)[None, :] - jnp.arange(kr)[:, None]) + NA_ROWS_MAX - 1
    c = jnp.arange(GRID_W)
    col_start = jnp.clip(c - NA_COLS // 2, 0, GRID_W - NA_COLS)
    col_ok = (c[None, :] >= col_start[:, None]) & (c[None, :] < col_start[:, None] + NA_COLS)
    co = jnp.clip(c[None, :] - c[:, None] + NA_COLS - 1, 0, 2 * NA_COLS - 2)
    pick_r = (ro[..., None] == jnp.arange(2 * NA_ROWS_MAX - 1)).astype(F32)
    pick_c = (co[..., None] == jnp.arange(2 * NA_COLS - 1)).astype(F32)
    b = jnp.einsum('hrqk,pjr->hpqjk',
                   jnp.einsum('hrc,qkc->hrqk', rpb.astype(F32), pick_c, precision=lax.Precision.HIGHEST),
                   pick_r, precision=lax.Precision.HIGHEST)
    b = jnp.where(col_ok[None, None, :, None, :], b, NEG_INF)
    return b.reshape(rpb.shape[0], kr, GRID_W, kr * GRID_W)


def _s5_mats(lam_re, lam_im, log_step, b_re, b_im, c_re, c_im):
    G, P, C = b_re.shape
    NB = G // GROUPS_PER_BLOCK
    eye = jnp.eye(GROUPS_PER_BLOCK, dtype=F32)

    def bd_in(bb):
        t = bb.reshape(NB, GROUPS_PER_BLOCK, P, C).transpose(0, 1, 3, 2)
        return jnp.einsum('jgcp,gh->jgchp', t, eye).reshape(NB, GROUPS_PER_BLOCK * C, GROUPS_PER_BLOCK * P)

    def bd_out(cc):
        t = cc.reshape(NB, GROUPS_PER_BLOCK, C, P).transpose(0, 1, 3, 2)
        return jnp.einsum('jgpc,gh->jgphc', t, eye).reshape(NB, GROUPS_PER_BLOCK * P, GROUPS_PER_BLOCK * C)

    out = []
    for d in range(2):
        step = jnp.exp(log_step[d].astype(F32))[:, None]
        lr = jnp.minimum(lam_re[d].astype(F32), -1e-4)
        li = lam_im[d].astype(F32)
        mag = jnp.exp(lr * step)
        ab_re = mag * jnp.cos(li * step)
        ab_im = mag * jnp.sin(li * step)
        den = lr * lr + li * li
        zr = ((ab_re - 1.0) * lr + ab_im * li) / den
        zi = (ab_im * lr - (ab_re - 1.0) * li) / den
        bb_re = zr[..., None] * b_re - zi[..., None] * b_im
        bb_im = zr[..., None] * b_im + zi[..., None] * b_re
        out.append((ab_re.reshape(1, G * P), ab_im.reshape(1, G * P), bd_in(bb_re), bd_in(bb_im),
                    bd_out(c_re[d].astype(F32)), bd_out(-c_im[d].astype(F32))))
    return tuple(out)


def _sigmoid(z):
    return 1.0 / (1.0 + jnp.exp(-z))


def _strided(t, dil, pad):
    S, C = t.shape
    t = t.reshape(S // dil, dil * C)
    return jnp.pad(t, ((DIL_HALF, DIL_HALF), (0, 0))) if pad else t


def _dilated_fwd(q, k, v, t5_bias):
    S, AW = q.shape
    H = AW // HEAD_DIM
    parts = []
    for _, dil in DILATED_BRANCHES:
        num, m, l = _wattn_fwd(_strided(q, dil, False), _strided(k, dil, True), _strided(v, dil, True),
                               _dil_bias(t5_bias, dil), kind="dil", H=H, dil=dil, qoff=0, koff=0, voff=0,
                               name=f"dilated{dil}_fwd")
        parts += [num.reshape(S, AW), m.reshape(S, AW), l.reshape(S, AW)]

    def merge(n1, m1, l1, n2, m2, l2, n3, m3, l3):
        mx = jnp.maximum(jnp.maximum(m1, m2), m3)
        w1, w2, w3 = jnp.exp(m1 - mx), jnp.exp(m2 - mx), jnp.exp(m3 - mx)
        den = w1 * l1 + w2 * l2 + w3 * l3
        o = (w1 * n1 + w2 * n2 + w3 * n3) / den
        return o, o, mx + jnp.log(den)

    return _ew(merge, parts, out_dtypes=[F32, BF16, F32], name="dilated_merge")


def _dilated_bwd(q, k, v, t5_bias, do, lse, delta):
    S, AW = q.shape
    H = AW // HEAD_DIM
    dqs, dks, dvs = [], [], []
    dt5 = jnp.zeros(t5_bias.shape, F32)
    for _, dil in DILATED_BRANCHES:
        bias, bias_vjp = jax.vjp(functools.partial(_dil_bias, dil=dil), t5_bias)
        dq, dk, dv, db = _wattn_bwd(_strided(q, dil, False), _strided(do, dil, False), _strided(lse, dil, False),
                                    _strided(delta, dil, False), _strided(k, dil, True), _strided(v, dil, True),
                                    bias, kind="dil", H=H, dil=dil, qoff=0, koff=0, voff=0, dq_dtype=F32,
                                    name=f"dilated{dil}_bwd")
        dqs.append(dq.reshape(S, AW))
        dks.append(dk[DIL_HALF:-DIL_HALF].reshape(S, AW))
        dvs.append(dv[DIL_HALF:-DIL_HALF].reshape(S, AW))
        dt5 = dt5 + bias_vjp(db)[0]
    add3 = lambda a, b, c: a + b + c
    return (_ew(add3, dqs, out_dtypes=[BF16], name="dilated_dq_sum")[0],
            _ew(add3, dks, out_dtypes=[BF16], name="dilated_dk_sum")[0],
            _ew(add3, dvs, out_dtypes=[BF16], name="dilated_dv_sum")[0], dt5)


def _s5_fwd(u, mats, d_skip, w_glu, j):
    u = _interleave(u)
    xs, ys = [], []
    for d in range(2):
        x_re, x_im, y_d = _s5_scan_fwd(u, mats[d], reverse=(d == 1), name=f"s5_scan_fwd{d}")
        xs += [x_re, x_im]
        ys.append(y_d)

    def act(y0, y1, u, dsk):
        y = (y0 + y1) + dsk * u
        return y, _gelu(y)

    y, yg = _ew(act, ys + [u], [d_skip], out_dtypes=[F32, F32], name="s5_gelu")
    z = _mm_nn(yg, w_glu, j, "row", name="s5_glu_fwd")
    ob = _ew(lambda yg, z: yg * _sigmoid(z), [yg, z], out_dtypes=[BF16], name="s5_gate")[0]
    return _interleave(ob, inverse=True), (xs, y, yg, z, u)


def _s5_bwd(dmerged, ob_off, mats, d_skip, w_glu, j, saved):
    xs, y, yg, z, u = saved
    BW = u.shape[1]
    NB = BW // LANES
    dmerged = _interleave(dmerged[:, ob_off * LANES:])
    ob_off = 0

    def gate_bwd(dob, yg, z):
        sg = _sigmoid(z)
        return dob * yg * (sg * (1.0 - sg)), dob * sg

    dz, dyg1 = _ew(gate_bwd, [(dmerged, ob_off), yg, z], out_dtypes=[BF16, F32], width=LANES, ncol=NB,
                   name="s5_gate_bwd")
    dw_glu = _mm_tn(yg, dz, w_glu.shape[0], "row", name="s5_glu_dw")
    dyg2 = _mm_nt(dz, w_glu, j, "row", name="s5_glu_dx")

    def act_bwd(d1, d2, y, u):
        dy = (d1 + d2) * _gelu_grad(y)
        return dy, jnp.sum(dy * u, axis=0, keepdims=True)

    dy, dd = _ew(act_bwd, [dyg1, dyg2, y, u], out_dtypes=[F32], n_acc=1, name="s5_gelu_bwd")
    dmats, dus = [], []
    for d in range(2):
        du_d, dmat = _s5_scan_bwd(dy, u, xs[2 * d], xs[2 * d + 1], mats[d], reverse=(d == 1),
                                  name=f"s5_scan_bwd{d}")
        dus.append(du_d)
        dmats.append(dmat)
    du = _ew(lambda dy, d0, d1, dsk: dy * dsk + (d0 + d1), [dy] + dus, [d_skip], out_dtypes=[BF16],
             name="s5_du_sum")[0]
    return _interleave(du, inverse=True), tuple(dmats), dd, dw_glu


def _ab_fwd(x, j, P, W):
    t5 = P["t5_bias"]
    AW = t5.shape[1] * HEAD_DIM
    hn = _rms_fwd(x, P["norm_mix"][2 * j][None], name="rms_fwd")
    proj = _mm_nn(hn, W["ab_w_in"], j, "col", name="ab_in_fwd")
    q, k, v = (proj[:, i * AW:(i + 1) * AW].astype(BF16) for i in range(3))
    u = proj[:, 3 * AW:]
    oa32, oa16, lse = _dilated_fwd(q, k, v, t5)
    mats = _s5_mats(*(P[n][j] for n in _S5_PARAMS))
    ob, s5_saved = _s5_fwd(u, mats, P["s5_d"][j][None], W["s5_w_glu"], j)
    merged = jnp.concatenate([oa16, ob], axis=1)
    x1 = _mm_nn(merged, W["ab_w_out"], j, "row", mode="res", res=x, name="ab_out_fwd")
    return x1, (x, hn, q, k, v, u, oa32, lse, merged, s5_saved)


def _ab_bwd(dx1, j, P, W, saved):
    x, hn, q, k, v, u, oa32, lse, merged, s5_saved = saved
    t5 = P["t5_bias"]
    AW = t5.shape[1] * HEAD_DIM
    J = W["ab_w_in"].shape[0]
    dmerged = _mm_nt(dx1, W["ab_w_out"], j, "row", name="ab_out_dx")
    dw_out = _mm_tn(merged, dx1, J, "row", name="ab_out_dw")
    delta, do16 = _attn_delta(dmerged, 0, oa32, name="dilated_delta")
    dq, dk, dv, dt5 = _dilated_bwd(q, k, v, t5, do16, lse, delta)
    s5_params = tuple(P[n][j] for n in _S5_PARAMS)
    mats, mats_vjp = jax.vjp(_s5_mats, *s5_params)
    du, dmats, dd, dw_glu = _s5_bwd(dmerged, AW // LANES, mats, P["s5_d"][j][None], W["s5_w_glu"], j, s5_saved)
    ds5 = mats_vjp(dmats)
    dproj = jnp.concatenate([dq, dk, dv, du], axis=1)
    dw_in = _mm_tn(hn, dproj, J, "col", name="ab_in_dw")
    dhn = _mm_nt(dproj, W["ab_w_in"], j, "col", name="ab_in_dx")
    dx, dg = _rms_bwd(dhn, x, P["norm_mix"][2 * j][None], dx1, name="rms_bwd")
    small = dict(zip(_S5_PARAMS, ds5), s5_d=dd[0], t5_bias=dt5)
    return dx, dg[0], dict(ab_w_in=dw_in, ab_w_out=dw_out, s5_w_glu=dw_glu), small


def _c_fwd(x, j, P, W):
    H = P["c_rpb"].shape[1]
    hn = _rms_fwd(x, P["norm_mix"][2 * j + 1][None], name="rms_fwd")
    qkv = _mm_nn(hn, W["c_w_qkv"], j, "col", out_dtype=BF16, name="c_qkv_fwd")
    bias = _na_bias(P["c_rpb"][j], x.shape[0] // GRID_W)
    o, lse = _wattn_fwd(qkv, qkv, qkv, bias, kind="na", H=H, dil=1, qoff=0, koff=H, voff=2 * H, name="na_fwd")
    x1 = _mm_nn(o, W["c_w_out"], j, "row", mode="res", res=x, name="c_out_fwd")
    return x1, (x, hn, qkv, o, lse)


def _c_bwd(dx1, j, P, W, saved):
    x, hn, qkv, o, lse = saved
    H = P["c_rpb"].shape[1]
    J = W["c_w_qkv"].shape[0]
    do = _mm_nt(dx1, W["c_w_out"], j, "row", name="c_out_dx")
    dw_out = _mm_tn(o, dx1, J, "row", name="c_out_dw")
    delta, do16 = _attn_delta(do, 0, o, name="na_delta")
    bias, bias_vjp = jax.vjp(functools.partial(_na_bias, rows=x.shape[0] // GRID_W), P["c_rpb"][j])
    dq, dk, dv, db = _wattn_bwd(qkv, do16, lse, delta, qkv, qkv, bias, kind="na", H=H, dil=1, qoff=0, koff=H,
                                voff=2 * H, dq_dtype=BF16, name="na_bwd")
    dqkv = jnp.concatenate([dq, dk.astype(BF16), dv.astype(BF16)], axis=1)
    dw_qkv = _mm_tn(hn, dqkv, J, "col", name="c_qkv_dw")
    dhn = _mm_nt(dqkv, W["c_w_qkv"], j, "col", name="c_qkv_dx")
    dx, dg = _rms_bwd(dhn, x, P["norm_mix"][2 * j + 1][None], dx1, name="rms_bwd")
    return dx, dg[0], dict(c_w_qkv=dw_qkv, c_w_out=dw_out), dict(c_rpb=bias_vjp(db)[0])


def _mlp_fwd(x, i, P, W):
    hn = _rms_fwd(x, P["norm_mlp"][i][None], name="rms_fwd")
    a, hdn = _mm_nn(hn, W["mlp_w1"], i, "col", mode="relu2", name="mlp_w1_fwd")
    x2 = _mm_nn(hdn, W["mlp_w2"], i, "row", mode="res", res=x, name="mlp_w2_fwd")
    return x2, (x, hn, a, hdn)


def _mlp_bwd(dx2, i, P, W, saved):
    x, hn, a, hdn = saved
    J = W["mlp_w1"].shape[0]
    da = _mm_nt(dx2, W["mlp_w2"], i, "row", out_dtype=BF16, mode="dact", act=a, name="mlp_w2_dx")
    dw2 = _mm_tn(hdn, dx2, J, "row", name="mlp_w2_dw")
    dw1 = _mm_tn(hn, da, J, "col", name="mlp_w1_dw")
    dhn = _mm_nt(da, W["mlp_w1"], i, "col", name="mlp_w1_dx")
    dx, dg = _rms_bwd(dhn, x, P["norm_mlp"][i][None], dx2, name="rms_bwd")
    return dx, dg[0], dict(mlp_w1=dw1, mlp_w2=dw2)


_S5_PARAMS = ("s5_lam_re", "s5_lam_im", "s5_log_step", "s5_b_re", "s5_b_im", "s5_c_re", "s5_c_im")
_BIG = ("ab_w_in", "ab_w_out", "s5_w_glu", "c_w_qkv", "c_w_out", "mlp_w1", "mlp_w2")
_SMALL = ("t5_bias", "s5_lam_re", "s5_lam_im", "s5_log_step", "s5_b_re", "s5_b_im", "s5_c_re", "s5_c_im", "s5_d",
          "c_rpb", "norm_mix", "norm_mlp", "norm_final")
_WEIGHTS = ("t5_bias", "ab_w_in", "ab_w_out", "s5_lam_re", "s5_lam_im", "s5_log_step", "s5_b_re", "s5_b_im",
            "s5_c_re", "s5_c_im", "s5_d", "s5_w_glu", "c_w_qkv", "c_w_out", "c_rpb", "norm_mix", "norm_mlp",
            "mlp_w1", "mlp_w2", "norm_final")


def _local_grads(x, target, P, W):
    depth = P["norm_mix"].shape[0]
    saved = []
    h = x
    for i in range(depth):
        h, s_mix = (_ab_fwd if i % 2 == 0 else _c_fwd)(h, i // 2, P, W)
        h, s_mlp = _mlp_fwd(h, i, P, W)
        saved.append((s_mix, s_mlp))
    dh, dg_final, loss_cols = _loss_and_grad(h, P["norm_final"][None], target, name="loss_head")
    big = {n: [None] * P[n].shape[0] for n in _BIG}
    small = {n: jnp.zeros(P[n].shape, F32) for n in _SMALL}
    small["norm_final"] = dg_final[0]
    for i in reversed(range(depth)):
        s_mix, s_mlp = saved[i]
        j = i // 2
        dh, dg, dbig = _mlp_bwd(dh, i, P, W, s_mlp)
        small["norm_mlp"] = small["norm_mlp"].at[i].set(dg)
        for n, g in dbig.items():
            big[n][i] = g
        dh, dg, dbig, dsmall = (_ab_bwd if i % 2 == 0 else _c_bwd)(dh, j, P, W, s_mix)
        small["norm_mix"] = small["norm_mix"].at[i].set(dg)
        for n, g in dbig.items():
            big[n][j] = g
        for n, g in dsmall.items():
            if n == "t5_bias":
                small[n] = small[n] + g
            else:
                small[n] = small[n].at[j].set(g.reshape(P[n].shape[1:]))
    return loss_cols[0, 0], dh, big, small


def _place():
    x, y, c = lax.axis_index("x"), lax.axis_index("y"), lax.axis_index("c")
    return x, y, c, ((1 - x, y), (x, 1 - y), (1 - x, 1 - y))


def _comm_call(body, arrays, out_shape, sems, *, name, in_place=False, **static):
    hbm = pl.BlockSpec(memory_space=pltpu.HBM)
    return pl.pallas_call(
        functools.partial(body, n=len(arrays), **static), name=name, in_specs=[hbm] * len(arrays),
        out_specs=tuple([hbm] * len(out_shape)), out_shape=tuple(out_shape),
        input_output_aliases={t: t for t in range(len(arrays))} if in_place else {},
        scratch_shapes=[pltpu.SemaphoreType.DMA((k,)) for k in sems])(*arrays)


def _cast_body(s_ref, w_ref, o_ref):
    o_ref[...] = w_ref[...].astype(o_ref.dtype)


def _allgather_body(*refs, n):
    bufs = refs[n:2 * n]
    ici_send, ici_recv, pair_send, pair_recv = refs[2 * n:]
    x, y, c, chips = _place()
    slots = [2 * x + y] + [2 * px + py for px, py in chips]

    def part(t, slot, half):
        h = bufs[t].shape[1] // 2
        return bufs[t].at[slot, pl.ds(half * h, h)]

    def over_ici(t, p, slot):
        return pltpu.make_async_remote_copy(
            src_ref=part(t, slot, c), dst_ref=part(t, slot, c), send_sem=ici_send.at[3 * t + p],
            recv_sem=ici_recv.at[3 * t + p], device_id=(*chips[p], c), device_id_type=MESH)

    def to_pair(t, p, half):
        return pltpu.make_async_remote_copy(
            src_ref=part(t, slots[1 + p], half), dst_ref=part(t, slots[1 + p], half), send_sem=pair_send.at[3 * t + p],
            recv_sem=pair_recv.at[3 * t + p], device_id=(x, y, 1 - c), device_id_type=MESH)

    sends = [over_ici(t, p, slots[0]) for t in range(n) for p in range(3)]
    for cp in sends:
        cp.start()
    for t in range(n):
        for p in range(3):
            over_ici(t, p, slots[1 + p]).wait_recv()
            cp = to_pair(t, p, c)
            cp.start()
            sends.append(cp)
    for t in range(n):
        for p in range(3):
            to_pair(t, p, 1 - c).wait_recv()
    for cp in sends:
        cp.wait_send()


def _allgather(weights, place):
    bufs = []
    for w in weights:
        L, Kd, Nd = w.shape
        tr = _tile(Kd, 256)
        bufs.append(_sliced_call(
            _cast_body, place, [w], [lambda l, i, p: (l, i, 0)], [(None, tr, Nd)], (None, None, tr, Nd),
            lambda l, i, p: (p[1], l, i, 0), jax.ShapeDtypeStruct((4, L, Kd, Nd), BF16), (L, Kd // tr),
            name="weights_cast"))
    n = len(bufs)
    return _comm_call(_allgather_body, bufs, [jax.ShapeDtypeStruct(b.shape, b.dtype) for b in bufs],
                      (3 * n,) * 4, in_place=True, name="weights_allgather")


def _pair_exchange_body(*refs, n):
    ins, outs = refs[:n], refs[n:2 * n]
    send_sems, recv_sems = refs[2 * n:]
    x, y, c, _ = _place()
    cps = []
    for t in range(n):
        h = outs[t].shape[0]
        cps.append(pltpu.make_async_remote_copy(
            src_ref=ins[t].at[pl.ds((1 - c) * h, h)], dst_ref=outs[t], send_sem=send_sems.at[t],
            recv_sem=recv_sems.at[t], device_id=(x, y, 1 - c), device_id_type=MESH))
    for cp in cps:
        cp.start()
    for cp in cps:
        cp.wait()


def _chip_exchange_body(*refs, n):
    ins, outs = refs[:n], refs[n:2 * n]
    send_sems, recv_sems = refs[2 * n:]
    x, y, c, chips = _place()
    cps = []
    for t in range(n):
        h = ins[t].shape[0]
        for p, (px, py) in enumerate(chips):
            cps.append(pltpu.make_async_remote_copy(
                src_ref=ins[t].at[pl.ds(0, h), 2 * px + py], dst_ref=outs[t].at[p], send_sem=send_sems.at[3 * t + p],
                recv_sem=recv_sems.at[3 * t + p], device_id=(px, py, c), device_id_type=MESH))
    for cp in cps:
        cp.start()
    for cp in cps:
        cp.wait()


def _pair_share_body(*refs, n):
    bufs = refs[n:2 * n]
    send_sems, recv_sems = refs[2 * n:]
    x, y, c, _ = _place()
    sends, recvs = [], []
    for t in range(n):
        h = bufs[t].shape[0] // 2
        for half, group in ((c, sends), (1 - c, recvs)):
            rows = bufs[t].at[pl.ds(half * h, h)]
            group.append(pltpu.make_async_remote_copy(
                src_ref=rows, dst_ref=rows, send_sem=send_sems.at[t], recv_sem=recv_sems.at[t],
                device_id=(x, y, 1 - c), device_id_type=MESH))
    for cp in sends:
        cp.start()
    for cp in recvs:
        cp.wait_recv()
    for cp in sends:
        cp.wait_send()


def _allreduce_body(in_ref, out_ref, send_sems, recv_sems, local_sem, n):
    x, y, c, _ = _place()
    flip = lambda v, bit: 1 - v if bit else v
    peers = [(flip(x, k & 4), flip(y, k & 2), flip(c, k & 1)) for k in range(1, 8)]

    def remote(k, slot):
        return pltpu.make_async_remote_copy(
            src_ref=in_ref, dst_ref=out_ref.at[slot], send_sem=send_sems.at[k], recv_sem=recv_sems.at[k],
            device_id=peers[k], device_id_type=MESH)

    local = pltpu.make_async_copy(in_ref, out_ref.at[4 * x + 2 * y + c], local_sem.at[0])
    sends = [remote(k, 4 * x + 2 * y + c) for k in range(7)]
    local.start()
    for cp in sends:
        cp.start()
    for k, (px, py, pc) in enumerate(peers):
        remote(k, 4 * px + 2 * py + pc).wait_recv()
    for cp in sends:
        cp.wait_send()
    local.wait()


def _sliced_call(body, scalars, arrays, in_maps, blocks, out_block, out_map, out_shape, grid, *, name):
    grid_spec = pltpu.PrefetchScalarGridSpec(
        num_scalar_prefetch=1, grid=grid,
        in_specs=[pl.BlockSpec(b, m) for b, m in zip(blocks, in_maps)],
        out_specs=pl.BlockSpec(out_block, out_map))
    return pl.pallas_call(
        functools.partial(body), name=name, grid_spec=grid_spec, out_shape=out_shape,
        compiler_params=pltpu.CompilerParams(vmem_limit_bytes=V7X_VMEM_LIMIT_BYTES))(scalars, *arrays)


def _chip_sum_body(s_ref, g_ref, r_ref, o_ref):
    o_ref[...] = (g_ref[...] + r_ref[...]).astype(o_ref.dtype)


def _final_sum_body(s_ref, g_ref, r1_ref, a_ref, b_ref, c_ref, o_ref):
    o_ref[...] = (((g_ref[...] + r1_ref[...]) + a_ref[...].astype(F32)) + b_ref[...].astype(F32)) + c_ref[...].astype(F32)


def _reduce_big(stacks, place):
    n = len(stacks)
    half = [jax.ShapeDtypeStruct((s.shape[0] // 2,) + s.shape[1:], F32) for s in stacks]
    from_pair = _comm_call(_pair_exchange_body, stacks, half, (n, n), name="grads_pair_exchange")
    chip16 = []
    for s, r in zip(stacks, from_pair):
        h, J, Kd, Nd = r.shape
        tr = _tile(Kd, 256)
        blk = (None, None, tr, Nd)
        chip16.append(_sliced_call(
            _chip_sum_body, place, [s, r],
            [lambda l, j, i, p: (p[0] * h + l, j, i, 0), lambda l, j, i, p: (l, j, i, 0)], [blk, blk], blk,
            lambda l, j, i, p: (l, j, i, 0), jax.ShapeDtypeStruct(r.shape, BF16), (h, J, Kd // tr),
            name="grads_chip_sum"))
    recv = [jax.ShapeDtypeStruct((3, a.shape[0]) + a.shape[2:], BF16) for a in chip16]
    from_chips = _comm_call(_chip_exchange_body, chip16, recv, (3 * n, 3 * n), name="grads_chip_exchange")
    sums = []
    for s, r, f in zip(stacks, from_pair, from_chips):
        h, J, Kd, Nd = r.shape
        tr = _tile(Kd, 256)
        blk4, blk3 = (None, None, tr, Nd), (None, tr, Nd)
        mine = lambda l, i, p: (l, p[1], i, 0)
        sums.append(_sliced_call(
            _final_sum_body, place, [s, r, f, f, f],
            [lambda l, i, p: (p[0] * h + l, p[1], i, 0), mine] + [functools.partial(lambda l, i, p, q: (q, l, i, 0), q=q)
                                                                  for q in range(3)],
            [blk4, blk4, blk4, blk4, blk4], blk3, lambda l, i, p: (p[0] * h + l, i, 0),
            jax.ShapeDtypeStruct((2 * h, Kd, Nd), F32), (h, Kd // tr), name="grads_final_sum"))
    return _comm_call(_pair_share_body, sums, [jax.ShapeDtypeStruct(s.shape, F32) for s in sums], (n, n),
                      in_place=True, name="grads_pair_share")


def _allreduce_small(buf):
    gathered = pl.pallas_call(
        functools.partial(_allreduce_body, n=1), name="small_allgather",
        in_specs=[pl.BlockSpec(memory_space=pltpu.HBM)], out_specs=pl.BlockSpec(memory_space=pltpu.HBM),
        out_shape=jax.ShapeDtypeStruct((8,) + buf.shape, F32),
        scratch_shapes=[pltpu.SemaphoreType.DMA((7,)), pltpu.SemaphoreType.DMA((7,)),
                        pltpu.SemaphoreType.DMA((1,))])(buf)

    def total(*b):
        acc = b[0]
        for t in b[1:]:
            acc = acc + t
        return acc

    return _ew(total, [gathered[i] for i in range(8)], out_dtypes=[F32], name="small_sum")[0]


def _pack(parts):
    flat = jnp.concatenate([p.reshape(-1).astype(F32) for p in parts])
    rows = -(-flat.shape[0] // (8 * LANES)) * 8
    return jnp.pad(flat, (0, rows * LANES - flat.shape[0])).reshape(rows, LANES)


def _unpack(buf, shapes):
    flat = buf.reshape(-1)
    out, at = [], 0
    for s in shapes:
        size = math.prod(s)
        out.append(flat[at:at + size].reshape(s))
        at += size
    return out


_INPUTS = ("x",) + _WEIGHTS + ("loss_target",) + tuple("m_" + n for n in _WEIGHTS) + tuple("v_" + n for n in _WEIGHTS)


def kernel(x, t5_bias, ab_w_in, ab_w_out, s5_lam_re, s5_lam_im, s5_log_step, s5_b_re, s5_b_im, s5_c_re, s5_c_im,
           s5_d, s5_w_glu, c_w_qkv, c_w_out, c_rpb, norm_mix, norm_mlp, mlp_w1, mlp_w2, norm_final, loss_target,
           m_t5_bias, m_ab_w_in, m_ab_w_out, m_s5_lam_re, m_s5_lam_im, m_s5_log_step, m_s5_b_re, m_s5_b_im,
           m_s5_c_re, m_s5_c_im, m_s5_d, m_s5_w_glu, m_c_w_qkv, m_c_w_out, m_c_rpb, m_norm_mix, m_norm_mlp,
           m_mlp_w1, m_mlp_w2, m_norm_final, v_t5_bias, v_ab_w_in, v_ab_w_out, v_s5_lam_re, v_s5_lam_im,
           v_s5_log_step, v_s5_b_re, v_s5_b_im, v_s5_c_re, v_s5_c_im, v_s5_d, v_s5_w_glu, v_c_w_qkv, v_c_w_out,
           v_c_rpb, v_norm_mix, v_norm_mlp, v_mlp_w1, v_mlp_w2, v_norm_final):
    args = (x, t5_bias, ab_w_in, ab_w_out, s5_lam_re, s5_lam_im, s5_log_step, s5_b_re, s5_b_im, s5_c_re, s5_c_im,
            s5_d, s5_w_glu, c_w_qkv, c_w_out, c_rpb, norm_mix, norm_mlp, mlp_w1, mlp_w2, norm_final, loss_target,
            m_t5_bias, m_ab_w_in, m_ab_w_out, m_s5_lam_re, m_s5_lam_im, m_s5_log_step, m_s5_b_re, m_s5_b_im,
            m_s5_c_re, m_s5_c_im, m_s5_d, m_s5_w_glu, m_c_w_qkv, m_c_w_out, m_c_rpb, m_norm_mix, m_norm_mlp,
            m_mlp_w1, m_mlp_w2, m_norm_final, v_t5_bias, v_ab_w_in, v_ab_w_out, v_s5_lam_re, v_s5_lam_im,
            v_s5_log_step, v_s5_b_re, v_s5_b_im, v_s5_c_re, v_s5_c_im, v_s5_d, v_s5_w_glu, v_c_w_qkv, v_c_w_out,
            v_c_rpb, v_norm_mix, v_norm_mlp, v_mlp_w1, v_mlp_w2, v_norm_final)
    A = dict(zip(_INPUTS, args, strict=True))
    P = {n: A[n] for n in _WEIGHTS}
    place = jnp.stack([lax.axis_index("c"), 2 * lax.axis_index("x") + lax.axis_index("y")]).astype(jnp.int32)

    gathered = _allgather([P[n] for n in _BIG], place)
    W = dict(zip(_BIG, gathered))
    loss, dx, big, small = _local_grads(A["x"][0], A["loss_target"][0], P, W)

    stacks = [jnp.stack(big[n]) for n in _BIG]
    big_grads = dict(zip(_BIG, _reduce_big(stacks, place)))
    small_shapes = [P[n].shape for n in _SMALL] + [(1,)]
    reduced = _unpack(_allreduce_small(_pack([small[n] for n in _SMALL] + [loss.reshape(1)])), small_shapes)
    small_grads = dict(zip(_SMALL, reduced[:-1]))
    loss = reduced[-1][0]

    grads, delta, new_m, new_v = {}, {}, {}, {}
    for n in _BIG:
        g = big_grads[n]
        two_d = lambda t: t.reshape(-1, t.shape[-1])
        d, m, v = _adamw(two_d(P[n]), two_d(g), two_d(A["m_" + n]), two_d(A["v_" + n]), name="adamw")
        grads[n] = g
        delta[n], new_m[n], new_v[n] = (t.reshape(g.shape) for t in (d, m, v))
    d, m, v = _adamw(_pack([P[n] for n in _SMALL]), _pack([small_grads[n] for n in _SMALL]),
                     _pack([A["m_" + n] for n in _SMALL]), _pack([A["v_" + n] for n in _SMALL]), name="adamw_small")
    shapes = [P[n].shape for n in _SMALL]
    for n, dn, mn, vn in zip(_SMALL, _unpack(d, shapes), _unpack(m, shapes), _unpack(v, shapes)):
        grads[n] = small_grads[n]
        delta[n], new_m[n], new_v[n] = dn, mn, vn
    return (loss, dx[None], *[grads[n] for n in _WEIGHTS], *[delta[n] for n in _WEIGHTS],
            *[new_m[n] for n in _WEIGHTS], *[new_v[n] for n in _WEIGHTS])
```

```python
import functools
import math

import jax
import jax.numpy as jnp
from jax import lax
from jax.experimental import pallas as pl
from jax.experimental.pallas import tpu as pltpu

F32 = jnp.float32
BF16 = jnp.bfloat16

HEAD_DIM = 128
LANES = 128
DILATED_BRANCHES = ((128, 1), (512, 4), (2048, 16))
A_QBLOCK = 128
DIL_HALF = 64
B_GROUP = 16
B_STATE = 64
GROUPS_PER_BLOCK = LANES // B_GROUP
STATE_PER_BLOCK = GROUPS_PER_BLOCK * B_STATE
GRID_W = 64
NA_ROWS_MAX = 8
NA_COLS = 16
T5_BUCKETS = 32
T5_MAX_DISTANCE = 1024
RMS_EPS = 1e-6
NEG_INF = -1e30
ADAM_LR = 0.001
ADAM_B1 = 0.9
ADAM_B2 = 0.999
ADAM_EPS = 1e-08
ADAM_WD = 0.01
ADAM_STEP = 10
V7X_VMEM_LIMIT_BYTES = 56 * 1024 * 1024

NN = (((1,), (0,)), ((), ()))
NT = (((1,), (1,)), ((), ()))
TN = (((0,), (0,)), ((), ()))
MESH = pl.DeviceIdType.MESH


def _tile(n, pref):
    if n <= pref:
        return n
    for t in range(pref - pref % LANES, 0, -LANES):
        if n % t == 0:
            return t
    t = pref
    while t >= 8:
        if n % t == 0:
            return t
        t //= 2
    return n


def _pcall(body, *, name, grid, in_specs, out_specs, out_shape, scratch=(), sem=None, aliases=None, **static):
    params = dict(vmem_limit_bytes=V7X_VMEM_LIMIT_BYTES)
    if sem is not None:
        params["dimension_semantics"] = sem
    return pl.pallas_call(
        functools.partial(body, **static), name=name, grid=grid, in_specs=in_specs, out_specs=out_specs,
        out_shape=out_shape, scratch_shapes=list(scratch), input_output_aliases=aliases or {},
        compiler_params=pltpu.CompilerParams(**params))


def _gather_plan(bufs, pieces, sems):
    ici_send, ici_recv, pair_send, pair_recv = sems
    x, y, c = lax.axis_index("x"), lax.axis_index("y"), lax.axis_index("c")
    chips = ((1 - x, y), (x, 1 - y), (1 - x, 1 - y))
    slots = [2 * x + y] + [2 * px + py for px, py in chips]
    every = [(k, p) for k in range(len(pieces)) for p in range(3)]

    def part(k, slot, half):
        t, layer = pieces[k]
        h = bufs[t].shape[2] // 2
        return bufs[t].at[slot, layer, pl.ds(half * h, h)]

    def over_ici(k, p, slot):
        return pltpu.make_async_remote_copy(
            src_ref=part(k, slot, c), dst_ref=part(k, slot, c), send_sem=ici_send.at[3 * k + p],
            recv_sem=ici_recv.at[3 * k + p], device_id=(*chips[p], c), device_id_type=MESH)

    def to_pair(k, p, half):
        return pltpu.make_async_remote_copy(
            src_ref=part(k, slots[1 + p], half), dst_ref=part(k, slots[1 + p], half),
            send_sem=pair_send.at[3 * k + p], recv_sem=pair_recv.at[3 * k + p], device_id=(x, y, 1 - c),
            device_id_type=MESH)

    def start():
        for k, p in every:
            over_ici(k, p, slots[0]).start()

    def finish():
        for k, p in every:
            over_ici(k, p, slots[1 + p]).wait_recv()
            to_pair(k, p, c).start()
        for k, p in every:
            to_pair(k, p, 1 - c).wait_recv()
        for k, p in every:
            over_ici(k, p, slots[0]).wait_send()
            to_pair(k, p, c).wait_send()

    return start, finish


def _gather_sems(pieces):
    return [pltpu.SemaphoreType.DMA((3 * len(pieces),))] * 4


def _hosted_body(*refs, host_body, n_in, n_out, n_buf, pieces, **static):
    ins = refs[:n_in]
    outs = refs[n_in + n_buf:n_in + n_buf + n_out]
    bufs = refs[n_in + n_buf + n_out:n_in + 2 * n_buf + n_out]
    scratch = refs[n_in + 2 * n_buf + n_out:-4]
    start, finish = _gather_plan(bufs, pieces, refs[-4:])
    ids = [pl.program_id(a) for a in range(3)]
    last = [pl.num_programs(a) - 1 for a in range(3)]

    @pl.when((ids[0] == 0) & (ids[1] == 0) & (ids[2] == 0))
    def _():
        start()

    host_body(*ins, *outs, *scratch, **static)

    @pl.when((ids[0] == last[0]) & (ids[1] == last[1]) & (ids[2] == last[2]))
    def _():
        finish()


def _gather_pieces_body(*refs, n, pieces):
    start, finish = _gather_plan(refs[n:2 * n], pieces, refs[2 * n:])
    start()
    finish()


def _named_pieces(W, pieces):
    names = sorted({n for n, _ in pieces})
    return names, [W[n] for n in names], tuple((names.index(n), layer) for n, layer in pieces)


def _gather_pieces(W, pieces):
    names, bufs, idx = _named_pieces(W, pieces)
    hbm = pl.BlockSpec(memory_space=pltpu.HBM)
    new = pl.pallas_call(
        functools.partial(_gather_pieces_body, n=len(bufs), pieces=idx), name="weights_gather",
        in_specs=[hbm] * len(bufs), out_specs=tuple([hbm] * len(bufs)),
        out_shape=tuple(jax.ShapeDtypeStruct(b.shape, b.dtype) for b in bufs),
        input_output_aliases={t: t for t in range(len(bufs))}, scratch_shapes=_gather_sems(idx))(*bufs)
    return {**W, **dict(zip(names, new))}


def _mm_finish(acc, rest, mode):
    if mode == "plain":
        rest[0][...] = acc.astype(rest[0].dtype)
    elif mode == "res":
        rest[1][...] = (rest[0][...] + acc).astype(rest[1].dtype)
    elif mode == "relu2":
        rest[0][...] = acc
        r = jnp.maximum(acc, 0.0)
        rest[1][...] = (r * r).astype(rest[1].dtype)
    elif mode == "dact":
        rest[1][...] = (acc * (2.0 * jnp.maximum(rest[0][...], 0.0))).astype(rest[1].dtype)


def _mm_body(a_ref, b_ref, *rest, nk, dims, mode):
    prod = lax.dot_general(a_ref[...].astype(BF16), b_ref[...].astype(BF16), dims, preferred_element_type=F32)
    if nk == 1:
        _mm_finish(prod, rest, mode)
        return
    acc_ref = rest[-1]
    k = pl.program_id(2)

    @pl.when(k == 0)
    def _():
        acc_ref[...] = prod

    @pl.when(k > 0)
    def _():
        acc_ref[...] += prod

    @pl.when(k == nk - 1)
    def _():
        _mm_finish(acc_ref[...], rest, mode)


def _w_dims(w, split):
    J, _, Kd, Nd = w.shape
    return (J, Kd, J * Nd, Kd, Nd) if split == "col" else (J, J * Kd, Nd, Kd, Nd)


def _w_spec(split, layer, tk, tn, Kd, Nd, kn_of):
    kps, nps = Kd // tk, Nd // tn

    def index(*g):
        kb, nb = kn_of(*g)
        if split == "col":
            return nb // nps, layer, kb, nb % nps
        return kb // kps, layer, kb % kps, nb

    return pl.BlockSpec((None, None, tk, tn), index)


def _mm_nn(a, w, layer, split, *, name, out_dtype=F32, mode="plain", res=None, host=None):
    M = a.shape[0]
    J, K, N, Kd, Nd = _w_dims(w, split)
    tm, tn, tk = _tile(M, 1024), _tile(Nd, 1024), _tile(Kd, 2048)
    in_specs = [pl.BlockSpec((tm, tk), lambda i, j, k: (i, k)),
                _w_spec(split, layer, tk, tn, Kd, Nd, lambda i, j, k: (k, j))]
    args = [a, w]
    o_spec = pl.BlockSpec((tm, tn), lambda i, j, k: (i, j))
    if mode == "res":
        in_specs.append(o_spec)
        args.append(res)
    if mode == "relu2":
        out_shape = (jax.ShapeDtypeStruct((M, N), F32), jax.ShapeDtypeStruct((M, N), BF16))
        out_specs = (o_spec, o_spec)
    else:
        out_shape = (jax.ShapeDtypeStruct((M, N), out_dtype),)
        out_specs = (o_spec,)
    grid = (M // tm, N // tn, K // tk)
    scratch = [pltpu.VMEM((tm, tn), F32)]
    if host is None:
        out = _pcall(_mm_body, name=name, grid=grid, in_specs=in_specs, out_specs=out_specs, out_shape=out_shape,
                     scratch=scratch, sem=("parallel", "parallel", "arbitrary"), nk=K // tk, dims=NN, mode=mode)(*args)
        return out if mode == "relu2" else out[0]
    W, pieces = host
    names, bufs, idx = _named_pieces(W, pieces)
    hbm = pl.BlockSpec(memory_space=pltpu.HBM)
    n_in, n_out, n_buf = len(args), len(out_shape), len(bufs)
    out = _pcall(_hosted_body, name=name, grid=grid, in_specs=in_specs + [hbm] * n_buf,
                 out_specs=tuple(out_specs) + (hbm,) * n_buf,
                 out_shape=tuple(out_shape) + tuple(jax.ShapeDtypeStruct(b.shape, b.dtype) for b in bufs),
                 scratch=scratch + _gather_sems(idx), sem=("arbitrary",) * 3,
                 aliases={n_in + t: n_out + t for t in range(n_buf)}, host_body=_mm_body, n_in=n_in, n_out=n_out,
                 n_buf=n_buf, pieces=idx, nk=K // tk, dims=NN, mode=mode)(*args, *bufs)
    res_out = out[:n_out] if mode == "relu2" else out[0]
    return res_out, {**W, **dict(zip(names, out[n_out:]))}


def _mm_nt(a, w, layer, split, *, name, out_dtype=F32, mode="plain", act=None):
    M = a.shape[0]
    J, K, N, Kd, Nd = _w_dims(w, split)
    tm, tko, tc = _tile(M, 1024), _tile(Kd, 1024), _tile(Nd, 2048)
    in_specs = [pl.BlockSpec((tm, tc), lambda i, j, c: (i, c)),
                _w_spec(split, layer, tko, tc, Kd, Nd, lambda i, j, c: (j, c))]
    args = [a, w]
    o_spec = pl.BlockSpec((tm, tko), lambda i, j, c: (i, j))
    if mode == "dact":
        in_specs.append(o_spec)
        args.append(act)
    return _pcall(_mm_body, name=name, grid=(M // tm, K // tko, N // tc), in_specs=in_specs, out_specs=o_spec,
                  out_shape=jax.ShapeDtypeStruct((M, K), out_dtype), scratch=[pltpu.VMEM((tm, tko), F32)],
                  sem=("parallel", "parallel", "arbitrary"), nk=N // tc, dims=NT, mode=mode)(*args)


def _mm_tn(a, b, J, split, *, name):
    M, K = a.shape
    N = b.shape[1]
    Kd, Nd = (K, N // J) if split == "col" else (K // J, N)
    tk, tn, tc = _tile(Kd, 1024), _tile(Nd, 1024), _tile(M, 2048)
    kps, nps = Kd // tk, Nd // tn
    in_specs = [pl.BlockSpec((tc, tk), lambda i, j, c: (c, i)),
                pl.BlockSpec((tc, tn), lambda i, j, c: (c, j))]
    if split == "col":
        o_spec = pl.BlockSpec((None, tk, tn), lambda i, j, c: (j // nps, i, j % nps))
    else:
        o_spec = pl.BlockSpec((None, tk, tn), lambda i, j, c: (i // kps, i % kps, j))
    return _pcall(_mm_body, name=name, grid=(K // tk, N // tn, M // tc), in_specs=in_specs, out_specs=o_spec,
                  out_shape=jax.ShapeDtypeStruct((J, Kd, Nd), F32), scratch=[pltpu.VMEM((tk, tn), F32)],
                  sem=("parallel", "parallel", "arbitrary"), nk=M // tc, dims=TN, mode="plain")(a, b)


def _bmm_body(*refs, n, dims):
    out_ref = refs[2 * n]
    acc = None
    for i in range(n):
        p = lax.dot_general(refs[i][...].astype(BF16), refs[n + i][...].astype(BF16), dims,
                            preferred_element_type=F32)
        acc = p if acc is None else acc + p
    out_ref[...] = acc.astype(out_ref.dtype)


def _bmm_nn(a_list, b_list, *, name):
    M = a_list[0].shape[0]
    NB, Ka, No = b_list[0].shape
    tm = _tile(M, 512)
    n = len(a_list)
    in_specs = ([pl.BlockSpec((tm, Ka), lambda i, j: (i, j))] * n
                + [pl.BlockSpec((None, Ka, No), lambda i, j: (j, 0, 0))] * n)
    return _pcall(_bmm_body, name=name, grid=(M // tm, NB), in_specs=in_specs,
                  out_specs=pl.BlockSpec((tm, No), lambda i, j: (i, j)),
                  out_shape=jax.ShapeDtypeStruct((M, NB * No), F32), sem=("parallel", "parallel"),
                  n=n, dims=NN)(*a_list, *b_list)


def _bmm_nt(a_list, b_list, *, name):
    M = a_list[0].shape[0]
    NB, Ka, No = b_list[0].shape
    tm = _tile(M, 512)
    n = len(a_list)
    in_specs = ([pl.BlockSpec((tm, No), lambda i, j: (i, j))] * n
                + [pl.BlockSpec((None, Ka, No), lambda i, j: (j, 0, 0))] * n)
    return _pcall(_bmm_body, name=name, grid=(M // tm, NB), in_specs=in_specs,
                  out_specs=pl.BlockSpec((tm, Ka), lambda i, j: (i, j)),
                  out_shape=jax.ShapeDtypeStruct((M, NB * Ka), F32), sem=("parallel", "parallel"),
                  n=n, dims=NT)(*a_list, *b_list)


def _bmm_tn_body(a_ref, c_ref, out_ref):
    @pl.when(pl.program_id(1) == 0)
    def _():
        out_ref[...] = jnp.zeros_like(out_ref)

    out_ref[...] += lax.dot_general(a_ref[...].astype(BF16), c_ref[...].astype(BF16), TN,
                                    preferred_element_type=F32)


def _bmm_tn(a, c, NB, *, name):
    M = a.shape[0]
    Ka, No = a.shape[1] // NB, c.shape[1] // NB
    tm = _tile(M, 512)
    return _pcall(_bmm_tn_body, name=name, grid=(NB, M // tm),
                  in_specs=[pl.BlockSpec((tm, Ka), lambda j, m: (m, j)), pl.BlockSpec((tm, No), lambda j, m: (m, j))],
                  out_specs=pl.BlockSpec((None, Ka, No), lambda j, m: (j, 0, 0)),
                  out_shape=jax.ShapeDtypeStruct((NB, Ka, No), F32), sem=("parallel", "arbitrary"))(a, c)


def _ew_body(*refs, fn, n_in, n_out, n_acc):
    res = fn(*[r[...] for r in refs[:n_in]])
    if not isinstance(res, (tuple, list)):
        res = (res,)
    outs = refs[n_in:n_in + n_out]
    accs = refs[n_in + n_out:]
    for o, r in zip(outs, res[:n_out]):
        o[...] = r.astype(o.dtype)
    if n_acc:
        first = pl.program_id(1) == 0
        for a, r in zip(accs, res[n_out:]):
            @pl.when(first)
            def _(a=a):
                a[...] = jnp.zeros_like(a)

            a[...] += r


def _ew(fn, rows, vecs=(), *, out_dtypes=(), n_acc=0, width=None, ncol=1, tr=256, name):
    rows = [r if isinstance(r, tuple) else (r, 0) for r in rows]
    R = rows[0][0].shape[0]
    C = width if width is not None else rows[0][0].shape[1]
    tr = _tile(R, tr)
    in_specs = [pl.BlockSpec((tr, C), functools.partial(lambda j, i, off: (i, off + j), off=off)) for _, off in rows]
    in_specs += [pl.BlockSpec((1, C), lambda j, i: (0, j)) for _ in vecs]
    out_shape = [jax.ShapeDtypeStruct((R, ncol * C), dt) for dt in out_dtypes]
    out_specs = [pl.BlockSpec((tr, C), lambda j, i: (i, j)) for _ in out_dtypes]
    out_shape += [jax.ShapeDtypeStruct((1, ncol * C), F32)] * n_acc
    out_specs += [pl.BlockSpec((1, C), lambda j, i: (0, j))] * n_acc
    res = _pcall(_ew_body, name=name, grid=(ncol, R // tr), in_specs=in_specs, out_specs=tuple(out_specs),
                 out_shape=tuple(out_shape), sem=("parallel", "arbitrary" if n_acc else "parallel"),
                 fn=fn, n_in=len(rows) + len(vecs), n_out=len(out_dtypes), n_acc=n_acc)(
        *[a for a, _ in rows], *vecs)
    return res


def _rms_fwd(x, g, *, name):
    def fn(x, g):
        r = lax.rsqrt(jnp.mean(x * x, axis=1, keepdims=True) + RMS_EPS)
        return (x * r) * g

    return _ew(fn, [x], [g], out_dtypes=[BF16], name=name)[0]


def _rms_bwd(dh, x, g, dres, *, name):
    def fn(dh, x, dres, g):
        r = lax.rsqrt(jnp.mean(x * x, axis=1, keepdims=True) + RMS_EPS)
        y = x * r
        dy = dh * g
        dx = r * (dy - y * jnp.mean(dy * y, axis=1, keepdims=True))
        return dres + dx, jnp.sum(dh * y, axis=0, keepdims=True)

    return _ew(fn, [dh, x, dres], [g], out_dtypes=[F32], n_acc=1, name=name)


def _loss_and_grad(x, g, target, *, name):
    D = x.shape[1]

    def fn(x, t, g):
        r = lax.rsqrt(jnp.mean(x * x, axis=1, keepdims=True) + RMS_EPS)
        y = x * r
        diff = y * g - t
        dh = diff * (1.0 / D)
        dy = dh * g
        dx = r * (dy - y * jnp.mean(dy * y, axis=1, keepdims=True))
        loss = jnp.sum(jnp.sum(diff * diff, axis=1, keepdims=True), axis=0, keepdims=True) * (0.5 / D)
        return dx, jnp.sum(dh * y, axis=0, keepdims=True), jnp.broadcast_to(loss, (1, D))

    return _ew(fn, [x, target], [g], out_dtypes=[F32], n_acc=2, name=name)


def _gelu(y):
    c = math.sqrt(2.0 / math.pi)
    return 0.5 * y * (1.0 + jnp.tanh(c * (y + 0.044715 * (y * y * y))))


def _gelu_grad(y):
    c = math.sqrt(2.0 / math.pi)
    t = jnp.tanh(c * (y + 0.044715 * (y * y * y)))
    return 0.5 * (1.0 + t) + 0.5 * y * (1.0 - t * t) * (c * (1.0 + 3 * 0.044715 * (y * y)))


def _adamw(w, g, m, v, *, name):
    def fn(w, g, m, v):
        m2 = ADAM_B1 * m + (1.0 - ADAM_B1) * g
        v2 = ADAM_B2 * v + (1.0 - ADAM_B2) * (g * g)
        m_hat = m2 / (1.0 - ADAM_B1 ** ADAM_STEP)
        v_hat = v2 / (1.0 - ADAM_B2 ** ADAM_STEP)
        delta = -ADAM_LR * (m_hat / (jnp.sqrt(v_hat) + ADAM_EPS) + ADAM_WD * w)
        return delta, m2, v2

    return _ew(fn, [w, g, m, v], out_dtypes=[F32, F32, F32], name=name)


def _window(kind, blk, QB, rows):
    if kind == "dil":
        return pl.multiple_of(blk * QB, QB), 0
    kr = min(NA_ROWS_MAX, rows)
    rs = jnp.clip(blk - kr // 2, 0, rows - kr)
    return pl.multiple_of(rs * GRID_W, GRID_W), blk - rs


def _scores(q, kw, bias, kind, start, QB, W, L_valid, scale):
    s = lax.dot_general(q, kw, NT, preferred_element_type=F32) * scale + bias
    if kind == "dil":
        kp = start + lax.broadcasted_iota(jnp.int32, (QB, W), 1)
        s = jnp.where((kp >= DIL_HALF) & (kp < DIL_HALF + L_valid), s, NEG_INF)
    return s


def _wattn_fwd_body(q_ref, k_ref, v_ref, b_ref, *outs, QB, SUB, W, kind, L_valid, rows, scale):
    n = pl.program_id(1)
    wins = [_window(kind, n * SUB + i, QB, rows) for i in range(SUB)]
    sls = [slice(i * QB, (i + 1) * QB) for i in range(SUB)]
    ss = [_scores(q_ref[sl, :], k_ref[pl.ds(start, W), :], b_ref[pat], kind, start, QB, W, L_valid, scale)
          for sl, (start, pat) in zip(sls, wins)]
    ms = [jnp.max(s, axis=1, keepdims=True) for s in ss]
    es = [jnp.exp(s - m) for s, m in zip(ss, ms)]
    ls = [jnp.sum(e, axis=1, keepdims=True) for e in es]
    for sl, (start, _), m, e, l in zip(sls, wins, ms, es, ls):
        vw = v_ref[pl.ds(start, W), :]
        if kind == "na":
            p = (e * (1.0 / l)).astype(BF16)
            o = lax.dot_general(p, vw, NN, preferred_element_type=F32)
            outs[0][sl, :] = o.astype(outs[0].dtype)
            outs[1][sl, :] = jnp.broadcast_to(m + jnp.log(l), (QB, LANES))
        else:
            outs[0][sl, :] = lax.dot_general(e.astype(BF16), vw, NN, preferred_element_type=F32)
            outs[1][sl, :] = jnp.broadcast_to(m, (QB, LANES))
            outs[2][sl, :] = jnp.broadcast_to(l, (QB, LANES))


def _wattn_bwd_body(q_ref, do_ref, lse_ref, dl_ref, k_ref, v_ref, b_ref, dq_ref, dk_ref, dv_ref, db_ref, *,
                    QB, SUB, W, kind, L_valid, rows, scale, dgroup):
    cb = pl.program_id(0)
    n = pl.program_id(1)

    @pl.when(n == 0)
    def _():
        dk_ref[...] = jnp.zeros_like(dk_ref)
        dv_ref[...] = jnp.zeros_like(dv_ref)

    @pl.when((n == 0) & (cb % dgroup == 0))
    def _():
        db_ref[...] = jnp.zeros_like(db_ref)

    wins = [_window(kind, n * SUB + i, QB, rows) for i in range(SUB)]
    sls = [slice(i * QB, (i + 1) * QB) for i in range(SUB)]
    ss = [_scores(q_ref[sl, :], k_ref[pl.ds(start, W), :], b_ref[pat], kind, start, QB, W, L_valid, scale)
          for sl, (start, pat) in zip(sls, wins)]
    dos = [do_ref[sl, :].astype(BF16) for sl in sls]
    dps = [lax.dot_general(do, v_ref[pl.ds(start, W), :], NT, preferred_element_type=F32)
           for do, (start, _) in zip(dos, wins)]
    ps = [jnp.exp(s - lse_ref[sl, :][:, :1]) for s, sl in zip(ss, sls)]
    if kind == "na":
        dls = [jnp.sum(do_ref[sl, :].astype(F32) * dl_ref[sl, :].astype(F32), axis=1, keepdims=True) for sl in sls]
    else:
        dls = [dl_ref[sl, :][:, :1] for sl in sls]
    dss = [p * (dp - dl) for p, dp, dl in zip(ps, dps, dls)]
    for sl, (start, pat), p, ds, do in zip(sls, wins, ps, dss, dos):
        q = q_ref[sl, :]
        db_ref[pat] += ds
        dsb = ds.astype(BF16)
        dq_ref[sl, :] = (lax.dot_general(dsb, k_ref[pl.ds(start, W), :], NN, preferred_element_type=F32)
                         * scale).astype(dq_ref.dtype)
        dk_ref[pl.ds(start, W), :] += lax.dot_general(dsb, q, TN, preferred_element_type=F32) * scale
        dv_ref[pl.ds(start, W), :] += lax.dot_general(p.astype(BF16), do, TN, preferred_element_type=F32)


def _wattn_geometry(kind, LQ, dil):
    if kind == "dil":
        QB, W, rows = A_QBLOCK, A_QBLOCK + 2 * DIL_HALF, 0
    else:
        rows = LQ // GRID_W
        QB, W = GRID_W, min(NA_ROWS_MAX, rows) * GRID_W
    blocks = LQ // QB
    SUB = 8 if blocks % 8 == 0 else 4 if blocks % 4 == 0 else 1
    return QB, W, rows, SUB


def _wattn_fwd(q, k, v, bias, *, kind, H, dil, qoff, koff, voff, name):
    LQ, LK = q.shape[0], k.shape[0]
    QB, W, rows, SUB = _wattn_geometry(kind, LQ, dil)
    ncb = H * dil
    NP = bias.shape[1]

    def col(cb, off):
        return off + (cb % dil) * H + cb // dil

    in_specs = [pl.BlockSpec((QB * SUB, LANES), lambda cb, n: (n, col(cb, qoff))),
                pl.BlockSpec((LK, LANES), lambda cb, n: (0, col(cb, koff))),
                pl.BlockSpec((LK, LANES), lambda cb, n: (0, col(cb, voff))),
                pl.BlockSpec((None, NP, QB, W), lambda cb, n: (cb // dil, 0, 0, 0))]
    o_spec = pl.BlockSpec((QB * SUB, LANES), lambda cb, n: (n, col(cb, 0)))
    shape = (LQ, ncb * LANES)
    if kind == "na":
        out_shape = (jax.ShapeDtypeStruct(shape, BF16), jax.ShapeDtypeStruct(shape, F32))
    else:
        out_shape = (jax.ShapeDtypeStruct(shape, F32),) * 3
    return _pcall(_wattn_fwd_body, name=name, grid=(ncb, LQ // (QB * SUB)), in_specs=in_specs,
                  out_specs=(o_spec,) * len(out_shape), out_shape=out_shape, sem=("parallel", "parallel"),
                  QB=QB, SUB=SUB, W=W, kind=kind, L_valid=LQ, rows=rows, scale=1.0 / math.sqrt(HEAD_DIM))(
        q, k, v, bias)


def _wattn_bwd(q, do, lse, delta, k, v, bias, *, kind, H, dil, qoff, koff, voff, dq_dtype, name):
    LQ, LK = q.shape[0], k.shape[0]
    QB, W, rows, SUB = _wattn_geometry(kind, LQ, dil)
    ncb = H * dil
    NP = bias.shape[1]

    def col(cb, off):
        return off + (cb % dil) * H + cb // dil

    q_spec = lambda off: pl.BlockSpec((QB * SUB, LANES), lambda cb, n: (n, col(cb, off)))
    kv_spec = lambda off: pl.BlockSpec((LK, LANES), lambda cb, n: (0, col(cb, off)))
    b_spec = pl.BlockSpec((None, NP, QB, W), lambda cb, n: (cb // dil, 0, 0, 0))
    in_specs = [q_spec(qoff), q_spec(0), q_spec(0), q_spec(0), kv_spec(koff), kv_spec(voff), b_spec]
    out_shape = (jax.ShapeDtypeStruct((LQ, ncb * LANES), dq_dtype), jax.ShapeDtypeStruct((LK, ncb * LANES), F32),
                 jax.ShapeDtypeStruct((LK, ncb * LANES), F32), jax.ShapeDtypeStruct(bias.shape, F32))
    out_specs = (q_spec(0), kv_spec(0), kv_spec(0), b_spec)
    return _pcall(_wattn_bwd_body, name=name, grid=(ncb, LQ // (QB * SUB)), in_specs=in_specs, out_specs=out_specs,
                  out_shape=out_shape, sem=("arbitrary", "arbitrary"),
                  QB=QB, SUB=SUB, W=W, kind=kind, L_valid=LQ, rows=rows, scale=1.0 / math.sqrt(HEAD_DIM),
                  dgroup=dil)(q, do, lse, delta, k, v, bias)


def _attn_delta(do, do_off, o, *, name):
    def fn(do, o):
        return jnp.broadcast_to(jnp.sum(do * o.astype(F32), axis=1, keepdims=True), do.shape), do

    return _ew(fn, [(do, do_off), o], out_dtypes=[F32, BF16], width=LANES, ncol=o.shape[1] // LANES, tr=512,
               name=name)


def _scan_fwd_body(bre_ref, bim_ref, are_ref, aim_ref, xre_ref, xim_ref, cr_ref, ci_ref, *, TC, reverse):
    @pl.when(pl.program_id(1) == 0)
    def _():
        cr_ref[...] = jnp.zeros_like(cr_ref)
        ci_ref[...] = jnp.zeros_like(ci_ref)

    ar = are_ref[...]
    ai = aim_ref[...]

    def step(s, carry):
        xr, xi = carry
        tau = TC - 1 - s if reverse else s
        nxr = ar * xr - ai * xi + bre_ref[pl.ds(tau, 1), :]
        nxi = ar * xi + ai * xr + bim_ref[pl.ds(tau, 1), :]
        xre_ref[pl.ds(tau, 1), :] = nxr
        xim_ref[pl.ds(tau, 1), :] = nxi
        return nxr, nxi

    xr, xi = lax.fori_loop(0, TC, step, (cr_ref[0:1, :], ci_ref[0:1, :]), unroll=8)
    cr_ref[0:1, :] = xr
    ci_ref[0:1, :] = xi


def _scan_geometry(S, NCH):
    return _tile(S, 512), _tile(NCH, 512)


def _scan_fwd(bu_re, bu_im, a_re, a_im, *, reverse, name):
    S, NCH = bu_re.shape
    TC, LB = _scan_geometry(S, NCH)
    nT = S // TC
    tmap = (lambda l, t: (nT - 1 - t, l)) if reverse else (lambda l, t: (t, l))
    row = pl.BlockSpec((TC, LB), tmap)
    vec = pl.BlockSpec((1, LB), lambda l, t: (0, l))
    return _pcall(_scan_fwd_body, name=name, grid=(NCH // LB, nT), in_specs=[row, row, vec, vec],
                  out_specs=(row, row), out_shape=(jax.ShapeDtypeStruct((S, NCH), F32),) * 2,
                  scratch=[pltpu.VMEM((8, LB), F32), pltpu.VMEM((8, LB), F32)], sem=("parallel", "arbitrary"),
                  TC=TC, reverse=reverse)(bu_re, bu_im, a_re, a_im)


def _scan_bwd_body(gre_ref, gim_ref, xre_ref, xim_ref, are_ref, aim_ref, hre_ref, him_ref, dar_ref, dai_ref,
                   cr_ref, ci_ref, *, TC, reverse):
    @pl.when(pl.program_id(1) == 0)
    def _():
        cr_ref[...] = jnp.zeros_like(cr_ref)
        ci_ref[...] = jnp.zeros_like(ci_ref)
        dar_ref[...] = jnp.zeros_like(dar_ref)
        dai_ref[...] = jnp.zeros_like(dai_ref)

    ar = are_ref[...]
    ai = aim_ref[...]

    def step(s, carry):
        hr, hi, sr, si = carry
        tau = TC - 1 - s if reverse else s
        xr = xre_ref[pl.ds(tau, 1), :]
        xi = xim_ref[pl.ds(tau, 1), :]
        sr = sr + (hr * xr + hi * xi)
        si = si + (hi * xr - hr * xi)
        nhr = gre_ref[pl.ds(tau, 1), :] + (ar * hr + ai * hi)
        nhi = gim_ref[pl.ds(tau, 1), :] + (ar * hi - ai * hr)
        hre_ref[pl.ds(tau, 1), :] = nhr
        him_ref[pl.ds(tau, 1), :] = nhi
        return nhr, nhi, sr, si

    z = jnp.zeros_like(ar)
    hr, hi, sr, si = lax.fori_loop(0, TC, step, (cr_ref[0:1, :], ci_ref[0:1, :], z, z), unroll=8)
    cr_ref[0:1, :] = hr
    ci_ref[0:1, :] = hi
    dar_ref[...] += sr
    dai_ref[...] += si


def _scan_bwd(g_re, g_im, x_re, x_im, a_re, a_im, *, reverse, name):
    S, NCH = g_re.shape
    TC, LB = _scan_geometry(S, NCH)
    nT = S // TC
    back = not reverse
    tmap = (lambda l, t: (nT - 1 - t, l)) if back else (lambda l, t: (t, l))
    row = pl.BlockSpec((TC, LB), tmap)
    vec = pl.BlockSpec((1, LB), lambda l, t: (0, l))
    return _pcall(_scan_bwd_body, name=name, grid=(NCH // LB, nT), in_specs=[row, row, row, row, vec, vec],
                  out_specs=(row, row, vec, vec),
                  out_shape=(jax.ShapeDtypeStruct((S, NCH), F32),) * 2 + (jax.ShapeDtypeStruct((1, NCH), F32),) * 2,
                  scratch=[pltpu.VMEM((8, LB), F32), pltpu.VMEM((8, LB), F32)], sem=("parallel", "arbitrary"),
                  TC=TC, reverse=back)(g_re, g_im, x_re, x_im, a_re, a_im)


def _bf(ref):
    return ref[...].astype(BF16)


def _s5_fwd_body(u_ref, br_ref, bi_ref, cr_ref, cin_ref, are_ref, aim_ref, xre_ref, xim_ref, y_ref,
                 bre_s, bim_s, car_r, car_i, *, TC, reverse):
    @pl.when(pl.program_id(1) == 0)
    def _():
        car_r[...] = jnp.zeros_like(car_r)
        car_i[...] = jnp.zeros_like(car_i)

    ub = _bf(u_ref)
    bre_s[...] = lax.dot_general(ub, _bf(br_ref), NN, preferred_element_type=F32)
    bim_s[...] = lax.dot_general(ub, _bf(bi_ref), NN, preferred_element_type=F32)
    ar = are_ref[...]
    ai = aim_ref[...]

    def step(s, carry):
        xr, xi = carry
        tau = TC - 1 - s if reverse else s
        nxr = ar * xr - ai * xi + bre_s[pl.ds(tau, 1), :]
        nxi = ar * xi + ai * xr + bim_s[pl.ds(tau, 1), :]
        xre_ref[pl.ds(tau, 1), :] = nxr
        xim_ref[pl.ds(tau, 1), :] = nxi
        return nxr, nxi

    xr, xi = lax.fori_loop(0, TC, step, (car_r[0:1, :], car_i[0:1, :]), unroll=8)
    car_r[0:1, :] = xr
    car_i[0:1, :] = xi
    y_ref[...] = (lax.dot_general(_bf(xre_ref), _bf(cr_ref), NN, preferred_element_type=F32)
                  + lax.dot_general(_bf(xim_ref), _bf(cin_ref), NN, preferred_element_type=F32))


def _s5_block_specs(S, BW, reverse):
    NB = BW // LANES
    TC = _tile(S, 512)
    nT = S // TC
    tmap = (lambda l, t: (nT - 1 - t, l)) if reverse else (lambda l, t: (t, l))
    narrow = pl.BlockSpec((TC, LANES), tmap)
    wide = pl.BlockSpec((TC, STATE_PER_BLOCK), tmap)
    vec = pl.BlockSpec((1, STATE_PER_BLOCK), lambda l, t: (0, l))
    w_in = pl.BlockSpec((None, LANES, STATE_PER_BLOCK), lambda l, t: (l, 0, 0))
    w_out = pl.BlockSpec((None, STATE_PER_BLOCK, LANES), lambda l, t: (l, 0, 0))
    return NB, TC, nT, narrow, wide, vec, w_in, w_out


def _s5_scan_fwd(u, mat, *, reverse, name):
    a_re, a_im, b_r, b_i, c_r, c_in = mat
    S, BW = u.shape
    NB, TC, nT, narrow, wide, vec, w_in, w_out = _s5_block_specs(S, BW, reverse)
    state = jax.ShapeDtypeStruct((S, NB * STATE_PER_BLOCK), F32)
    return _pcall(_s5_fwd_body, name=name, grid=(NB, nT), in_specs=[narrow, w_in, w_in, w_out, w_out, vec, vec],
                  out_specs=(wide, wide, narrow), out_shape=(state, state, jax.ShapeDtypeStruct((S, BW), F32)),
                  scratch=[pltpu.VMEM((TC, STATE_PER_BLOCK), F32)] * 2 + [pltpu.VMEM((8, STATE_PER_BLOCK), F32)] * 2,
                  sem=("parallel", "arbitrary"), TC=TC, reverse=reverse)(u, b_r, b_i, c_r, c_in, a_re, a_im)


def _s5_bwd_body(dy_ref, u_ref, xre_ref, xim_ref, br_ref, bi_ref, cr_ref, cin_ref, are_ref, aim_ref,
                 du_ref, dar_ref, dai_ref, dbr_ref, dbi_ref, dcr_ref, dcin_ref, hre_s, him_s, car_r, car_i, *,
                 TC, reverse):
    @pl.when(pl.program_id(1) == 0)
    def _():
        for r in (car_r, car_i, dar_ref, dai_ref, dbr_ref, dbi_ref, dcr_ref, dcin_ref):
            r[...] = jnp.zeros_like(r)

    dyb = _bf(dy_ref)
    hre_s[...] = lax.dot_general(dyb, _bf(cr_ref), NT, preferred_element_type=F32)
    him_s[...] = lax.dot_general(dyb, _bf(cin_ref), NT, preferred_element_type=F32)
    dcr_ref[...] += lax.dot_general(_bf(xre_ref), dyb, TN, preferred_element_type=F32)
    dcin_ref[...] += lax.dot_general(_bf(xim_ref), dyb, TN, preferred_element_type=F32)
    ar = are_ref[...]
    ai = aim_ref[...]

    def step(s, carry):
        hr, hi, sr, si = carry
        tau = TC - 1 - s if reverse else s
        xr = xre_ref[pl.ds(tau, 1), :]
        xi = xim_ref[pl.ds(tau, 1), :]
        sr = sr + (hr * xr + hi * xi)
        si = si + (hi * xr - hr * xi)
        nhr = hre_s[pl.ds(tau, 1), :] + (ar * hr + ai * hi)
        nhi = him_s[pl.ds(tau, 1), :] + (ar * hi - ai * hr)
        hre_s[pl.ds(tau, 1), :] = nhr
        him_s[pl.ds(tau, 1), :] = nhi
        return nhr, nhi, sr, si

    z = jnp.zeros_like(ar)
    hr, hi, sr, si = lax.fori_loop(0, TC, step, (car_r[0:1, :], car_i[0:1, :], z, z), unroll=8)
    car_r[0:1, :] = hr
    car_i[0:1, :] = hi
    dar_ref[...] += sr
    dai_ref[...] += si
    hrb, hib, ub = _bf(hre_s), _bf(him_s), _bf(u_ref)
    du_ref[...] = (lax.dot_general(hrb, _bf(br_ref), NT, preferred_element_type=F32)
                   + lax.dot_general(hib, _bf(bi_ref), NT, preferred_element_type=F32))
    dbr_ref[...] += lax.dot_general(ub, hrb, TN, preferred_element_type=F32)
    dbi_ref[...] += lax.dot_general(ub, hib, TN, preferred_element_type=F32)


def _s5_scan_bwd(dy, u, x_re, x_im, mat, *, reverse, name):
    a_re, a_im, b_r, b_i, c_r, c_in = mat
    S, BW = u.shape
    NB, TC, nT, narrow, wide, vec, w_in, w_out = _s5_block_specs(S, BW, not reverse)
    shapes = (jax.ShapeDtypeStruct((S, BW), F32),) + tuple(jax.ShapeDtypeStruct(m.shape, F32) for m in mat)
    res = _pcall(_s5_bwd_body, name=name, grid=(NB, nT),
                 in_specs=[narrow, narrow, wide, wide, w_in, w_in, w_out, w_out, vec, vec],
                 out_specs=(narrow, vec, vec, w_in, w_in, w_out, w_out), out_shape=shapes,
                 scratch=[pltpu.VMEM((TC, STATE_PER_BLOCK), F32)] * 2 + [pltpu.VMEM((8, STATE_PER_BLOCK), F32)] * 2,
                 sem=("parallel", "arbitrary"), TC=TC, reverse=not reverse)(
        dy, u, x_re, x_im, b_r, b_i, c_r, c_in, a_re, a_im)
    return res[0], tuple(res[1:])


SUBLANES = 8
QUADS = STATE_PER_BLOCK // LANES


def _cmul(ar, ai, br, bi):
    return ar * br - ai * bi, ar * bi + ai * br


def _seg_rows(tau, SEG, desc):
    return pl.ds(SEG - 1 - tau if desc else tau, SUBLANES, stride=SEG)


def _seg_carries(j, desc, end_r, end_i, a64r, a64i, car_r, car_i, cst_r, cst_i):
    cols = slice(j * LANES, (j + 1) * LANES)
    cr, ci = car_r[0:1, cols], car_i[0:1, cols]
    for i in (reversed(range(SUBLANES)) if desc else range(SUBLANES)):
        cst_r[j, i:i + 1, :] = cr
        cst_i[j, i:i + 1, :] = ci
        pr, pi = _cmul(a64r, a64i, cr, ci)
        cr, ci = pr + end_r[j, i:i + 1, :], pi + end_i[j, i:i + 1, :]
    car_r[0:1, cols] = cr
    car_i[0:1, cols] = ci


def _seg_correct(j, SEG, xr_ref, xi_ref, pr_ref, pi_ref, cst_r, cst_i):
    cols = slice(j * LANES, (j + 1) * LANES)
    pr, pi = pr_ref[:, cols], pi_ref[:, cols]
    for i in range(SUBLANES):
        rows = slice(i * SEG, (i + 1) * SEG)
        dr, di = _cmul(pr, pi, cst_r[j, i:i + 1, :], cst_i[j, i:i + 1, :])
        xr_ref[j, rows, :] += dr
        xi_ref[j, rows, :] += di


def _s5seg_fwd_body(u_ref, br_ref, bi_ref, cr_ref, cin_ref, are_ref, aim_ref, pr_ref, pi_ref, a64r_ref, a64i_ref,
                    xre_ref, xim_ref, y_ref, bre_s, bim_s, car_r, car_i, cst_r, cst_i, end_r, end_i, *, TC, desc):
    SEG = TC // SUBLANES

    @pl.when(pl.program_id(1) == 0)
    def _():
        car_r[...] = jnp.zeros_like(car_r)
        car_i[...] = jnp.zeros_like(car_i)

    ub = _bf(u_ref)
    cols = [slice(j * LANES, (j + 1) * LANES) for j in range(QUADS)]
    for j in range(QUADS):
        bre_s[j] = lax.dot_general(ub, br_ref[:, cols[j]].astype(BF16), NN, preferred_element_type=F32)
        bim_s[j] = lax.dot_general(ub, bi_ref[:, cols[j]].astype(BF16), NN, preferred_element_type=F32)
    ar = [are_ref[:, c] for c in cols]
    ai = [aim_ref[:, c] for c in cols]
    xr = [jnp.zeros((SUBLANES, LANES), F32)] * QUADS
    xi = [jnp.zeros((SUBLANES, LANES), F32)] * QUADS
    for tau in range(SEG):
        rows = _seg_rows(tau, SEG, desc)
        for j in range(QUADS):
            pr, pi = _cmul(ar[j], ai[j], xr[j], xi[j])
            xr[j] = pr + bre_s[j, rows, :]
            xi[j] = pi + bim_s[j, rows, :]
            xre_ref[j, rows, :] = xr[j]
            xim_ref[j, rows, :] = xi[j]
    y = None
    for j in range(QUADS):
        end_r[j] = xr[j]
        end_i[j] = xi[j]
        _seg_carries(j, desc, end_r, end_i, a64r_ref[:, cols[j]], a64i_ref[:, cols[j]], car_r, car_i, cst_r, cst_i)
        _seg_correct(j, SEG, xre_ref, xim_ref, pr_ref, pi_ref, cst_r, cst_i)
        part = (lax.dot_general(xre_ref[j].astype(BF16), cr_ref[cols[j], :].astype(BF16), NN, preferred_element_type=F32)
                + lax.dot_general(xim_ref[j].astype(BF16), cin_ref[cols[j], :].astype(BF16), NN, preferred_element_type=F32))
        y = part if y is None else y + part
    y_ref[...] = y


def _s5seg_specs(S, BW, desc):
    NB = BW // LANES
    TC = _tile(S, 512)
    nT = S // TC
    tmap = (lambda l, t: (nT - 1 - t, l)) if desc else (lambda l, t: (t, l))
    xmap = (lambda l, t: (l, nT - 1 - t, 0)) if desc else (lambda l, t: (l, t, 0))
    narrow = pl.BlockSpec((TC, LANES), tmap)
    state = pl.BlockSpec((QUADS, TC, LANES), xmap)
    vec = pl.BlockSpec((1, STATE_PER_BLOCK), lambda l, t: (0, l))
    tab = pl.BlockSpec((TC // SUBLANES, STATE_PER_BLOCK), lambda l, t: (0, l))
    w_in = pl.BlockSpec((None, LANES, STATE_PER_BLOCK), lambda l, t: (l, 0, 0))
    w_out = pl.BlockSpec((None, STATE_PER_BLOCK, LANES), lambda l, t: (l, 0, 0))
    scratch = ([pltpu.VMEM((QUADS, TC, LANES), F32)] * 2 + [pltpu.VMEM((SUBLANES, STATE_PER_BLOCK), F32)] * 2
               + [pltpu.VMEM((QUADS, SUBLANES, LANES), F32)] * 4)
    return NB, TC, nT, narrow, state, vec, tab, w_in, w_out, scratch


def _powers(ar, ai, n):
    pr, pi = jnp.ones_like(ar), jnp.zeros_like(ai)
    mr, mi = ar, ai
    while pr.shape[0] < n + 1:
        qr, qi = _cmul(mr, mi, pr, pi)
        pr, pi = jnp.concatenate([pr, qr]), jnp.concatenate([pi, qi])
        mr, mi = _cmul(mr, mi, mr, mi)
    return pr[:n + 1], pi[:n + 1]


def _seg_tables(a_re, a_im, SEG, desc):
    pr, pi = _powers(lax.stop_gradient(a_re), lax.stop_gradient(a_im), SEG)
    nat = (lambda t: t[::-1]) if desc else (lambda t: t)
    return nat(pr[1:]), nat(pi[1:]), pr[:-1], pi[:-1], pr[SEG:], pi[SEG:]


def _s5_scan_fwd(u, mat, *, reverse, name):
    a_re, a_im, b_r, b_i, c_r, c_in = mat
    S, BW = u.shape
    NB, TC, nT, narrow, state, vec, tab, w_in, w_out, scratch = _s5seg_specs(S, BW, reverse)
    p_re, p_im, _, _, a64r, a64i = _seg_tables(a_re, a_im, TC // SUBLANES, reverse)
    xs = jax.ShapeDtypeStruct((NB * QUADS, S, LANES), F32)
    return _pcall(_s5seg_fwd_body, name=name, grid=(NB, nT),
                  in_specs=[narrow, w_in, w_in, w_out, w_out, vec, vec, tab, tab, vec, vec],
                  out_specs=(state, state, narrow), out_shape=(xs, xs, jax.ShapeDtypeStruct((S, BW), F32)),
                  scratch=scratch, sem=("parallel", "arbitrary"), TC=TC, desc=reverse)(
        u, b_r, b_i, c_r, c_in, a_re, a_im, p_re, p_im, a64r, a64i)


def _s5seg_bwd_body(dy_ref, u_ref, xre_ref, xim_ref, br_ref, bi_ref, cr_ref, cin_ref, are_ref, aim_ref,
                    pr_ref, pi_ref, qr_ref, qi_ref, a64r_ref, a64i_ref,
                    du_ref, dar_ref, dai_ref, dbr_ref, dbi_ref, dcr_ref, dcin_ref,
                    hre_s, him_s, car_r, car_i, cst_r, cst_i, end_r, end_i, *, TC, desc):
    SEG = TC // SUBLANES

    @pl.when(pl.program_id(1) == 0)
    def _():
        for r in (car_r, car_i, dar_ref, dai_ref, dbr_ref, dbi_ref, dcr_ref, dcin_ref):
            r[...] = jnp.zeros_like(r)

    dyb = _bf(dy_ref)
    cols = [slice(j * LANES, (j + 1) * LANES) for j in range(QUADS)]
    for j in range(QUADS):
        hre_s[j] = lax.dot_general(dyb, cr_ref[cols[j], :].astype(BF16), NT, preferred_element_type=F32)
        him_s[j] = lax.dot_general(dyb, cin_ref[cols[j], :].astype(BF16), NT, preferred_element_type=F32)
        dcr_ref[cols[j], :] += lax.dot_general(xre_ref[j].astype(BF16), dyb, TN, preferred_element_type=F32)
        dcin_ref[cols[j], :] += lax.dot_general(xim_ref[j].astype(BF16), dyb, TN, preferred_element_type=F32)
    ar = [are_ref[:, c] for c in cols]
    ai = [aim_ref[:, c] for c in cols]
    zero = [jnp.zeros((SUBLANES, LANES), F32)] * QUADS
    hr, hi, dr, di, er, ei = (list(zero) for _ in range(6))
    for tau in range(SEG):
        rows = _seg_rows(tau, SEG, desc)
        for j in range(QUADS):
            xr, xi = xre_ref[j, rows, :], xim_ref[j, rows, :]
            qr, qi = qr_ref[tau:tau + 1, cols[j]], qi_ref[tau:tau + 1, cols[j]]
            dr[j] = dr[j] + (hr[j] * xr + hi[j] * xi)
            di[j] = di[j] + (hi[j] * xr - hr[j] * xi)
            er[j] = er[j] + (qr * xr + qi * xi)
            ei[j] = ei[j] + (qi * xr - qr * xi)
            pr, pi = _cmul(ar[j], ai[j], hr[j], hi[j])
            hr[j] = pr + hre_s[j, rows, :]
            hi[j] = pi + him_s[j, rows, :]
            hre_s[j, rows, :] = hr[j]
            him_s[j, rows, :] = hi[j]
    ub = _bf(u_ref)
    du = None
    for j in range(QUADS):
        end_r[j] = hr[j]
        end_i[j] = hi[j]
        _seg_carries(j, desc, end_r, end_i, a64r_ref[:, cols[j]], a64i_ref[:, cols[j]], car_r, car_i, cst_r, cst_i)
        fr, fi = _cmul(cst_r[j], cst_i[j], er[j], ei[j])
        dar_ref[:, cols[j]] += jnp.sum(dr[j] + fr, axis=0, keepdims=True)
        dai_ref[:, cols[j]] += jnp.sum(di[j] + fi, axis=0, keepdims=True)
        _seg_correct(j, SEG, hre_s, him_s, pr_ref, pi_ref, cst_r, cst_i)
        hrb, hib = hre_s[j].astype(BF16), him_s[j].astype(BF16)
        part = (lax.dot_general(hrb, br_ref[:, cols[j]].astype(BF16), NT, preferred_element_type=F32)
                + lax.dot_general(hib, bi_ref[:, cols[j]].astype(BF16), NT, preferred_element_type=F32))
        du = part if du is None else du + part
        dbr_ref[:, cols[j]] += lax.dot_general(ub, hrb, TN, preferred_element_type=F32)
        dbi_ref[:, cols[j]] += lax.dot_general(ub, hib, TN, preferred_element_type=F32)
    du_ref[...] = du


def _s5_scan_bwd(dy, u, x_re, x_im, mat, *, reverse, name):
    a_re, a_im, b_r, b_i, c_r, c_in = mat
    S, BW = u.shape
    desc = not reverse
    NB, TC, nT, narrow, state, vec, tab, w_in, w_out, scratch = _s5seg_specs(S, BW, desc)
    p_re, p_im, q_re, q_im, a64r, a64i = _seg_tables(a_re, -a_im, TC // SUBLANES, desc)
    shapes = (jax.ShapeDtypeStruct((S, BW), F32),) + tuple(jax.ShapeDtypeStruct(m.shape, F32) for m in mat)
    res = _pcall(_s5seg_bwd_body, name=name, grid=(NB, nT),
                 in_specs=[narrow, narrow, state, state, w_in, w_in, w_out, w_out, vec, vec, tab, tab, tab, tab, vec, vec],
                 out_specs=(narrow, vec, vec, w_in, w_in, w_out, w_out), out_shape=shapes,
                 scratch=scratch, sem=("parallel", "arbitrary"), TC=TC, desc=desc)(
        dy, u, x_re, x_im, b_r, b_i, c_r, c_in, a_re, -a_im, p_re, p_im, q_re, q_im, a64r, a64i)
    return res[0], tuple(res[1:])


def _interleave(t, inverse=False):
    S, C = t.shape
    TC = _tile(S, 512)
    a, b = (TC // SUBLANES, SUBLANES) if inverse else (SUBLANES, TC // SUBLANES)
    return t.reshape(S // TC, a, b, C).transpose(0, 2, 1, 3).reshape(S, C)


def _tile_carries(desc, end_r, end_i, a64r, a64i, car_r, car_i, cst_r, cst_i):
    cr, ci = car_r[0:1, :], car_i[0:1, :]
    for i in (reversed(range(SUBLANES)) if desc else range(SUBLANES)):
        cst_r[i:i + 1, :] = cr
        cst_i[i:i + 1, :] = ci
        pr, pi = _cmul(a64r, a64i, cr, ci)
        cr, ci = pr + end_r[i:i + 1, :], pi + end_i[i:i + 1, :]
    car_r[0:1, :] = cr
    car_i[0:1, :] = ci


def _tile_correct(SEG, xr_ref, xi_ref, pr_ref, pi_ref, cst_r, cst_i):
    cr, ci = cst_r[...], cst_i[...]
    for t in range(SEG):
        rows = pl.ds(SUBLANES * t, SUBLANES)
        dr, di = _cmul(pr_ref[t:t + 1, :], pi_ref[t:t + 1, :], cr, ci)
        xr_ref[rows, :] += dr
        xi_ref[rows, :] += di


def _s5il_fwd_body(u_ref, br_ref, bi_ref, cr_ref, cin_ref, are_ref, aim_ref, pr_ref, pi_ref, a64r_ref, a64i_ref,
                   xre_ref, xim_ref, y_ref, bre_s, bim_s, car_r, car_i, cst_r, cst_i, end_r, end_i, *, TC, desc):
    SEG = TC // SUBLANES

    @pl.when(pl.program_id(1) == 0)
    def _():
        car_r[...] = jnp.zeros_like(car_r)
        car_i[...] = jnp.zeros_like(car_i)

    ub = _bf(u_ref)
    bre_s[...] = lax.dot_general(ub, _bf(br_ref), NN, preferred_element_type=F32)
    bim_s[...] = lax.dot_general(ub, _bf(bi_ref), NN, preferred_element_type=F32)
    ar, ai = are_ref[...], aim_ref[...]
    xr = xi = jnp.zeros((SUBLANES, STATE_PER_BLOCK), F32)
    for tau in range(SEG):
        rows = pl.ds(SUBLANES * (SEG - 1 - tau if desc else tau), SUBLANES)
        pr, pi = _cmul(ar, ai, xr, xi)
        xr = pr + bre_s[rows, :]
        xi = pi + bim_s[rows, :]
        xre_ref[rows, :] = xr
        xim_ref[rows, :] = xi
    end_r[...] = xr
    end_i[...] = xi
    _tile_carries(desc, end_r, end_i, a64r_ref[...], a64i_ref[...], car_r, car_i, cst_r, cst_i)
    _tile_correct(SEG, xre_ref, xim_ref, pr_ref, pi_ref, cst_r, cst_i)
    y_ref[...] = (lax.dot_general(_bf(xre_ref), _bf(cr_ref), NN, preferred_element_type=F32)
                  + lax.dot_general(_bf(xim_ref), _bf(cin_ref), NN, preferred_element_type=F32))


def _s5il_specs(S, BW, desc):
    NB = BW // LANES
    TC = _tile(S, 512)
    nT = S // TC
    tmap = (lambda l, t: (nT - 1 - t, l)) if desc else (lambda l, t: (t, l))
    narrow = pl.BlockSpec((TC, LANES), tmap)
    wide = pl.BlockSpec((TC, STATE_PER_BLOCK), tmap)
    vec = pl.BlockSpec((1, STATE_PER_BLOCK), lambda l, t: (0, l))
    tab = pl.BlockSpec((TC // SUBLANES, STATE_PER_BLOCK), lambda l, t: (0, l))
    w_in = pl.BlockSpec((None, LANES, STATE_PER_BLOCK), lambda l, t: (l, 0, 0))
    w_out = pl.BlockSpec((None, STATE_PER_BLOCK, LANES), lambda l, t: (l, 0, 0))
    scratch = [pltpu.VMEM((TC, STATE_PER_BLOCK), F32)] * 2 + [pltpu.VMEM((SUBLANES, STATE_PER_BLOCK), F32)] * 6
    return NB, TC, nT, narrow, wide, vec, tab, w_in, w_out, scratch


def _s5_scan_fwd(u, mat, *, reverse, name):
    a_re, a_im, b_r, b_i, c_r, c_in = mat
    S, BW = u.shape
    NB, TC, nT, narrow, wide, vec, tab, w_in, w_out, scratch = _s5il_specs(S, BW, reverse)
    p_re, p_im, _, _, a64r, a64i = _seg_tables(a_re, a_im, TC // SUBLANES, reverse)
    xs = jax.ShapeDtypeStruct((S, NB * STATE_PER_BLOCK), F32)
    return _pcall(_s5il_fwd_body, name=name, grid=(NB, nT),
                  in_specs=[narrow, w_in, w_in, w_out, w_out, vec, vec, tab, tab, vec, vec],
                  out_specs=(wide, wide, narrow), out_shape=(xs, xs, jax.ShapeDtypeStruct((S, BW), F32)),
                  scratch=scratch, sem=("parallel", "arbitrary"), TC=TC, desc=reverse)(
        u, b_r, b_i, c_r, c_in, a_re, a_im, p_re, p_im, a64r, a64i)


def _s5il_bwd_body(dy_ref, u_ref, xre_ref, xim_ref, br_ref, bi_ref, cr_ref, cin_ref, are_ref, aim_ref,
                   pr_ref, pi_ref, qr_ref, qi_ref, a64r_ref, a64i_ref,
                   du_ref, dar_ref, dai_ref, dbr_ref, dbi_ref, dcr_ref, dcin_ref,
                   hre_s, him_s, car_r, car_i, cst_r, cst_i, end_r, end_i, *, TC, desc):
    SEG = TC // SUBLANES

    @pl.when(pl.program_id(1) == 0)
    def _():
        for r in (car_r, car_i, dar_ref, dai_ref, dbr_ref, dbi_ref, dcr_ref, dcin_ref):
            r[...] = jnp.zeros_like(r)

    dyb = _bf(dy_ref)
    hre_s[...] = lax.dot_general(dyb, _bf(cr_ref), NT, preferred_element_type=F32)
    him_s[...] = lax.dot_general(dyb, _bf(cin_ref), NT, preferred_element_type=F32)
    dcr_ref[...] += lax.dot_general(_bf(xre_ref), dyb, TN, preferred_element_type=F32)
    dcin_ref[...] += lax.dot_general(_bf(xim_ref), dyb, TN, preferred_element_type=F32)
    ar, ai = are_ref[...], aim_ref[...]
    hr = hi = dr = di = er = ei = jnp.zeros((SUBLANES, STATE_PER_BLOCK), F32)
    for tau in range(SEG):
        rows = pl.ds(SUBLANES * (SEG - 1 - tau if desc else tau), SUBLANES)
        xr, xi = xre_ref[rows, :], xim_ref[rows, :]
        qr, qi = qr_ref[tau:tau + 1, :], qi_ref[tau:tau + 1, :]
        dr = dr + (hr * xr + hi * xi)
        di = di + (hi * xr - hr * xi)
        er = er + (qr * xr + qi * xi)
        ei = ei + (qi * xr - qr * xi)
        pr, pi = _cmul(ar, ai, hr, hi)
        hr = pr + hre_s[rows, :]
        hi = pi + him_s[rows, :]
        hre_s[rows, :] = hr
        him_s[rows, :] = hi
    end_r[...] = hr
    end_i[...] = hi
    _tile_carries(desc, end_r, end_i, a64r_ref[...], a64i_ref[...], car_r, car_i, cst_r, cst_i)
    fr, fi = _cmul(cst_r[...], cst_i[...], er, ei)
    dar_ref[...] += jnp.sum(dr + fr, axis=0, keepdims=True)
    dai_ref[...] += jnp.sum(di + fi, axis=0, keepdims=True)
    _tile_correct(SEG, hre_s, him_s, pr_ref, pi_ref, cst_r, cst_i)
    hrb, hib, ub = _bf(hre_s), _bf(him_s), _bf(u_ref)
    du_ref[...] = (lax.dot_general(hrb, _bf(br_ref), NT, preferred_element_type=F32)
                   + lax.dot_general(hib, _bf(bi_ref), NT, preferred_element_type=F32))
    dbr_ref[...] += lax.dot_general(ub, hrb, TN, preferred_element_type=F32)
    dbi_ref[...] += lax.dot_general(ub, hib, TN, preferred_element_type=F32)


def _s5_scan_bwd(dy, u, x_re, x_im, mat, *, reverse, name):
    a_re, a_im, b_r, b_i, c_r, c_in = mat
    S, BW = u.shape
    desc = not reverse
    NB, TC, nT, narrow, wide, vec, tab, w_in, w_out, scratch = _s5il_specs(S, BW, desc)
    p_re, p_im, q_re, q_im, a64r, a64i = _seg_tables(a_re, -a_im, TC // SUBLANES, desc)
    shapes = (jax.ShapeDtypeStruct((S, BW), F32),) + tuple(jax.ShapeDtypeStruct(m.shape, F32) for m in mat)
    res = _pcall(_s5il_bwd_body, name=name, grid=(NB, nT),
                 in_specs=[narrow, narrow, wide, wide, w_in, w_in, w_out, w_out, vec, vec, tab, tab, tab, tab, vec, vec],
                 out_specs=(narrow, vec, vec, w_in, w_in, w_out, w_out), out_shape=shapes,
                 scratch=scratch, sem=("parallel", "arbitrary"), TC=TC, desc=desc)(
        dy, u, x_re, x_im, b_r, b_i, c_r, c_in, a_re, -a_im, p_re, p_im, q_re, q_im, a64r, a64i)
    return res[0], tuple(res[1:])


def _t5_bucket(rel):
    half = T5_BUCKETS // 2
    max_exact = half // 2
    n = jnp.abs(rel)
    nf = jnp.maximum(n, 1).astype(F32)
    large = max_exact + (jnp.log(nf / max_exact) / math.log(T5_MAX_DISTANCE / max_exact)
                         * (half - max_exact)).astype(jnp.int32)
    large = jnp.minimum(large, half - 1)
    return jnp.where(rel > 0, half, 0) + jnp.where(n < max_exact, n, large)


def _dil_bias(t5_bias, dil):
    W = A_QBLOCK + 2 * DIL_HALF
    off = jnp.arange(W)[None, :] - DIL_HALF - jnp.arange(A_QBLOCK)[:, None]
    pick = (_t5_bucket(off * dil)[..., None] == jnp.arange(T5_BUCKETS)).astype(F32)
    b = jnp.einsum('qkb,bh->hqk', pick, t5_bias.astype(F32), precision=lax.Precision.HIGHEST)
    return jnp.where(jnp.abs(off) <= DIL_HALF, b, NEG_INF)[:, None]


def _na_bias(rpb, rows):
    kr = min(NA_ROWS_MAX, rows)
    ro = (jnp.arange(kr)[None, :] - jnp.arange(kr)[:, None]) + NA_ROWS_MAX - 1
    c = jnp.arange(GRID_W)
    col_start = jnp.clip(c - NA_COLS // 2, 0, GRID_W - NA_COLS)
    col_ok = (c[None, :] >= col_start[:, None]) & (c[None, :] < col_start[:, None] + NA_COLS)
    co = jnp.clip(c[None, :] - c[:, None] + NA_COLS - 1, 0, 2 * NA_COLS - 2)
    pick_r = (ro[..., None] == jnp.arange(2 * NA_ROWS_MAX - 1)).astype(F32)
    pick_c = (co[..., None] == jnp.arange(2 * NA_COLS - 1)).astype(F32)
    b = jnp.einsum('hrqk,pjr->hpqjk',
                   jnp.einsum('hrc,qkc->hrqk', rpb.astype(F32), pick_c, precision=lax.Precision.HIGHEST),
                   pick_r, precision=lax.Precision.HIGHEST)
    b = jnp.where(col_ok[None, None, :, None, :], b, NEG_INF)
    return b.reshape(rpb.shape[0], kr, GRID_W, kr * GRID_W)


def _s5_mats(lam_re, lam_im, log_step, b_re, b_im, c_re, c_im):
    G, P, C = b_re.shape
    NB = G // GROUPS_PER_BLOCK
    eye = jnp.eye(GROUPS_PER_BLOCK, dtype=F32)

    def bd_in(bb):
        t = bb.reshape(NB, GROUPS_PER_BLOCK, P, C).transpose(0, 1, 3, 2)
        return jnp.einsum('jgcp,gh->jgchp', t, eye).reshape(NB, GROUPS_PER_BLOCK * C, GROUPS_PER_BLOCK * P)

    def bd_out(cc):
        t = cc.reshape(NB, GROUPS_PER_BLOCK, C, P).transpose(0, 1, 3, 2)
        return jnp.einsum('jgpc,gh->jgphc', t, eye).reshape(NB, GROUPS_PER_BLOCK * P, GROUPS_PER_BLOCK * C)

    out = []
    for d in range(2):
        step = jnp.exp(log_step[d].astype(F32))[:, None]
        lr = jnp.minimum(lam_re[d].astype(F32), -1e-4)
        li = lam_im[d].astype(F32)
        mag = jnp.exp(lr * step)
        ab_re = mag * jnp.cos(li * step)
        ab_im = mag * jnp.sin(li * step)
        den = lr * lr + li * li
        zr = ((ab_re - 1.0) * lr + ab_im * li) / den
        zi = (ab_im * lr - (ab_re - 1.0) * li) / den
        bb_re = zr[..., None] * b_re - zi[..., None] * b_im
        bb_im = zr[..., None] * b_im + zi[..., None] * b_re
        out.append((ab_re.reshape(1, G * P), ab_im.reshape(1, G * P), bd_in(bb_re), bd_in(bb_im),
                    bd_out(c_re[d].astype(F32)), bd_out(-c_im[d].astype(F32))))
    return tuple(out)


def _sigmoid(z):
    return 1.0 / (1.0 + jnp.exp(-z))


def _strided(t, dil, pad):
    S, C = t.shape
    t = t.reshape(S // dil, dil * C)
    return jnp.pad(t, ((DIL_HALF, DIL_HALF), (0, 0))) if pad else t


def _dilated_fwd(q, k, v, t5_bias):
    S, AW = q.shape
    H = AW // HEAD_DIM
    parts = []
    for _, dil in DILATED_BRANCHES:
        num, m, l = _wattn_fwd(_strided(q, dil, False), _strided(k, dil, True), _strided(v, dil, True),
                               _dil_bias(t5_bias, dil), kind="dil", H=H, dil=dil, qoff=0, koff=0, voff=0,
                               name=f"dilated{dil}_fwd")
        parts += [num.reshape(S, AW), m.reshape(S, AW), l.reshape(S, AW)]

    def merge(n1, m1, l1, n2, m2, l2, n3, m3, l3):
        mx = jnp.maximum(jnp.maximum(m1, m2), m3)
        w1, w2, w3 = jnp.exp(m1 - mx), jnp.exp(m2 - mx), jnp.exp(m3 - mx)
        den = w1 * l1 + w2 * l2 + w3 * l3
        o = (w1 * n1 + w2 * n2 + w3 * n3) / den
        return o, o, mx + jnp.log(den)

    return _ew(merge, parts, out_dtypes=[F32, BF16, F32], name="dilated_merge")


def _dilated_bwd(q, k, v, t5_bias, do, lse, delta):
    S, AW = q.shape
    H = AW // HEAD_DIM
    dqs, dks, dvs = [], [], []
    dt5 = jnp.zeros(t5_bias.shape, F32)
    for _, dil in DILATED_BRANCHES:
        bias, bias_vjp = jax.vjp(functools.partial(_dil_bias, dil=dil), t5_bias)
        dq, dk, dv, db = _wattn_bwd(_strided(q, dil, False), _strided(do, dil, False), _strided(lse, dil, False),
                                    _strided(delta, dil, False), _strided(k, dil, True), _strided(v, dil, True),
                                    bias, kind="dil", H=H, dil=dil, qoff=0, koff=0, voff=0, dq_dtype=F32,
                                    name=f"dilated{dil}_bwd")
        dqs.append(dq.reshape(S, AW))
        dks.append(dk[DIL_HALF:-DIL_HALF].reshape(S, AW))
        dvs.append(dv[DIL_HALF:-DIL_HALF].reshape(S, AW))
        dt5 = dt5 + bias_vjp(db)[0]
    add3 = lambda a, b, c: a + b + c
    return (_ew(add3, dqs, out_dtypes=[BF16], name="dilated_dq_sum")[0],
            _ew(add3, dks, out_dtypes=[BF16], name="dilated_dk_sum")[0],
            _ew(add3, dvs, out_dtypes=[BF16], name="dilated_dv_sum")[0], dt5)


def _s5_fwd(u, mats, d_skip, w_glu, j):
    u = _interleave(u)
    xs, ys = [], []
    for d in range(2):
        x_re, x_im, y_d = _s5_scan_fwd(u, mats[d], reverse=(d == 1), name=f"s5_scan_fwd{d}")
        xs += [x_re, x_im]
        ys.append(y_d)

    def act(y0, y1, u, dsk):
        y = (y0 + y1) + dsk * u
        return y, _gelu(y)

    y, yg = _ew(act, ys + [u], [d_skip], out_dtypes=[F32, F32], name="s5_gelu")
    z = _mm_nn(yg, w_glu, j, "row", name="s5_glu_fwd")
    ob = _ew(lambda yg, z: yg * _sigmoid(z), [yg, z], out_dtypes=[BF16], name="s5_gate")[0]
    return _interleave(ob, inverse=True), (xs, y, yg, z, u)


def _s5_bwd(dmerged, ob_off, mats, d_skip, w_glu, j, saved):
    xs, y, yg, z, u = saved
    BW = u.shape[1]
    NB = BW // LANES
    dmerged = _interleave(dmerged[:, ob_off * LANES:])
    ob_off = 0

    def gate_bwd(dob, yg, z):
        sg = _sigmoid(z)
        return dob * yg * (sg * (1.0 - sg)), dob * sg

    dz, dyg1 = _ew(gate_bwd, [(dmerged, ob_off), yg, z], out_dtypes=[BF16, F32], width=LANES, ncol=NB,
                   name="s5_gate_bwd")
    dw_glu = _mm_tn(yg, dz, w_glu.shape[0], "row", name="s5_glu_dw")
    dyg2 = _mm_nt(dz, w_glu, j, "row", name="s5_glu_dx")

    def act_bwd(d1, d2, y, u):
        dy = (d1 + d2) * _gelu_grad(y)
        return dy, jnp.sum(dy * u, axis=0, keepdims=True)

    dy, dd = _ew(act_bwd, [dyg1, dyg2, y, u], out_dtypes=[F32], n_acc=1, name="s5_gelu_bwd")
    dmats, dus = [], []
    for d in range(2):
        du_d, dmat = _s5_scan_bwd(dy, u, xs[2 * d], xs[2 * d + 1], mats[d], reverse=(d == 1),
                                  name=f"s5_scan_bwd{d}")
        dus.append(du_d)
        dmats.append(dmat)
    du = _ew(lambda dy, d0, d1, dsk: dy * dsk + (d0 + d1), [dy] + dus, [d_skip], out_dtypes=[BF16],
             name="s5_du_sum")[0]
    return _interleave(du, inverse=True), tuple(dmats), dd, dw_glu


def _ab_fwd(x, j, P, W):
    t5 = P["t5_bias"]
    AW = t5.shape[1] * HEAD_DIM
    hn = _rms_fwd(x, P["norm_mix"][2 * j][None], name="rms_fwd")
    proj = _mm_nn(hn, W["ab_w_in"], j, "col", name="ab_in_fwd")
    q, k, v = (proj[:, i * AW:(i + 1) * AW].astype(BF16) for i in range(3))
    u = proj[:, 3 * AW:]
    oa32, oa16, lse = _dilated_fwd(q, k, v, t5)
    mats = _s5_mats(*(P[n][j] for n in _S5_PARAMS))
    ob, s5_saved = _s5_fwd(u, mats, P["s5_d"][j][None], W["s5_w_glu"], j)
    merged = jnp.concatenate([oa16, ob], axis=1)
    x1 = _mm_nn(merged, W["ab_w_out"], j, "row", mode="res", res=x, name="ab_out_fwd")
    return x1, (x, hn, q, k, v, u, oa32, lse, merged, s5_saved)


def _ab_bwd(dx1, j, P, W, saved):
    x, hn, q, k, v, u, oa32, lse, merged, s5_saved = saved
    t5 = P["t5_bias"]
    AW = t5.shape[1] * HEAD_DIM
    J = W["ab_w_in"].shape[0]
    dmerged = _mm_nt(dx1, W["ab_w_out"], j, "row", name="ab_out_dx")
    dw_out = _mm_tn(merged, dx1, J, "row", name="ab_out_dw")
    delta, do16 = _attn_delta(dmerged, 0, oa32, name="dilated_delta")
    dq, dk, dv, dt5 = _dilated_bwd(q, k, v, t5, do16, lse, delta)
    s5_params = tuple(P[n][j] for n in _S5_PARAMS)
    mats, mats_vjp = jax.vjp(_s5_mats, *s5_params)
    du, dmats, dd, dw_glu = _s5_bwd(dmerged, AW // LANES, mats, P["s5_d"][j][None], W["s5_w_glu"], j, s5_saved)
    ds5 = mats_vjp(dmats)
    dproj = jnp.concatenate([dq, dk, dv, du], axis=1)
    dw_in = _mm_tn(hn, dproj, J, "col", name="ab_in_dw")
    dhn = _mm_nt(dproj, W["ab_w_in"], j, "col", name="ab_in_dx")
    dx, dg = _rms_bwd(dhn, x, P["norm_mix"][2 * j][None], dx1, name="rms_bwd")
    small = dict(zip(_S5_PARAMS, ds5), s5_d=dd[0], t5_bias=dt5)
    return dx, dg[0], dict(ab_w_in=dw_in, ab_w_out=dw_out, s5_w_glu=dw_glu), small


def _c_fwd(x, j, P, W):
    H = P["c_rpb"].shape[1]
    hn = _rms_fwd(x, P["norm_mix"][2 * j + 1][None], name="rms_fwd")
    qkv = _mm_nn(hn, W["c_w_qkv"], j, "col", out_dtype=BF16, name="c_qkv_fwd")
    bias = _na_bias(P["c_rpb"][j], x.shape[0] // GRID_W)
    o, lse = _wattn_fwd(qkv, qkv, qkv, bias, kind="na", H=H, dil=1, qoff=0, koff=H, voff=2 * H, name="na_fwd")
    x1 = _mm_nn(o, W["c_w_out"], j, "row", mode="res", res=x, name="c_out_fwd")
    return x1, (x, hn, qkv, o, lse)


def _c_bwd(dx1, j, P, W, saved):
    x, hn, qkv, o, lse = saved
    H = P["c_rpb"].shape[1]
    J = W["c_w_qkv"].shape[0]
    do = _mm_nt(dx1, W["c_w_out"], j, "row", name="c_out_dx")
    dw_out = _mm_tn(o, dx1, J, "row", name="c_out_dw")
    bias, bias_vjp = jax.vjp(functools.partial(_na_bias, rows=x.shape[0] // GRID_W), P["c_rpb"][j])
    dq, dk, dv, db = _wattn_bwd(qkv, do, lse, o, qkv, qkv, bias, kind="na", H=H, dil=1, qoff=0, koff=H,
                                voff=2 * H, dq_dtype=BF16, name="na_bwd")
    dqkv = jnp.concatenate([dq, dk.astype(BF16), dv.astype(BF16)], axis=1)
    dw_qkv = _mm_tn(hn, dqkv, J, "col", name="c_qkv_dw")
    dhn = _mm_nt(dqkv, W["c_w_qkv"], j, "col", name="c_qkv_dx")
    dx, dg = _rms_bwd(dhn, x, P["norm_mix"][2 * j + 1][None], dx1, name="rms_bwd")
    return dx, dg[0], dict(c_w_qkv=dw_qkv, c_w_out=dw_out), dict(c_rpb=bias_vjp(db)[0])


def _layer_pieces(i):
    j = i // 2
    mixer = [("ab_w_in", j), ("ab_w_out", j), ("s5_w_glu", j)] if i % 2 == 0 else [("c_w_qkv", j), ("c_w_out", j)]
    return mixer, [("mlp_w1", i)], [("mlp_w2", i)]


def _mlp_fwd(x, i, P, W, depth):
    hn = _rms_fwd(x, P["norm_mlp"][i][None], name="rms_fwd")
    if i + 1 < depth:
        mixer, w1, w2 = _layer_pieces(i + 1)
        (a, hdn), W = _mm_nn(hn, W["mlp_w1"], i, "col", mode="relu2", name="mlp_w1_fwd", host=(W, mixer + w2))
        x2, W = _mm_nn(hdn, W["mlp_w2"], i, "row", mode="res", res=x, name="mlp_w2_fwd", host=(W, w1))
    else:
        a, hdn = _mm_nn(hn, W["mlp_w1"], i, "col", mode="relu2", name="mlp_w1_fwd")
        x2 = _mm_nn(hdn, W["mlp_w2"], i, "row", mode="res", res=x, name="mlp_w2_fwd")
    return x2, (x, hn, a, hdn), W


def _mlp_bwd(dx2, i, P, W, saved):
    x, hn, a, hdn = saved
    J = W["mlp_w1"].shape[0]
    da = _mm_nt(dx2, W["mlp_w2"], i, "row", out_dtype=BF16, mode="dact", act=a, name="mlp_w2_dx")
    dw2 = _mm_tn(hdn, dx2, J, "row", name="mlp_w2_dw")
    dw1 = _mm_tn(hn, da, J, "col", name="mlp_w1_dw")
    dhn = _mm_nt(da, W["mlp_w1"], i, "col", name="mlp_w1_dx")
    dx, dg = _rms_bwd(dhn, x, P["norm_mlp"][i][None], dx2, name="rms_bwd")
    return dx, dg[0], dict(mlp_w1=dw1, mlp_w2=dw2)


_S5_PARAMS = ("s5_lam_re", "s5_lam_im", "s5_log_step", "s5_b_re", "s5_b_im", "s5_c_re", "s5_c_im")
_BIG = ("ab_w_in", "ab_w_out", "s5_w_glu", "c_w_qkv", "c_w_out", "mlp_w1", "mlp_w2")
_SMALL = ("t5_bias", "s5_lam_re", "s5_lam_im", "s5_log_step", "s5_b_re", "s5_b_im", "s5_c_re", "s5_c_im", "s5_d",
          "c_rpb", "norm_mix", "norm_mlp", "norm_final")
_WEIGHTS = ("t5_bias", "ab_w_in", "ab_w_out", "s5_lam_re", "s5_lam_im", "s5_log_step", "s5_b_re", "s5_b_im",
            "s5_c_re", "s5_c_im", "s5_d", "s5_w_glu", "c_w_qkv", "c_w_out", "c_rpb", "norm_mix", "norm_mlp",
            "mlp_w1", "mlp_w2", "norm_final")


def _local_grads(x, target, P, W):
    depth = P["norm_mix"].shape[0]
    saved = []
    h = x
    W = _gather_pieces(W, sum(_layer_pieces(0), []))
    for i in range(depth):
        h, s_mix = (_ab_fwd if i % 2 == 0 else _c_fwd)(h, i // 2, P, W)
        h, s_mlp, W = _mlp_fwd(h, i, P, W, depth)
        saved.append((s_mix, s_mlp))
    dh, dg_final, loss_cols = _loss_and_grad(h, P["norm_final"][None], target, name="loss_head")
    big = {n: [None] * P[n].shape[0] for n in _BIG}
    small = {n: jnp.zeros(P[n].shape, F32) for n in _SMALL}
    small["norm_final"] = dg_final[0]
    for i in reversed(range(depth)):
        s_mix, s_mlp = saved[i]
        j = i // 2
        dh, dg, dbig = _mlp_bwd(dh, i, P, W, s_mlp)
        small["norm_mlp"] = small["norm_mlp"].at[i].set(dg)
        for n, g in dbig.items():
            big[n][i] = g
        dh, dg, dbig, dsmall = (_ab_bwd if i % 2 == 0 else _c_bwd)(dh, j, P, W, s_mix)
        small["norm_mix"] = small["norm_mix"].at[i].set(dg)
        for n, g in dbig.items():
            big[n][j] = g
        for n, g in dsmall.items():
            if n == "t5_bias":
                small[n] = small[n] + g
            else:
                small[n] = small[n].at[j].set(g.reshape(P[n].shape[1:]))
    return loss_cols[0, 0], dh, big, small


def _place():
    x, y, c = lax.axis_index("x"), lax.axis_index("y"), lax.axis_index("c")
    return x, y, c, ((1 - x, y), (x, 1 - y), (1 - x, 1 - y))


def _comm_call(body, arrays, out_shape, sems, *, name, in_place=False, **static):
    hbm = pl.BlockSpec(memory_space=pltpu.HBM)
    return pl.pallas_call(
        functools.partial(body, n=len(arrays), **static), name=name, in_specs=[hbm] * len(arrays),
        out_specs=tuple([hbm] * len(out_shape)), out_shape=tuple(out_shape),
        input_output_aliases={t: t for t in range(len(arrays))} if in_place else {},
        scratch_shapes=[pltpu.SemaphoreType.DMA((k,)) for k in sems])(*arrays)


def _cast_body(s_ref, w_ref, o_ref):
    o_ref[...] = w_ref[...].astype(o_ref.dtype)


def _allgather_body(*refs, n):
    bufs = refs[n:2 * n]
    ici_send, ici_recv, pair_send, pair_recv = refs[2 * n:]
    x, y, c, chips = _place()
    slots = [2 * x + y] + [2 * px + py for px, py in chips]

    def part(t, slot, half):
        h = bufs[t].shape[1] // 2
        return bufs[t].at[slot, pl.ds(half * h, h)]

    def over_ici(t, p, slot):
        return pltpu.make_async_remote_copy(
            src_ref=part(t, slot, c), dst_ref=part(t, slot, c), send_sem=ici_send.at[3 * t + p],
            recv_sem=ici_recv.at[3 * t + p], device_id=(*chips[p], c), device_id_type=MESH)

    def to_pair(t, p, half):
        return pltpu.make_async_remote_copy(
            src_ref=part(t, slots[1 + p], half), dst_ref=part(t, slots[1 + p], half), send_sem=pair_send.at[3 * t + p],
            recv_sem=pair_recv.at[3 * t + p], device_id=(x, y, 1 - c), device_id_type=MESH)

    sends = [over_ici(t, p, slots[0]) for t in range(n) for p in range(3)]
    for cp in sends:
        cp.start()
    for t in range(n):
        for p in range(3):
            over_ici(t, p, slots[1 + p]).wait_recv()
            cp = to_pair(t, p, c)
            cp.start()
            sends.append(cp)
    for t in range(n):
        for p in range(3):
            to_pair(t, p, 1 - c).wait_recv()
    for cp in sends:
        cp.wait_send()


def _allgather(weights, place):
    bufs = []
    for w in weights:
        L, Kd, Nd = w.shape
        tr = _tile(Kd, 256)
        bufs.append(_sliced_call(
            _cast_body, place, [w], [lambda l, i, p: (l, i, 0)], [(None, tr, Nd)], (None, None, tr, Nd),
            lambda l, i, p: (p[1], l, i, 0), jax.ShapeDtypeStruct((4, L, Kd, Nd), BF16), (L, Kd // tr),
            name="weights_cast"))
    return bufs


def _pair_exchange_body(*refs, n):
    ins, outs = refs[:n], refs[n:2 * n]
    send_sems, recv_sems = refs[2 * n:]
    x, y, c, _ = _place()
    cps = []
    for t in range(n):
        h = outs[t].shape[0]
        cps.append(pltpu.make_async_remote_copy(
            src_ref=ins[t].at[pl.ds((1 - c) * h, h)], dst_ref=outs[t], send_sem=send_sems.at[t],
            recv_sem=recv_sems.at[t], device_id=(x, y, 1 - c), device_id_type=MESH))
    for cp in cps:
        cp.start()
    for cp in cps:
        cp.wait()


def _chip_exchange_body(*refs, n):
    ins, outs = refs[:n], refs[n:2 * n]
    send_sems, recv_sems = refs[2 * n:]
    x, y, c, chips = _place()
    cps = []
    for t in range(n):
        h = ins[t].shape[0]
        for p, (px, py) in enumerate(chips):
            cps.append(pltpu.make_async_remote_copy(
                src_ref=ins[t].at[pl.ds(0, h), 2 * px + py], dst_ref=outs[t].at[p], send_sem=send_sems.at[3 * t + p],
                recv_sem=recv_sems.at[3 * t + p], device_id=(px, py, c), device_id_type=MESH))
    for cp in cps:
        cp.start()
    for cp in cps:
        cp.wait()


def _pair_share_body(*refs, n):
    bufs = refs[n:2 * n]
    send_sems, recv_sems = refs[2 * n:]
    x, y, c, _ = _place()
    sends, recvs = [], []
    for t in range(n):
        h = bufs[t].shape[0] // 2
        for half, group in ((c, sends), (1 - c, recvs)):
            rows = bufs[t].at[pl.ds(half * h, h)]
            group.append(pltpu.make_async_remote_copy(
                src_ref=rows, dst_ref=rows, send_sem=send_sems.at[t], recv_sem=recv_sems.at[t],
                device_id=(x, y, 1 - c), device_id_type=MESH))
    for cp in sends:
        cp.start()
    for cp in recvs:
        cp.wait_recv()
    for cp in sends:
        cp.wait_send()


def _allreduce_body(in_ref, out_ref, send_sems, recv_sems, local_sem, n):
    x, y, c, _ = _place()
    flip = lambda v, bit: 1 - v if bit else v
    peers = [(flip(x, k & 4), flip(y, k & 2), flip(c, k & 1)) for k in range(1, 8)]

    def remote(k, slot):
        return pltpu.make_async_remote_copy(
            src_ref=in_ref, dst_ref=out_ref.at[slot], send_sem=send_sems.at[k], recv_sem=recv_sems.at[k],
            device_id=peers[k], device_id_type=MESH)

    local = pltpu.make_async_copy(in_ref, out_ref.at[4 * x + 2 * y + c], local_sem.at[0])
    sends = [remote(k, 4 * x + 2 * y + c) for k in range(7)]
    local.start()
    for cp in sends:
        cp.start()
    for k, (px, py, pc) in enumerate(peers):
        remote(k, 4 * px + 2 * py + pc).wait_recv()
    for cp in sends:
        cp.wait_send()
    local.wait()


def _sliced_call(body, scalars, arrays, in_maps, blocks, out_block, out_map, out_shape, grid, *, name):
    grid_spec = pltpu.PrefetchScalarGridSpec(
        num_scalar_prefetch=1, grid=grid,
        in_specs=[pl.BlockSpec(b, m) for b, m in zip(blocks, in_maps)],
        out_specs=pl.BlockSpec(out_block, out_map))
    return pl.pallas_call(
        functools.partial(body), name=name, grid_spec=grid_spec, out_shape=out_shape,
        compiler_params=pltpu.CompilerParams(vmem_limit_bytes=V7X_VMEM_LIMIT_BYTES))(scalars, *arrays)


def _chip_sum_body(s_ref, g_ref, r_ref, o_ref):
    o_ref[...] = (g_ref[...] + r_ref[...]).astype(o_ref.dtype)


def _final_sum_body(s_ref, g_ref, r1_ref, a_ref, b_ref, c_ref, o_ref):
    o_ref[...] = (((g_ref[...] + r1_ref[...]) + a_ref[...].astype(F32)) + b_ref[...].astype(F32)) + c_ref[...].astype(F32)


def _reduce_big(stacks, place):
    n = len(stacks)
    half = [jax.ShapeDtypeStruct((s.shape[0] // 2,) + s.shape[1:], F32) for s in stacks]
    from_pair = _comm_call(_pair_exchange_body, stacks, half, (n, n), name="grads_pair_exchange")
    chip16 = []
    for s, r in zip(stacks, from_pair):
        h, J, Kd, Nd = r.shape
        tr = _tile(Kd, 256)
        blk = (None, None, tr, Nd)
        chip16.append(_sliced_call(
            _chip_sum_body, place, [s, r],
            [lambda l, j, i, p: (p[0] * h + l, j, i, 0), lambda l, j, i, p: (l, j, i, 0)], [blk, blk], blk,
            lambda l, j, i, p: (l, j, i, 0), jax.ShapeDtypeStruct(r.shape, BF16), (h, J, Kd // tr),
            name="grads_chip_sum"))
    recv = [jax.ShapeDtypeStruct((3, a.shape[0]) + a.shape[2:], BF16) for a in chip16]
    from_chips = _comm_call(_chip_exchange_body, chip16, recv, (3 * n, 3 * n), name="grads_chip_exchange")
    sums = []
    for s, r, f in zip(stacks, from_pair, from_chips):
        h, J, Kd, Nd = r.shape
        tr = _tile(Kd, 256)
        blk4, blk3 = (None, None, tr, Nd), (None, tr, Nd)
        mine = lambda l, i, p: (l, p[1], i, 0)
        sums.append(_sliced_call(
            _final_sum_body, place, [s, r, f, f, f],
            [lambda l, i, p: (p[0] * h + l, p[1], i, 0), mine] + [functools.partial(lambda l, i, p, q: (q, l, i, 0), q=q)
                                                                  for q in range(3)],
            [blk4, blk4, blk4, blk4, blk4], blk3, lambda l, i, p: (p[0] * h + l, i, 0),
            jax.ShapeDtypeStruct((2 * h, Kd, Nd), F32), (h, Kd // tr), name="grads_final_sum"))
    return _comm_call(_pair_share_body, sums, [jax.ShapeDtypeStruct(s.shape, F32) for s in sums], (n, n),
                      in_place=True, name="grads_pair_share")


def _allreduce_small(buf):
    gathered = pl.pallas_call(
        functools.partial(_allreduce_body, n=1), name="small_allgather",
        in_specs=[pl.BlockSpec(memory_space=pltpu.HBM)], out_specs=pl.BlockSpec(memory_space=pltpu.HBM),
        out_shape=jax.ShapeDtypeStruct((8,) + buf.shape, F32),
        scratch_shapes=[pltpu.SemaphoreType.DMA((7,)), pltpu.SemaphoreType.DMA((7,)),
                        pltpu.SemaphoreType.DMA((1,))])(buf)

    def total(*b):
        acc = b[0]
        for t in b[1:]:
            acc = acc + t
        return acc

    return _ew(total, [gathered[i] for i in range(8)], out_dtypes=[F32], name="small_sum")[0]


def _pack(parts):
    flat = jnp.concatenate([p.reshape(-1).astype(F32) for p in parts])
    rows = -(-flat.shape[0] // (8 * LANES)) * 8
    return jnp.pad(flat, (0, rows * LANES - flat.shape[0])).reshape(rows, LANES)


def _unpack(buf, shapes):
    flat = buf.reshape(-1)
    out, at = [], 0
    for s in shapes:
        size = math.prod(s)
        out.append(flat[at:at + size].reshape(s))
        at += size
    return out


_INPUTS = ("x",) + _WEIGHTS + ("loss_target",) + tuple("m_" + n for n in _WEIGHTS) + tuple("v_" + n for n in _WEIGHTS)


def kernel(x, t5_bias, ab_w_in, ab_w_out, s5_lam_re, s5_lam_im, s5_log_step, s5_b_re, s5_b_im, s5_c_re, s5_c_im,
           s5_d, s5_w_glu, c_w_qkv, c_w_out, c_rpb, norm_mix, norm_mlp, mlp_w1, mlp_w2, norm_final, loss_target,
           m_t5_bias, m_ab_w_in, m_ab_w_out, m_s5_lam_re, m_s5_lam_im, m_s5_log_step, m_s5_b_re, m_s5_b_im,
           m_s5_c_re, m_s5_c_im, m_s5_d, m_s5_w_glu, m_c_w_qkv, m_c_w_out, m_c_rpb, m_norm_mix, m_norm_mlp,
           m_mlp_w1, m_mlp_w2, m_norm_final, v_t5_bias, v_ab_w_in, v_ab_w_out, v_s5_lam_re, v_s5_lam_im,
           v_s5_log_step, v_s5_b_re, v_s5_b_im, v_s5_c_re, v_s5_c_im, v_s5_d, v_s5_w_glu, v_c_w_qkv, v_c_w_out,
           v_c_rpb, v_norm_mix, v_norm_mlp, v_mlp_w1, v_mlp_w2, v_norm_final):
    args = (x, t5_bias, ab_w_in, ab_w_out, s5_lam_re, s5_lam_im, s5_log_step, s5_b_re, s5_b_im, s5_c_re, s5_c_im,
            s5_d, s5_w_glu, c_w_qkv, c_w_out, c_rpb, norm_mix, norm_mlp, mlp_w1, mlp_w2, norm_final, loss_target,
            m_t5_bias, m_ab_w_in, m_ab_w_out, m_s5_lam_re, m_s5_lam_im, m_s5_log_step, m_s5_b_re, m_s5_b_im,
            m_s5_c_re, m_s5_c_im, m_s5_d, m_s5_w_glu, m_c_w_qkv, m_c_w_out, m_c_rpb, m_norm_mix, m_norm_mlp,
            m_mlp_w1, m_mlp_w2, m_norm_final, v_t5_bias, v_ab_w_in, v_ab_w_out, v_s5_lam_re, v_s5_lam_im,
            v_s5_log_step, v_s5_b_re, v_s5_b_im, v_s5_c_re, v_s5_c_im, v_s5_d, v_s5_w_glu, v_c_w_qkv, v_c_w_out,
            v_c_rpb, v_norm_mix, v_norm_mlp, v_mlp_w1, v_mlp_w2, v_norm_final)
    A = dict(zip(_INPUTS, args, strict=True))
    P = {n: A[n] for n in _WEIGHTS}
    place = jnp.stack([lax.axis_index("c"), 2 * lax.axis_index("x") + lax.axis_index("y")]).astype(jnp.int32)

    gathered = _allgather([P[n] for n in _BIG], place)
    W = dict(zip(_BIG, gathered))
    loss, dx, big, small = _local_grads(A["x"][0], A["loss_target"][0], P, W)

    stacks = [jnp.stack(big[n]) for n in _BIG]
    big_grads = dict(zip(_BIG, _reduce_big(stacks, place)))
    small_shapes = [P[n].shape for n in _SMALL] + [(1,)]
    reduced = _unpack(_allreduce_small(_pack([small[n] for n in _SMALL] + [loss.reshape(1)])), small_shapes)
    small_grads = dict(zip(_SMALL, reduced[:-1]))
    loss = reduced[-1][0]

    grads, delta, new_m, new_v = {}, {}, {}, {}
    for n in _BIG:
        g = big_grads[n]
        two_d = lambda t: t.reshape(-1, t.shape[-1])
        d, m, v = _adamw(two_d(P[n]), two_d(g), two_d(A["m_" + n]), two_d(A["v_" + n]), name="adamw")
        grads[n] = g
        delta[n], new_m[n], new_v[n] = (t.reshape(g.shape) for t in (d, m, v))
    d, m, v = _adamw(_pack([P[n] for n in _SMALL]), _pack([small_grads[n] for n in _SMALL]),
                     _pack([A["m_" + n] for n in _SMALL]), _pack([A["v_" + n] for n in _SMALL]), name="adamw_small")
    shapes = [P[n].shape for n in _SMALL]
    for n, dn, mn, vn in zip(_SMALL, _unpack(d, shapes), _unpack(m, shapes), _unpack(v, shapes)):
        grads[n] = small_grads[n]
        delta[n], new_m[n], new_v[n] = dn, mn, vn
    return (loss, dx[None], *[grads[n] for n in _WEIGHTS], *[delta[n] for n in _WEIGHTS],
            *[new_m[n] for n in _WEIGHTS], *[new_v[n] for n in _WEIGHTS])
```

```python
import functools
import math

import jax
import jax.numpy as jnp
from jax import lax
from jax.experimental import pallas as pl
from jax.experimental.pallas import tpu as pltpu

F32 = jnp.float32
BF16 = jnp.bfloat16

HEAD_DIM = 128
LANES = 128
DILATED_BRANCHES = ((128, 1), (512, 4), (2048, 16))
A_QBLOCK = 128
DIL_HALF = 64
B_GROUP = 16
B_STATE = 64
GROUPS_PER_BLOCK = LANES // B_GROUP
STATE_PER_BLOCK = GROUPS_PER_BLOCK * B_STATE
GRID_W = 64
NA_ROWS_MAX = 8
NA_COLS = 16
T5_BUCKETS = 32
T5_MAX_DISTANCE = 1024
RMS_EPS = 1e-6
NEG_INF = -1e30
ADAM_LR = 0.001
ADAM_B1 = 0.9
ADAM_B2 = 0.999
ADAM_EPS = 1e-08
ADAM_WD = 0.01
ADAM_STEP = 10
V7X_VMEM_LIMIT_BYTES = 56 * 1024 * 1024

NN = (((1,), (0,)), ((), ()))
NT = (((1,), (1,)), ((), ()))
TN = (((0,), (0,)), ((), ()))
MESH = pl.DeviceIdType.MESH


def _tile(n, pref):
    if n <= pref:
        return n
    for t in range(pref - pref % LANES, 0, -LANES):
        if n % t == 0:
            return t
    t = pref
    while t >= 8:
        if n % t == 0:
            return t
        t //= 2
    return n


def _pcall(body, *, name, grid, in_specs, out_specs, out_shape, scratch=(), sem=None, aliases=None, **static):
    params = dict(vmem_limit_bytes=V7X_VMEM_LIMIT_BYTES)
    if sem is not None:
        params["dimension_semantics"] = sem
    return pl.pallas_call(
        functools.partial(body, **static), name=name, grid=grid, in_specs=in_specs, out_specs=out_specs,
        out_shape=out_shape, scratch_shapes=list(scratch), input_output_aliases=aliases or {},
        compiler_params=pltpu.CompilerParams(**params))


def _gather_plan(bufs, pieces, sems):
    ici_send, ici_recv, pair_send, pair_recv = sems
    x, y, c = lax.axis_index("x"), lax.axis_index("y"), lax.axis_index("c")
    chips = ((1 - x, y), (x, 1 - y), (1 - x, 1 - y))
    slots = [2 * x + y] + [2 * px + py for px, py in chips]
    every = [(k, p) for k in range(len(pieces)) for p in range(3)]

    def part(k, slot, half):
        t, layer = pieces[k]
        h = bufs[t].shape[2] // 2
        return bufs[t].at[slot, layer, pl.ds(half * h, h)]

    def over_ici(k, p, slot):
        return pltpu.make_async_remote_copy(
            src_ref=part(k, slot, c), dst_ref=part(k, slot, c), send_sem=ici_send.at[3 * k + p],
            recv_sem=ici_recv.at[3 * k + p], device_id=(*chips[p], c), device_id_type=MESH)

    def to_pair(k, p, half):
        return pltpu.make_async_remote_copy(
            src_ref=part(k, slots[1 + p], half), dst_ref=part(k, slots[1 + p], half),
            send_sem=pair_send.at[3 * k + p], recv_sem=pair_recv.at[3 * k + p], device_id=(x, y, 1 - c),
            device_id_type=MESH)

    def start():
        for k, p in every:
            over_ici(k, p, slots[0]).start()

    def finish():
        for k, p in every:
            over_ici(k, p, slots[1 + p]).wait_recv()
            to_pair(k, p, c).start()
        for k, p in every:
            to_pair(k, p, 1 - c).wait_recv()
        for k, p in every:
            over_ici(k, p, slots[0]).wait_send()
            to_pair(k, p, c).wait_send()

    return start, finish


def _gather_sems(pieces):
    return [pltpu.SemaphoreType.DMA((3 * len(pieces),))] * 4


def _hosted_body(*refs, host_body, n_in, n_out, n_buf, pieces, **static):
    ins = refs[:n_in]
    outs = refs[n_in + n_buf:n_in + n_buf + n_out]
    bufs = refs[n_in + n_buf + n_out:n_in + 2 * n_buf + n_out]
    scratch = refs[n_in + 2 * n_buf + n_out:-4]
    start, finish = _gather_plan(bufs, pieces, refs[-4:])
    ids = [pl.program_id(a) for a in range(3)]
    last = [pl.num_programs(a) - 1 for a in range(3)]

    @pl.when((ids[0] == 0) & (ids[1] == 0) & (ids[2] == 0))
    def _():
        start()

    host_body(*ins, *outs, *scratch, **static)

    @pl.when((ids[0] == last[0]) & (ids[1] == last[1]) & (ids[2] == last[2]))
    def _():
        finish()


def _gather_pieces_body(*refs, n, pieces):
    start, finish = _gather_plan(refs[n:2 * n], pieces, refs[2 * n:])
    start()
    finish()


def _named_pieces(W, pieces):
    names = sorted({n for n, _ in pieces})
    return names, [W[n] for n in names], tuple((names.index(n), layer) for n, layer in pieces)


def _gather_pieces(W, pieces):
    names, bufs, idx = _named_pieces(W, pieces)
    hbm = pl.BlockSpec(memory_space=pltpu.HBM)
    new = pl.pallas_call(
        functools.partial(_gather_pieces_body, n=len(bufs), pieces=idx), name="weights_gather",
        in_specs=[hbm] * len(bufs), out_specs=tuple([hbm] * len(bufs)),
        out_shape=tuple(jax.ShapeDtypeStruct(b.shape, b.dtype) for b in bufs),
        input_output_aliases={t: t for t in range(len(bufs))}, scratch_shapes=_gather_sems(idx))(*bufs)
    return {**W, **dict(zip(names, new))}


def _mm_finish(acc, rest, mode):
    if mode == "plain":
        rest[0][...] = acc.astype(rest[0].dtype)
    elif mode == "res":
        rest[1][...] = (rest[0][...] + acc).astype(rest[1].dtype)
    elif mode == "relu2":
        rest[0][...] = acc
        r = jnp.maximum(acc, 0.0)
        rest[1][...] = (r * r).astype(rest[1].dtype)
    elif mode == "dact":
        rest[1][...] = (acc * (2.0 * jnp.maximum(rest[0][...], 0.0))).astype(rest[1].dtype)


def _mm_body(a_ref, b_ref, *rest, nk, dims, mode):
    prod = lax.dot_general(a_ref[...].astype(BF16), b_ref[...].astype(BF16), dims, preferred_element_type=F32)
    if nk == 1:
        _mm_finish(prod, rest, mode)
        return
    acc_ref = rest[-1]
    k = pl.program_id(2)

    @pl.when(k == 0)
    def _():
        acc_ref[...] = prod

    @pl.when(k > 0)
    def _():
        acc_ref[...] += prod

    @pl.when(k == nk - 1)
    def _():
        _mm_finish(acc_ref[...], rest, mode)


def _w_dims(w, split):
    J, _, Kd, Nd = w.shape
    return (J, Kd, J * Nd, Kd, Nd) if split == "col" else (J, J * Kd, Nd, Kd, Nd)


def _w_spec(split, layer, tk, tn, Kd, Nd, kn_of):
    kps, nps = Kd // tk, Nd // tn

    def index(*g):
        kb, nb = kn_of(*g)
        if split == "col":
            return nb // nps, layer, kb, nb % nps
        return kb // kps, layer, kb % kps, nb

    return pl.BlockSpec((None, None, tk, tn), index)


def _mm_nn(a, w, layer, split, *, name, out_dtype=F32, mode="plain", res=None, host=None):
    M = a.shape[0]
    J, K, N, Kd, Nd = _w_dims(w, split)
    tm, tn, tk = _tile(M, 1024), _tile(Nd, 1024), _tile(Kd, 2048)
    in_specs = [pl.BlockSpec((tm, tk), lambda i, j, k: (i, k)),
                _w_spec(split, layer, tk, tn, Kd, Nd, lambda i, j, k: (k, j))]
    args = [a, w]
    o_spec = pl.BlockSpec((tm, tn), lambda i, j, k: (i, j))
    if mode == "res":
        in_specs.append(o_spec)
        args.append(res)
    if mode == "relu2":
        out_shape = (jax.ShapeDtypeStruct((M, N), F32), jax.ShapeDtypeStruct((M, N), BF16))
        out_specs = (o_spec, o_spec)
    else:
        out_shape = (jax.ShapeDtypeStruct((M, N), out_dtype),)
        out_specs = (o_spec,)
    grid = (M // tm, N // tn, K // tk)
    scratch = [pltpu.VMEM((tm, tn), F32)]
    if host is None:
        out = _pcall(_mm_body, name=name, grid=grid, in_specs=in_specs, out_specs=out_specs, out_shape=out_shape,
                     scratch=scratch, sem=("parallel", "parallel", "arbitrary"), nk=K // tk, dims=NN, mode=mode)(*args)
        return out if mode == "relu2" else out[0]
    W, pieces = host
    names, bufs, idx = _named_pieces(W, pieces)
    hbm = pl.BlockSpec(memory_space=pltpu.HBM)
    n_in, n_out, n_buf = len(args), len(out_shape), len(bufs)
    out = _pcall(_hosted_body, name=name, grid=grid, in_specs=in_specs + [hbm] * n_buf,
                 out_specs=tuple(out_specs) + (hbm,) * n_buf,
                 out_shape=tuple(out_shape) + tuple(jax.ShapeDtypeStruct(b.shape, b.dtype) for b in bufs),
                 scratch=scratch + _gather_sems(idx), sem=("arbitrary",) * 3,
                 aliases={n_in + t: n_out + t for t in range(n_buf)}, host_body=_mm_body, n_in=n_in, n_out=n_out,
                 n_buf=n_buf, pieces=idx, nk=K // tk, dims=NN, mode=mode)(*args, *bufs)
    res_out = out[:n_out] if mode == "relu2" else out[0]
    return res_out, {**W, **dict(zip(names, out[n_out:]))}


def _mm_nt(a, w, layer, split, *, name, out_dtype=F32, mode="plain", act=None):
    M = a.shape[0]
    J, K, N, Kd, Nd = _w_dims(w, split)
    tm, tko, tc = _tile(M, 1024), _tile(Kd, 1024), _tile(Nd, 2048)
    in_specs = [pl.BlockSpec((tm, tc), lambda i, j, c: (i, c)),
                _w_spec(split, layer, tko, tc, Kd, Nd, lambda i, j, c: (j, c))]
    args = [a, w]
    o_spec = pl.BlockSpec((tm, tko), lambda i, j, c: (i, j))
    if mode == "dact":
        in_specs.append(o_spec)
        args.append(act)
    return _pcall(_mm_body, name=name, grid=(M // tm, K // tko, N // tc), in_specs=in_specs, out_specs=o_spec,
                  out_shape=jax.ShapeDtypeStruct((M, K), out_dtype), scratch=[pltpu.VMEM((tm, tko), F32)],
                  sem=("parallel", "parallel", "arbitrary"), nk=N // tc, dims=NT, mode=mode)(*args)


def _mm_tn(a, b, J, split, *, name):
    M, K = a.shape
    N = b.shape[1]
    Kd, Nd = (K, N // J) if split == "col" else (K // J, N)
    tk, tn, tc = _tile(Kd, 1024), _tile(Nd, 1024), _tile(M, 2048)
    kps, nps = Kd // tk, Nd // tn
    in_specs = [pl.BlockSpec((tc, tk), lambda i, j, c: (c, i)),
                pl.BlockSpec((tc, tn), lambda i, j, c: (c, j))]
    if split == "col":
        o_spec = pl.BlockSpec((None, tk, tn), lambda i, j, c: (j // nps, i, j % nps))
    else:
        o_spec = pl.BlockSpec((None, tk, tn), lambda i, j, c: (i // kps, i % kps, j))
    return _pcall(_mm_body, name=name, grid=(K // tk, N // tn, M // tc), in_specs=in_specs, out_specs=o_spec,
                  out_shape=jax.ShapeDtypeStruct((J, Kd, Nd), F32), scratch=[pltpu.VMEM((tk, tn), F32)],
                  sem=("parallel", "parallel", "arbitrary"), nk=M // tc, dims=TN, mode="plain")(a, b)


def _ew_body(*refs, fn, n_in, n_out, n_acc):
    res = fn(*[r[...] for r in refs[:n_in]])
    if not isinstance(res, (tuple, list)):
        res = (res,)
    outs = refs[n_in:n_in + n_out]
    accs = refs[n_in + n_out:]
    for o, r in zip(outs, res[:n_out]):
        o[...] = r.astype(o.dtype)
    if n_acc:
        first = pl.program_id(1) == 0
        for a, r in zip(accs, res[n_out:]):
            @pl.when(first)
            def _(a=a):
                a[...] = jnp.zeros_like(a)

            a[...] += r


def _ew(fn, rows, vecs=(), *, out_dtypes=(), n_acc=0, width=None, ncol=1, tr=256, name):
    rows = [r if isinstance(r, tuple) else (r, 0) for r in rows]
    R = rows[0][0].shape[0]
    C = width if width is not None else rows[0][0].shape[1]
    tr = _tile(R, tr)
    in_specs = [pl.BlockSpec((tr, C), functools.partial(lambda j, i, off: (i, off + j), off=off)) for _, off in rows]
    in_specs += [pl.BlockSpec((1, C), lambda j, i: (0, j)) for _ in vecs]
    out_shape = [jax.ShapeDtypeStruct((R, ncol * C), dt) for dt in out_dtypes]
    out_specs = [pl.BlockSpec((tr, C), lambda j, i: (i, j)) for _ in out_dtypes]
    out_shape += [jax.ShapeDtypeStruct((1, ncol * C), F32)] * n_acc
    out_specs += [pl.BlockSpec((1, C), lambda j, i: (0, j))] * n_acc
    res = _pcall(_ew_body, name=name, grid=(ncol, R // tr), in_specs=in_specs, out_specs=tuple(out_specs),
                 out_shape=tuple(out_shape), sem=("parallel", "arbitrary" if n_acc else "parallel"),
                 fn=fn, n_in=len(rows) + len(vecs), n_out=len(out_dtypes), n_acc=n_acc)(
        *[a for a, _ in rows], *vecs)
    return res


def _rms_fwd(x, g, *, name):
    def fn(x, g):
        r = lax.rsqrt(jnp.mean(x * x, axis=1, keepdims=True) + RMS_EPS)
        return (x * r) * g

    return _ew(fn, [x], [g], out_dtypes=[BF16], name=name)[0]


def _rms_bwd(dh, x, g, dres, *, name):
    def fn(dh, x, dres, g):
        r = lax.rsqrt(jnp.mean(x * x, axis=1, keepdims=True) + RMS_EPS)
        y = x * r
        dy = dh * g
        dx = r * (dy - y * jnp.mean(dy * y, axis=1, keepdims=True))
        return dres + dx, jnp.sum(dh * y, axis=0, keepdims=True)

    return _ew(fn, [dh, x, dres], [g], out_dtypes=[F32], n_acc=1, name=name)


def _loss_and_grad(x, g, target, *, name):
    D = x.shape[1]

    def fn(x, t, g):
        r = lax.rsqrt(jnp.mean(x * x, axis=1, keepdims=True) + RMS_EPS)
        y = x * r
        diff = y * g - t
        dh = diff * (1.0 / D)
        dy = dh * g
        dx = r * (dy - y * jnp.mean(dy * y, axis=1, keepdims=True))
        loss = jnp.sum(jnp.sum(diff * diff, axis=1, keepdims=True), axis=0, keepdims=True) * (0.5 / D)
        return dx, jnp.sum(dh * y, axis=0, keepdims=True), jnp.broadcast_to(loss, (1, D))

    return _ew(fn, [x, target], [g], out_dtypes=[F32], n_acc=2, name=name)


def _gelu(y):
    c = math.sqrt(2.0 / math.pi)
    return 0.5 * y * (1.0 + jnp.tanh(c * (y + 0.044715 * (y * y * y))))


def _gelu_grad(y):
    c = math.sqrt(2.0 / math.pi)
    t = jnp.tanh(c * (y + 0.044715 * (y * y * y)))
    return 0.5 * (1.0 + t) + 0.5 * y * (1.0 - t * t) * (c * (1.0 + 3 * 0.044715 * (y * y)))


def _adamw(w, g, m, v, *, name):
    def fn(w, g, m, v):
        m2 = ADAM_B1 * m + (1.0 - ADAM_B1) * g
        v2 = ADAM_B2 * v + (1.0 - ADAM_B2) * (g * g)
        m_hat = m2 / (1.0 - ADAM_B1 ** ADAM_STEP)
        v_hat = v2 / (1.0 - ADAM_B2 ** ADAM_STEP)
        delta = -ADAM_LR * (m_hat / (jnp.sqrt(v_hat) + ADAM_EPS) + ADAM_WD * w)
        return delta, m2, v2

    return _ew(fn, [w, g, m, v], out_dtypes=[F32, F32, F32], name=name)


def _window(kind, blk, QB, rows):
    if kind == "dil":
        return pl.multiple_of(blk * QB, QB), 0
    kr = min(NA_ROWS_MAX, rows)
    rs = jnp.clip(blk - kr // 2, 0, rows - kr)
    return pl.multiple_of(rs * GRID_W, GRID_W), blk - rs


def _scores(q, kw, bias, kind, start, QB, W, L_valid, scale):
    s = lax.dot_general(q, kw, NT, preferred_element_type=F32) * scale + bias
    if kind == "dil":
        kp = start + lax.broadcasted_iota(jnp.int32, (QB, W), 1)
        s = jnp.where((kp >= DIL_HALF) & (kp < DIL_HALF + L_valid), s, NEG_INF)
    return s


def _wattn_fwd_body(q_ref, k_ref, v_ref, b_ref, *outs, QB, SUB, W, kind, L_valid, rows, scale):
    n = pl.program_id(1)
    wins = [_window(kind, n * SUB + i, QB, rows) for i in range(SUB)]
    sls = [slice(i * QB, (i + 1) * QB) for i in range(SUB)]
    ss = [_scores(q_ref[sl, :], k_ref[pl.ds(start, W), :], b_ref[pat], kind, start, QB, W, L_valid, scale)
          for sl, (start, pat) in zip(sls, wins)]
    ms = [jnp.max(s, axis=1, keepdims=True) for s in ss]
    es = [jnp.exp(s - m) for s, m in zip(ss, ms)]
    ls = [jnp.sum(e, axis=1, keepdims=True) for e in es]
    for sl, (start, _), m, e, l in zip(sls, wins, ms, es, ls):
        vw = v_ref[pl.ds(start, W), :]
        if kind == "na":
            p = (e * (1.0 / l)).astype(BF16)
            o = lax.dot_general(p, vw, NN, preferred_element_type=F32)
            outs[0][sl, :] = o.astype(outs[0].dtype)
            outs[1][sl, :] = jnp.broadcast_to(m + jnp.log(l), (QB, LANES))
        else:
            outs[0][sl, :] = lax.dot_general(e.astype(BF16), vw, NN, preferred_element_type=F32)
            outs[1][sl, :] = jnp.broadcast_to(m, (QB, LANES))
            outs[2][sl, :] = jnp.broadcast_to(l, (QB, LANES))


def _wattn_bwd_body(q_ref, do_ref, lse_ref, dl_ref, k_ref, v_ref, b_ref, dq_ref, dk_ref, dv_ref, db_ref, *,
                    QB, SUB, W, kind, L_valid, rows, scale, dgroup):
    cb = pl.program_id(0)
    n = pl.program_id(1)

    @pl.when(n == 0)
    def _():
        dk_ref[...] = jnp.zeros_like(dk_ref)
        dv_ref[...] = jnp.zeros_like(dv_ref)

    @pl.when((n == 0) & (cb % dgroup == 0))
    def _():
        db_ref[...] = jnp.zeros_like(db_ref)

    wins = [_window(kind, n * SUB + i, QB, rows) for i in range(SUB)]
    sls = [slice(i * QB, (i + 1) * QB) for i in range(SUB)]
    ss = [_scores(q_ref[sl, :], k_ref[pl.ds(start, W), :], b_ref[pat], kind, start, QB, W, L_valid, scale)
          for sl, (start, pat) in zip(sls, wins)]
    dos = [do_ref[sl, :].astype(BF16) for sl in sls]
    dps = [lax.dot_general(do, v_ref[pl.ds(start, W), :], NT, preferred_element_type=F32)
           for do, (start, _) in zip(dos, wins)]
    ps = [jnp.exp(s - lse_ref[sl, :][:, :1]) for s, sl in zip(ss, sls)]
    if kind == "na":
        dls = [jnp.sum(do_ref[sl, :].astype(F32) * dl_ref[sl, :].astype(F32), axis=1, keepdims=True) for sl in sls]
    else:
        dls = [dl_ref[sl, :][:, :1] for sl in sls]
    dss = [p * (dp - dl) for p, dp, dl in zip(ps, dps, dls)]
    for sl, (start, pat), p, ds, do in zip(sls, wins, ps, dss, dos):
        q = q_ref[sl, :]
        db_ref[pat] += ds
        dsb = ds.astype(BF16)
        dq_ref[sl, :] = (lax.dot_general(dsb, k_ref[pl.ds(start, W), :], NN, preferred_element_type=F32)
                         * scale).astype(dq_ref.dtype)
        dk_ref[pl.ds(start, W), :] += lax.dot_general(dsb, q, TN, preferred_element_type=F32) * scale
        dv_ref[pl.ds(start, W), :] += lax.dot_general(p.astype(BF16), do, TN, preferred_element_type=F32)


def _wattn_geometry(kind, LQ, dil):
    if kind == "dil":
        QB, W, rows = A_QBLOCK, A_QBLOCK + 2 * DIL_HALF, 0
    else:
        rows = LQ // GRID_W
        QB, W = GRID_W, min(NA_ROWS_MAX, rows) * GRID_W
    blocks = LQ // QB
    SUB = 8 if blocks % 8 == 0 else 4 if blocks % 4 == 0 else 1
    return QB, W, rows, SUB


def _wattn_fwd(q, k, v, bias, *, kind, H, dil, qoff, koff, voff, name):
    LQ, LK = q.shape[0], k.shape[0]
    QB, W, rows, SUB = _wattn_geometry(kind, LQ, dil)
    ncb = H * dil
    NP = bias.shape[1]

    def col(cb, off):
        return off + (cb % dil) * H + cb // dil

    in_specs = [pl.BlockSpec((QB * SUB, LANES), lambda cb, n: (n, col(cb, qoff))),
                pl.BlockSpec((LK, LANES), lambda cb, n: (0, col(cb, koff))),
                pl.BlockSpec((LK, LANES), lambda cb, n: (0, col(cb, voff))),
                pl.BlockSpec((None, NP, QB, W), lambda cb, n: (cb // dil, 0, 0, 0))]
    o_spec = pl.BlockSpec((QB * SUB, LANES), lambda cb, n: (n, col(cb, 0)))
    shape = (LQ, ncb * LANES)
    if kind == "na":
        out_shape = (jax.ShapeDtypeStruct(shape, BF16), jax.ShapeDtypeStruct(shape, F32))
    else:
        out_shape = (jax.ShapeDtypeStruct(shape, F32),) * 3
    return _pcall(_wattn_fwd_body, name=name, grid=(ncb, LQ // (QB * SUB)), in_specs=in_specs,
                  out_specs=(o_spec,) * len(out_shape), out_shape=out_shape, sem=("parallel", "parallel"),
                  QB=QB, SUB=SUB, W=W, kind=kind, L_valid=LQ, rows=rows, scale=1.0 / math.sqrt(HEAD_DIM))(
        q, k, v, bias)


def _wattn_bwd(q, do, lse, delta, k, v, bias, *, kind, H, dil, qoff, koff, voff, dq_dtype, name):
    LQ, LK = q.shape[0], k.shape[0]
    QB, W, rows, SUB = _wattn_geometry(kind, LQ, dil)
    ncb = H * dil
    NP = bias.shape[1]

    def col(cb, off):
        return off + (cb % dil) * H + cb // dil

    q_spec = lambda off: pl.BlockSpec((QB * SUB, LANES), lambda cb, n: (n, col(cb, off)))
    kv_spec = lambda off: pl.BlockSpec((LK, LANES), lambda cb, n: (0, col(cb, off)))
    b_spec = pl.BlockSpec((None, NP, QB, W), lambda cb, n: (cb // dil, 0, 0, 0))
    in_specs = [q_spec(qoff), q_spec(0), q_spec(0), q_spec(0), kv_spec(koff), kv_spec(voff), b_spec]
    out_shape = (jax.ShapeDtypeStruct((LQ, ncb * LANES), dq_dtype), jax.ShapeDtypeStruct((LK, ncb * LANES), F32),
                 jax.ShapeDtypeStruct((LK, ncb * LANES), F32), jax.ShapeDtypeStruct(bias.shape, F32))
    out_specs = (q_spec(0), kv_spec(0), kv_spec(0), b_spec)
    return _pcall(_wattn_bwd_body, name=name, grid=(ncb, LQ // (QB * SUB)), in_specs=in_specs, out_specs=out_specs,
                  out_shape=out_shape, sem=("arbitrary", "arbitrary"),
                  QB=QB, SUB=SUB, W=W, kind=kind, L_valid=LQ, rows=rows, scale=1.0 / math.sqrt(HEAD_DIM),
                  dgroup=dil)(q, do, lse, delta, k, v, bias)


def _attn_delta(do, do_off, o, *, name):
    def fn(do, o):
        return jnp.broadcast_to(jnp.sum(do * o.astype(F32), axis=1, keepdims=True), do.shape), do

    return _ew(fn, [(do, do_off), o], out_dtypes=[F32, BF16], width=LANES, ncol=o.shape[1] // LANES, tr=512,
               name=name)


SUBLANES = 8


def _bf(ref):
    return ref[...].astype(BF16)


def _cmul(ar, ai, br, bi):
    return ar * br - ai * bi, ar * bi + ai * br


def _powers(ar, ai, n):
    pr, pi = jnp.ones_like(ar), jnp.zeros_like(ai)
    mr, mi = ar, ai
    while pr.shape[0] < n + 1:
        qr, qi = _cmul(mr, mi, pr, pi)
        pr, pi = jnp.concatenate([pr, qr]), jnp.concatenate([pi, qi])
        mr, mi = _cmul(mr, mi, mr, mi)
    return pr[:n + 1], pi[:n + 1]


def _seg_tables(a_re, a_im, SEG, desc):
    pr, pi = _powers(lax.stop_gradient(a_re), lax.stop_gradient(a_im), SEG)
    nat = (lambda t: t[::-1]) if desc else (lambda t: t)
    return nat(pr[1:]), nat(pi[1:]), pr[:-1], pi[:-1], pr[SEG:], pi[SEG:]


def _interleave(t, inverse=False):
    S, C = t.shape
    TC = _tile(S, 512)
    a, b = (TC // SUBLANES, SUBLANES) if inverse else (SUBLANES, TC // SUBLANES)
    return t.reshape(S // TC, a, b, C).transpose(0, 2, 1, 3).reshape(S, C)


def _tile_carries(desc, end_r, end_i, a64r, a64i, car_r, car_i, cst_r, cst_i):
    cr, ci = car_r[0:1, :], car_i[0:1, :]
    for i in (reversed(range(SUBLANES)) if desc else range(SUBLANES)):
        cst_r[i:i + 1, :] = cr
        cst_i[i:i + 1, :] = ci
        pr, pi = _cmul(a64r, a64i, cr, ci)
        cr, ci = pr + end_r[i:i + 1, :], pi + end_i[i:i + 1, :]
    car_r[0:1, :] = cr
    car_i[0:1, :] = ci


def _tile_correct(SEG, xr_ref, xi_ref, pr_ref, pi_ref, cst_r, cst_i):
    cr, ci = cst_r[...], cst_i[...]
    for t in range(SEG):
        rows = pl.ds(SUBLANES * t, SUBLANES)
        dr, di = _cmul(pr_ref[t:t + 1, :], pi_ref[t:t + 1, :], cr, ci)
        xr_ref[rows, :] += dr
        xi_ref[rows, :] += di


def _s5il_fwd_body(u_ref, br_ref, bi_ref, cr_ref, cin_ref, are_ref, aim_ref, pr_ref, pi_ref, a64r_ref, a64i_ref,
                   xre_ref, xim_ref, y_ref, bre_s, bim_s, car_r, car_i, cst_r, cst_i, end_r, end_i, *, TC, desc):
    SEG = TC // SUBLANES

    @pl.when(pl.program_id(1) == 0)
    def _():
        car_r[...] = jnp.zeros_like(car_r)
        car_i[...] = jnp.zeros_like(car_i)

    ub = _bf(u_ref)
    bre_s[...] = lax.dot_general(ub, _bf(br_ref), NN, preferred_element_type=F32)
    bim_s[...] = lax.dot_general(ub, _bf(bi_ref), NN, preferred_element_type=F32)
    ar, ai = are_ref[...], aim_ref[...]
    xr = xi = jnp.zeros((SUBLANES, STATE_PER_BLOCK), F32)
    for tau in range(SEG):
        rows = pl.ds(SUBLANES * (SEG - 1 - tau if desc else tau), SUBLANES)
        pr, pi = _cmul(ar, ai, xr, xi)
        xr = pr + bre_s[rows, :]
        xi = pi + bim_s[rows, :]
        xre_ref[rows, :] = xr
        xim_ref[rows, :] = xi
    end_r[...] = xr
    end_i[...] = xi
    _tile_carries(desc, end_r, end_i, a64r_ref[...], a64i_ref[...], car_r, car_i, cst_r, cst_i)
    _tile_correct(SEG, xre_ref, xim_ref, pr_ref, pi_ref, cst_r, cst_i)
    y_ref[...] = (lax.dot_general(_bf(xre_ref), _bf(cr_ref), NN, preferred_element_type=F32)
                  + lax.dot_general(_bf(xim_ref), _bf(cin_ref), NN, preferred_element_type=F32))


def _s5il_specs(S, BW, desc):
    NB = BW // LANES
    TC = _tile(S, 512)
    nT = S // TC
    tmap = (lambda l, t: (nT - 1 - t, l)) if desc else (lambda l, t: (t, l))
    narrow = pl.BlockSpec((TC, LANES), tmap)
    wide = pl.BlockSpec((TC, STATE_PER_BLOCK), tmap)
    vec = pl.BlockSpec((1, STATE_PER_BLOCK), lambda l, t: (0, l))
    tab = pl.BlockSpec((TC // SUBLANES, STATE_PER_BLOCK), lambda l, t: (0, l))
    w_in = pl.BlockSpec((None, LANES, STATE_PER_BLOCK), lambda l, t: (l, 0, 0))
    w_out = pl.BlockSpec((None, STATE_PER_BLOCK, LANES), lambda l, t: (l, 0, 0))
    scratch = [pltpu.VMEM((TC, STATE_PER_BLOCK), F32)] * 2 + [pltpu.VMEM((SUBLANES, STATE_PER_BLOCK), F32)] * 6
    return NB, TC, nT, narrow, wide, vec, tab, w_in, w_out, scratch


def _s5_scan_fwd(u, mat, *, reverse, name):
    a_re, a_im, b_r, b_i, c_r, c_in = mat
    S, BW = u.shape
    NB, TC, nT, narrow, wide, vec, tab, w_in, w_out, scratch = _s5il_specs(S, BW, reverse)
    p_re, p_im, _, _, a64r, a64i = _seg_tables(a_re, a_im, TC // SUBLANES, reverse)
    xs = jax.ShapeDtypeStruct((S, NB * STATE_PER_BLOCK), F32)
    return _pcall(_s5il_fwd_body, name=name, grid=(NB, nT),
                  in_specs=[narrow, w_in, w_in, w_out, w_out, vec, vec, tab, tab, vec, vec],
                  out_specs=(wide, wide, narrow), out_shape=(xs, xs, jax.ShapeDtypeStruct((S, BW), F32)),
                  scratch=scratch, sem=("parallel", "arbitrary"), TC=TC, desc=reverse)(
        u, b_r, b_i, c_r, c_in, a_re, a_im, p_re, p_im, a64r, a64i)


def _s5il_bwd_body(dy_ref, u_ref, xre_ref, xim_ref, br_ref, bi_ref, cr_ref, cin_ref, are_ref, aim_ref,
                   pr_ref, pi_ref, qr_ref, qi_ref, a64r_ref, a64i_ref,
                   du_ref, dar_ref, dai_ref, dbr_ref, dbi_ref, dcr_ref, dcin_ref,
                   hre_s, him_s, car_r, car_i, cst_r, cst_i, end_r, end_i, *, TC, desc):
    SEG = TC // SUBLANES

    @pl.when(pl.program_id(1) == 0)
    def _():
        for r in (car_r, car_i, dar_ref, dai_ref, dbr_ref, dbi_ref, dcr_ref, dcin_ref):
            r[...] = jnp.zeros_like(r)

    dyb = _bf(dy_ref)
    hre_s[...] = lax.dot_general(dyb, _bf(cr_ref), NT, preferred_element_type=F32)
    him_s[...] = lax.dot_general(dyb, _bf(cin_ref), NT, preferred_element_type=F32)
    dcr_ref[...] += lax.dot_general(_bf(xre_ref), dyb, TN, preferred_element_type=F32)
    dcin_ref[...] += lax.dot_general(_bf(xim_ref), dyb, TN, preferred_element_type=F32)
    ar, ai = are_ref[...], aim_ref[...]
    hr = hi = dr = di = er = ei = jnp.zeros((SUBLANES, STATE_PER_BLOCK), F32)
    for tau in range(SEG):
        rows = pl.ds(SUBLANES * (SEG - 1 - tau if desc else tau), SUBLANES)
        xr, xi = xre_ref[rows, :], xim_ref[rows, :]
        qr, qi = qr_ref[tau:tau + 1, :], qi_ref[tau:tau + 1, :]
        dr = dr + (hr * xr + hi * xi)
        di = di + (hi * xr - hr * xi)
        er = er + (qr * xr + qi * xi)
        ei = ei + (qi * xr - qr * xi)
        pr, pi = _cmul(ar, ai, hr, hi)
        hr = pr + hre_s[rows, :]
        hi = pi + him_s[rows, :]
        hre_s[rows, :] = hr
        him_s[rows, :] = hi
    end_r[...] = hr
    end_i[...] = hi
    _tile_carries(desc, end_r, end_i, a64r_ref[...], a64i_ref[...], car_r, car_i, cst_r, cst_i)
    fr, fi = _cmul(cst_r[...], cst_i[...], er, ei)
    dar_ref[...] += jnp.sum(dr + fr, axis=0, keepdims=True)
    dai_ref[...] += jnp.sum(di + fi, axis=0, keepdims=True)
    _tile_correct(SEG, hre_s, him_s, pr_ref, pi_ref, cst_r, cst_i)
    hrb, hib, ub = _bf(hre_s), _bf(him_s), _bf(u_ref)
    du_ref[...] = (lax.dot_general(hrb, _bf(br_ref), NT, preferred_element_type=F32)
                   + lax.dot_general(hib, _bf(bi_ref), NT, preferred_element_type=F32))
    dbr_ref[...] += lax.dot_general(ub, hrb, TN, preferred_element_type=F32)
    dbi_ref[...] += lax.dot_general(ub, hib, TN, preferred_element_type=F32)


def _s5_scan_bwd(dy, u, x_re, x_im, mat, *, reverse, name):
    a_re, a_im, b_r, b_i, c_r, c_in = mat
    S, BW = u.shape
    desc = not reverse
    NB, TC, nT, narrow, wide, vec, tab, w_in, w_out, scratch = _s5il_specs(S, BW, desc)
    p_re, p_im, q_re, q_im, a64r, a64i = _seg_tables(a_re, -a_im, TC // SUBLANES, desc)
    shapes = (jax.ShapeDtypeStruct((S, BW), F32),) + tuple(jax.ShapeDtypeStruct(m.shape, F32) for m in mat)
    res = _pcall(_s5il_bwd_body, name=name, grid=(NB, nT),
                 in_specs=[narrow, narrow, wide, wide, w_in, w_in, w_out, w_out, vec, vec, tab, tab, tab, tab, vec, vec],
                 out_specs=(narrow, vec, vec, w_in, w_in, w_out, w_out), out_shape=shapes,
                 scratch=scratch, sem=("parallel", "arbitrary"), TC=TC, desc=desc)(
        dy, u, x_re, x_im, b_r, b_i, c_r, c_in, a_re, -a_im, p_re, p_im, q_re, q_im, a64r, a64i)
    return res[0], tuple(res[1:])


def _t5_bucket(rel):
    half = T5_BUCKETS // 2
    max_exact = half // 2
    n = jnp.abs(rel)
    nf = jnp.maximum(n, 1).astype(F32)
    large = max_exact + (jnp.log(nf / max_exact) / math.log(T5_MAX_DISTANCE / max_exact)
                         * (half - max_exact)).astype(jnp.int32)
    large = jnp.minimum(large, half - 1)
    return jnp.where(rel > 0, half, 0) + jnp.where(n < max_exact, n, large)


def _dil_bias(t5_bias, dil):
    W = A_QBLOCK + 2 * DIL_HALF
    off = jnp.arange(W)[None, :] - DIL_HALF - jnp.arange(A_QBLOCK)[:, None]
    pick = (_t5_bucket(off * dil)[..., None] == jnp.arange(T5_BUCKETS)).astype(F32)
    b = jnp.einsum('qkb,bh->hqk', pick, t5_bias.astype(F32), precision=lax.Precision.HIGHEST)
    return jnp.where(jnp.abs(off) <= DIL_HALF, b, NEG_INF)[:, None]


def _na_bias(rpb, rows):
    kr = min(NA_ROWS_MAX, rows)
    ro = (jnp.arange(kr)[None, :] - jnp.arange(kr)[:, None]) + NA_ROWS_MAX - 1
    c = jnp.arange(GRID_W)
    col_start = jnp.clip(c - NA_COLS // 2, 0, GRID_W - NA_COLS)
    col_ok = (c[None, :] >= col_start[:, None]) & (c[None, :] < col_start[:, None] + NA_COLS)
    co = jnp.clip(c[None, :] - c[:, None] + NA_COLS - 1, 0, 2 * NA_COLS - 2)
    pick_r = (ro[..., None] == jnp.arange(2 * NA_ROWS_MAX - 1)).astype(F32)
    pick_c = (co[..., None] == jnp.arange(2 * NA_COLS - 1)).astype(F32)
    b = jnp.einsum('hrqk,pjr->hpqjk',
                   jnp.einsum('hrc,qkc->hrqk', rpb.astype(F32), pick_c, precision=lax.Precision.HIGHEST),
                   pick_r, precision=lax.Precision.HIGHEST)
    b = jnp.where(col_ok[None, None, :, None, :], b, NEG_INF)
    return b.reshape(rpb.shape[0], kr, GRID_W, kr * GRID_W)


def _s5_mats(lam_re, lam_im, log_step, b_re, b_im, c_re, c_im):
    G, P, C = b_re.shape
    NB = G // GROUPS_PER_BLOCK
    eye = jnp.eye(GROUPS_PER_BLOCK, dtype=F32)

    def bd_in(bb):
        t = bb.reshape(NB, GROUPS_PER_BLOCK, P, C).transpose(0, 1, 3, 2)
        return jnp.einsum('jgcp,gh->jgchp', t, eye).reshape(NB, GROUPS_PER_BLOCK * C, GROUPS_PER_BLOCK * P)

    def bd_out(cc):
        t = cc.reshape(NB, GROUPS_PER_BLOCK, C, P).transpose(0, 1, 3, 2)
        return jnp.einsum('jgpc,gh->jgphc', t, eye).reshape(NB, GROUPS_PER_BLOCK * P, GROUPS_PER_BLOCK * C)

    out = []
    for d in range(2):
        step = jnp.exp(log_step[d].astype(F32))[:, None]
        lr = jnp.minimum(lam_re[d].astype(F32), -1e-4)
        li = lam_im[d].astype(F32)
        mag = jnp.exp(lr * step)
        ab_re = mag * jnp.cos(li * step)
        ab_im = mag * jnp.sin(li * step)
        den = lr * lr + li * li
        zr = ((ab_re - 1.0) * lr + ab_im * li) / den
        zi = (ab_im * lr - (ab_re - 1.0) * li) / den
        bb_re = zr[..., None] * b_re - zi[..., None] * b_im
        bb_im = zr[..., None] * b_im + zi[..., None] * b_re
        out.append((ab_re.reshape(1, G * P), ab_im.reshape(1, G * P), bd_in(bb_re), bd_in(bb_im),
                    bd_out(c_re[d].astype(F32)), bd_out(-c_im[d].astype(F32))))
    return tuple(out)


def _sigmoid(z):
    return 1.0 / (1.0 + jnp.exp(-z))


def _strided(t, dil, pad):
    S, C = t.shape
    t = t.reshape(S // dil, dil * C)
    return jnp.pad(t, ((DIL_HALF, DIL_HALF), (0, 0))) if pad else t


def _dilated_fwd(q, k, v, t5_bias):
    S, AW = q.shape
    H = AW // HEAD_DIM
    parts = []
    for _, dil in DILATED_BRANCHES:
        num, m, l = _wattn_fwd(_strided(q, dil, False), _strided(k, dil, True), _strided(v, dil, True),
                               _dil_bias(t5_bias, dil), kind="dil", H=H, dil=dil, qoff=0, koff=0, voff=0,
                               name=f"dilated{dil}_fwd")
        parts += [num.reshape(S, AW), m.reshape(S, AW), l.reshape(S, AW)]

    def merge(n1, m1, l1, n2, m2, l2, n3, m3, l3):
        mx = jnp.maximum(jnp.maximum(m1, m2), m3)
        w1, w2, w3 = jnp.exp(m1 - mx), jnp.exp(m2 - mx), jnp.exp(m3 - mx)
        den = w1 * l1 + w2 * l2 + w3 * l3
        o = (w1 * n1 + w2 * n2 + w3 * n3) / den
        return o, o, mx + jnp.log(den)

    return _ew(merge, parts, out_dtypes=[F32, BF16, F32], name="dilated_merge")


def _dilated_bwd(q, k, v, t5_bias, do, lse, delta):
    S, AW = q.shape
    H = AW // HEAD_DIM
    dqs, dks, dvs = [], [], []
    dt5 = jnp.zeros(t5_bias.shape, F32)
    for _, dil in DILATED_BRANCHES:
        bias, bias_vjp = jax.vjp(functools.partial(_dil_bias, dil=dil), t5_bias)
        dq, dk, dv, db = _wattn_bwd(_strided(q, dil, False), _strided(do, dil, False), _strided(lse, dil, False),
                                    _strided(delta, dil, False), _strided(k, dil, True), _strided(v, dil, True),
                                    bias, kind="dil", H=H, dil=dil, qoff=0, koff=0, voff=0, dq_dtype=F32,
                                    name=f"dilated{dil}_bwd")
        dqs.append(dq.reshape(S, AW))
        dks.append(dk[DIL_HALF:-DIL_HALF].reshape(S, AW))
        dvs.append(dv[DIL_HALF:-DIL_HALF].reshape(S, AW))
        dt5 = dt5 + bias_vjp(db)[0]
    add3 = lambda a, b, c: a + b + c
    return (_ew(add3, dqs, out_dtypes=[BF16], name="dilated_dq_sum")[0],
            _ew(add3, dks, out_dtypes=[BF16], name="dilated_dk_sum")[0],
            _ew(add3, dvs, out_dtypes=[BF16], name="dilated_dv_sum")[0], dt5)


def _s5_fwd(u, mats, d_skip, w_glu, j):
    u = _interleave(u)
    xs, ys = [], []
    for d in range(2):
        x_re, x_im, y_d = _s5_scan_fwd(u, mats[d], reverse=(d == 1), name=f"s5_scan_fwd{d}")
        xs += [x_re, x_im]
        ys.append(y_d)

    def act(y0, y1, u, dsk):
        y = (y0 + y1) + dsk * u
        return y, _gelu(y)

    y, yg = _ew(act, ys + [u], [d_skip], out_dtypes=[F32, F32], name="s5_gelu")
    z = _mm_nn(yg, w_glu, j, "row", name="s5_glu_fwd")
    ob = _ew(lambda yg, z: yg * _sigmoid(z), [yg, z], out_dtypes=[BF16], name="s5_gate")[0]
    return _interleave(ob, inverse=True), (xs, y, yg, z, u)


def _s5_bwd(dmerged, ob_off, mats, d_skip, w_glu, j, saved):
    xs, y, yg, z, u = saved
    dob = _interleave(dmerged[:, ob_off * LANES:])

    def gate_bwd(dob, yg, z):
        sg = _sigmoid(z)
        return dob * yg * (sg * (1.0 - sg)), dob * sg

    dz, dyg1 = _ew(gate_bwd, [dob, yg, z], out_dtypes=[BF16, F32], name="s5_gate_bwd")
    dw_glu = _mm_tn(yg, dz, w_glu.shape[0], "row", name="s5_glu_dw")
    dyg2 = _mm_nt(dz, w_glu, j, "row", name="s5_glu_dx")

    def act_bwd(d1, d2, y, u):
        dy = (d1 + d2) * _gelu_grad(y)
        return dy, jnp.sum(dy * u, axis=0, keepdims=True)

    dy, dd = _ew(act_bwd, [dyg1, dyg2, y, u], out_dtypes=[F32], n_acc=1, name="s5_gelu_bwd")
    dmats, dus = [], []
    for d in range(2):
        du_d, dmat = _s5_scan_bwd(dy, u, xs[2 * d], xs[2 * d + 1], mats[d], reverse=(d == 1),
                                  name=f"s5_scan_bwd{d}")
        dus.append(du_d)
        dmats.append(dmat)
    du = _ew(lambda dy, d0, d1, dsk: dy * dsk + (d0 + d1), [dy] + dus, [d_skip], out_dtypes=[BF16],
             name="s5_du_sum")[0]
    return _interleave(du, inverse=True), tuple(dmats), dd, dw_glu


def _ab_fwd(x, j, P, W):
    t5 = P["t5_bias"]
    AW = t5.shape[1] * HEAD_DIM
    hn = _rms_fwd(x, P["norm_mix"][2 * j][None], name="rms_fwd")
    proj = _mm_nn(hn, W["ab_w_in"], j, "col", name="ab_in_fwd")
    q, k, v = (proj[:, i * AW:(i + 1) * AW].astype(BF16) for i in range(3))
    u = proj[:, 3 * AW:]
    oa32, oa16, lse = _dilated_fwd(q, k, v, t5)
    mats = _s5_mats(*(P[n][j] for n in _S5_PARAMS))
    ob, s5_saved = _s5_fwd(u, mats, P["s5_d"][j][None], W["s5_w_glu"], j)
    merged = jnp.concatenate([oa16, ob], axis=1)
    x1 = _mm_nn(merged, W["ab_w_out"], j, "row", mode="res", res=x, name="ab_out_fwd")
    return x1, (x, hn, q, k, v, u, oa32, lse, merged, s5_saved)


def _ab_bwd(dx1, j, P, W, saved):
    x, hn, q, k, v, u, oa32, lse, merged, s5_saved = saved
    t5 = P["t5_bias"]
    AW = t5.shape[1] * HEAD_DIM
    J = W["ab_w_in"].shape[0]
    dmerged = _mm_nt(dx1, W["ab_w_out"], j, "row", name="ab_out_dx")
    dw_out = _mm_tn(merged, dx1, J, "row", name="ab_out_dw")
    delta, do16 = _attn_delta(dmerged, 0, oa32, name="dilated_delta")
    dq, dk, dv, dt5 = _dilated_bwd(q, k, v, t5, do16, lse, delta)
    s5_params = tuple(P[n][j] for n in _S5_PARAMS)
    mats, mats_vjp = jax.vjp(_s5_mats, *s5_params)
    du, dmats, dd, dw_glu = _s5_bwd(dmerged, AW // LANES, mats, P["s5_d"][j][None], W["s5_w_glu"], j, s5_saved)
    ds5 = mats_vjp(dmats)
    dproj = jnp.concatenate([dq, dk, dv, du], axis=1)
    dw_in = _mm_tn(hn, dproj, J, "col", name="ab_in_dw")
    dhn = _mm_nt(dproj, W["ab_w_in"], j, "col", name="ab_in_dx")
    dx, dg = _rms_bwd(dhn, x, P["norm_mix"][2 * j][None], dx1, name="rms_bwd")
    small = dict(zip(_S5_PARAMS, ds5), s5_d=dd[0], t5_bias=dt5)
    return dx, dg[0], dict(ab_w_in=dw_in, ab_w_out=dw_out, s5_w_glu=dw_glu), small


def _c_fwd(x, j, P, W):
    H = P["c_rpb"].shape[1]
    hn = _rms_fwd(x, P["norm_mix"][2 * j + 1][None], name="rms_fwd")
    qkv = _mm_nn(hn, W["c_w_qkv"], j, "col", out_dtype=BF16, name="c_qkv_fwd")
    bias = _na_bias(P["c_rpb"][j], x.shape[0] // GRID_W)
    o, lse = _wattn_fwd(qkv, qkv, qkv, bias, kind="na", H=H, dil=1, qoff=0, koff=H, voff=2 * H, name="na_fwd")
    x1 = _mm_nn(o, W["c_w_out"], j, "row", mode="res", res=x, name="c_out_fwd")
    return x1, (x, hn, qkv, o, lse)


def _c_bwd(dx1, j, P, W, saved):
    x, hn, qkv, o, lse = saved
    H = P["c_rpb"].shape[1]
    J = W["c_w_qkv"].shape[0]
    do = _mm_nt(dx1, W["c_w_out"], j, "row", name="c_out_dx")
    dw_out = _mm_tn(o, dx1, J, "row", name="c_out_dw")
    bias, bias_vjp = jax.vjp(functools.partial(_na_bias, rows=x.shape[0] // GRID_W), P["c_rpb"][j])
    dq, dk, dv, db = _wattn_bwd(qkv, do, lse, o, qkv, qkv, bias, kind="na", H=H, dil=1, qoff=0, koff=H,
                                voff=2 * H, dq_dtype=BF16, name="na_bwd")
    dqkv = jnp.concatenate([dq, dk.astype(BF16), dv.astype(BF16)], axis=1)
    dw_qkv = _mm_tn(hn, dqkv, J, "col", name="c_qkv_dw")
    dhn = _mm_nt(dqkv, W["c_w_qkv"], j, "col", name="c_qkv_dx")
    dx, dg = _rms_bwd(dhn, x, P["norm_mix"][2 * j + 1][None], dx1, name="rms_bwd")
    return dx, dg[0], dict(c_w_qkv=dw_qkv, c_w_out=dw_out), dict(c_rpb=bias_vjp(db)[0])


def _layer_pieces(i):
    j = i // 2
    mixer = [("ab_w_in", j), ("ab_w_out", j), ("s5_w_glu", j)] if i % 2 == 0 else [("c_w_qkv", j), ("c_w_out", j)]
    return mixer, [("mlp_w1", i)], [("mlp_w2", i)]


def _mlp_fwd(x, i, P, W, depth):
    hn = _rms_fwd(x, P["norm_mlp"][i][None], name="rms_fwd")
    if i + 1 < depth:
        mixer, w1, w2 = _layer_pieces(i + 1)
        (a, hdn), W = _mm_nn(hn, W["mlp_w1"], i, "col", mode="relu2", name="mlp_w1_fwd", host=(W, mixer + w2))
        x2, W = _mm_nn(hdn, W["mlp_w2"], i, "row", mode="res", res=x, name="mlp_w2_fwd", host=(W, w1))
    else:
        a, hdn = _mm_nn(hn, W["mlp_w1"], i, "col", mode="relu2", name="mlp_w1_fwd")
        x2 = _mm_nn(hdn, W["mlp_w2"], i, "row", mode="res", res=x, name="mlp_w2_fwd")
    return x2, (x, hn, a, hdn), W


def _mlp_bwd(dx2, i, P, W, saved):
    x, hn, a, hdn = saved
    J = W["mlp_w1"].shape[0]
    da = _mm_nt(dx2, W["mlp_w2"], i, "row", out_dtype=BF16, mode="dact", act=a, name="mlp_w2_dx")
    dw2 = _mm_tn(hdn, dx2, J, "row", name="mlp_w2_dw")
    dw1 = _mm_tn(hn, da, J, "col", name="mlp_w1_dw")
    dhn = _mm_nt(da, W["mlp_w1"], i, "col", name="mlp_w1_dx")
    dx, dg = _rms_bwd(dhn, x, P["norm_mlp"][i][None], dx2, name="rms_bwd")
    return dx, dg[0], dict(mlp_w1=dw1, mlp_w2=dw2)


_S5_PARAMS = ("s5_lam_re", "s5_lam_im", "s5_log_step", "s5_b_re", "s5_b_im", "s5_c_re", "s5_c_im")
_BIG = ("ab_w_in", "ab_w_out", "s5_w_glu", "c_w_qkv", "c_w_out", "mlp_w1", "mlp_w2")
_SMALL = ("t5_bias", "s5_lam_re", "s5_lam_im", "s5_log_step", "s5_b_re", "s5_b_im", "s5_c_re", "s5_c_im", "s5_d",
          "c_rpb", "norm_mix", "norm_mlp", "norm_final")
_WEIGHTS = ("t5_bias", "ab_w_in", "ab_w_out", "s5_lam_re", "s5_lam_im", "s5_log_step", "s5_b_re", "s5_b_im",
            "s5_c_re", "s5_c_im", "s5_d", "s5_w_glu", "c_w_qkv", "c_w_out", "c_rpb", "norm_mix", "norm_mlp",
            "mlp_w1", "mlp_w2", "norm_final")


def _local_grads(x, target, P, W):
    depth = P["norm_mix"].shape[0]
    saved = []
    h = x
    W = _gather_pieces(W, sum(_layer_pieces(0), []))
    for i in range(depth):
        h, s_mix = (_ab_fwd if i % 2 == 0 else _c_fwd)(h, i // 2, P, W)
        h, s_mlp, W = _mlp_fwd(h, i, P, W, depth)
        saved.append((s_mix, s_mlp))
    dh, dg_final, loss_cols = _loss_and_grad(h, P["norm_final"][None], target, name="loss_head")
    big = {n: [None] * P[n].shape[0] for n in _BIG}
    small = {n: jnp.zeros(P[n].shape, F32) for n in _SMALL}
    small["norm_final"] = dg_final[0]
    for i in reversed(range(depth)):
        s_mix, s_mlp = saved[i]
        j = i // 2
        dh, dg, dbig = _mlp_bwd(dh, i, P, W, s_mlp)
        small["norm_mlp"] = small["norm_mlp"].at[i].set(dg)
        for n, g in dbig.items():
            big[n][i] = g
        dh, dg, dbig, dsmall = (_ab_bwd if i % 2 == 0 else _c_bwd)(dh, j, P, W, s_mix)
        small["norm_mix"] = small["norm_mix"].at[i].set(dg)
        for n, g in dbig.items():
            big[n][j] = g
        for n, g in dsmall.items():
            if n == "t5_bias":
                small[n] = small[n] + g
            else:
                small[n] = small[n].at[j].set(g.reshape(P[n].shape[1:]))
    return loss_cols[0, 0], dh, big, small


def _place():
    x, y, c = lax.axis_index("x"), lax.axis_index("y"), lax.axis_index("c")
    return x, y, c, ((1 - x, y), (x, 1 - y), (1 - x, 1 - y))


def _comm_call(body, arrays, out_shape, sems, *, name, in_place=False, **static):
    hbm = pl.BlockSpec(memory_space=pltpu.HBM)
    return pl.pallas_call(
        functools.partial(body, n=len(arrays), **static), name=name, in_specs=[hbm] * len(arrays),
        out_specs=tuple([hbm] * len(out_shape)), out_shape=tuple(out_shape),
        input_output_aliases={t: t for t in range(len(arrays))} if in_place else {},
        scratch_shapes=[pltpu.SemaphoreType.DMA((k,)) for k in sems])(*arrays)


def _cast_body(s_ref, w_ref, o_ref):
    o_ref[...] = w_ref[...].astype(o_ref.dtype)


def _cast_weights(weights, place):
    bufs = []
    for w in weights:
        L, Kd, Nd = w.shape
        tr = _tile(Kd, 256)
        bufs.append(_sliced_call(
            _cast_body, place, [w], [lambda l, i, p: (l, i, 0)], [(None, tr, Nd)], (None, None, tr, Nd),
            lambda l, i, p: (p[1], l, i, 0), jax.ShapeDtypeStruct((4, L, Kd, Nd), BF16), (L, Kd // tr),
            name="weights_cast"))
    return bufs


def _pair_exchange_body(*refs, n):
    ins, outs = refs[:n], refs[n:2 * n]
    send_sems, recv_sems = refs[2 * n:]
    x, y, c, _ = _place()
    cps = []
    for t in range(n):
        h = outs[t].shape[0]
        cps.append(pltpu.make_async_remote_copy(
            src_ref=ins[t].at[pl.ds((1 - c) * h, h)], dst_ref=outs[t], send_sem=send_sems.at[t],
            recv_sem=recv_sems.at[t], device_id=(x, y, 1 - c), device_id_type=MESH))
    for cp in cps:
        cp.start()
    for cp in cps:
        cp.wait()


def _chip_exchange_body(*refs, n):
    ins, outs = refs[:n], refs[n:2 * n]
    send_sems, recv_sems = refs[2 * n:]
    x, y, c, chips = _place()
    cps = []
    for t in range(n):
        h = ins[t].shape[0]
        for p, (px, py) in enumerate(chips):
            cps.append(pltpu.make_async_remote_copy(
                src_ref=ins[t].at[pl.ds(0, h), 2 * px + py], dst_ref=outs[t].at[p], send_sem=send_sems.at[3 * t + p],
                recv_sem=recv_sems.at[3 * t + p], device_id=(px, py, c), device_id_type=MESH))
    for cp in cps:
        cp.start()
    for cp in cps:
        cp.wait()


def _pair_share_body(*refs, n):
    bufs = refs[n:2 * n]
    send_sems, recv_sems = refs[2 * n:]
    x, y, c, _ = _place()
    sends, recvs = [], []
    for t in range(n):
        h = bufs[t].shape[0] // 2
        for half, group in ((c, sends), (1 - c, recvs)):
            rows = bufs[t].at[pl.ds(half * h, h)]
            group.append(pltpu.make_async_remote_copy(
                src_ref=rows, dst_ref=rows, send_sem=send_sems.at[t], recv_sem=recv_sems.at[t],
                device_id=(x, y, 1 - c), device_id_type=MESH))
    for cp in sends:
        cp.start()
    for cp in recvs:
        cp.wait_recv()
    for cp in sends:
        cp.wait_send()


def _allreduce_body(in_ref, out_ref, send_sems, recv_sems, local_sem, n):
    x, y, c, _ = _place()
    flip = lambda v, bit: 1 - v if bit else v
    peers = [(flip(x, k & 4), flip(y, k & 2), flip(c, k & 1)) for k in range(1, 8)]

    def remote(k, slot):
        return pltpu.make_async_remote_copy(
            src_ref=in_ref, dst_ref=out_ref.at[slot], send_sem=send_sems.at[k], recv_sem=recv_sems.at[k],
            device_id=peers[k], device_id_type=MESH)

    local = pltpu.make_async_copy(in_ref, out_ref.at[4 * x + 2 * y + c], local_sem.at[0])
    sends = [remote(k, 4 * x + 2 * y + c) for k in range(7)]
    local.start()
    for cp in sends:
        cp.start()
    for k, (px, py, pc) in enumerate(peers):
        remote(k, 4 * px + 2 * py + pc).wait_recv()
    for cp in sends:
        cp.wait_send()
    local.wait()


def _sliced_call(body, scalars, arrays, in_maps, blocks, out_block, out_map, out_shape, grid, *, name):
    grid_spec = pltpu.PrefetchScalarGridSpec(
        num_scalar_prefetch=1, grid=grid,
        in_specs=[pl.BlockSpec(b, m) for b, m in zip(blocks, in_maps)],
        out_specs=pl.BlockSpec(out_block, out_map))
    return pl.pallas_call(
        functools.partial(body), name=name, grid_spec=grid_spec, out_shape=out_shape,
        compiler_params=pltpu.CompilerParams(vmem_limit_bytes=V7X_VMEM_LIMIT_BYTES))(scalars, *arrays)


def _chip_sum_body(s_ref, g_ref, r_ref, o_ref):
    o_ref[...] = (g_ref[...] + r_ref[...]).astype(o_ref.dtype)


def _final_sum_body(s_ref, g_ref, r1_ref, a_ref, b_ref, c_ref, o_ref):
    o_ref[...] = (((g_ref[...] + r1_ref[...]) + a_ref[...].astype(F32)) + b_ref[...].astype(F32)) + c_ref[...].astype(F32)


def _reduce_big(stacks, place):
    n = len(stacks)
    half = [jax.ShapeDtypeStruct((s.shape[0] // 2,) + s.shape[1:], F32) for s in stacks]
    from_pair = _comm_call(_pair_exchange_body, stacks, half, (n, n), name="grads_pair_exchange")
    chip16 = []
    for s, r in zip(stacks, from_pair):
        h, J, Kd, Nd = r.shape
        tr = _tile(Kd, 256)
        blk = (None, None, tr, Nd)
        chip16.append(_sliced_call(
            _chip_sum_body, place, [s, r],
            [lambda l, j, i, p: (p[0] * h + l, j, i, 0), lambda l, j, i, p: (l, j, i, 0)], [blk, blk], blk,
            lambda l, j, i, p: (l, j, i, 0), jax.ShapeDtypeStruct(r.shape, BF16), (h, J, Kd // tr),
            name="grads_chip_sum"))
    recv = [jax.ShapeDtypeStruct((3, a.shape[0]) + a.shape[2:], BF16) for a in chip16]
    from_chips = _comm_call(_chip_exchange_body, chip16, recv, (3 * n, 3 * n), name="grads_chip_exchange")
    sums = []
    for s, r, f in zip(stacks, from_pair, from_chips):
        h, J, Kd, Nd = r.shape
        tr = _tile(Kd, 256)
        blk4, blk3 = (None, None, tr, Nd), (None, tr, Nd)
        mine = lambda l, i, p: (l, p[1], i, 0)
        sums.append(_sliced_call(
            _final_sum_body, place, [s, r, f, f, f],
            [lambda l, i, p: (p[0] * h + l, p[1], i, 0), mine] + [functools.partial(lambda l, i, p, q: (q, l, i, 0), q=q)
                                                                  for q in range(3)],
            [blk4, blk4, blk4, blk4, blk4], blk3, lambda l, i, p: (p[0] * h + l, i, 0),
            jax.ShapeDtypeStruct((2 * h, Kd, Nd), F32), (h, Kd // tr), name="grads_final_sum"))
    return _comm_call(_pair_share_body, sums, [jax.ShapeDtypeStruct(s.shape, F32) for s in sums], (n, n),
                      in_place=True, name="grads_pair_share")


def _allreduce_small(buf):
    gathered = pl.pallas_call(
        functools.partial(_allreduce_body, n=1), name="small_allgather",
        in_specs=[pl.BlockSpec(memory_space=pltpu.HBM)], out_specs=pl.BlockSpec(memory_space=pltpu.HBM),
        out_shape=jax.ShapeDtypeStruct((8,) + buf.shape, F32),
        scratch_shapes=[pltpu.SemaphoreType.DMA((7,)), pltpu.SemaphoreType.DMA((7,)),
                        pltpu.SemaphoreType.DMA((1,))])(buf)

    def total(*b):
        acc = b[0]
        for t in b[1:]:
            acc = acc + t
        return acc

    return _ew(total, [gathered[i] for i in range(8)], out_dtypes=[F32], name="small_sum")[0]


def _pack(parts):
    flat = jnp.concatenate([p.reshape(-1).astype(F32) for p in parts])
    rows = -(-flat.shape[0] // (8 * LANES)) * 8
    return jnp.pad(flat, (0, rows * LANES - flat.shape[0])).reshape(rows, LANES)


def _unpack(buf, shapes):
    flat = buf.reshape(-1)
    out, at = [], 0
    for s in shapes:
        size = math.prod(s)
        out.append(flat[at:at + size].reshape(s))
        at += size
    return out


_INPUTS = ("x",) + _WEIGHTS + ("loss_target",) + tuple("m_" + n for n in _WEIGHTS) + tuple("v_" + n for n in _WEIGHTS)


def kernel(x, t5_bias, ab_w_in, ab_w_out, s5_lam_re, s5_lam_im, s5_log_step, s5_b_re, s5_b_im, s5_c_re, s5_c_im,
           s5_d, s5_w_glu, c_w_qkv, c_w_out, c_rpb, norm_mix, norm_mlp, mlp_w1, mlp_w2, norm_final, loss_target,
           m_t5_bias, m_ab_w_in, m_ab_w_out, m_s5_lam_re, m_s5_lam_im, m_s5_log_step, m_s5_b_re, m_s5_b_im,
           m_s5_c_re, m_s5_c_im, m_s5_d, m_s5_w_glu, m_c_w_qkv, m_c_w_out, m_c_rpb, m_norm_mix, m_norm_mlp,
           m_mlp_w1, m_mlp_w2, m_norm_final, v_t5_bias, v_ab_w_in, v_ab_w_out, v_s5_lam_re, v_s5_lam_im,
           v_s5_log_step, v_s5_b_re, v_s5_b_im, v_s5_c_re, v_s5_c_im, v_s5_d, v_s5_w_glu, v_c_w_qkv, v_c_w_out,
           v_c_rpb, v_norm_mix, v_norm_mlp, v_mlp_w1, v_mlp_w2, v_norm_final):
    args = (x, t5_bias, ab_w_in, ab_w_out, s5_lam_re, s5_lam_im, s5_log_step, s5_b_re, s5_b_im, s5_c_re, s5_c_im,
            s5_d, s5_w_glu, c_w_qkv, c_w_out, c_rpb, norm_mix, norm_mlp, mlp_w1, mlp_w2, norm_final, loss_target,
            m_t5_bias, m_ab_w_in, m_ab_w_out, m_s5_lam_re, m_s5_lam_im, m_s5_log_step, m_s5_b_re, m_s5_b_im,
            m_s5_c_re, m_s5_c_im, m_s5_d, m_s5_w_glu, m_c_w_qkv, m_c_w_out, m_c_rpb, m_norm_mix, m_norm_mlp,
            m_mlp_w1, m_mlp_w2, m_norm_final, v_t5_bias, v_ab_w_in, v_ab_w_out, v_s5_lam_re, v_s5_lam_im,
            v_s5_log_step, v_s5_b_re, v_s5_b_im, v_s5_c_re, v_s5_c_im, v_s5_d, v_s5_w_glu, v_c_w_qkv, v_c_w_out,
            v_c_rpb, v_norm_mix, v_norm_mlp, v_mlp_w1, v_mlp_w2, v_norm_final)
    A = dict(zip(_INPUTS, args, strict=True))
    P = {n: A[n] for n in _WEIGHTS}
    place = jnp.stack([lax.axis_index("c"), 2 * lax.axis_index("x") + lax.axis_index("y")]).astype(jnp.int32)

    gathered = _cast_weights([P[n] for n in _BIG], place)
    W = dict(zip(_BIG, gathered))
    loss, dx, big, small = _local_grads(A["x"][0], A["loss_target"][0], P, W)

    stacks = [jnp.stack(big[n]) for n in _BIG]
    big_grads = dict(zip(_BIG, _reduce_big(stacks, place)))
    small_shapes = [P[n].shape for n in _SMALL] + [(1,)]
    reduced = _unpack(_allreduce_small(_pack([small[n] for n in _SMALL] + [loss.reshape(1)])), small_shapes)
    small_grads = dict(zip(_SMALL, reduced[:-1]))
    loss = reduced[-1][0]

    grads, delta, new_m, new_v = {}, {}, {}, {}
    for n in _BIG:
        g = big_grads[n]
        two_d = lambda t: t.reshape(-1, t.shape[-1])
        d, m, v = _adamw(two_d(P[n]), two_d(g), two_d(A["m_" + n]), two_d(A["v_" + n]), name="adamw")
        grads[n] = g
        delta[n], new_m[n], new_v[n] = (t.reshape(g.shape) for t in (d, m, v))
    d, m, v = _adamw(_pack([P[n] for n in _SMALL]), _pack([small_grads[n] for n in _SMALL]),
                     _pack([A["m_" + n] for n in _SMALL]), _pack([A["v_" + n] for n in _SMALL]), name="adamw_small")
    shapes = [P[n].shape for n in _SMALL]
    for n, dn, mn, vn in zip(_SMALL, _unpack(d, shapes), _unpack(m, shapes), _unpack(v, shapes)):
        grads[n] = small_grads[n]
        delta[n], new_m[n], new_v[n] = dn, mn, vn
    return (loss, dx[None], *[grads[n] for n in _WEIGHTS], *[delta[n] for n in _WEIGHTS],
            *[new_m[n] for n in _WEIGHTS], *[new_v[n] for n in _WEIGHTS])
```

```python
import functools
import math

import jax
import jax.numpy as jnp
from jax import lax
from jax.experimental import pallas as pl
from jax.experimental.pallas import tpu as pltpu

F32 = jnp.float32
BF16 = jnp.bfloat16

HEAD_DIM = 128
LANES = 128
DILATED_BRANCHES = ((128, 1), (512, 4), (2048, 16))
A_QBLOCK = 128
DIL_HALF = 64
B_GROUP = 16
B_STATE = 64
GROUPS_PER_BLOCK = LANES // B_GROUP
STATE_PER_BLOCK = GROUPS_PER_BLOCK * B_STATE
GRID_W = 64
NA_ROWS_MAX = 8
NA_COLS = 16
T5_BUCKETS = 32
T5_MAX_DISTANCE = 1024
RMS_EPS = 1e-6
NEG_INF = -1e30
ADAM_LR = 0.001
ADAM_B1 = 0.9
ADAM_B2 = 0.999
ADAM_EPS = 1e-08
ADAM_WD = 0.01
ADAM_STEP = 10
V7X_VMEM_LIMIT_BYTES = 56 * 1024 * 1024

NN = (((1,), (0,)), ((), ()))
NT = (((1,), (1,)), ((), ()))
TN = (((0,), (0,)), ((), ()))
MESH = pl.DeviceIdType.MESH


def _tile(n, pref):
    if n <= pref:
        return n
    for t in range(pref - pref % LANES, 0, -LANES):
        if n % t == 0:
            return t
    t = pref
    while t >= 8:
        if n % t == 0:
            return t
        t //= 2
    return n


def _pcall(body, *, name, grid, in_specs, out_specs, out_shape, scratch=(), sem=None, aliases=None, **static):
    params = dict(vmem_limit_bytes=V7X_VMEM_LIMIT_BYTES)
    if sem is not None:
        params["dimension_semantics"] = sem
    return pl.pallas_call(
        functools.partial(body, **static), name=name, grid=grid, in_specs=in_specs, out_specs=out_specs,
        out_shape=out_shape, scratch_shapes=list(scratch), input_output_aliases=aliases or {},
        compiler_params=pltpu.CompilerParams(**params))


def _gather_plan(bufs, pieces, sems):
    ici_send, ici_recv, pair_send, pair_recv = sems
    x, y, c = lax.axis_index("x"), lax.axis_index("y"), lax.axis_index("c")
    chips = ((1 - x, y), (x, 1 - y), (1 - x, 1 - y))
    slots = [2 * x + y] + [2 * px + py for px, py in chips]
    every = [(k, p) for k in range(len(pieces)) for p in range(3)]

    def part(k, slot, half):
        t, layer = pieces[k]
        h = bufs[t].shape[2] // 2
        return bufs[t].at[slot, layer, pl.ds(half * h, h)]

    def over_ici(k, p, slot):
        return pltpu.make_async_remote_copy(
            src_ref=part(k, slot, c), dst_ref=part(k, slot, c), send_sem=ici_send.at[3 * k + p],
            recv_sem=ici_recv.at[3 * k + p], device_id=(*chips[p], c), device_id_type=MESH)

    def to_pair(k, p, half):
        return pltpu.make_async_remote_copy(
            src_ref=part(k, slots[1 + p], half), dst_ref=part(k, slots[1 + p], half),
            send_sem=pair_send.at[3 * k + p], recv_sem=pair_recv.at[3 * k + p], device_id=(x, y, 1 - c),
            device_id_type=MESH)

    def start():
        for k, p in every:
            over_ici(k, p, slots[0]).start()

    def finish():
        for k, p in every:
            over_ici(k, p, slots[1 + p]).wait_recv()
            to_pair(k, p, c).start()
        for k, p in every:
            to_pair(k, p, 1 - c).wait_recv()
        for k, p in every:
            over_ici(k, p, slots[0]).wait_send()
            to_pair(k, p, c).wait_send()

    return start, finish


def _gather_sems(pieces):
    return [pltpu.SemaphoreType.DMA((3 * len(pieces),))] * 4


def _hosted_body(*refs, host_body, n_in, n_out, n_buf, pieces, **static):
    ins = refs[:n_in]
    outs = refs[n_in + n_buf:n_in + n_buf + n_out]
    bufs = refs[n_in + n_buf + n_out:n_in + 2 * n_buf + n_out]
    scratch = refs[n_in + 2 * n_buf + n_out:-4]
    start, finish = _gather_plan(bufs, pieces, refs[-4:])
    ids = [pl.program_id(a) for a in range(3)]
    last = [pl.num_programs(a) - 1 for a in range(3)]

    @pl.when((ids[0] == 0) & (ids[1] == 0) & (ids[2] == 0))
    def _():
        start()

    host_body(*ins, *outs, *scratch, **static)

    @pl.when((ids[0] == last[0]) & (ids[1] == last[1]) & (ids[2] == last[2]))
    def _():
        finish()


def _gather_pieces_body(*refs, n, pieces):
    start, finish = _gather_plan(refs[n:2 * n], pieces, refs[2 * n:])
    start()
    finish()


def _named_pieces(W, pieces):
    names = sorted({n for n, _ in pieces})
    return names, [W[n] for n in names], tuple((names.index(n), layer) for n, layer in pieces)


def _gather_pieces(W, pieces):
    names, bufs, idx = _named_pieces(W, pieces)
    hbm = pl.BlockSpec(memory_space=pltpu.HBM)
    new = pl.pallas_call(
        functools.partial(_gather_pieces_body, n=len(bufs), pieces=idx), name="weights_gather",
        in_specs=[hbm] * len(bufs), out_specs=tuple([hbm] * len(bufs)),
        out_shape=tuple(jax.ShapeDtypeStruct(b.shape, b.dtype) for b in bufs),
        input_output_aliases={t: t for t in range(len(bufs))}, scratch_shapes=_gather_sems(idx))(*bufs)
    return {**W, **dict(zip(names, new))}


def _mm_finish(acc, rest, mode):
    if mode == "plain":
        rest[0][...] = acc.astype(rest[0].dtype)
    elif mode == "res":
        rest[1][...] = (rest[0][...] + acc).astype(rest[1].dtype)
    elif mode == "relu2":
        rest[0][...] = acc
        r = jnp.maximum(acc, 0.0)
        rest[1][...] = (r * r).astype(rest[1].dtype)
    elif mode == "dact":
        rest[1][...] = (acc * (2.0 * jnp.maximum(rest[0][...], 0.0))).astype(rest[1].dtype)


def _mm_body(a_ref, b_ref, *rest, nk, dims, mode):
    prod = lax.dot_general(a_ref[...].astype(BF16), b_ref[...].astype(BF16), dims, preferred_element_type=F32)
    if nk == 1:
        _mm_finish(prod, rest, mode)
        return
    acc_ref = rest[-1]
    k = pl.program_id(2)

    @pl.when(k == 0)
    def _():
        acc_ref[...] = prod

    @pl.when(k > 0)
    def _():
        acc_ref[...] += prod

    @pl.when(k == nk - 1)
    def _():
        _mm_finish(acc_ref[...], rest, mode)


def _w_dims(w, split):
    J, _, Kd, Nd = w.shape
    return (J, Kd, J * Nd, Kd, Nd) if split == "col" else (J, J * Kd, Nd, Kd, Nd)


def _w_spec(split, layer, tk, tn, Kd, Nd, kn_of):
    kps, nps = Kd // tk, Nd // tn

    def index(*g):
        kb, nb = kn_of(*g)
        if split == "col":
            return nb // nps, layer, kb, nb % nps
        return kb // kps, layer, kb % kps, nb

    return pl.BlockSpec((None, None, tk, tn), index)


def _mm_nn(a, w, layer, split, *, name, out_dtype=F32, mode="plain", res=None, host=None):
    M = a.shape[0]
    J, K, N, Kd, Nd = _w_dims(w, split)
    tm, tn, tk = _tile(M, 1024), _tile(Nd, 1024), _tile(Kd, 2048)
    in_specs = [pl.BlockSpec((tm, tk), lambda i, j, k: (i, k)),
                _w_spec(split, layer, tk, tn, Kd, Nd, lambda i, j, k: (k, j))]
    args = [a, w]
    o_spec = pl.BlockSpec((tm, tn), lambda i, j, k: (i, j))
    if mode == "res":
        in_specs.append(o_spec)
        args.append(res)
    if mode == "relu2":
        out_shape = (jax.ShapeDtypeStruct((M, N), F32), jax.ShapeDtypeStruct((M, N), BF16))
        out_specs = (o_spec, o_spec)
    else:
        out_shape = (jax.ShapeDtypeStruct((M, N), out_dtype),)
        out_specs = (o_spec,)
    grid = (M // tm, N // tn, K // tk)
    scratch = [pltpu.VMEM((tm, tn), F32)]
    if host is None:
        out = _pcall(_mm_body, name=name, grid=grid, in_specs=in_specs, out_specs=out_specs, out_shape=out_shape,
                     scratch=scratch, sem=("parallel", "parallel", "arbitrary"), nk=K // tk, dims=NN, mode=mode)(*args)
        return out if mode == "relu2" else out[0]
    W, pieces = host
    names, bufs, idx = _named_pieces(W, pieces)
    hbm = pl.BlockSpec(memory_space=pltpu.HBM)
    n_in, n_out, n_buf = len(args), len(out_shape), len(bufs)
    out = _pcall(_hosted_body, name=name, grid=grid, in_specs=in_specs + [hbm] * n_buf,
                 out_specs=tuple(out_specs) + (hbm,) * n_buf,
                 out_shape=tuple(out_shape) + tuple(jax.ShapeDtypeStruct(b.shape, b.dtype) for b in bufs),
                 scratch=scratch + _gather_sems(idx), sem=("arbitrary",) * 3,
                 aliases={n_in + t: n_out + t for t in range(n_buf)}, host_body=_mm_body, n_in=n_in, n_out=n_out,
                 n_buf=n_buf, pieces=idx, nk=K // tk, dims=NN, mode=mode)(*args, *bufs)
    res_out = out[:n_out] if mode == "relu2" else out[0]
    return res_out, {**W, **dict(zip(names, out[n_out:]))}


def _mm_nt(a, w, layer, split, *, name, out_dtype=F32, mode="plain", act=None):
    M = a.shape[0]
    J, K, N, Kd, Nd = _w_dims(w, split)
    tm, tko, tc = _tile(M, 1024), _tile(Kd, 1024), _tile(Nd, 2048)
    in_specs = [pl.BlockSpec((tm, tc), lambda i, j, c: (i, c)),
                _w_spec(split, layer, tko, tc, Kd, Nd, lambda i, j, c: (j, c))]
    args = [a, w]
    o_spec = pl.BlockSpec((tm, tko), lambda i, j, c: (i, j))
    if mode == "dact":
        in_specs.append(o_spec)
        args.append(act)
    return _pcall(_mm_body, name=name, grid=(M // tm, K // tko, N // tc), in_specs=in_specs, out_specs=o_spec,
                  out_shape=jax.ShapeDtypeStruct((M, K), out_dtype), scratch=[pltpu.VMEM((tm, tko), F32)],
                  sem=("parallel", "parallel", "arbitrary"), nk=N // tc, dims=NT, mode=mode)(*args)


def _mm_tn(a, b, J, split, *, name):
    M, K = a.shape
    N = b.shape[1]
    Kd, Nd = (K, N // J) if split == "col" else (K // J, N)
    tk, tn, tc = _tile(Kd, 1024), _tile(Nd, 1024), _tile(M, 2048)
    kps, nps = Kd // tk, Nd // tn
    in_specs = [pl.BlockSpec((tc, tk), lambda i, j, c: (c, i)),
                pl.BlockSpec((tc, tn), lambda i, j, c: (c, j))]
    if split == "col":
        o_spec = pl.BlockSpec((None, tk, tn), lambda i, j, c: (j // nps, i, j % nps))
    else:
        o_spec = pl.BlockSpec((None, tk, tn), lambda i, j, c: (i // kps, i % kps, j))
    return _pcall(_mm_body, name=name, grid=(K // tk, N // tn, M // tc), in_specs=in_specs, out_specs=o_spec,
                  out_shape=jax.ShapeDtypeStruct((J, Kd, Nd), F32), scratch=[pltpu.VMEM((tk, tn), F32)],
                  sem=("parallel", "parallel", "arbitrary"), nk=M // tc, dims=TN, mode="plain")(a, b)


def _ew_body(*refs, fn, n_in, n_out, n_acc):
    res = fn(*[r[...] for r in refs[:n_in]])
    if not isinstance(res, (tuple, list)):
        res = (res,)
    outs = refs[n_in:n_in + n_out]
    accs = refs[n_in + n_out:]
    for o, r in zip(outs, res[:n_out]):
        o[...] = r.astype(o.dtype)
    if n_acc:
        first = pl.program_id(1) == 0
        for a, r in zip(accs, res[n_out:]):
            @pl.when(first)
            def _(a=a):
                a[...] = jnp.zeros_like(a)

            a[...] += r


def _ew(fn, rows, vecs=(), *, out_dtypes=(), n_acc=0, width=None, ncol=1, tr=256, name):
    rows = [r if isinstance(r, tuple) else (r, 0) for r in rows]
    R = rows[0][0].shape[0]
    C = width if width is not None else rows[0][0].shape[1]
    tr = _tile(R, tr)
    in_specs = [pl.BlockSpec((tr, C), functools.partial(lambda j, i, off: (i, off + j), off=off)) for _, off in rows]
    in_specs += [pl.BlockSpec((1, C), lambda j, i: (0, j)) for _ in vecs]
    out_shape = [jax.ShapeDtypeStruct((R, ncol * C), dt) for dt in out_dtypes]
    out_specs = [pl.BlockSpec((tr, C), lambda j, i: (i, j)) for _ in out_dtypes]
    out_shape += [jax.ShapeDtypeStruct((1, ncol * C), F32)] * n_acc
    out_specs += [pl.BlockSpec((1, C), lambda j, i: (0, j))] * n_acc
    res = _pcall(_ew_body, name=name, grid=(ncol, R // tr), in_specs=in_specs, out_specs=tuple(out_specs),
                 out_shape=tuple(out_shape), sem=("parallel", "arbitrary" if n_acc else "parallel"),
                 fn=fn, n_in=len(rows) + len(vecs), n_out=len(out_dtypes), n_acc=n_acc)(
        *[a for a, _ in rows], *vecs)
    return res


def _rms_fwd(x, g, *, name):
    def fn(x, g):
        r = lax.rsqrt(jnp.mean(x * x, axis=1, keepdims=True) + RMS_EPS)
        return (x * r) * g

    return _ew(fn, [x], [g], out_dtypes=[BF16], name=name)[0]


def _rms_bwd(dh, x, g, dres, *, name):
    def fn(dh, x, dres, g):
        r = lax.rsqrt(jnp.mean(x * x, axis=1, keepdims=True) + RMS_EPS)
        y = x * r
        dy = dh * g
        dx = dres + r * (dy - y * jnp.mean(dy * y, axis=1, keepdims=True))
        return dx, dx, jnp.sum(dh * y, axis=0, keepdims=True)

    dx, dx16, dg = _ew(fn, [dh, x, dres], [g], out_dtypes=[F32, BF16], n_acc=1, name=name)
    return (dx, dx16), dg


def _loss_and_grad(x, g, target, *, name):
    D = x.shape[1]

    def fn(x, t, g):
        r = lax.rsqrt(jnp.mean(x * x, axis=1, keepdims=True) + RMS_EPS)
        y = x * r
        diff = y * g - t
        dh = diff * (1.0 / D)
        dy = dh * g
        dx = r * (dy - y * jnp.mean(dy * y, axis=1, keepdims=True))
        loss = jnp.sum(jnp.sum(diff * diff, axis=1, keepdims=True), axis=0, keepdims=True) * (0.5 / D)
        return dx, dx, jnp.sum(dh * y, axis=0, keepdims=True), jnp.broadcast_to(loss, (1, D))

    dx, dx16, dg, loss = _ew(fn, [x, target], [g], out_dtypes=[F32, BF16], n_acc=2, name=name)
    return (dx, dx16), dg, loss


def _gelu(y):
    c = math.sqrt(2.0 / math.pi)
    return 0.5 * y * (1.0 + jnp.tanh(c * (y + 0.044715 * (y * y * y))))


def _gelu_grad(y):
    c = math.sqrt(2.0 / math.pi)
    t = jnp.tanh(c * (y + 0.044715 * (y * y * y)))
    return 0.5 * (1.0 + t) + 0.5 * y * (1.0 - t * t) * (c * (1.0 + 3 * 0.044715 * (y * y)))


def _adamw(w, g, m, v, *, name):
    def fn(w, g, m, v):
        m2 = ADAM_B1 * m + (1.0 - ADAM_B1) * g
        v2 = ADAM_B2 * v + (1.0 - ADAM_B2) * (g * g)
        m_hat = m2 / (1.0 - ADAM_B1 ** ADAM_STEP)
        v_hat = v2 / (1.0 - ADAM_B2 ** ADAM_STEP)
        delta = -ADAM_LR * (m_hat / (jnp.sqrt(v_hat) + ADAM_EPS) + ADAM_WD * w)
        return delta, m2, v2

    return _ew(fn, [w, g, m, v], out_dtypes=[F32, F32, F32], name=name)


def _window(kind, blk, QB, rows):
    if kind == "dil":
        return pl.multiple_of(blk * QB, QB), 0
    kr = min(NA_ROWS_MAX, rows)
    rs = jnp.clip(blk - kr // 2, 0, rows - kr)
    return pl.multiple_of(rs * GRID_W, GRID_W), blk - rs


def _scores(q, kw, bias, kind, start, QB, W, L_valid, scale):
    s = lax.dot_general(q, kw, NT, preferred_element_type=F32) * scale + bias
    if kind == "dil":
        kp = start + lax.broadcasted_iota(jnp.int32, (QB, W), 1)
        s = jnp.where((kp >= DIL_HALF) & (kp < DIL_HALF + L_valid), s, NEG_INF)
    return s


def _wattn_fwd_body(q_ref, k_ref, v_ref, b_ref, *outs, QB, SUB, W, kind, L_valid, rows, scale):
    n = pl.program_id(1)
    wins = [_window(kind, n * SUB + i, QB, rows) for i in range(SUB)]
    sls = [slice(i * QB, (i + 1) * QB) for i in range(SUB)]
    ss = [_scores(q_ref[sl, :], k_ref[pl.ds(start, W), :], b_ref[pat], kind, start, QB, W, L_valid, scale)
          for sl, (start, pat) in zip(sls, wins)]
    ms = [jnp.max(s, axis=1, keepdims=True) for s in ss]
    es = [jnp.exp(s - m) for s, m in zip(ss, ms)]
    ls = [jnp.sum(e, axis=1, keepdims=True) for e in es]
    for sl, (start, _), m, e, l in zip(sls, wins, ms, es, ls):
        vw = v_ref[pl.ds(start, W), :]
        if kind == "na":
            p = (e * (1.0 / l)).astype(BF16)
            o = lax.dot_general(p, vw, NN, preferred_element_type=F32)
            outs[0][sl, :] = o.astype(outs[0].dtype)
            outs[1][sl, :] = jnp.broadcast_to(m + jnp.log(l), (QB, LANES))
        else:
            outs[0][sl, :] = lax.dot_general(e.astype(BF16), vw, NN, preferred_element_type=F32)
            outs[1][sl, :] = jnp.broadcast_to(m, (QB, LANES))
            outs[2][sl, :] = jnp.broadcast_to(l, (QB, LANES))


def _wattn_bwd_body(q_ref, do_ref, lse_ref, dl_ref, k_ref, v_ref, b_ref, dq_ref, dk_ref, dv_ref, db_ref, *,
                    QB, SUB, W, kind, L_valid, rows, scale, dgroup):
    cb = pl.program_id(0)
    n = pl.program_id(1)

    @pl.when(n == 0)
    def _():
        dk_ref[...] = jnp.zeros_like(dk_ref)
        dv_ref[...] = jnp.zeros_like(dv_ref)

    @pl.when((n == 0) & (cb % dgroup == 0))
    def _():
        db_ref[...] = jnp.zeros_like(db_ref)

    wins = [_window(kind, n * SUB + i, QB, rows) for i in range(SUB)]
    sls = [slice(i * QB, (i + 1) * QB) for i in range(SUB)]
    ss = [_scores(q_ref[sl, :], k_ref[pl.ds(start, W), :], b_ref[pat], kind, start, QB, W, L_valid, scale)
          for sl, (start, pat) in zip(sls, wins)]
    dos = [do_ref[sl, :].astype(BF16) for sl in sls]
    dps = [lax.dot_general(do, v_ref[pl.ds(start, W), :], NT, preferred_element_type=F32)
           for do, (start, _) in zip(dos, wins)]
    ps = [jnp.exp(s - lse_ref[sl, :][:, :1]) for s, sl in zip(ss, sls)]
    if kind == "na":
        dls = [jnp.sum(do_ref[sl, :].astype(F32) * dl_ref[sl, :].astype(F32), axis=1, keepdims=True) for sl in sls]
    else:
        dls = [dl_ref[sl, :][:, :1] for sl in sls]
    dss = [p * (dp - dl) for p, dp, dl in zip(ps, dps, dls)]
    for sl, (start, pat), p, ds, do in zip(sls, wins, ps, dss, dos):
        q = q_ref[sl, :]
        db_ref[pat] += ds
        dsb = ds.astype(BF16)
        dq_ref[sl, :] = (lax.dot_general(dsb, k_ref[pl.ds(start, W), :], NN, preferred_element_type=F32)
                         * scale).astype(dq_ref.dtype)
        dk_ref[pl.ds(start, W), :] += lax.dot_general(dsb, q, TN, preferred_element_type=F32) * scale
        dv_ref[pl.ds(start, W), :] += lax.dot_general(p.astype(BF16), do, TN, preferred_element_type=F32)


def _wattn_geometry(kind, LQ, dil):
    if kind == "dil":
        QB, W, rows = A_QBLOCK, A_QBLOCK + 2 * DIL_HALF, 0
    else:
        rows = LQ // GRID_W
        QB, W = GRID_W, min(NA_ROWS_MAX, rows) * GRID_W
    blocks = LQ // QB
    SUB = 8 if blocks % 8 == 0 else 4 if blocks % 4 == 0 else 1
    return QB, W, rows, SUB


def _wattn_fwd(q, k, v, bias, *, kind, H, dil, qoff, koff, voff, name):
    LQ, LK = q.shape[0], k.shape[0]
    QB, W, rows, SUB = _wattn_geometry(kind, LQ, dil)
    ncb = H * dil
    NP = bias.shape[1]

    def col(cb, off):
        return off + (cb % dil) * H + cb // dil

    in_specs = [pl.BlockSpec((QB * SUB, LANES), lambda cb, n: (n, col(cb, qoff))),
                pl.BlockSpec((LK, LANES), lambda cb, n: (0, col(cb, koff))),
                pl.BlockSpec((LK, LANES), lambda cb, n: (0, col(cb, voff))),
                pl.BlockSpec((None, NP, QB, W), lambda cb, n: (cb // dil, 0, 0, 0))]
    o_spec = pl.BlockSpec((QB * SUB, LANES), lambda cb, n: (n, col(cb, 0)))
    shape = (LQ, ncb * LANES)
    if kind == "na":
        out_shape = (jax.ShapeDtypeStruct(shape, BF16), jax.ShapeDtypeStruct(shape, F32))
    else:
        out_shape = (jax.ShapeDtypeStruct(shape, F32),) * 3
    return _pcall(_wattn_fwd_body, name=name, grid=(ncb, LQ // (QB * SUB)), in_specs=in_specs,
                  out_specs=(o_spec,) * len(out_shape), out_shape=out_shape, sem=("parallel", "parallel"),
                  QB=QB, SUB=SUB, W=W, kind=kind, L_valid=LQ, rows=rows, scale=1.0 / math.sqrt(HEAD_DIM))(
        q, k, v, bias)


def _wattn_bwd(q, do, lse, delta, k, v, bias, *, kind, H, dil, qoff, koff, voff, dq_dtype, name):
    LQ, LK = q.shape[0], k.shape[0]
    QB, W, rows, SUB = _wattn_geometry(kind, LQ, dil)
    ncb = H * dil
    NP = bias.shape[1]

    def col(cb, off):
        return off + (cb % dil) * H + cb // dil

    q_spec = lambda off: pl.BlockSpec((QB * SUB, LANES), lambda cb, n: (n, col(cb, off)))
    kv_spec = lambda off: pl.BlockSpec((LK, LANES), lambda cb, n: (0, col(cb, off)))
    b_spec = pl.BlockSpec((None, NP, QB, W), lambda cb, n: (cb // dil, 0, 0, 0))
    in_specs = [q_spec(qoff), q_spec(0), q_spec(0), q_spec(0), kv_spec(koff), kv_spec(voff), b_spec]
    out_shape = (jax.ShapeDtypeStruct((LQ, ncb * LANES), dq_dtype), jax.ShapeDtypeStruct((LK, ncb * LANES), F32),
                 jax.ShapeDtypeStruct((LK, ncb * LANES), F32), jax.ShapeDtypeStruct(bias.shape, F32))
    out_specs = (q_spec(0), kv_spec(0), kv_spec(0), b_spec)
    return _pcall(_wattn_bwd_body, name=name, grid=(ncb, LQ // (QB * SUB)), in_specs=in_specs, out_specs=out_specs,
                  out_shape=out_shape, sem=("arbitrary", "arbitrary"),
                  QB=QB, SUB=SUB, W=W, kind=kind, L_valid=LQ, rows=rows, scale=1.0 / math.sqrt(HEAD_DIM),
                  dgroup=dil)(q, do, lse, delta, k, v, bias)


def _attn_delta(do, do_off, o, *, name):
    def fn(do, o):
        return jnp.broadcast_to(jnp.sum(do * o.astype(F32), axis=1, keepdims=True), do.shape), do

    return _ew(fn, [(do, do_off), o], out_dtypes=[F32, BF16], width=LANES, ncol=o.shape[1] // LANES, tr=512,
               name=name)


SUBLANES = 8


def _bf(ref):
    return ref[...].astype(BF16)


def _cmul(ar, ai, br, bi):
    return ar * br - ai * bi, ar * bi + ai * br


def _powers(ar, ai, n):
    pr, pi = jnp.ones_like(ar), jnp.zeros_like(ai)
    mr, mi = ar, ai
    while pr.shape[0] < n + 1:
        qr, qi = _cmul(mr, mi, pr, pi)
        pr, pi = jnp.concatenate([pr, qr]), jnp.concatenate([pi, qi])
        mr, mi = _cmul(mr, mi, mr, mi)
    return pr[:n + 1], pi[:n + 1]


def _seg_tables(a_re, a_im, SEG, desc):
    pr, pi = _powers(lax.stop_gradient(a_re), lax.stop_gradient(a_im), SEG)
    nat = (lambda t: t[::-1]) if desc else (lambda t: t)
    return nat(pr[1:]), nat(pi[1:]), pr[:-1], pi[:-1], pr[SEG:], pi[SEG:]


def _interleave(t, inverse=False):
    S, C = t.shape
    TC = _tile(S, 512)
    a, b = (TC // SUBLANES, SUBLANES) if inverse else (SUBLANES, TC // SUBLANES)
    return t.reshape(S // TC, a, b, C).transpose(0, 2, 1, 3).reshape(S, C)


def _tile_carries(desc, end_r, end_i, a64r, a64i, car_r, car_i, cst_r, cst_i):
    cr, ci = car_r[0:1, :], car_i[0:1, :]
    for i in (reversed(range(SUBLANES)) if desc else range(SUBLANES)):
        cst_r[i:i + 1, :] = cr
        cst_i[i:i + 1, :] = ci
        pr, pi = _cmul(a64r, a64i, cr, ci)
        cr, ci = pr + end_r[i:i + 1, :], pi + end_i[i:i + 1, :]
    car_r[0:1, :] = cr
    car_i[0:1, :] = ci


def _tile_correct(SEG, xr_ref, xi_ref, pr_ref, pi_ref, cst_r, cst_i):
    cr, ci = cst_r[...], cst_i[...]
    for t in range(SEG):
        rows = pl.ds(SUBLANES * t, SUBLANES)
        dr, di = _cmul(pr_ref[t:t + 1, :], pi_ref[t:t + 1, :], cr, ci)
        xr_ref[rows, :] += dr
        xi_ref[rows, :] += di


def _s5il_fwd_body(u_ref, br_ref, bi_ref, cr_ref, cin_ref, are_ref, aim_ref, pr_ref, pi_ref, a64r_ref, a64i_ref,
                   xre_ref, xim_ref, y_ref, bre_s, bim_s, car_r, car_i, cst_r, cst_i, end_r, end_i, *, TC, desc):
    SEG = TC // SUBLANES

    @pl.when(pl.program_id(1) == 0)
    def _():
        car_r[...] = jnp.zeros_like(car_r)
        car_i[...] = jnp.zeros_like(car_i)

    ub = _bf(u_ref)
    bre_s[...] = lax.dot_general(ub, _bf(br_ref), NN, preferred_element_type=F32)
    bim_s[...] = lax.dot_general(ub, _bf(bi_ref), NN, preferred_element_type=F32)
    ar, ai = are_ref[...], aim_ref[...]
    xr = xi = jnp.zeros((SUBLANES, STATE_PER_BLOCK), F32)
    for tau in range(SEG):
        rows = pl.ds(SUBLANES * (SEG - 1 - tau if desc else tau), SUBLANES)
        pr, pi = _cmul(ar, ai, xr, xi)
        xr = pr + bre_s[rows, :]
        xi = pi + bim_s[rows, :]
        xre_ref[rows, :] = xr
        xim_ref[rows, :] = xi
    end_r[...] = xr
    end_i[...] = xi
    _tile_carries(desc, end_r, end_i, a64r_ref[...], a64i_ref[...], car_r, car_i, cst_r, cst_i)
    _tile_correct(SEG, xre_ref, xim_ref, pr_ref, pi_ref, cst_r, cst_i)
    y_ref[...] = (lax.dot_general(_bf(xre_ref), _bf(cr_ref), NN, preferred_element_type=F32)
                  + lax.dot_general(_bf(xim_ref), _bf(cin_ref), NN, preferred_element_type=F32))


def _s5il_specs(S, BW, desc):
    NB = BW // LANES
    TC = _tile(S, 512)
    nT = S // TC
    tmap = (lambda l, t: (nT - 1 - t, l)) if desc else (lambda l, t: (t, l))
    narrow = pl.BlockSpec((TC, LANES), tmap)
    wide = pl.BlockSpec((TC, STATE_PER_BLOCK), tmap)
    vec = pl.BlockSpec((1, STATE_PER_BLOCK), lambda l, t: (0, l))
    tab = pl.BlockSpec((TC // SUBLANES, STATE_PER_BLOCK), lambda l, t: (0, l))
    w_in = pl.BlockSpec((None, LANES, STATE_PER_BLOCK), lambda l, t: (l, 0, 0))
    w_out = pl.BlockSpec((None, STATE_PER_BLOCK, LANES), lambda l, t: (l, 0, 0))
    scratch = [pltpu.VMEM((TC, STATE_PER_BLOCK), F32)] * 2 + [pltpu.VMEM((SUBLANES, STATE_PER_BLOCK), F32)] * 6
    return NB, TC, nT, narrow, wide, vec, tab, w_in, w_out, scratch


def _s5_scan_fwd(u, mat, *, reverse, name):
    a_re, a_im, b_r, b_i, c_r, c_in = mat
    S, BW = u.shape
    NB, TC, nT, narrow, wide, vec, tab, w_in, w_out, scratch = _s5il_specs(S, BW, reverse)
    p_re, p_im, _, _, a64r, a64i = _seg_tables(a_re, a_im, TC // SUBLANES, reverse)
    xs = jax.ShapeDtypeStruct((S, NB * STATE_PER_BLOCK), F32)
    return _pcall(_s5il_fwd_body, name=name, grid=(NB, nT),
                  in_specs=[narrow, w_in, w_in, w_out, w_out, vec, vec, tab, tab, vec, vec],
                  out_specs=(wide, wide, narrow), out_shape=(xs, xs, jax.ShapeDtypeStruct((S, BW), F32)),
                  scratch=scratch, sem=("parallel", "arbitrary"), TC=TC, desc=reverse)(
        u, b_r, b_i, c_r, c_in, a_re, a_im, p_re, p_im, a64r, a64i)


def _s5il_bwd_body(dy_ref, u_ref, xre_ref, xim_ref, br_ref, bi_ref, cr_ref, cin_ref, are_ref, aim_ref,
                   pr_ref, pi_ref, qr_ref, qi_ref, a64r_ref, a64i_ref,
                   du_ref, dar_ref, dai_ref, dbr_ref, dbi_ref, dcr_ref, dcin_ref,
                   hre_s, him_s, car_r, car_i, cst_r, cst_i, end_r, end_i, *, TC, desc):
    SEG = TC // SUBLANES

    @pl.when(pl.program_id(1) == 0)
    def _():
        for r in (car_r, car_i, dar_ref, dai_ref, dbr_ref, dbi_ref, dcr_ref, dcin_ref):
            r[...] = jnp.zeros_like(r)

    dyb = _bf(dy_ref)
    hre_s[...] = lax.dot_general(dyb, _bf(cr_ref), NT, preferred_element_type=F32)
    him_s[...] = lax.dot_general(dyb, _bf(cin_ref), NT, preferred_element_type=F32)
    dcr_ref[...] += lax.dot_general(_bf(xre_ref), dyb, TN, preferred_element_type=F32)
    dcin_ref[...] += lax.dot_general(_bf(xim_ref), dyb, TN, preferred_element_type=F32)
    ar, ai = are_ref[...], aim_ref[...]
    hr = hi = dr = di = er = ei = jnp.zeros((SUBLANES, STATE_PER_BLOCK), F32)
    for tau in range(SEG):
        rows = pl.ds(SUBLANES * (SEG - 1 - tau if desc else tau), SUBLANES)
        xr, xi = xre_ref[rows, :], xim_ref[rows, :]
        qr, qi = qr_ref[tau:tau + 1, :], qi_ref[tau:tau + 1, :]
        dr = dr + (hr * xr + hi * xi)
        di = di + (hi * xr - hr * xi)
        er = er + (qr * xr + qi * xi)
        ei = ei + (qi * xr - qr * xi)
        pr, pi = _cmul(ar, ai, hr, hi)
        hr = pr + hre_s[rows, :]
        hi = pi + him_s[rows, :]
        hre_s[rows, :] = hr
        him_s[rows, :] = hi
    end_r[...] = hr
    end_i[...] = hi
    _tile_carries(desc, end_r, end_i, a64r_ref[...], a64i_ref[...], car_r, car_i, cst_r, cst_i)
    fr, fi = _cmul(cst_r[...], cst_i[...], er, ei)
    dar_ref[...] += jnp.sum(dr + fr, axis=0, keepdims=True)
    dai_ref[...] += jnp.sum(di + fi, axis=0, keepdims=True)
    _tile_correct(SEG, hre_s, him_s, pr_ref, pi_ref, cst_r, cst_i)
    hrb, hib, ub = _bf(hre_s), _bf(him_s), _bf(u_ref)
    du_ref[...] = (lax.dot_general(hrb, _bf(br_ref), NT, preferred_element_type=F32)
                   + lax.dot_general(hib, _bf(bi_ref), NT, preferred_element_type=F32))
    dbr_ref[...] += lax.dot_general(ub, hrb, TN, preferred_element_type=F32)
    dbi_ref[...] += lax.dot_general(ub, hib, TN, preferred_element_type=F32)


def _s5_scan_bwd(dy, u, x_re, x_im, mat, *, reverse, name):
    a_re, a_im, b_r, b_i, c_r, c_in = mat
    S, BW = u.shape
    desc = not reverse
    NB, TC, nT, narrow, wide, vec, tab, w_in, w_out, scratch = _s5il_specs(S, BW, desc)
    p_re, p_im, q_re, q_im, a64r, a64i = _seg_tables(a_re, -a_im, TC // SUBLANES, desc)
    shapes = (jax.ShapeDtypeStruct((S, BW), F32),) + tuple(jax.ShapeDtypeStruct(m.shape, F32) for m in mat)
    res = _pcall(_s5il_bwd_body, name=name, grid=(NB, nT),
                 in_specs=[narrow, narrow, wide, wide, w_in, w_in, w_out, w_out, vec, vec, tab, tab, tab, tab, vec, vec],
                 out_specs=(narrow, vec, vec, w_in, w_in, w_out, w_out), out_shape=shapes,
                 scratch=scratch, sem=("parallel", "arbitrary"), TC=TC, desc=desc)(
        dy, u, x_re, x_im, b_r, b_i, c_r, c_in, a_re, -a_im, p_re, p_im, q_re, q_im, a64r, a64i)
    return res[0], tuple(res[1:])


def _t5_bucket(rel):
    half = T5_BUCKETS // 2
    max_exact = half // 2
    n = jnp.abs(rel)
    nf = jnp.maximum(n, 1).astype(F32)
    large = max_exact + (jnp.log(nf / max_exact) / math.log(T5_MAX_DISTANCE / max_exact)
                         * (half - max_exact)).astype(jnp.int32)
    large = jnp.minimum(large, half - 1)
    return jnp.where(rel > 0, half, 0) + jnp.where(n < max_exact, n, large)


def _dil_bias(t5_bias, dil):
    W = A_QBLOCK + 2 * DIL_HALF
    off = jnp.arange(W)[None, :] - DIL_HALF - jnp.arange(A_QBLOCK)[:, None]
    pick = (_t5_bucket(off * dil)[..., None] == jnp.arange(T5_BUCKETS)).astype(F32)
    b = jnp.einsum('qkb,bh->hqk', pick, t5_bias.astype(F32), precision=lax.Precision.HIGHEST)
    return jnp.where(jnp.abs(off) <= DIL_HALF, b, NEG_INF)[:, None]


def _na_bias(rpb, rows):
    kr = min(NA_ROWS_MAX, rows)
    ro = (jnp.arange(kr)[None, :] - jnp.arange(kr)[:, None]) + NA_ROWS_MAX - 1
    c = jnp.arange(GRID_W)
    col_start = jnp.clip(c - NA_COLS // 2, 0, GRID_W - NA_COLS)
    col_ok = (c[None, :] >= col_start[:, None]) & (c[None, :] < col_start[:, None] + NA_COLS)
    co = jnp.clip(c[None, :] - c[:, None] + NA_COLS - 1, 0, 2 * NA_COLS - 2)
    pick_r = (ro[..., None] == jnp.arange(2 * NA_ROWS_MAX - 1)).astype(F32)
    pick_c = (co[..., None] == jnp.arange(2 * NA_COLS - 1)).astype(F32)
    b = jnp.einsum('hrqk,pjr->hpqjk',
                   jnp.einsum('hrc,qkc->hrqk', rpb.astype(F32), pick_c, precision=lax.Precision.HIGHEST),
                   pick_r, precision=lax.Precision.HIGHEST)
    b = jnp.where(col_ok[None, None, :, None, :], b, NEG_INF)
    return b.reshape(rpb.shape[0], kr, GRID_W, kr * GRID_W)


def _s5_mats(lam_re, lam_im, log_step, b_re, b_im, c_re, c_im):
    G, P, C = b_re.shape
    NB = G // GROUPS_PER_BLOCK
    eye = jnp.eye(GROUPS_PER_BLOCK, dtype=F32)

    def bd_in(bb):
        t = bb.reshape(NB, GROUPS_PER_BLOCK, P, C).transpose(0, 1, 3, 2)
        return jnp.einsum('jgcp,gh->jgchp', t, eye).reshape(NB, GROUPS_PER_BLOCK * C, GROUPS_PER_BLOCK * P)

    def bd_out(cc):
        t = cc.reshape(NB, GROUPS_PER_BLOCK, C, P).transpose(0, 1, 3, 2)
        return jnp.einsum('jgpc,gh->jgphc', t, eye).reshape(NB, GROUPS_PER_BLOCK * P, GROUPS_PER_BLOCK * C)

    out = []
    for d in range(2):
        step = jnp.exp(log_step[d].astype(F32))[:, None]
        lr = jnp.minimum(lam_re[d].astype(F32), -1e-4)
        li = lam_im[d].astype(F32)
        mag = jnp.exp(lr * step)
        ab_re = mag * jnp.cos(li * step)
        ab_im = mag * jnp.sin(li * step)
        den = lr * lr + li * li
        zr = ((ab_re - 1.0) * lr + ab_im * li) / den
        zi = (ab_im * lr - (ab_re - 1.0) * li) / den
        bb_re = zr[..., None] * b_re - zi[..., None] * b_im
        bb_im = zr[..., None] * b_im + zi[..., None] * b_re
        out.append((ab_re.reshape(1, G * P), ab_im.reshape(1, G * P), bd_in(bb_re), bd_in(bb_im),
                    bd_out(c_re[d].astype(F32)), bd_out(-c_im[d].astype(F32))))
    return tuple(out)


def _sigmoid(z):
    return 1.0 / (1.0 + jnp.exp(-z))


def _strided(t, dil, pad):
    S, C = t.shape
    t = t.reshape(S // dil, dil * C)
    return jnp.pad(t, ((DIL_HALF, DIL_HALF), (0, 0))) if pad else t


def _dilated_fwd(q, k, v, t5_bias):
    S, AW = q.shape
    H = AW // HEAD_DIM
    parts = []
    for _, dil in DILATED_BRANCHES:
        num, m, l = _wattn_fwd(_strided(q, dil, False), _strided(k, dil, True), _strided(v, dil, True),
                               _dil_bias(t5_bias, dil), kind="dil", H=H, dil=dil, qoff=0, koff=0, voff=0,
                               name=f"dilated{dil}_fwd")
        parts += [num.reshape(S, AW), m.reshape(S, AW), l.reshape(S, AW)]

    def merge(n1, m1, l1, n2, m2, l2, n3, m3, l3):
        mx = jnp.maximum(jnp.maximum(m1, m2), m3)
        w1, w2, w3 = jnp.exp(m1 - mx), jnp.exp(m2 - mx), jnp.exp(m3 - mx)
        den = w1 * l1 + w2 * l2 + w3 * l3
        o = (w1 * n1 + w2 * n2 + w3 * n3) / den
        return o, o, mx + jnp.log(den)

    return _ew(merge, parts, out_dtypes=[F32, BF16, F32], name="dilated_merge")


def _dilated_bwd(q, k, v, t5_bias, do, lse, delta):
    S, AW = q.shape
    H = AW // HEAD_DIM
    dqs, dks, dvs = [], [], []
    dt5 = jnp.zeros(t5_bias.shape, F32)
    for _, dil in DILATED_BRANCHES:
        bias, bias_vjp = jax.vjp(functools.partial(_dil_bias, dil=dil), t5_bias)
        dq, dk, dv, db = _wattn_bwd(_strided(q, dil, False), _strided(do, dil, False), _strided(lse, dil, False),
                                    _strided(delta, dil, False), _strided(k, dil, True), _strided(v, dil, True),
                                    bias, kind="dil", H=H, dil=dil, qoff=0, koff=0, voff=0, dq_dtype=F32,
                                    name=f"dilated{dil}_bwd")
        dqs.append(dq.reshape(S, AW))
        dks.append(dk[DIL_HALF:-DIL_HALF].reshape(S, AW))
        dvs.append(dv[DIL_HALF:-DIL_HALF].reshape(S, AW))
        dt5 = dt5 + bias_vjp(db)[0]
    add3 = lambda a, b, c: a + b + c
    return (_ew(add3, dqs, out_dtypes=[BF16], name="dilated_dq_sum")[0],
            _ew(add3, dks, out_dtypes=[BF16], name="dilated_dk_sum")[0],
            _ew(add3, dvs, out_dtypes=[BF16], name="dilated_dv_sum")[0], dt5)


def _s5_fwd(u, mats, d_skip, w_glu, j):
    u = _interleave(u)
    xs, ys = [], []
    for d in range(2):
        x_re, x_im, y_d = _s5_scan_fwd(u, mats[d], reverse=(d == 1), name=f"s5_scan_fwd{d}")
        xs += [x_re, x_im]
        ys.append(y_d)

    def act(y0, y1, u, dsk):
        y = (y0 + y1) + dsk * u
        return y, _gelu(y)

    y, yg = _ew(act, ys + [u], [d_skip], out_dtypes=[F32, F32], name="s5_gelu")
    z = _mm_nn(yg, w_glu, j, "row", name="s5_glu_fwd")
    ob = _ew(lambda yg, z: yg * _sigmoid(z), [yg, z], out_dtypes=[BF16], name="s5_gate")[0]
    return _interleave(ob, inverse=True), (xs, y, yg, z, u)


def _s5_bwd(dmerged, ob_off, mats, d_skip, w_glu, j, saved):
    xs, y, yg, z, u = saved
    dob = _interleave(dmerged[:, ob_off * LANES:])

    def gate_bwd(dob, yg, z):
        sg = _sigmoid(z)
        return dob * yg * (sg * (1.0 - sg)), dob * sg

    dz, dyg1 = _ew(gate_bwd, [dob, yg, z], out_dtypes=[BF16, F32], name="s5_gate_bwd")
    dw_glu = _mm_tn(yg, dz, w_glu.shape[0], "row", name="s5_glu_dw")
    dyg2 = _mm_nt(dz, w_glu, j, "row", name="s5_glu_dx")

    def act_bwd(d1, d2, y, u):
        dy = (d1 + d2) * _gelu_grad(y)
        return dy, jnp.sum(dy * u, axis=0, keepdims=True)

    dy, dd = _ew(act_bwd, [dyg1, dyg2, y, u], out_dtypes=[F32], n_acc=1, name="s5_gelu_bwd")
    dmats, dus = [], []
    for d in range(2):
        du_d, dmat = _s5_scan_bwd(dy, u, xs[2 * d], xs[2 * d + 1], mats[d], reverse=(d == 1),
                                  name=f"s5_scan_bwd{d}")
        dus.append(du_d)
        dmats.append(dmat)
    du = _ew(lambda dy, d0, d1, dsk: dy * dsk + (d0 + d1), [dy] + dus, [d_skip], out_dtypes=[BF16],
             name="s5_du_sum")[0]
    return _interleave(du, inverse=True), tuple(dmats), dd, dw_glu


def _ab_fwd(x, j, P, W, first=False):
    t5 = P["t5_bias"]
    AW = t5.shape[1] * HEAD_DIM
    hn = _rms_fwd(x, P["norm_mix"][2 * j][None], name="rms_fwd")
    if first:
        proj, W = _mm_nn(hn, W["ab_w_in"], j, "col", name="ab_in_fwd", host=(W, [("ab_w_out", j), ("s5_w_glu", j)]))
    else:
        proj = _mm_nn(hn, W["ab_w_in"], j, "col", name="ab_in_fwd")
    q, k, v = (proj[:, i * AW:(i + 1) * AW].astype(BF16) for i in range(3))
    u = proj[:, 3 * AW:]
    oa32, oa16, lse = _dilated_fwd(q, k, v, t5)
    mats = _s5_mats(*(P[n][j] for n in _S5_PARAMS))
    ob, s5_saved = _s5_fwd(u, mats, P["s5_d"][j][None], W["s5_w_glu"], j)
    merged = jnp.concatenate([oa16, ob], axis=1)
    if first:
        x1, W = _mm_nn(merged, W["ab_w_out"], j, "row", mode="res", res=x, name="ab_out_fwd",
                       host=(W, [("mlp_w1", 0), ("mlp_w2", 0)]))
    else:
        x1 = _mm_nn(merged, W["ab_w_out"], j, "row", mode="res", res=x, name="ab_out_fwd")
    return x1, (x, hn, q, k, v, u, oa32, lse, merged, s5_saved), W


def _ab_bwd(dx1, j, P, W, saved):
    x, hn, q, k, v, u, oa32, lse, merged, s5_saved = saved
    t5 = P["t5_bias"]
    AW = t5.shape[1] * HEAD_DIM
    J = W["ab_w_in"].shape[0]
    dx1, dx1_16 = dx1
    dmerged = _mm_nt(dx1_16, W["ab_w_out"], j, "row", name="ab_out_dx")
    dw_out = _mm_tn(merged, dx1_16, J, "row", name="ab_out_dw")
    delta, do16 = _attn_delta(dmerged, 0, oa32, name="dilated_delta")
    dq, dk, dv, dt5 = _dilated_bwd(q, k, v, t5, do16, lse, delta)
    s5_params = tuple(P[n][j] for n in _S5_PARAMS)
    mats, mats_vjp = jax.vjp(_s5_mats, *s5_params)
    du, dmats, dd, dw_glu = _s5_bwd(dmerged, AW // LANES, mats, P["s5_d"][j][None], W["s5_w_glu"], j, s5_saved)
    ds5 = mats_vjp(dmats)
    dproj = jnp.concatenate([dq, dk, dv, du], axis=1)
    dw_in = _mm_tn(hn, dproj, J, "col", name="ab_in_dw")
    dhn = _mm_nt(dproj, W["ab_w_in"], j, "col", name="ab_in_dx")
    dx, dg = _rms_bwd(dhn, x, P["norm_mix"][2 * j][None], dx1, name="rms_bwd")
    small = dict(zip(_S5_PARAMS, ds5), s5_d=dd[0], t5_bias=dt5)
    return dx, dg[0], dict(ab_w_in=dw_in, ab_w_out=dw_out, s5_w_glu=dw_glu), small


def _c_fwd(x, j, P, W):
    H = P["c_rpb"].shape[1]
    hn = _rms_fwd(x, P["norm_mix"][2 * j + 1][None], name="rms_fwd")
    qkv = _mm_nn(hn, W["c_w_qkv"], j, "col", out_dtype=BF16, name="c_qkv_fwd")
    bias = _na_bias(P["c_rpb"][j], x.shape[0] // GRID_W)
    o, lse = _wattn_fwd(qkv, qkv, qkv, bias, kind="na", H=H, dil=1, qoff=0, koff=H, voff=2 * H, name="na_fwd")
    x1 = _mm_nn(o, W["c_w_out"], j, "row", mode="res", res=x, name="c_out_fwd")
    return x1, (x, hn, qkv, o, lse)


def _c_bwd(dx1, j, P, W, saved):
    x, hn, qkv, o, lse = saved
    H = P["c_rpb"].shape[1]
    J = W["c_w_qkv"].shape[0]
    dx1, dx1_16 = dx1
    do = _mm_nt(dx1_16, W["c_w_out"], j, "row", name="c_out_dx")
    dw_out = _mm_tn(o, dx1_16, J, "row", name="c_out_dw")
    bias, bias_vjp = jax.vjp(functools.partial(_na_bias, rows=x.shape[0] // GRID_W), P["c_rpb"][j])
    dq, dk, dv, db = _wattn_bwd(qkv, do, lse, o, qkv, qkv, bias, kind="na", H=H, dil=1, qoff=0, koff=H,
                                voff=2 * H, dq_dtype=BF16, name="na_bwd")
    dqkv = jnp.concatenate([dq, dk.astype(BF16), dv.astype(BF16)], axis=1)
    dw_qkv = _mm_tn(hn, dqkv, J, "col", name="c_qkv_dw")
    dhn = _mm_nt(dqkv, W["c_w_qkv"], j, "col", name="c_qkv_dx")
    dx, dg = _rms_bwd(dhn, x, P["norm_mix"][2 * j + 1][None], dx1, name="rms_bwd")
    return dx, dg[0], dict(c_w_qkv=dw_qkv, c_w_out=dw_out), dict(c_rpb=bias_vjp(db)[0])


def _layer_pieces(i):
    j = i // 2
    mixer = [("ab_w_in", j), ("ab_w_out", j), ("s5_w_glu", j)] if i % 2 == 0 else [("c_w_qkv", j), ("c_w_out", j)]
    return mixer, [("mlp_w1", i)], [("mlp_w2", i)]


def _mlp_fwd(x, i, P, W, depth):
    hn = _rms_fwd(x, P["norm_mlp"][i][None], name="rms_fwd")
    if i + 1 < depth:
        mixer, w1, w2 = _layer_pieces(i + 1)
        (a, hdn), W = _mm_nn(hn, W["mlp_w1"], i, "col", mode="relu2", name="mlp_w1_fwd", host=(W, mixer + w2))
        x2, W = _mm_nn(hdn, W["mlp_w2"], i, "row", mode="res", res=x, name="mlp_w2_fwd", host=(W, w1))
    else:
        a, hdn = _mm_nn(hn, W["mlp_w1"], i, "col", mode="relu2", name="mlp_w1_fwd")
        x2 = _mm_nn(hdn, W["mlp_w2"], i, "row", mode="res", res=x, name="mlp_w2_fwd")
    return x2, (x, hn, a, hdn), W


def _mlp_bwd(dx2, i, P, W, saved):
    x, hn, a, hdn = saved
    J = W["mlp_w1"].shape[0]
    dx2, dx2_16 = dx2
    da = _mm_nt(dx2_16, W["mlp_w2"], i, "row", out_dtype=BF16, mode="dact", act=a, name="mlp_w2_dx")
    dw2 = _mm_tn(hdn, dx2_16, J, "row", name="mlp_w2_dw")
    dw1 = _mm_tn(hn, da, J, "col", name="mlp_w1_dw")
    dhn = _mm_nt(da, W["mlp_w1"], i, "col", name="mlp_w1_dx")
    dx, dg = _rms_bwd(dhn, x, P["norm_mlp"][i][None], dx2, name="rms_bwd")
    return dx, dg[0], dict(mlp_w1=dw1, mlp_w2=dw2)


_S5_PARAMS = ("s5_lam_re", "s5_lam_im", "s5_log_step", "s5_b_re", "s5_b_im", "s5_c_re", "s5_c_im")
_BIG = ("ab_w_in", "ab_w_out", "s5_w_glu", "c_w_qkv", "c_w_out", "mlp_w1", "mlp_w2")
_SMALL = ("t5_bias", "s5_lam_re", "s5_lam_im", "s5_log_step", "s5_b_re", "s5_b_im", "s5_c_re", "s5_c_im", "s5_d",
          "c_rpb", "norm_mix", "norm_mlp", "norm_final")
_WEIGHTS = ("t5_bias", "ab_w_in", "ab_w_out", "s5_lam_re", "s5_lam_im", "s5_log_step", "s5_b_re", "s5_b_im",
            "s5_c_re", "s5_c_im", "s5_d", "s5_w_glu", "c_w_qkv", "c_w_out", "c_rpb", "norm_mix", "norm_mlp",
            "mlp_w1", "mlp_w2", "norm_final")


def _local_grads(x, target, P, W):
    depth = P["norm_mix"].shape[0]
    saved = []
    h = x
    W = _gather_pieces(W, [("ab_w_in", 0)])
    for i in range(depth):
        if i % 2 == 0:
            h, s_mix, W = _ab_fwd(h, i // 2, P, W, first=(i == 0))
        else:
            h, s_mix = _c_fwd(h, i // 2, P, W)
        h, s_mlp, W = _mlp_fwd(h, i, P, W, depth)
        saved.append((s_mix, s_mlp))
    dh, dg_final, loss_cols = _loss_and_grad(h, P["norm_final"][None], target, name="loss_head")
    big = {n: [None] * P[n].shape[0] for n in _BIG}
    small = {n: jnp.zeros(P[n].shape, F32) for n in _SMALL}
    small["norm_final"] = dg_final[0]
    for i in reversed(range(depth)):
        s_mix, s_mlp = saved[i]
        j = i // 2
        dh, dg, dbig = _mlp_bwd(dh, i, P, W, s_mlp)
        small["norm_mlp"] = small["norm_mlp"].at[i].set(dg)
        for n, g in dbig.items():
            big[n][i] = g
        dh, dg, dbig, dsmall = (_ab_bwd if i % 2 == 0 else _c_bwd)(dh, j, P, W, s_mix)
        small["norm_mix"] = small["norm_mix"].at[i].set(dg)
        for n, g in dbig.items():
            big[n][j] = g
        for n, g in dsmall.items():
            if n == "t5_bias":
                small[n] = small[n] + g
            else:
                small[n] = small[n].at[j].set(g.reshape(P[n].shape[1:]))
    return loss_cols[0, 0], dh[0], big, small


def _place():
    x, y, c = lax.axis_index("x"), lax.axis_index("y"), lax.axis_index("c")
    return x, y, c, ((1 - x, y), (x, 1 - y), (1 - x, 1 - y))


def _comm_call(body, arrays, out_shape, sems, *, name, in_place=False, **static):
    hbm = pl.BlockSpec(memory_space=pltpu.HBM)
    return pl.pallas_call(
        functools.partial(body, n=len(arrays), **static), name=name, in_specs=[hbm] * len(arrays),
        out_specs=tuple([hbm] * len(out_shape)), out_shape=tuple(out_shape),
        input_output_aliases={t: t for t in range(len(arrays))} if in_place else {},
        scratch_shapes=[pltpu.SemaphoreType.DMA((k,)) for k in sems])(*arrays)


def _cast_body(s_ref, w_ref, o_ref):
    o_ref[...] = w_ref[...].astype(o_ref.dtype)


def _cast_weights(weights, place):
    bufs = []
    for w in weights:
        L, Kd, Nd = w.shape
        tr = _tile(Kd, 256)
        bufs.append(_sliced_call(
            _cast_body, place, [w], [lambda l, i, p: (l, i, 0)], [(None, tr, Nd)], (None, None, tr, Nd),
            lambda l, i, p: (p[1], l, i, 0), jax.ShapeDtypeStruct((4, L, Kd, Nd), BF16), (L, Kd // tr),
            name="weights_cast"))
    return bufs


def _pair_exchange_body(*refs, n):
    ins, outs = refs[:n], refs[n:2 * n]
    send_sems, recv_sems = refs[2 * n:]
    x, y, c, _ = _place()
    cps = []
    for t in range(n):
        h = outs[t].shape[0]
        cps.append(pltpu.make_async_remote_copy(
            src_ref=ins[t].at[pl.ds((1 - c) * h, h)], dst_ref=outs[t], send_sem=send_sems.at[t],
            recv_sem=recv_sems.at[t], device_id=(x, y, 1 - c), device_id_type=MESH))
    for cp in cps:
        cp.start()
    for cp in cps:
        cp.wait()


def _chip_exchange_body(*refs, n):
    ins, outs = refs[:n], refs[n:2 * n]
    send_sems, recv_sems = refs[2 * n:]
    x, y, c, chips = _place()
    cps = []
    for t in range(n):
        h = ins[t].shape[0]
        for p, (px, py) in enumerate(chips):
            cps.append(pltpu.make_async_remote_copy(
                src_ref=ins[t].at[pl.ds(0, h), 2 * px + py], dst_ref=outs[t].at[p], send_sem=send_sems.at[3 * t + p],
                recv_sem=recv_sems.at[3 * t + p], device_id=(px, py, c), device_id_type=MESH))
    for cp in cps:
        cp.start()
    for cp in cps:
        cp.wait()


def _pair_share_body(*refs, n):
    bufs = refs[n:2 * n]
    send_sems, recv_sems = refs[2 * n:]
    x, y, c, _ = _place()
    sends, recvs = [], []
    for t in range(n):
        h = bufs[t].shape[0] // 2
        for half, group in ((c, sends), (1 - c, recvs)):
            rows = bufs[t].at[pl.ds(half * h, h)]
            group.append(pltpu.make_async_remote_copy(
                src_ref=rows, dst_ref=rows, send_sem=send_sems.at[t], recv_sem=recv_sems.at[t],
                device_id=(x, y, 1 - c), device_id_type=MESH))
    for cp in sends:
        cp.start()
    for cp in recvs:
        cp.wait_recv()
    for cp in sends:
        cp.wait_send()


def _allreduce_body(in_ref, out_ref, send_sems, recv_sems, local_sem, n):
    x, y, c, _ = _place()
    flip = lambda v, bit: 1 - v if bit else v
    peers = [(flip(x, k & 4), flip(y, k & 2), flip(c, k & 1)) for k in range(1, 8)]

    def remote(k, slot):
        return pltpu.make_async_remote_copy(
            src_ref=in_ref, dst_ref=out_ref.at[slot], send_sem=send_sems.at[k], recv_sem=recv_sems.at[k],
            device_id=peers[k], device_id_type=MESH)

    local = pltpu.make_async_copy(in_ref, out_ref.at[4 * x + 2 * y + c], local_sem.at[0])
    sends = [remote(k, 4 * x + 2 * y + c) for k in range(7)]
    local.start()
    for cp in sends:
        cp.start()
    for k, (px, py, pc) in enumerate(peers):
        remote(k, 4 * px + 2 * py + pc).wait_recv()
    for cp in sends:
        cp.wait_send()
    local.wait()


def _sliced_call(body, scalars, arrays, in_maps, blocks, out_block, out_map, out_shape, grid, *, name):
    grid_spec = pltpu.PrefetchScalarGridSpec(
        num_scalar_prefetch=1, grid=grid,
        in_specs=[pl.BlockSpec(b, m) for b, m in zip(blocks, in_maps)],
        out_specs=pl.BlockSpec(out_block, out_map))
    return pl.pallas_call(
        functools.partial(body), name=name, grid_spec=grid_spec, out_shape=out_shape,
        compiler_params=pltpu.CompilerParams(vmem_limit_bytes=V7X_VMEM_LIMIT_BYTES))(scalars, *arrays)


def _chip_sum_body(s_ref, g_ref, r_ref, o_ref):
    o_ref[...] = (g_ref[...] + r_ref[...]).astype(o_ref.dtype)


def _final_sum_body(s_ref, g_ref, r1_ref, a_ref, b_ref, c_ref, o_ref):
    o_ref[...] = (((g_ref[...] + r1_ref[...]) + a_ref[...].astype(F32)) + b_ref[...].astype(F32)) + c_ref[...].astype(F32)


def _reduce_big(stacks, place):
    n = len(stacks)
    half = [jax.ShapeDtypeStruct((s.shape[0] // 2,) + s.shape[1:], F32) for s in stacks]
    from_pair = _comm_call(_pair_exchange_body, stacks, half, (n, n), name="grads_pair_exchange")
    chip16 = []
    for s, r in zip(stacks, from_pair):
        h, J, Kd, Nd = r.shape
        tr = _tile(Kd, 256)
        blk = (None, None, tr, Nd)
        chip16.append(_sliced_call(
            _chip_sum_body, place, [s, r],
            [lambda l, j, i, p: (p[0] * h + l, j, i, 0), lambda l, j, i, p: (l, j, i, 0)], [blk, blk], blk,
            lambda l, j, i, p: (l, j, i, 0), jax.ShapeDtypeStruct(r.shape, BF16), (h, J, Kd // tr),
            name="grads_chip_sum"))
    recv = [jax.ShapeDtypeStruct((3, a.shape[0]) + a.shape[2:], BF16) for a in chip16]
    from_chips = _comm_call(_chip_exchange_body, chip16, recv, (3 * n, 3 * n), name="grads_chip_exchange")
    sums = []
    for s, r, f in zip(stacks, from_pair, from_chips):
        h, J, Kd, Nd = r.shape
        tr = _tile(Kd, 256)
        blk4, blk3 = (None, None, tr, Nd), (None, tr, Nd)
        mine = lambda l, i, p: (l, p[1], i, 0)
        sums.append(_sliced_call(
            _final_sum_body, place, [s, r, f, f, f],
            [lambda l, i, p: (p[0] * h + l, p[1], i, 0), mine] + [functools.partial(lambda l, i, p, q: (q, l, i, 0), q=q)
                                                                  for q in range(3)],
            [blk4, blk4, blk4, blk4, blk4], blk3, lambda l, i, p: (p[0] * h + l, i, 0),
            jax.ShapeDtypeStruct((2 * h, Kd, Nd), F32), (h, Kd // tr), name="grads_final_sum"))
    return _comm_call(_pair_share_body, sums, [jax.ShapeDtypeStruct(s.shape, F32) for s in sums], (n, n),
                      in_place=True, name="grads_pair_share")


def _allreduce_small(buf):
    gathered = pl.pallas_call(
        functools.partial(_allreduce_body, n=1), name="small_allgather",
        in_specs=[pl.BlockSpec(memory_space=pltpu.HBM)], out_specs=pl.BlockSpec(memory_space=pltpu.HBM),
        out_shape=jax.ShapeDtypeStruct((8,) + buf.shape, F32),
        scratch_shapes=[pltpu.SemaphoreType.DMA((7,)), pltpu.SemaphoreType.DMA((7,)),
                        pltpu.SemaphoreType.DMA((1,))])(buf)

    def total(*b):
        acc = b[0]
        for t in b[1:]:
            acc = acc + t
        return acc

    return _ew(total, [gathered[i] for i in range(8)], out_dtypes=[F32], name="small_sum")[0]


def _pack(parts):
    flat = jnp.concatenate([p.reshape(-1).astype(F32) for p in parts])
    rows = -(-flat.shape[0] // (8 * LANES)) * 8
    return jnp.pad(flat, (0, rows * LANES - flat.shape[0])).reshape(rows, LANES)


def _unpack(buf, shapes):
    flat = buf.reshape(-1)
    out, at = [], 0
    for s in shapes:
        size = math.prod(s)
        out.append(flat[at:at + size].reshape(s))
        at += size
    return out


_INPUTS = ("x",) + _WEIGHTS + ("loss_target",) + tuple("m_" + n for n in _WEIGHTS) + tuple("v_" + n for n in _WEIGHTS)


def kernel(x, t5_bias, ab_w_in, ab_w_out, s5_lam_re, s5_lam_im, s5_log_step, s5_b_re, s5_b_im, s5_c_re, s5_c_im,
           s5_d, s5_w_glu, c_w_qkv, c_w_out, c_rpb, norm_mix, norm_mlp, mlp_w1, mlp_w2, norm_final, loss_target,
           m_t5_bias, m_ab_w_in, m_ab_w_out, m_s5_lam_re, m_s5_lam_im, m_s5_log_step, m_s5_b_re, m_s5_b_im,
           m_s5_c_re, m_s5_c_im, m_s5_d, m_s5_w_glu, m_c_w_qkv, m_c_w_out, m_c_rpb, m_norm_mix, m_norm_mlp,
           m_mlp_w1, m_mlp_w2, m_norm_final, v_t5_bias, v_ab_w_in, v_ab_w_out, v_s5_lam_re, v_s5_lam_im,
           v_s5_log_step, v_s5_b_re, v_s5_b_im, v_s5_c_re, v_s5_c_im, v_s5_d, v_s5_w_glu, v_c_w_qkv, v_c_w_out,
           v_c_rpb, v_norm_mix, v_norm_mlp, v_mlp_w1, v_mlp_w2, v_norm_final):
    args = (x, t5_bias, ab_w_in, ab_w_out, s5_lam_re, s5_lam_im, s5_log_step, s5_b_re, s5_b_im, s5_c_re, s5_c_im,
            s5_d, s5_w_glu, c_w_qkv, c_w_out, c_rpb, norm_mix, norm_mlp, mlp_w1, mlp_w2, norm_final, loss_target,
            m_t5_bias, m_ab_w_in, m_ab_w_out, m_s5_lam_re, m_s5_lam_im, m_s5_log_step, m_s5_b_re, m_s5_b_im,
            m_s5_c_re, m_s5_c_im, m_s5_d, m_s5_w_glu, m_c_w_qkv, m_c_w_out, m_c_rpb, m_norm_mix, m_norm_mlp,
            m_mlp_w1, m_mlp_w2, m_norm_final, v_t5_bias, v_ab_w_in, v_ab_w_out, v_s5_lam_re, v_s5_lam_im,
            v_s5_log_step, v_s5_b_re, v_s5_b_im, v_s5_c_re, v_s5_c_im, v_s5_d, v_s5_w_glu, v_c_w_qkv, v_c_w_out,
            v_c_rpb, v_norm_mix, v_norm_mlp, v_mlp_w1, v_mlp_w2, v_norm_final)
    A = dict(zip(_INPUTS, args, strict=True))
    P = {n: A[n] for n in _WEIGHTS}
    place = jnp.stack([lax.axis_index("c"), 2 * lax.axis_index("x") + lax.axis_index("y")]).astype(jnp.int32)

    gathered = _cast_weights([P[n] for n in _BIG], place)
    W = dict(zip(_BIG, gathered))
    loss, dx, big, small = _local_grads(A["x"][0], A["loss_target"][0], P, W)

    stacks = [jnp.stack(big[n]) for n in _BIG]
    big_grads = dict(zip(_BIG, _reduce_big(stacks, place)))
    small_shapes = [P[n].shape for n in _SMALL] + [(1,)]
    reduced = _unpack(_allreduce_small(_pack([small[n] for n in _SMALL] + [loss.reshape(1)])), small_shapes)
    small_grads = dict(zip(_SMALL, reduced[:-1]))
    loss = reduced[-1][0]

    grads, delta, new_m, new_v = {}, {}, {}, {}
    for n in _BIG:
        g = big_grads[n]
        two_d = lambda t: t.reshape(-1, t.shape[-1])
        d, m, v = _adamw(two_d(P[n]), two_d(g), two_d(A["m_" + n]), two_d(A["v_" + n]), name="adamw")
        grads[n] = g
        delta[n], new_m[n], new_v[n] = (t.reshape(g.shape) for t in (d, m, v))
    d, m, v = _adamw(_pack([P[n] for n in _SMALL]), _pack([small_grads[n] for n in _SMALL]),
                     _pack([A["m_" + n] for n in _SMALL]), _pack([A["v_" + n] for n in _SMALL]), name="adamw_small")
    shapes = [P[n].shape for n in _SMALL]
    for n, dn, mn, vn in zip(_SMALL, _unpack(d, shapes), _unpack(m, shapes), _unpack(v, shapes)):
        grads[n] = small_grads[n]
        delta[n], new_m[n], new_v[n] = dn, mn, vn
    return (loss, dx[None], *[grads[n] for n in _WEIGHTS], *[delta[n] for n in _WEIGHTS],
            *[new_m[n] for n in _WEIGHTS], *[new_v[n] for n in _WEIGHTS])
```

```python
import functools
import math

import jax
import jax.numpy as jnp
from jax import lax
from jax.experimental import pallas as pl
from jax.experimental.pallas import tpu as pltpu

F32 = jnp.float32
BF16 = jnp.bfloat16

HEAD_DIM = 128
LANES = 128
DILATED_BRANCHES = ((128, 1), (512, 4), (2048, 16))
A_QBLOCK = 128
DIL_HALF = 64
B_GROUP = 16
B_STATE = 64
GROUPS_PER_BLOCK = LANES // B_GROUP
STATE_PER_BLOCK = GROUPS_PER_BLOCK * B_STATE
GRID_W = 64
NA_ROWS_MAX = 8
NA_COLS = 16
T5_BUCKETS = 32
T5_MAX_DISTANCE = 1024
RMS_EPS = 1e-6
NEG_INF = -1e30
ADAM_LR = 0.001
ADAM_B1 = 0.9
ADAM_B2 = 0.999
ADAM_EPS = 1e-08
ADAM_WD = 0.01
ADAM_STEP = 10
V7X_VMEM_LIMIT_BYTES = 56 * 1024 * 1024

NN = (((1,), (0,)), ((), ()))
NT = (((1,), (1,)), ((), ()))
TN = (((0,), (0,)), ((), ()))
MESH = pl.DeviceIdType.MESH


def _tile(n, pref):
    if n <= pref:
        return n
    for t in range(pref - pref % LANES, 0, -LANES):
        if n % t == 0:
            return t
    t = pref
    while t >= 8:
        if n % t == 0:
            return t
        t //= 2
    return n


def _pcall(body, *, name, grid, in_specs, out_specs, out_shape, scratch=(), sem=None, aliases=None, **static):
    params = dict(vmem_limit_bytes=V7X_VMEM_LIMIT_BYTES)
    if sem is not None:
        params["dimension_semantics"] = sem
    return pl.pallas_call(
        functools.partial(body, **static), name=name, grid=grid, in_specs=in_specs, out_specs=out_specs,
        out_shape=out_shape, scratch_shapes=list(scratch), input_output_aliases=aliases or {},
        compiler_params=pltpu.CompilerParams(**params))


def _gather_plan(bufs, pieces, sems):
    ici_send, ici_recv, pair_send, pair_recv = sems
    x, y, c = lax.axis_index("x"), lax.axis_index("y"), lax.axis_index("c")
    chips = ((1 - x, y), (x, 1 - y), (1 - x, 1 - y))
    slots = [2 * x + y] + [2 * px + py for px, py in chips]
    every = [(k, p) for k in range(len(pieces)) for p in range(3)]

    def part(k, slot, half):
        t, layer = pieces[k]
        h = bufs[t].shape[2] // 2
        return bufs[t].at[slot, layer, pl.ds(half * h, h)]

    def over_ici(k, p, slot):
        return pltpu.make_async_remote_copy(
            src_ref=part(k, slot, c), dst_ref=part(k, slot, c), send_sem=ici_send.at[3 * k + p],
            recv_sem=ici_recv.at[3 * k + p], device_id=(*chips[p], c), device_id_type=MESH)

    def to_pair(k, p, half):
        return pltpu.make_async_remote_copy(
            src_ref=part(k, slots[1 + p], half), dst_ref=part(k, slots[1 + p], half),
            send_sem=pair_send.at[3 * k + p], recv_sem=pair_recv.at[3 * k + p], device_id=(x, y, 1 - c),
            device_id_type=MESH)

    def start():
        for k, p in every:
            over_ici(k, p, slots[0]).start()

    def finish():
        for k, p in every:
            over_ici(k, p, slots[1 + p]).wait_recv()
            to_pair(k, p, c).start()
        for k, p in every:
            to_pair(k, p, 1 - c).wait_recv()
        for k, p in every:
            over_ici(k, p, slots[0]).wait_send()
            to_pair(k, p, c).wait_send()

    return start, finish


def _gather_sems(pieces):
    return [pltpu.SemaphoreType.DMA((3 * len(pieces),))] * 4


def _hosted_body(*refs, host_body, n_in, n_out, n_buf, pieces, **static):
    ins = refs[:n_in]
    outs = refs[n_in + n_buf:n_in + n_buf + n_out]
    bufs = refs[n_in + n_buf + n_out:n_in + 2 * n_buf + n_out]
    scratch = refs[n_in + 2 * n_buf + n_out:-4]
    start, finish = _gather_plan(bufs, pieces, refs[-4:])
    ids = [pl.program_id(a) for a in range(3)]
    last = [pl.num_programs(a) - 1 for a in range(3)]

    @pl.when((ids[0] == 0) & (ids[1] == 0) & (ids[2] == 0))
    def _():
        start()

    host_body(*ins, *outs, *scratch, **static)

    @pl.when((ids[0] == last[0]) & (ids[1] == last[1]) & (ids[2] == last[2]))
    def _():
        finish()


def _gather_pieces_body(*refs, n, pieces):
    start, finish = _gather_plan(refs[n:2 * n], pieces, refs[2 * n:])
    start()
    finish()


def _named_pieces(W, pieces):
    names = sorted({n for n, _ in pieces})
    return names, [W[n] for n in names], tuple((names.index(n), layer) for n, layer in pieces)


def _gather_pieces(W, pieces):
    names, bufs, idx = _named_pieces(W, pieces)
    hbm = pl.BlockSpec(memory_space=pltpu.HBM)
    new = pl.pallas_call(
        functools.partial(_gather_pieces_body, n=len(bufs), pieces=idx), name="weights_gather",
        in_specs=[hbm] * len(bufs), out_specs=tuple([hbm] * len(bufs)),
        out_shape=tuple(jax.ShapeDtypeStruct(b.shape, b.dtype) for b in bufs),
        input_output_aliases={t: t for t in range(len(bufs))}, scratch_shapes=_gather_sems(idx))(*bufs)
    return {**W, **dict(zip(names, new))}


def _mm_finish(acc, rest, mode):
    if mode == "plain":
        rest[0][...] = acc.astype(rest[0].dtype)
    elif mode == "res":
        rest[1][...] = (rest[0][...] + acc).astype(rest[1].dtype)
    elif mode == "relu2":
        rest[0][...] = acc
        r = jnp.maximum(acc, 0.0)
        rest[1][...] = (r * r).astype(rest[1].dtype)
    elif mode == "dact":
        rest[1][...] = (acc * (2.0 * jnp.maximum(rest[0][...], 0.0))).astype(rest[1].dtype)


def _mm_body(a_ref, b_ref, *rest, nk, dims, mode):
    prod = lax.dot_general(a_ref[...].astype(BF16), b_ref[...].astype(BF16), dims, preferred_element_type=F32)
    if nk == 1:
        _mm_finish(prod, rest, mode)
        return
    acc_ref = rest[-1]
    k = pl.program_id(2)

    @pl.when(k == 0)
    def _():
        acc_ref[...] = prod

    @pl.when(k > 0)
    def _():
        acc_ref[...] += prod

    @pl.when(k == nk - 1)
    def _():
        _mm_finish(acc_ref[...], rest, mode)


def _w_dims(w, split):
    J, _, Kd, Nd = w.shape
    return (J, Kd, J * Nd, Kd, Nd) if split == "col" else (J, J * Kd, Nd, Kd, Nd)


def _w_spec(split, layer, tk, tn, Kd, Nd, kn_of):
    kps, nps = Kd // tk, Nd // tn

    def index(*g):
        kb, nb = kn_of(*g)
        if split == "col":
            return nb // nps, layer, kb, nb % nps
        return kb // kps, layer, kb % kps, nb

    return pl.BlockSpec((None, None, tk, tn), index)


def _mm_nn(a, w, layer, split, *, name, out_dtype=F32, mode="plain", res=None, host=None):
    M = a.shape[0]
    J, K, N, Kd, Nd = _w_dims(w, split)
    tm, tn, tk = _tile(M, 1024), _tile(Nd, 1024), _tile(Kd, 2048)
    in_specs = [pl.BlockSpec((tm, tk), lambda i, j, k: (i, k)),
                _w_spec(split, layer, tk, tn, Kd, Nd, lambda i, j, k: (k, j))]
    args = [a, w]
    o_spec = pl.BlockSpec((tm, tn), lambda i, j, k: (i, j))
    if mode == "res":
        in_specs.append(o_spec)
        args.append(res)
    if mode == "relu2":
        out_shape = (jax.ShapeDtypeStruct((M, N), F32), jax.ShapeDtypeStruct((M, N), BF16))
        out_specs = (o_spec, o_spec)
    else:
        out_shape = (jax.ShapeDtypeStruct((M, N), out_dtype),)
        out_specs = (o_spec,)
    grid = (M // tm, N // tn, K // tk)
    scratch = [pltpu.VMEM((tm, tn), F32)]
    if host is None:
        out = _pcall(_mm_body, name=name, grid=grid, in_specs=in_specs, out_specs=out_specs, out_shape=out_shape,
                     scratch=scratch, sem=("parallel", "parallel", "arbitrary"), nk=K // tk, dims=NN, mode=mode)(*args)
        return out if mode == "relu2" else out[0]
    W, pieces = host
    names, bufs, idx = _named_pieces(W, pieces)
    hbm = pl.BlockSpec(memory_space=pltpu.HBM)
    n_in, n_out, n_buf = len(args), len(out_shape), len(bufs)
    out = _pcall(_hosted_body, name=name, grid=grid, in_specs=in_specs + [hbm] * n_buf,
                 out_specs=tuple(out_specs) + (hbm,) * n_buf,
                 out_shape=tuple(out_shape) + tuple(jax.ShapeDtypeStruct(b.shape, b.dtype) for b in bufs),
                 scratch=scratch + _gather_sems(idx), sem=("arbitrary",) * 3,
                 aliases={n_in + t: n_out + t for t in range(n_buf)}, host_body=_mm_body, n_in=n_in, n_out=n_out,
                 n_buf=n_buf, pieces=idx, nk=K // tk, dims=NN, mode=mode)(*args, *bufs)
    res_out = out[:n_out] if mode == "relu2" else out[0]
    return res_out, {**W, **dict(zip(names, out[n_out:]))}


def _mm_nt(a, w, layer, split, *, name, out_dtype=F32, mode="plain", act=None):
    M = a.shape[0]
    J, K, N, Kd, Nd = _w_dims(w, split)
    tm, tko, tc = _tile(M, 1024), _tile(Kd, 1024), _tile(Nd, 2048)
    in_specs = [pl.BlockSpec((tm, tc), lambda i, j, c: (i, c)),
                _w_spec(split, layer, tko, tc, Kd, Nd, lambda i, j, c: (j, c))]
    args = [a, w]
    o_spec = pl.BlockSpec((tm, tko), lambda i, j, c: (i, j))
    if mode == "dact":
        in_specs.append(o_spec)
        args.append(act)
    return _pcall(_mm_body, name=name, grid=(M // tm, K // tko, N // tc), in_specs=in_specs, out_specs=o_spec,
                  out_shape=jax.ShapeDtypeStruct((M, K), out_dtype), scratch=[pltpu.VMEM((tm, tko), F32)],
                  sem=("parallel", "parallel", "arbitrary"), nk=N // tc, dims=NT, mode=mode)(*args)


def _mm_tn(a, b, J, split, *, name):
    M, K = a.shape
    N = b.shape[1]
    Kd, Nd = (K, N // J) if split == "col" else (K // J, N)
    tk, tn, tc = _tile(Kd, 1024), _tile(Nd, 1024), _tile(M, 2048)
    kps, nps = Kd // tk, Nd // tn
    in_specs = [pl.BlockSpec((tc, tk), lambda i, j, c: (c, i)),
                pl.BlockSpec((tc, tn), lambda i, j, c: (c, j))]
    if split == "col":
        o_spec = pl.BlockSpec((None, tk, tn), lambda i, j, c: (j // nps, i, j % nps))
    else:
        o_spec = pl.BlockSpec((None, tk, tn), lambda i, j, c: (i // kps, i % kps, j))
    return _pcall(_mm_body, name=name, grid=(K // tk, N // tn, M // tc), in_specs=in_specs, out_specs=o_spec,
                  out_shape=jax.ShapeDtypeStruct((J, Kd, Nd), F32), scratch=[pltpu.VMEM((tk, tn), F32)],
                  sem=("parallel", "parallel", "arbitrary"), nk=M // tc, dims=TN, mode="plain")(a, b)


def _ew_body(*refs, fn, n_in, n_out, n_acc):
    res = fn(*[r[...] for r in refs[:n_in]])
    if not isinstance(res, (tuple, list)):
        res = (res,)
    outs = refs[n_in:n_in + n_out]
    accs = refs[n_in + n_out:]
    for o, r in zip(outs, res[:n_out]):
        o[...] = r.astype(o.dtype)
    if n_acc:
        first = pl.program_id(1) == 0
        for a, r in zip(accs, res[n_out:]):
            @pl.when(first)
            def _(a=a):
                a[...] = jnp.zeros_like(a)

            a[...] += r


def _ew(fn, rows, vecs=(), *, out_dtypes=(), n_acc=0, width=None, ncol=1, tr=256, name):
    rows = [r if isinstance(r, tuple) else (r, 0) for r in rows]
    R = rows[0][0].shape[0]
    C = width if width is not None else rows[0][0].shape[1]
    tr = _tile(R, tr)
    in_specs = [pl.BlockSpec((tr, C), functools.partial(lambda j, i, off: (i, off + j), off=off)) for _, off in rows]
    in_specs += [pl.BlockSpec((1, C), lambda j, i: (0, j)) for _ in vecs]
    out_shape = [jax.ShapeDtypeStruct((R, ncol * C), dt) for dt in out_dtypes]
    out_specs = [pl.BlockSpec((tr, C), lambda j, i: (i, j)) for _ in out_dtypes]
    out_shape += [jax.ShapeDtypeStruct((1, ncol * C), F32)] * n_acc
    out_specs += [pl.BlockSpec((1, C), lambda j, i: (0, j))] * n_acc
    res = _pcall(_ew_body, name=name, grid=(ncol, R // tr), in_specs=in_specs, out_specs=tuple(out_specs),
                 out_shape=tuple(out_shape), sem=("parallel", "arbitrary" if n_acc else "parallel"),
                 fn=fn, n_in=len(rows) + len(vecs), n_out=len(out_dtypes), n_acc=n_acc)(
        *[a for a, _ in rows], *vecs)
    return res


def _rms_fwd(x, g, *, name):
    def fn(x, g):
        r = lax.rsqrt(jnp.mean(x * x, axis=1, keepdims=True) + RMS_EPS)
        return (x * r) * g

    return _ew(fn, [x], [g], out_dtypes=[BF16], name=name)[0]


def _rms_bwd(dh, x, g, dres, *, name):
    def fn(dh, x, dres, g):
        r = lax.rsqrt(jnp.mean(x * x, axis=1, keepdims=True) + RMS_EPS)
        y = x * r
        dy = dh * g
        dx = dres + r * (dy - y * jnp.mean(dy * y, axis=1, keepdims=True))
        return dx, dx, jnp.sum(dh * y, axis=0, keepdims=True)

    dx, dx16, dg = _ew(fn, [dh, x, dres], [g], out_dtypes=[F32, BF16], n_acc=1, name=name)
    return (dx, dx16), dg


def _loss_and_grad(x, g, target, *, name):
    D = x.shape[1]

    def fn(x, t, g):
        r = lax.rsqrt(jnp.mean(x * x, axis=1, keepdims=True) + RMS_EPS)
        y = x * r
        diff = y * g - t
        dh = diff * (1.0 / D)
        dy = dh * g
        dx = r * (dy - y * jnp.mean(dy * y, axis=1, keepdims=True))
        loss = jnp.sum(jnp.sum(diff * diff, axis=1, keepdims=True), axis=0, keepdims=True) * (0.5 / D)
        return dx, dx, jnp.sum(dh * y, axis=0, keepdims=True), jnp.broadcast_to(loss, (1, D))

    dx, dx16, dg, loss = _ew(fn, [x, target], [g], out_dtypes=[F32, BF16], n_acc=2, name=name)
    return (dx, dx16), dg, loss


def _gelu(y):
    c = math.sqrt(2.0 / math.pi)
    return 0.5 * y * (1.0 + jnp.tanh(c * (y + 0.044715 * (y * y * y))))


def _gelu_grad(y):
    c = math.sqrt(2.0 / math.pi)
    t = jnp.tanh(c * (y + 0.044715 * (y * y * y)))
    return 0.5 * (1.0 + t) + 0.5 * y * (1.0 - t * t) * (c * (1.0 + 3 * 0.044715 * (y * y)))


def _adamw(w, g, m, v, *, name):
    def fn(w, g, m, v):
        m2 = ADAM_B1 * m + (1.0 - ADAM_B1) * g
        v2 = ADAM_B2 * v + (1.0 - ADAM_B2) * (g * g)
        m_hat = m2 / (1.0 - ADAM_B1 ** ADAM_STEP)
        v_hat = v2 / (1.0 - ADAM_B2 ** ADAM_STEP)
        delta = -ADAM_LR * (m_hat / (jnp.sqrt(v_hat) + ADAM_EPS) + ADAM_WD * w)
        return delta, m2, v2

    return _ew(fn, [w, g, m, v], out_dtypes=[F32, F32, F32], name=name)


def _window(kind, blk, QB, rows):
    if kind == "dil":
        return pl.multiple_of(blk * QB, QB), 0
    kr = min(NA_ROWS_MAX, rows)
    rs = jnp.clip(blk - kr // 2, 0, rows - kr)
    return pl.multiple_of(rs * GRID_W, GRID_W), blk - rs


def _scores(q, kw, bias, kind, start, QB, W, L_valid, scale):
    s = lax.dot_general(q, kw, NT, preferred_element_type=F32) * scale + bias
    if kind == "dil":
        kp = start + lax.broadcasted_iota(jnp.int32, (QB, W), 1)
        s = jnp.where((kp >= DIL_HALF) & (kp < DIL_HALF + L_valid), s, NEG_INF)
    return s


def _wattn_fwd_body(q_ref, k_ref, v_ref, b_ref, *outs, QB, SUB, W, kind, L_valid, rows, scale):
    n = pl.program_id(1)
    wins = [_window(kind, n * SUB + i, QB, rows) for i in range(SUB)]
    sls = [slice(i * QB, (i + 1) * QB) for i in range(SUB)]
    ss = [_scores(q_ref[sl, :], k_ref[pl.ds(start, W), :], b_ref[pat], kind, start, QB, W, L_valid, scale)
          for sl, (start, pat) in zip(sls, wins)]
    ms = [jnp.max(s, axis=1, keepdims=True) for s in ss]
    es = [jnp.exp(s - m) for s, m in zip(ss, ms)]
    ls = [jnp.sum(e, axis=1, keepdims=True) for e in es]
    for sl, (start, _), m, e, l in zip(sls, wins, ms, es, ls):
        vw = v_ref[pl.ds(start, W), :]
        if kind == "na":
            p = (e * (1.0 / l)).astype(BF16)
            o = lax.dot_general(p, vw, NN, preferred_element_type=F32)
            outs[0][sl, :] = o.astype(outs[0].dtype)
            outs[1][sl, :] = jnp.broadcast_to(m + jnp.log(l), (QB, LANES))
        else:
            outs[0][sl, :] = lax.dot_general(e.astype(BF16), vw, NN, preferred_element_type=F32)
            outs[1][sl, :] = jnp.broadcast_to(m, (QB, LANES))
            outs[2][sl, :] = jnp.broadcast_to(l, (QB, LANES))


def _wattn_bwd_body(q_ref, do_ref, lse_ref, dl_ref, k_ref, v_ref, b_ref, dq_ref, dk_ref, dv_ref, db_ref, *,
                    QB, SUB, W, kind, L_valid, rows, scale, dgroup):
    cb = pl.program_id(0)
    n = pl.program_id(1)

    @pl.when(n == 0)
    def _():
        dk_ref[...] = jnp.zeros_like(dk_ref)
        dv_ref[...] = jnp.zeros_like(dv_ref)

    @pl.when((n == 0) & (cb % dgroup == 0))
    def _():
        db_ref[...] = jnp.zeros_like(db_ref)

    wins = [_window(kind, n * SUB + i, QB, rows) for i in range(SUB)]
    sls = [slice(i * QB, (i + 1) * QB) for i in range(SUB)]
    ss = [_scores(q_ref[sl, :], k_ref[pl.ds(start, W), :], b_ref[pat], kind, start, QB, W, L_valid, scale)
          for sl, (start, pat) in zip(sls, wins)]
    dos = [do_ref[sl, :].astype(BF16) for sl in sls]
    dps = [lax.dot_general(do, v_ref[pl.ds(start, W), :], NT, preferred_element_type=F32)
           for do, (start, _) in zip(dos, wins)]
    ps = [jnp.exp(s - lse_ref[sl, :][:, :1]) for s, sl in zip(ss, sls)]
    if kind == "na":
        dls = [jnp.sum(do_ref[sl, :].astype(F32) * dl_ref[sl, :].astype(F32), axis=1, keepdims=True) for sl in sls]
    else:
        dls = [dl_ref[sl, :][:, :1] for sl in sls]
    dss = [p * (dp - dl) for p, dp, dl in zip(ps, dps, dls)]
    for sl, (start, pat), p, ds, do in zip(sls, wins, ps, dss, dos):
        q = q_ref[sl, :]
        db_ref[pat] += ds
        dsb = ds.astype(BF16)
        dq_ref[sl, :] = (lax.dot_general(dsb, k_ref[pl.ds(start, W), :], NN, preferred_element_type=F32)
                         * scale).astype(dq_ref.dtype)
        dk_ref[pl.ds(start, W), :] += lax.dot_general(dsb, q, TN, preferred_element_type=F32) * scale
        dv_ref[pl.ds(start, W), :] += lax.dot_general(p.astype(BF16), do, TN, preferred_element_type=F32)


def _wattn_geometry(kind, LQ, dil):
    if kind == "dil":
        QB, W, rows = A_QBLOCK, A_QBLOCK + 2 * DIL_HALF, 0
    else:
        rows = LQ // GRID_W
        QB, W = GRID_W, min(NA_ROWS_MAX, rows) * GRID_W
    blocks = LQ // QB
    SUB = 8 if blocks % 8 == 0 else 4 if blocks % 4 == 0 else 1
    return QB, W, rows, SUB


def _wattn_fwd(q, k, v, bias, *, kind, H, dil, qoff, koff, voff, name):
    LQ, LK = q.shape[0], k.shape[0]
    QB, W, rows, SUB = _wattn_geometry(kind, LQ, dil)
    ncb = H * dil
    NP = bias.shape[1]

    def col(cb, off):
        return off + (cb % dil) * H + cb // dil

    in_specs = [pl.BlockSpec((QB * SUB, LANES), lambda cb, n: (n, col(cb, qoff))),
                pl.BlockSpec((LK, LANES), lambda cb, n: (0, col(cb, koff))),
                pl.BlockSpec((LK, LANES), lambda cb, n: (0, col(cb, voff))),
                pl.BlockSpec((None, NP, QB, W), lambda cb, n: (cb // dil, 0, 0, 0))]
    o_spec = pl.BlockSpec((QB * SUB, LANES), lambda cb, n: (n, col(cb, 0)))
    shape = (LQ, ncb * LANES)
    if kind == "na":
        out_shape = (jax.ShapeDtypeStruct(shape, BF16), jax.ShapeDtypeStruct(shape, F32))
    else:
        out_shape = (jax.ShapeDtypeStruct(shape, F32),) * 3
    return _pcall(_wattn_fwd_body, name=name, grid=(ncb, LQ // (QB * SUB)), in_specs=in_specs,
                  out_specs=(o_spec,) * len(out_shape), out_shape=out_shape, sem=("parallel", "parallel"),
                  QB=QB, SUB=SUB, W=W, kind=kind, L_valid=LQ, rows=rows, scale=1.0 / math.sqrt(HEAD_DIM))(
        q, k, v, bias)


def _wattn_bwd(q, do, lse, delta, k, v, bias, *, kind, H, dil, qoff, koff, voff, dq_dtype, name):
    LQ, LK = q.shape[0], k.shape[0]
    QB, W, rows, SUB = _wattn_geometry(kind, LQ, dil)
    ncb = H * dil
    NP = bias.shape[1]

    def col(cb, off):
        return off + (cb % dil) * H + cb // dil

    q_spec = lambda off: pl.BlockSpec((QB * SUB, LANES), lambda cb, n: (n, col(cb, off)))
    kv_spec = lambda off: pl.BlockSpec((LK, LANES), lambda cb, n: (0, col(cb, off)))
    b_spec = pl.BlockSpec((None, NP, QB, W), lambda cb, n: (cb // dil, 0, 0, 0))
    in_specs = [q_spec(qoff), q_spec(0), q_spec(0), q_spec(0), kv_spec(koff), kv_spec(voff), b_spec]
    out_shape = (jax.ShapeDtypeStruct((LQ, ncb * LANES), dq_dtype), jax.ShapeDtypeStruct((LK, ncb * LANES), F32),
                 jax.ShapeDtypeStruct((LK, ncb * LANES), F32), jax.ShapeDtypeStruct(bias.shape, F32))
    out_specs = (q_spec(0), kv_spec(0), kv_spec(0), b_spec)
    return _pcall(_wattn_bwd_body, name=name, grid=(ncb, LQ // (QB * SUB)), in_specs=in_specs, out_specs=out_specs,
                  out_shape=out_shape, sem=("arbitrary", "arbitrary"),
                  QB=QB, SUB=SUB, W=W, kind=kind, L_valid=LQ, rows=rows, scale=1.0 / math.sqrt(HEAD_DIM),
                  dgroup=dil)(q, do, lse, delta, k, v, bias)


def _attn_delta(do, do_off, o, *, name):
    def fn(do, o):
        return jnp.broadcast_to(jnp.sum(do * o.astype(F32), axis=1, keepdims=True), do.shape), do

    return _ew(fn, [(do, do_off), o], out_dtypes=[F32, BF16], width=LANES, ncol=o.shape[1] // LANES, tr=512,
               name=name)


SUBLANES = 8


def _bf(ref):
    return ref[...].astype(BF16)


def _cmul(ar, ai, br, bi):
    return ar * br - ai * bi, ar * bi + ai * br


def _powers(ar, ai, n):
    pr, pi = jnp.ones_like(ar), jnp.zeros_like(ai)
    mr, mi = ar, ai
    while pr.shape[0] < n + 1:
        qr, qi = _cmul(mr, mi, pr, pi)
        pr, pi = jnp.concatenate([pr, qr]), jnp.concatenate([pi, qi])
        mr, mi = _cmul(mr, mi, mr, mi)
    return pr[:n + 1], pi[:n + 1]


def _seg_tables(a_re, a_im, SEG, desc):
    pr, pi = _powers(lax.stop_gradient(a_re), lax.stop_gradient(a_im), SEG)
    nat = (lambda t: t[::-1]) if desc else (lambda t: t)
    return nat(pr[1:]), nat(pi[1:]), pr[:-1], pi[:-1], pr[SEG:], pi[SEG:]


def _interleave(t, inverse=False):
    S, C = t.shape
    TC = _tile(S, 512)
    a, b = (TC // SUBLANES, SUBLANES) if inverse else (SUBLANES, TC // SUBLANES)
    return t.reshape(S // TC, a, b, C).transpose(0, 2, 1, 3).reshape(S, C)


def _tile_carries(desc, end_r, end_i, a64r, a64i, car_r, car_i, cst_r, cst_i):
    cr, ci = car_r[0:1, :], car_i[0:1, :]
    for i in (reversed(range(SUBLANES)) if desc else range(SUBLANES)):
        cst_r[i:i + 1, :] = cr
        cst_i[i:i + 1, :] = ci
        pr, pi = _cmul(a64r, a64i, cr, ci)
        cr, ci = pr + end_r[i:i + 1, :], pi + end_i[i:i + 1, :]
    car_r[0:1, :] = cr
    car_i[0:1, :] = ci


def _tile_correct(SEG, xr_ref, xi_ref, pr_ref, pi_ref, cst_r, cst_i):
    cr, ci = cst_r[...], cst_i[...]
    for t in range(SEG):
        rows = pl.ds(SUBLANES * t, SUBLANES)
        dr, di = _cmul(pr_ref[t:t + 1, :], pi_ref[t:t + 1, :], cr, ci)
        xr_ref[rows, :] += dr
        xi_ref[rows, :] += di


def _s5il_fwd_body(u_ref, br_ref, bi_ref, cr_ref, cin_ref, are_ref, aim_ref, pr_ref, pi_ref, a64r_ref, a64i_ref,
                   xre_ref, xim_ref, y_ref, bre_s, bim_s, car_r, car_i, cst_r, cst_i, end_r, end_i, *, TC, desc):
    SEG = TC // SUBLANES

    @pl.when(pl.program_id(1) == 0)
    def _():
        car_r[...] = jnp.zeros_like(car_r)
        car_i[...] = jnp.zeros_like(car_i)

    ub = _bf(u_ref)
    bre_s[...] = lax.dot_general(ub, _bf(br_ref), NN, preferred_element_type=F32)
    bim_s[...] = lax.dot_general(ub, _bf(bi_ref), NN, preferred_element_type=F32)
    ar, ai = are_ref[...], aim_ref[...]
    xr = xi = jnp.zeros((SUBLANES, STATE_PER_BLOCK), F32)
    for tau in range(SEG):
        rows = pl.ds(SUBLANES * (SEG - 1 - tau if desc else tau), SUBLANES)
        pr, pi = _cmul(ar, ai, xr, xi)
        xr = pr + bre_s[rows, :]
        xi = pi + bim_s[rows, :]
        xre_ref[rows, :] = xr
        xim_ref[rows, :] = xi
    end_r[...] = xr
    end_i[...] = xi
    _tile_carries(desc, end_r, end_i, a64r_ref[...], a64i_ref[...], car_r, car_i, cst_r, cst_i)
    _tile_correct(SEG, xre_ref, xim_ref, pr_ref, pi_ref, cst_r, cst_i)
    y_ref[...] = (lax.dot_general(_bf(xre_ref), _bf(cr_ref), NN, preferred_element_type=F32)
                  + lax.dot_general(_bf(xim_ref), _bf(cin_ref), NN, preferred_element_type=F32))


def _s5il_specs(S, BW, desc):
    NB = BW // LANES
    TC = _tile(S, 512)
    nT = S // TC
    tmap = (lambda l, t: (nT - 1 - t, l)) if desc else (lambda l, t: (t, l))
    narrow = pl.BlockSpec((TC, LANES), tmap)
    wide = pl.BlockSpec((TC, STATE_PER_BLOCK), tmap)
    vec = pl.BlockSpec((1, STATE_PER_BLOCK), lambda l, t: (0, l))
    tab = pl.BlockSpec((TC // SUBLANES, STATE_PER_BLOCK), lambda l, t: (0, l))
    w_in = pl.BlockSpec((None, LANES, STATE_PER_BLOCK), lambda l, t: (l, 0, 0))
    w_out = pl.BlockSpec((None, STATE_PER_BLOCK, LANES), lambda l, t: (l, 0, 0))
    scratch = [pltpu.VMEM((TC, STATE_PER_BLOCK), F32)] * 2 + [pltpu.VMEM((SUBLANES, STATE_PER_BLOCK), F32)] * 6
    return NB, TC, nT, narrow, wide, vec, tab, w_in, w_out, scratch


def _s5_scan_fwd(u, mat, *, reverse, name):
    a_re, a_im, b_r, b_i, c_r, c_in = mat
    S, BW = u.shape
    NB, TC, nT, narrow, wide, vec, tab, w_in, w_out, scratch = _s5il_specs(S, BW, reverse)
    p_re, p_im, _, _, a64r, a64i = _seg_tables(a_re, a_im, TC // SUBLANES, reverse)
    xs = jax.ShapeDtypeStruct((S, NB * STATE_PER_BLOCK), F32)
    return _pcall(_s5il_fwd_body, name=name, grid=(NB, nT),
                  in_specs=[narrow, w_in, w_in, w_out, w_out, vec, vec, tab, tab, vec, vec],
                  out_specs=(wide, wide, narrow), out_shape=(xs, xs, jax.ShapeDtypeStruct((S, BW), F32)),
                  scratch=scratch, sem=("parallel", "arbitrary"), TC=TC, desc=reverse)(
        u, b_r, b_i, c_r, c_in, a_re, a_im, p_re, p_im, a64r, a64i)


def _s5il_bwd_body(dy_ref, u_ref, xre_ref, xim_ref, br_ref, bi_ref, cr_ref, cin_ref, are_ref, aim_ref,
                   pr_ref, pi_ref, qr_ref, qi_ref, a64r_ref, a64i_ref,
                   du_ref, dar_ref, dai_ref, dbr_ref, dbi_ref, dcr_ref, dcin_ref,
                   hre_s, him_s, car_r, car_i, cst_r, cst_i, end_r, end_i, *, TC, desc):
    SEG = TC // SUBLANES

    @pl.when(pl.program_id(1) == 0)
    def _():
        for r in (car_r, car_i, dar_ref, dai_ref, dbr_ref, dbi_ref, dcr_ref, dcin_ref):
            r[...] = jnp.zeros_like(r)

    dyb = _bf(dy_ref)
    hre_s[...] = lax.dot_general(dyb, _bf(cr_ref), NT, preferred_element_type=F32)
    him_s[...] = lax.dot_general(dyb, _bf(cin_ref), NT, preferred_element_type=F32)
    dcr_ref[...] += lax.dot_general(_bf(xre_ref), dyb, TN, preferred_element_type=F32)
    dcin_ref[...] += lax.dot_general(_bf(xim_ref), dyb, TN, preferred_element_type=F32)
    ar, ai = are_ref[...], aim_ref[...]
    hr = hi = dr = di = er = ei = jnp.zeros((SUBLANES, STATE_PER_BLOCK), F32)
    for tau in range(SEG):
        rows = pl.ds(SUBLANES * (SEG - 1 - tau if desc else tau), SUBLANES)
        xr, xi = xre_ref[rows, :], xim_ref[rows, :]
        qr, qi = qr_ref[tau:tau + 1, :], qi_ref[tau:tau + 1, :]
        dr = dr + (hr * xr + hi * xi)
        di = di + (hi * xr - hr * xi)
        er = er + (qr * xr + qi * xi)
        ei = ei + (qi * xr - qr * xi)
        pr, pi = _cmul(ar, ai, hr, hi)
        hr = pr + hre_s[rows, :]
        hi = pi + him_s[rows, :]
        hre_s[rows, :] = hr
        him_s[rows, :] = hi
    end_r[...] = hr
    end_i[...] = hi
    _tile_carries(desc, end_r, end_i, a64r_ref[...], a64i_ref[...], car_r, car_i, cst_r, cst_i)
    fr, fi = _cmul(cst_r[...], cst_i[...], er, ei)
    dar_ref[...] += jnp.sum(dr + fr, axis=0, keepdims=True)
    dai_ref[...] += jnp.sum(di + fi, axis=0, keepdims=True)
    _tile_correct(SEG, hre_s, him_s, pr_ref, pi_ref, cst_r, cst_i)
    hrb, hib, ub = _bf(hre_s), _bf(him_s), _bf(u_ref)
    du_ref[...] = (lax.dot_general(hrb, _bf(br_ref), NT, preferred_element_type=F32)
                   + lax.dot_general(hib, _bf(bi_ref), NT, preferred_element_type=F32))
    dbr_ref[...] += lax.dot_general(ub, hrb, TN, preferred_element_type=F32)
    dbi_ref[...] += lax.dot_general(ub, hib, TN, preferred_element_type=F32)


def _s5_scan_bwd(dy, u, x_re, x_im, mat, *, reverse, name):
    a_re, a_im, b_r, b_i, c_r, c_in = mat
    S, BW = u.shape
    desc = not reverse
    NB, TC, nT, narrow, wide, vec, tab, w_in, w_out, scratch = _s5il_specs(S, BW, desc)
    p_re, p_im, q_re, q_im, a64r, a64i = _seg_tables(a_re, -a_im, TC // SUBLANES, desc)
    shapes = (jax.ShapeDtypeStruct((S, BW), F32),) + tuple(jax.ShapeDtypeStruct(m.shape, F32) for m in mat)
    res = _pcall(_s5il_bwd_body, name=name, grid=(NB, nT),
                 in_specs=[narrow, narrow, wide, wide, w_in, w_in, w_out, w_out, vec, vec, tab, tab, tab, tab, vec, vec],
                 out_specs=(narrow, vec, vec, w_in, w_in, w_out, w_out), out_shape=shapes,
                 scratch=scratch, sem=("parallel", "arbitrary"), TC=TC, desc=desc)(
        dy, u, x_re, x_im, b_r, b_i, c_r, c_in, a_re, -a_im, p_re, p_im, q_re, q_im, a64r, a64i)
    return res[0], tuple(res[1:])


def _t5_bucket(rel):
    half = T5_BUCKETS // 2
    max_exact = half // 2
    n = jnp.abs(rel)
    nf = jnp.maximum(n, 1).astype(F32)
    large = max_exact + (jnp.log(nf / max_exact) / math.log(T5_MAX_DISTANCE / max_exact)
                         * (half - max_exact)).astype(jnp.int32)
    large = jnp.minimum(large, half - 1)
    return jnp.where(rel > 0, half, 0) + jnp.where(n < max_exact, n, large)


def _dil_bias(t5_bias, dil):
    W = A_QBLOCK + 2 * DIL_HALF
    off = jnp.arange(W)[None, :] - DIL_HALF - jnp.arange(A_QBLOCK)[:, None]
    pick = (_t5_bucket(off * dil)[..., None] == jnp.arange(T5_BUCKETS)).astype(F32)
    b = jnp.einsum('qkb,bh->hqk', pick, t5_bias.astype(F32), precision=lax.Precision.HIGHEST)
    return jnp.where(jnp.abs(off) <= DIL_HALF, b, NEG_INF)[:, None]


def _na_bias(rpb, rows):
    kr = min(NA_ROWS_MAX, rows)
    ro = (jnp.arange(kr)[None, :] - jnp.arange(kr)[:, None]) + NA_ROWS_MAX - 1
    c = jnp.arange(GRID_W)
    col_start = jnp.clip(c - NA_COLS // 2, 0, GRID_W - NA_COLS)
    col_ok = (c[None, :] >= col_start[:, None]) & (c[None, :] < col_start[:, None] + NA_COLS)
    co = jnp.clip(c[None, :] - c[:, None] + NA_COLS - 1, 0, 2 * NA_COLS - 2)
    pick_r = (ro[..., None] == jnp.arange(2 * NA_ROWS_MAX - 1)).astype(F32)
    pick_c = (co[..., None] == jnp.arange(2 * NA_COLS - 1)).astype(F32)
    b = jnp.einsum('hrqk,pjr->hpqjk',
                   jnp.einsum('hrc,qkc->hrqk', rpb.astype(F32), pick_c, precision=lax.Precision.HIGHEST),
                   pick_r, precision=lax.Precision.HIGHEST)
    b = jnp.where(col_ok[None, None, :, None, :], b, NEG_INF)
    return b.reshape(rpb.shape[0], kr, GRID_W, kr * GRID_W)


def _s5_mats(lam_re, lam_im, log_step, b_re, b_im, c_re, c_im):
    G, P, C = b_re.shape
    NB = G // GROUPS_PER_BLOCK
    eye = jnp.eye(GROUPS_PER_BLOCK, dtype=F32)

    def bd_in(bb):
        t = bb.reshape(NB, GROUPS_PER_BLOCK, P, C).transpose(0, 1, 3, 2)
        return jnp.einsum('jgcp,gh->jgchp', t, eye).reshape(NB, GROUPS_PER_BLOCK * C, GROUPS_PER_BLOCK * P)

    def bd_out(cc):
        t = cc.reshape(NB, GROUPS_PER_BLOCK, C, P).transpose(0, 1, 3, 2)
        return jnp.einsum('jgpc,gh->jgphc', t, eye).reshape(NB, GROUPS_PER_BLOCK * P, GROUPS_PER_BLOCK * C)

    out = []
    for d in range(2):
        step = jnp.exp(log_step[d].astype(F32))[:, None]
        lr = jnp.minimum(lam_re[d].astype(F32), -1e-4)
        li = lam_im[d].astype(F32)
        mag = jnp.exp(lr * step)
        ab_re = mag * jnp.cos(li * step)
        ab_im = mag * jnp.sin(li * step)
        den = lr * lr + li * li
        zr = ((ab_re - 1.0) * lr + ab_im * li) / den
        zi = (ab_im * lr - (ab_re - 1.0) * li) / den
        bb_re = zr[..., None] * b_re - zi[..., None] * b_im
        bb_im = zr[..., None] * b_im + zi[..., None] * b_re
        out.append((ab_re.reshape(1, G * P), ab_im.reshape(1, G * P), bd_in(bb_re), bd_in(bb_im),
                    bd_out(c_re[d].astype(F32)), bd_out(-c_im[d].astype(F32))))
    return tuple(out)


def _sigmoid(z):
    return 1.0 / (1.0 + jnp.exp(-z))


def _strided(t, dil, pad):
    S, C = t.shape
    t = t.reshape(S // dil, dil * C)
    return jnp.pad(t, ((DIL_HALF, DIL_HALF), (0, 0))) if pad else t


def _dilated_fwd(q, k, v, t5_bias):
    S, AW = q.shape
    H = AW // HEAD_DIM
    parts = []
    for _, dil in DILATED_BRANCHES:
        num, m, l = _wattn_fwd(_strided(q, dil, False), _strided(k, dil, True), _strided(v, dil, True),
                               _dil_bias(t5_bias, dil), kind="dil", H=H, dil=dil, qoff=0, koff=0, voff=0,
                               name=f"dilated{dil}_fwd")
        parts += [num.reshape(S, AW), m.reshape(S, AW), l.reshape(S, AW)]

    def merge(n1, m1, l1, n2, m2, l2, n3, m3, l3):
        mx = jnp.maximum(jnp.maximum(m1, m2), m3)
        w1, w2, w3 = jnp.exp(m1 - mx), jnp.exp(m2 - mx), jnp.exp(m3 - mx)
        den = w1 * l1 + w2 * l2 + w3 * l3
        o = (w1 * n1 + w2 * n2 + w3 * n3) / den
        return o, o, mx + jnp.log(den)

    return _ew(merge, parts, out_dtypes=[F32, BF16, F32], name="dilated_merge")


def _dilated_bwd(q, k, v, t5_bias, do, lse, delta):
    S, AW = q.shape
    H = AW // HEAD_DIM
    dqs, dks, dvs = [], [], []
    dt5 = jnp.zeros(t5_bias.shape, F32)
    for _, dil in DILATED_BRANCHES:
        bias, bias_vjp = jax.vjp(functools.partial(_dil_bias, dil=dil), t5_bias)
        dq, dk, dv, db = _wattn_bwd(_strided(q, dil, False), _strided(do, dil, False), _strided(lse, dil, False),
                                    _strided(delta, dil, False), _strided(k, dil, True), _strided(v, dil, True),
                                    bias, kind="dil", H=H, dil=dil, qoff=0, koff=0, voff=0, dq_dtype=F32,
                                    name=f"dilated{dil}_bwd")
        dqs.append(dq.reshape(S, AW))
        dks.append(dk[DIL_HALF:-DIL_HALF].reshape(S, AW))
        dvs.append(dv[DIL_HALF:-DIL_HALF].reshape(S, AW))
        dt5 = dt5 + bias_vjp(db)[0]
    add3 = lambda a, b, c: a + b + c
    return (_ew(add3, dqs, out_dtypes=[BF16], name="dilated_dq_sum")[0],
            _ew(add3, dks, out_dtypes=[BF16], name="dilated_dk_sum")[0],
            _ew(add3, dvs, out_dtypes=[BF16], name="dilated_dv_sum")[0], dt5)


def _s5_fwd(u, mats, d_skip, w_glu, j):
    u = _interleave(u)
    xs, ys = [], []
    for d in range(2):
        x_re, x_im, y_d = _s5_scan_fwd(u, mats[d], reverse=(d == 1), name=f"s5_scan_fwd{d}")
        xs += [x_re, x_im]
        ys.append(y_d)

    def act(y0, y1, u, dsk):
        y = (y0 + y1) + dsk * u
        return y, _gelu(y)

    y, yg = _ew(act, ys + [u], [d_skip], out_dtypes=[F32, F32], name="s5_gelu")
    z = _mm_nn(yg, w_glu, j, "row", name="s5_glu_fwd")
    ob = _ew(lambda yg, z: yg * _sigmoid(z), [yg, z], out_dtypes=[BF16], name="s5_gate")[0]
    return _interleave(ob, inverse=True), (xs, y, yg, z, u)


def _s5_bwd(dmerged, ob_off, mats, d_skip, w_glu, j, saved):
    xs, y, yg, z, u = saved
    dob = _interleave(dmerged[:, ob_off * LANES:])

    def gate_bwd(dob, yg, z):
        sg = _sigmoid(z)
        return dob * yg * (sg * (1.0 - sg)), dob * sg

    dz, dyg1 = _ew(gate_bwd, [dob, yg, z], out_dtypes=[BF16, F32], name="s5_gate_bwd")
    dw_glu = _mm_tn(yg, dz, w_glu.shape[0], "row", name="s5_glu_dw")
    dyg2 = _mm_nt(dz, w_glu, j, "row", name="s5_glu_dx")

    def act_bwd(d1, d2, y, u):
        dy = (d1 + d2) * _gelu_grad(y)
        return dy, jnp.sum(dy * u, axis=0, keepdims=True)

    dy, dd = _ew(act_bwd, [dyg1, dyg2, y, u], out_dtypes=[F32], n_acc=1, name="s5_gelu_bwd")
    dmats, dus = [], []
    for d in range(2):
        du_d, dmat = _s5_scan_bwd(dy, u, xs[2 * d], xs[2 * d + 1], mats[d], reverse=(d == 1),
                                  name=f"s5_scan_bwd{d}")
        dus.append(du_d)
        dmats.append(dmat)
    du = _ew(lambda dy, d0, d1, dsk: dy * dsk + (d0 + d1), [dy] + dus, [d_skip], out_dtypes=[BF16],
             name="s5_du_sum")[0]
    return _interleave(du, inverse=True), tuple(dmats), dd, dw_glu


def _ab_fwd(x, j, P, W, first=False):
    t5 = P["t5_bias"]
    AW = t5.shape[1] * HEAD_DIM
    hn = _rms_fwd(x, P["norm_mix"][2 * j][None], name="rms_fwd")
    if first:
        proj, W = _mm_nn(hn, W["ab_w_in"], j, "col", name="ab_in_fwd", host=(W, [("ab_w_out", j), ("s5_w_glu", j)]))
    else:
        proj = _mm_nn(hn, W["ab_w_in"], j, "col", name="ab_in_fwd")
    q, k, v = (proj[:, i * AW:(i + 1) * AW].astype(BF16) for i in range(3))
    u = proj[:, 3 * AW:]
    oa32, oa16, lse = _dilated_fwd(q, k, v, t5)
    mats = _s5_mats(*(P[n][j] for n in _S5_PARAMS))
    ob, s5_saved = _s5_fwd(u, mats, P["s5_d"][j][None], W["s5_w_glu"], j)
    merged = jnp.concatenate([oa16, ob], axis=1)
    if first:
        x1, W = _mm_nn(merged, W["ab_w_out"], j, "row", mode="res", res=x, name="ab_out_fwd",
                       host=(W, [("mlp_w1", 0), ("mlp_w2", 0)]))
    else:
        x1 = _mm_nn(merged, W["ab_w_out"], j, "row", mode="res", res=x, name="ab_out_fwd")
    return x1, (x, hn, q, k, v, u, oa32, lse, merged, s5_saved), W


def _ab_bwd(dx1, j, P, W, saved):
    x, hn, q, k, v, u, oa32, lse, merged, s5_saved = saved
    t5 = P["t5_bias"]
    AW = t5.shape[1] * HEAD_DIM
    J = W["ab_w_in"].shape[0]
    dx1, dx1_16 = dx1
    dmerged = _mm_nt(dx1_16, W["ab_w_out"], j, "row", name="ab_out_dx")
    dw_out = _mm_tn(merged, dx1_16, J, "row", name="ab_out_dw")
    delta, do16 = _attn_delta(dmerged, 0, oa32, name="dilated_delta")
    dq, dk, dv, dt5 = _dilated_bwd(q, k, v, t5, do16, lse, delta)
    s5_params = tuple(P[n][j] for n in _S5_PARAMS)
    mats, mats_vjp = jax.vjp(_s5_mats, *s5_params)
    du, dmats, dd, dw_glu = _s5_bwd(dmerged, AW // LANES, mats, P["s5_d"][j][None], W["s5_w_glu"], j, s5_saved)
    ds5 = mats_vjp(dmats)
    dproj = jnp.concatenate([dq, dk, dv, du], axis=1)
    dw_in = _mm_tn(hn, dproj, J, "col", name="ab_in_dw")
    dhn = _mm_nt(dproj, W["ab_w_in"], j, "col", name="ab_in_dx")
    dx, dg = _rms_bwd(dhn, x, P["norm_mix"][2 * j][None], dx1, name="rms_bwd")
    small = dict(zip(_S5_PARAMS, ds5), s5_d=dd[0], t5_bias=dt5)
    return dx, dg[0], dict(ab_w_in=dw_in, ab_w_out=dw_out, s5_w_glu=dw_glu), small


def _c_fwd(x, j, P, W):
    H = P["c_rpb"].shape[1]
    hn = _rms_fwd(x, P["norm_mix"][2 * j + 1][None], name="rms_fwd")
    qkv = _mm_nn(hn, W["c_w_qkv"], j, "col", out_dtype=BF16, name="c_qkv_fwd")
    bias = _na_bias(P["c_rpb"][j], x.shape[0] // GRID_W)
    o, lse = _wattn_fwd(qkv, qkv, qkv, bias, kind="na", H=H, dil=1, qoff=0, koff=H, voff=2 * H, name="na_fwd")
    x1 = _mm_nn(o, W["c_w_out"], j, "row", mode="res", res=x, name="c_out_fwd")
    return x1, (x, hn, qkv, o, lse)


def _c_bwd(dx1, j, P, W, saved):
    x, hn, qkv, o, lse = saved
    H = P["c_rpb"].shape[1]
    J = W["c_w_qkv"].shape[0]
    dx1, dx1_16 = dx1
    do = _mm_nt(dx1_16, W["c_w_out"], j, "row", name="c_out_dx")
    dw_out = _mm_tn(o, dx1_16, J, "row", name="c_out_dw")
    bias, bias_vjp = jax.vjp(functools.partial(_na_bias, rows=x.shape[0] // GRID_W), P["c_rpb"][j])
    dq, dk, dv, db = _wattn_bwd(qkv, do, lse, o, qkv, qkv, bias, kind="na", H=H, dil=1, qoff=0, koff=H,
                                voff=2 * H, dq_dtype=BF16, name="na_bwd")
    dqkv = jnp.concatenate([dq, dk.astype(BF16), dv.astype(BF16)], axis=1)
    dw_qkv = _mm_tn(hn, dqkv, J, "col", name="c_qkv_dw")
    dhn = _mm_nt(dqkv, W["c_w_qkv"], j, "col", name="c_qkv_dx")
    dx, dg = _rms_bwd(dhn, x, P["norm_mix"][2 * j + 1][None], dx1, name="rms_bwd")
    return dx, dg[0], dict(c_w_qkv=dw_qkv, c_w_out=dw_out), dict(c_rpb=bias_vjp(db)[0])


def _layer_pieces(i):
    j = i // 2
    mixer = [("ab_w_in", j), ("ab_w_out", j), ("s5_w_glu", j)] if i % 2 == 0 else [("c_w_qkv", j), ("c_w_out", j)]
    return mixer, [("mlp_w1", i)], [("mlp_w2", i)]


def _mlp_fwd(x, i, P, W, depth):
    hn = _rms_fwd(x, P["norm_mlp"][i][None], name="rms_fwd")
    if i + 1 < depth:
        mixer, w1, w2 = _layer_pieces(i + 1)
        (a, hdn), W = _mm_nn(hn, W["mlp_w1"], i, "col", mode="relu2", name="mlp_w1_fwd", host=(W, mixer + w2))
        x2, W = _mm_nn(hdn, W["mlp_w2"], i, "row", mode="res", res=x, name="mlp_w2_fwd", host=(W, w1))
    else:
        a, hdn = _mm_nn(hn, W["mlp_w1"], i, "col", mode="relu2", name="mlp_w1_fwd")
        x2 = _mm_nn(hdn, W["mlp_w2"], i, "row", mode="res", res=x, name="mlp_w2_fwd")
    return x2, (x, hn, a, hdn), W


def _mlp_bwd(dx2, i, P, W, saved):
    x, hn, a, hdn = saved
    J = W["mlp_w1"].shape[0]
    dx2, dx2_16 = dx2
    da = _mm_nt(dx2_16, W["mlp_w2"], i, "row", out_dtype=BF16, mode="dact", act=a, name="mlp_w2_dx")
    dw2 = _mm_tn(hdn, dx2_16, J, "row", name="mlp_w2_dw")
    dw1 = _mm_tn(hn, da, J, "col", name="mlp_w1_dw")
    dhn = _mm_nt(da, W["mlp_w1"], i, "col", name="mlp_w1_dx")
    dx, dg = _rms_bwd(dhn, x, P["norm_mlp"][i][None], dx2, name="rms_bwd")
    return dx, dg[0], dict(mlp_w1=dw1, mlp_w2=dw2)


_S5_PARAMS = ("s5_lam_re", "s5_lam_im", "s5_log_step", "s5_b_re", "s5_b_im", "s5_c_re", "s5_c_im")
_BIG = ("ab_w_in", "ab_w_out", "s5_w_glu", "c_w_qkv", "c_w_out", "mlp_w1", "mlp_w2")
_SMALL = ("t5_bias", "s5_lam_re", "s5_lam_im", "s5_log_step", "s5_b_re", "s5_b_im", "s5_c_re", "s5_c_im", "s5_d",
          "c_rpb", "norm_mix", "norm_mlp", "norm_final")
_WEIGHTS = ("t5_bias", "ab_w_in", "ab_w_out", "s5_lam_re", "s5_lam_im", "s5_log_step", "s5_b_re", "s5_b_im",
            "s5_c_re", "s5_c_im", "s5_d", "s5_w_glu", "c_w_qkv", "c_w_out", "c_rpb", "norm_mix", "norm_mlp",
            "mlp_w1", "mlp_w2", "norm_final")


def _local_grads(x, target, P, W):
    depth = P["norm_mix"].shape[0]
    saved = []
    h = x
    W = _gather_pieces(W, [("ab_w_in", 0)])
    for i in range(depth):
        if i % 2 == 0:
            h, s_mix, W = _ab_fwd(h, i // 2, P, W, first=(i == 0))
        else:
            h, s_mix = _c_fwd(h, i // 2, P, W)
        h, s_mlp, W = _mlp_fwd(h, i, P, W, depth)
        saved.append((s_mix, s_mlp))
    dh, dg_final, loss_cols = _loss_and_grad(h, P["norm_final"][None], target, name="loss_head")
    big = {n: [None] * P[n].shape[0] for n in _BIG}
    small = {n: jnp.zeros(P[n].shape, F32) for n in _SMALL}
    small["norm_final"] = dg_final[0]
    for i in reversed(range(depth)):
        s_mix, s_mlp = saved[i]
        j = i // 2
        dh, dg, dbig = _mlp_bwd(dh, i, P, W, s_mlp)
        small["norm_mlp"] = small["norm_mlp"].at[i].set(dg)
        for n, g in dbig.items():
            big[n][i] = g
        dh, dg, dbig, dsmall = (_ab_bwd if i % 2 == 0 else _c_bwd)(dh, j, P, W, s_mix)
        small["norm_mix"] = small["norm_mix"].at[i].set(dg)
        for n, g in dbig.items():
            big[n][j] = g
        for n, g in dsmall.items():
            if n == "t5_bias":
                small[n] = small[n] + g
            else:
                small[n] = small[n].at[j].set(g.reshape(P[n].shape[1:]))
    return loss_cols[0, 0], dh[0], big, small


def _place():
    x, y, c = lax.axis_index("x"), lax.axis_index("y"), lax.axis_index("c")
    return x, y, c, ((1 - x, y), (x, 1 - y), (1 - x, 1 - y))


def _comm_call(body, arrays, out_shape, sems, *, name, in_place=False, **static):
    hbm = pl.BlockSpec(memory_space=pltpu.HBM)
    return pl.pallas_call(
        functools.partial(body, n=len(arrays), **static), name=name, in_specs=[hbm] * len(arrays),
        out_specs=tuple([hbm] * len(out_shape)), out_shape=tuple(out_shape),
        input_output_aliases={t: t for t in range(len(arrays))} if in_place else {},
        scratch_shapes=[pltpu.SemaphoreType.DMA((k,)) for k in sems])(*arrays)


def _cast_body(s_ref, w_ref, o_ref):
    o_ref[...] = w_ref[...].astype(o_ref.dtype)


def _cast_weights(weights, place):
    bufs = []
    for w in weights:
        L, Kd, Nd = w.shape
        tr = _tile(Kd, 256)
        bufs.append(_sliced_call(
            _cast_body, place, [w], [lambda l, i, p: (l, i, 0)], [(None, tr, Nd)], (None, None, tr, Nd),
            lambda l, i, p: (p[1], l, i, 0), jax.ShapeDtypeStruct((4, L, Kd, Nd), BF16), (L, Kd // tr),
            name="weights_cast"))
    return bufs


def _pair_exchange_body(*refs, n):
    ins, outs = refs[:n], refs[n:2 * n]
    send_sems, recv_sems, small_send, small_recv, small_local = refs[2 * n:]
    x, y, c, _ = _place()
    cps = []
    for t in range(n - 1):
        h = outs[t].shape[0]
        cps.append(pltpu.make_async_remote_copy(
            src_ref=ins[t].at[pl.ds((1 - c) * h, h)], dst_ref=outs[t], send_sem=send_sems.at[t],
            recv_sem=recv_sems.at[t], device_id=(x, y, 1 - c), device_id_type=MESH))
    for cp in cps:
        cp.start()
    _allreduce_body(ins[n - 1], outs[n - 1], small_send, small_recv, small_local, 1)
    for cp in cps:
        cp.wait()


def _chip_exchange_body(*refs, n):
    ins, outs = refs[:n], refs[n:2 * n]
    send_sems, recv_sems = refs[2 * n:]
    x, y, c, chips = _place()
    cps = []
    for t in range(n):
        h = ins[t].shape[0]
        for p, (px, py) in enumerate(chips):
            cps.append(pltpu.make_async_remote_copy(
                src_ref=ins[t].at[pl.ds(0, h), 2 * px + py], dst_ref=outs[t].at[p], send_sem=send_sems.at[3 * t + p],
                recv_sem=recv_sems.at[3 * t + p], device_id=(px, py, c), device_id_type=MESH))
    for cp in cps:
        cp.start()
    for cp in cps:
        cp.wait()


def _pair_share_body(*refs, n):
    bufs = refs[n:2 * n]
    send_sems, recv_sems = refs[2 * n:]
    x, y, c, _ = _place()
    sends, recvs = [], []
    for t in range(n):
        h = bufs[t].shape[0] // 2
        for half, group in ((c, sends), (1 - c, recvs)):
            rows = bufs[t].at[pl.ds(half * h, h)]
            group.append(pltpu.make_async_remote_copy(
                src_ref=rows, dst_ref=rows, send_sem=send_sems.at[t], recv_sem=recv_sems.at[t],
                device_id=(x, y, 1 - c), device_id_type=MESH))
    for cp in sends:
        cp.start()
    for cp in recvs:
        cp.wait_recv()
    for cp in sends:
        cp.wait_send()


def _allreduce_body(in_ref, out_ref, send_sems, recv_sems, local_sem, n):
    x, y, c, _ = _place()
    flip = lambda v, bit: 1 - v if bit else v
    peers = [(flip(x, k & 4), flip(y, k & 2), flip(c, k & 1)) for k in range(1, 8)]

    def remote(k, slot):
        return pltpu.make_async_remote_copy(
            src_ref=in_ref, dst_ref=out_ref.at[slot], send_sem=send_sems.at[k], recv_sem=recv_sems.at[k],
            device_id=peers[k], device_id_type=MESH)

    local = pltpu.make_async_copy(in_ref, out_ref.at[4 * x + 2 * y + c], local_sem.at[0])
    sends = [remote(k, 4 * x + 2 * y + c) for k in range(7)]
    local.start()
    for cp in sends:
        cp.start()
    for k, (px, py, pc) in enumerate(peers):
        remote(k, 4 * px + 2 * py + pc).wait_recv()
    for cp in sends:
        cp.wait_send()
    local.wait()


def _sliced_call(body, scalars, arrays, in_maps, blocks, out_block, out_map, out_shape, grid, *, name):
    grid_spec = pltpu.PrefetchScalarGridSpec(
        num_scalar_prefetch=1, grid=grid,
        in_specs=[pl.BlockSpec(b, m) for b, m in zip(blocks, in_maps)],
        out_specs=pl.BlockSpec(out_block, out_map))
    return pl.pallas_call(
        functools.partial(body), name=name, grid_spec=grid_spec, out_shape=out_shape,
        compiler_params=pltpu.CompilerParams(vmem_limit_bytes=V7X_VMEM_LIMIT_BYTES))(scalars, *arrays)


def _chip_sum_body(s_ref, g_ref, r_ref, o_ref):
    o_ref[...] = (g_ref[...] + r_ref[...]).astype(o_ref.dtype)


def _final_sum_body(s_ref, g_ref, r1_ref, a_ref, b_ref, c_ref, o_ref):
    o_ref[...] = (((g_ref[...] + r1_ref[...]) + a_ref[...].astype(F32)) + b_ref[...].astype(F32)) + c_ref[...].astype(F32)


def _reduce_big(stacks, place, small):
    n = len(stacks)
    half = [jax.ShapeDtypeStruct((s.shape[0] // 2,) + s.shape[1:], F32) for s in stacks]
    from_pair = _comm_call(_pair_exchange_body, stacks + [small], half + [jax.ShapeDtypeStruct((8,) + small.shape, F32)],
                           (n, n, 7, 7, 1), name="grads_pair_exchange")
    gathered, from_pair = from_pair[-1], from_pair[:-1]
    chip16 = []
    for s, r in zip(stacks, from_pair):
        h, J, Kd, Nd = r.shape
        tr = _tile(Kd, 256)
        blk = (None, None, tr, Nd)
        chip16.append(_sliced_call(
            _chip_sum_body, place, [s, r],
            [lambda l, j, i, p: (p[0] * h + l, j, i, 0), lambda l, j, i, p: (l, j, i, 0)], [blk, blk], blk,
            lambda l, j, i, p: (l, j, i, 0), jax.ShapeDtypeStruct(r.shape, BF16), (h, J, Kd // tr),
            name="grads_chip_sum"))
    recv = [jax.ShapeDtypeStruct((3, a.shape[0]) + a.shape[2:], BF16) for a in chip16]
    from_chips = _comm_call(_chip_exchange_body, chip16, recv, (3 * n, 3 * n), name="grads_chip_exchange")
    sums = []
    for s, r, f in zip(stacks, from_pair, from_chips):
        h, J, Kd, Nd = r.shape
        tr = _tile(Kd, 256)
        blk4, blk3 = (None, None, tr, Nd), (None, tr, Nd)
        mine = lambda l, i, p: (l, p[1], i, 0)
        sums.append(_sliced_call(
            _final_sum_body, place, [s, r, f, f, f],
            [lambda l, i, p: (p[0] * h + l, p[1], i, 0), mine] + [functools.partial(lambda l, i, p, q: (q, l, i, 0), q=q)
                                                                  for q in range(3)],
            [blk4, blk4, blk4, blk4, blk4], blk3, lambda l, i, p: (p[0] * h + l, i, 0),
            jax.ShapeDtypeStruct((2 * h, Kd, Nd), F32), (h, Kd // tr), name="grads_final_sum"))
    shared = _comm_call(_pair_share_body, sums, [jax.ShapeDtypeStruct(s.shape, F32) for s in sums], (n, n),
                        in_place=True, name="grads_pair_share")
    return shared, gathered


def _sum_small(gathered):
    def total(*b):
        acc = b[0]
        for t in b[1:]:
            acc = acc + t
        return acc

    return _ew(total, [gathered[i] for i in range(8)], out_dtypes=[F32], name="small_sum")[0]


def _pack(parts):
    flat = jnp.concatenate([p.reshape(-1).astype(F32) for p in parts])
    rows = -(-flat.shape[0] // (8 * LANES)) * 8
    return jnp.pad(flat, (0, rows * LANES - flat.shape[0])).reshape(rows, LANES)


def _unpack(buf, shapes):
    flat = buf.reshape(-1)
    out, at = [], 0
    for s in shapes:
        size = math.prod(s)
        out.append(flat[at:at + size].reshape(s))
        at += size
    return out


_INPUTS = ("x",) + _WEIGHTS + ("loss_target",) + tuple("m_" + n for n in _WEIGHTS) + tuple("v_" + n for n in _WEIGHTS)


def kernel(x, t5_bias, ab_w_in, ab_w_out, s5_lam_re, s5_lam_im, s5_log_step, s5_b_re, s5_b_im, s5_c_re, s5_c_im,
           s5_d, s5_w_glu, c_w_qkv, c_w_out, c_rpb, norm_mix, norm_mlp, mlp_w1, mlp_w2, norm_final, loss_target,
           m_t5_bias, m_ab_w_in, m_ab_w_out, m_s5_lam_re, m_s5_lam_im, m_s5_log_step, m_s5_b_re, m_s5_b_im,
           m_s5_c_re, m_s5_c_im, m_s5_d, m_s5_w_glu, m_c_w_qkv, m_c_w_out, m_c_rpb, m_norm_mix, m_norm_mlp,
           m_mlp_w1, m_mlp_w2, m_norm_final, v_t5_bias, v_ab_w_in, v_ab_w_out, v_s5_lam_re, v_s5_lam_im,
           v_s5_log_step, v_s5_b_re, v_s5_b_im, v_s5_c_re, v_s5_c_im, v_s5_d, v_s5_w_glu, v_c_w_qkv, v_c_w_out,
           v_c_rpb, v_norm_mix, v_norm_mlp, v_mlp_w1, v_mlp_w2, v_norm_final):
    args = (x, t5_bias, ab_w_in, ab_w_out, s5_lam_re, s5_lam_im, s5_log_step, s5_b_re, s5_b_im, s5_c_re, s5_c_im,
            s5_d, s5_w_glu, c_w_qkv, c_w_out, c_rpb, norm_mix, norm_mlp, mlp_w1, mlp_w2, norm_final, loss_target,
            m_t5_bias, m_ab_w_in, m_ab_w_out, m_s5_lam_re, m_s5_lam_im, m_s5_log_step, m_s5_b_re, m_s5_b_im,
            m_s5_c_re, m_s5_c_im, m_s5_d, m_s5_w_glu, m_c_w_qkv, m_c_w_out, m_c_rpb, m_norm_mix, m_norm_mlp,
            m_mlp_w1, m_mlp_w2, m_norm_final, v_t5_bias, v_ab_w_in, v_ab_w_out, v_s5_lam_re, v_s5_lam_im,
            v_s5_log_step, v_s5_b_re, v_s5_b_im, v_s5_c_re, v_s5_c_im, v_s5_d, v_s5_w_glu, v_c_w_qkv, v_c_w_out,
            v_c_rpb, v_norm_mix, v_norm_mlp, v_mlp_w1, v_mlp_w2, v_norm_final)
    A = dict(zip(_INPUTS, args, strict=True))
    P = {n: A[n] for n in _WEIGHTS}
    place = jnp.stack([lax.axis_index("c"), 2 * lax.axis_index("x") + lax.axis_index("y")]).astype(jnp.int32)

    gathered = _cast_weights([P[n] for n in _BIG], place)
    W = dict(zip(_BIG, gathered))
    loss, dx, big, small = _local_grads(A["x"][0], A["loss_target"][0], P, W)

    stacks = [jnp.stack(big[n]) for n in _BIG]
    reduced_big, gathered = _reduce_big(stacks, place, _pack([small[n] for n in _SMALL] + [loss.reshape(1)]))
    big_grads = dict(zip(_BIG, reduced_big))
    small_shapes = [P[n].shape for n in _SMALL] + [(1,)]
    reduced = _unpack(_sum_small(gathered), small_shapes)
    small_grads = dict(zip(_SMALL, reduced[:-1]))
    loss = reduced[-1][0]

    grads, delta, new_m, new_v = {}, {}, {}, {}
    for n in _BIG:
        g = big_grads[n]
        two_d = lambda t: t.reshape(-1, t.shape[-1])
        d, m, v = _adamw(two_d(P[n]), two_d(g), two_d(A["m_" + n]), two_d(A["v_" + n]), name="adamw")
        grads[n] = g
        delta[n], new_m[n], new_v[n] = (t.reshape(g.shape) for t in (d, m, v))
    d, m, v = _adamw(_pack([P[n] for n in _SMALL]), _pack([small_grads[n] for n in _SMALL]),
                     _pack([A["m_" + n] for n in _SMALL]), _pack([A["v_" + n] for n in _SMALL]), name="adamw_small")
    shapes = [P[n].shape for n in _SMALL]
    for n, dn, mn, vn in zip(_SMALL, _unpack(d, shapes), _unpack(m, shapes), _unpack(v, shapes)):
        grads[n] = small_grads[n]
        delta[n], new_m[n], new_v[n] = dn, mn, vn
    return (loss, dx[None], *[grads[n] for n in _WEIGHTS], *[delta[n] for n in _WEIGHTS],
            *[new_m[n] for n in _WEIGHTS], *[new_v[n] for n in _WEIGHTS])
```
